```python
import jax
import jax.numpy as jnp
from jax import lax
import numpy as np

D_MODEL = 1024
BATCH = 32
SEQ = 256
DEPTH = 2
DEC_BATCH = 4
DEC_SEQ = 4096
PAST_LEN = 256

GRID_W = 64
HEAD_DIM = 64
N_GROUPS = 4
GROUP_WIDTH = D_MODEL // N_GROUPS
N_HEADS = GROUP_WIDTH // HEAD_DIM
D_FF = ((8 * D_MODEL // 3 + 255) // 256) * 256
N_MOD = 9
GLA_RANK = 16
GLA_TAU = 16.0
CHUNK = 64
Q_BLOCK = 128
NA_KH = 8
NA_KW = 16
NA_QC = 16
NA_KCB = NA_QC + NA_KW
ROPE_BASE = 10000.0
EPS = 1e-6
SPLIT_SIZES = (GROUP_WIDTH,) * 4 + (2 * GLA_RANK,) + (GROUP_WIDTH,) * 3 + (GROUP_WIDTH,) * 4 + (2 * N_HEADS, 2 * N_HEADS) + (GROUP_WIDTH,) * 4
N_IN = sum(SPLIT_SIZES)
SPLIT_POINTS = tuple(int(s) for s in np.cumsum(SPLIT_SIZES)[:-1])

kernel_name = 'hybrid_diffusion_step'


def rmsnorm(x, w):
    xf = x.astype(jnp.float32)
    y = xf * lax.rsqrt(jnp.mean(xf * xf, axis=-1, keepdims=True) + EPS)
    return (y * w.astype(jnp.float32)).astype(x.dtype)


def modulate(h, shift, scale):
    return h * (1 + scale[:, None]) + shift[:, None]


def swiglu(h, w1, w3, w2):
    return (jax.nn.silu(h @ w1) * (h @ w3)) @ w2


def heads(t):
    B, T, _ = t.shape
    return t.reshape(B, T, -1, HEAD_DIM).transpose(0, 2, 1, 3)


def merge_heads(t):
    B, H, T, d = t.shape
    return t.transpose(0, 2, 1, 3).reshape(B, T, H * d)


def to_chunks(t):
    B, H, T = t.shape[:3]
    return jnp.moveaxis(t.reshape(B, H, T // CHUNK, CHUNK, *t.shape[3:]), 2, 0)


def from_chunks(t):
    t = jnp.moveaxis(t, 0, 2)
    return t.reshape(t.shape[0], t.shape[1], -1, *t.shape[4:])


def axial_rope(x, pos_r, pos_c):
    half = HEAD_DIM // 2
    quarter = half // 2
    inv = ROPE_BASE ** (-jnp.arange(quarter, dtype=jnp.float32) / quarter)

    def rot(xp, pos):
        ang = pos[:, None] * inv
        cos, sin = jnp.cos(ang), jnp.sin(ang)
        x1, x2 = xp[..., :quarter], xp[..., quarter:]
        return jnp.concatenate([x1 * cos - x2 * sin, x1 * sin + x2 * cos], axis=-1)

    return jnp.concatenate([rot(x[..., :half], pos_r), rot(x[..., half:], pos_c)], axis=-1)


def gla_scan(seqs, state):
    causal = jnp.tril(jnp.ones((CHUNK, CHUNK), bool))

    def step(S, xs):
        qc, kc, vc, ac = xs
        b = jnp.cumsum(ac, axis=2)
        diff = jnp.where(causal[:, :, None], b[:, :, :, None, :] - b[:, :, None, :, :], -jnp.inf)
        att = jnp.einsum('bhid,bhjd,bhijd->bhij', qc, kc, jnp.exp(diff))
        o = att @ vc + jnp.einsum('bhid,bhde->bhie', qc * jnp.exp(b), S)
        b_last = b[:, :, -1:, :]
        S = jnp.exp(b_last[:, :, 0, :, None]) * S + jnp.einsum('bhjd,bhje->bhde', kc * jnp.exp(b_last - b), vc)
        return S, o

    S, o = lax.scan(step, state[0], tuple(to_chunks(t) for t in seqs))
    return from_chunks(o), (S,)


def retention_scan(seqs, state, log_g):
    q, k, v = seqs
    idx = jnp.arange(CHUNK, dtype=jnp.float32)
    diff = idx[:, None] - idx[None, :]
    decay = jnp.where(diff >= 0, jnp.exp(jnp.maximum(diff, 0.0) * log_g[:, None, None]), 0.0)
    q_decay = jnp.exp((idx + 1.0) * log_g[:, None])
    k_decay = jnp.exp((CHUNK - 1.0 - idx) * log_g[:, None])
    chunk_decay = jnp.exp(CHUNK * log_g)

    def step(S, xs):
        qc, kc, vc = xs
        att = jnp.einsum('bhid,bhjd->bhij', qc, kc) * decay
        o = att @ vc + jnp.einsum('bhid,bhde->bhie', qc * q_decay[..., None], S)
        S = chunk_decay[:, None, None] * S + jnp.einsum('bhjd,bhje->bhde', kc * k_decay[..., None], vc)
        return S, o

    S, o = lax.scan(step, state[0], (to_chunks(q), to_chunks(k), to_chunks(v)))
    return from_chunks(o), (S,)


def mlstm_scan(seqs, state):
    causal = jnp.tril(jnp.ones((CHUNK, CHUNK), bool))

    def step(carry, xs):
        C, n, m = carry
        qc, kc, vc, ic, fc = xs
        F = jnp.cumsum(fc, axis=-1)
        logw = jnp.where(causal, F[..., :, None] - F[..., None, :] + ic[..., None, :], -jnp.inf)
        log_inter = F + m[..., None]
        m_i = jnp.maximum(log_inter, jnp.max(logw, axis=-1))
        w = jnp.exp(logw - m_i[..., None])
        w_inter = jnp.exp(log_inter - m_i)
        s = jnp.einsum('bhid,bhjd->bhij', qc, kc) * w
        num = s @ vc + w_inter[..., None] * jnp.einsum('bhid,bhde->bhie', qc, C)
        den = jnp.sum(s, axis=-1) + w_inter * jnp.einsum('bhid,bhd->bhi', qc, n)
        hc = num / jnp.maximum(jnp.abs(den), jnp.exp(-m_i))[..., None]
        m_new = m_i[..., -1]
        w_prev = jnp.exp(F[..., -1] + m - m_new)
        kw = kc * jnp.exp(F[..., -1:] - F + ic - m_new[..., None])[..., None]
        C = w_prev[..., None, None] * C + jnp.einsum('bhjd,bhje->bhde', kw, vc)
        n = w_prev[..., None] * n + jnp.sum(kw, axis=2)
        return (C, n, m_new), hc

    carry, o = lax.scan(step, state, tuple(to_chunks(t) for t in seqs))
    return from_chunks(o), carry


def bidir(run, seqs_f, seqs_b, st_f, st_b):
    o_f, s_f = run(0, seqs_f, st_f)
    o_b, s_b = run(1, tuple(jnp.flip(t, axis=2) for t in seqs_b), st_b)
    return o_f + jnp.flip(o_b, axis=2), s_f, s_b


def dense_attend(q, k, v):
    B, H, T, hd = q.shape
    qb = jnp.moveaxis(q.reshape(B, H, T // Q_BLOCK, Q_BLOCK, hd), 2, 0)

    def blk(qi):
        s = jnp.einsum('bhqd,bhkd->bhqk', qi, k).astype(jnp.float32)
        return jnp.einsum('bhqk,bhkd->bhqd', jax.nn.softmax(s, axis=-1).astype(v.dtype), v)

    o = lax.map(blk, qb)
    return jnp.moveaxis(o, 0, 2).reshape(B, H, T, hd)


def na_latent(q, k, v, k_ctx, v_ctx, rpb):
    B, H, T, hd = q.shape
    rows = T // GRID_W
    kh = min(NA_KH, rows)
    ncb = GRID_W // NA_QC
    r = np.arange(rows)
    krow = np.clip(r - kh // 2, 0, rows - kh)[:, None] + np.arange(kh)
    c0 = np.arange(ncb) * NA_QC
    kcol = np.clip(c0 - NA_KW // 2, 0, GRID_W - NA_KCB)[:, None] + np.arange(NA_KCB)
    qcol = c0[:, None] + np.arange(NA_QC)
    cs = np.clip(qcol - NA_KW // 2, 0, GRID_W - NA_KW)
    inwin = (kcol[:, None, :] >= cs[:, :, None]) & (kcol[:, None, :] < cs[:, :, None] + NA_KW)
    drow = krow - r[:, None] + NA_KH - 1
    dcol = np.clip(kcol[:, None, :] - qcol[:, :, None], -(NA_KW - 1), NA_KW - 1) + NA_KW - 1
    bias = rpb[:, drow[:, None, None, :, None], dcol[None, :, :, None, :]]

    qg = q.reshape(B, H, rows, ncb, NA_QC, hd)
    kb = jnp.take(k.reshape(B, H, rows, GRID_W, hd)[:, :, krow], kcol, axis=4)
    vb = jnp.take(v.reshape(B, H, rows, GRID_W, hd)[:, :, krow], kcol, axis=4)
    s_loc = jnp.einsum('bhrnqd,bhrkncd->bhrnqkc', qg, kb).astype(jnp.float32) + bias[None].astype(jnp.float32)
    s_loc = jnp.where(inwin[:, :, None, :], s_loc, -jnp.inf)
    s_ctx = jnp.einsum('bhrnqd,bhsd->bhrnqs', qg, k_ctx).astype(jnp.float32)
    n_loc = kh * NA_KCB
    s = jnp.concatenate([s_loc.reshape(*s_loc.shape[:5], n_loc), s_ctx], axis=-1)
    pr = jax.nn.softmax(s, axis=-1).astype(v.dtype)
    p_loc = pr[..., :n_loc].reshape(s_loc.shape)
    p_ctx = pr[..., n_loc:]
    o = jnp.einsum('bhrnqkc,bhrkncd->bhrnqd', p_loc, vb) + jnp.einsum('bhrnqs,bhsd->bhrnqd', p_ctx, v_ctx)
    return o.reshape(B, H, T, hd)


def token_mixing(h, p, l, ctx):
    f32 = jnp.float32
    B, T, _ = h.shape
    scale = HEAD_DIM ** -0.5
    u = h @ p['w_in'][l]
    (gq, gk, gv, gg, ga, nq, nk, nv, mq, mk, mv, mo, mi, mf, rq, rk, rv, rg) = jnp.split(u, SPLIT_POINTS, axis=-1)
    latent = ctx is not None
    if latent:
        def cached(name, d):
            return ctx[name][:, d].astype(f32)
        gla_st = [(cached('gla', d),) for d in (0, 1)]
        ml_st = [(cached('mC', d), cached('mn', d), cached('mm', d)) for d in (0, 1)]
        ret_st = [(cached('ret', d),) for d in (0, 1)]
    else:
        zS = jnp.zeros((B, N_HEADS, HEAD_DIM, HEAD_DIM), f32)
        zn = jnp.zeros((B, N_HEADS, HEAD_DIM), f32)
        zm = jnp.zeros((B, N_HEADS), f32)
        gla_st = [(zS,), (zS,)]
        ml_st = [(zS, zn, zm), (zS, zn, zm)]
        ret_st = [(zS,), (zS,)]

    qa = heads(gq).astype(f32) * scale
    ka = heads(gk).astype(f32)
    va = heads(gv).astype(f32)
    a_pre = jnp.einsum('btzr,zrk->zbtk', ga.reshape(B, T, 2, GLA_RANK).astype(f32), p['gla_w_a2'][l].astype(f32)) + p['gla_b_a'][l].astype(f32)[:, None, None, :]
    log_a = jax.nn.log_sigmoid(a_pre) / GLA_TAU
    oa, ga_f, ga_b = bidir(lambda d, s, st: gla_scan(s, st), (qa, ka, va, heads(log_a[0])), (qa, ka, va, heads(log_a[1])), gla_st[0], gla_st[1])
    out_a = merge_heads(rmsnorm(oa, p['gla_norm_w'][l])) * jax.nn.silu(gg.astype(f32))

    qn = heads(nq) * scale
    kn = heads(nk)
    vn = heads(nv)
    if latent:
        on = na_latent(qn, kn, vn, ctx['na_k'].astype(h.dtype), ctx['na_v'].astype(h.dtype), p['na_rpb'][l])
    else:
        on = dense_attend(qn, kn, vn)
    out_b = merge_heads(on).astype(f32)

    qm = heads(mq).astype(f32) * scale
    km = heads(mk).astype(f32)
    vm = heads(mv).astype(f32)
    ig = (mi.reshape(B, T, 2, N_HEADS).astype(f32) + p['mlstm_b_i'][l].astype(f32)).transpose(2, 0, 3, 1)
    lf = jax.nn.log_sigmoid(mf.reshape(B, T, 2, N_HEADS).astype(f32) + p['mlstm_b_f'][l].astype(f32)).transpose(2, 0, 3, 1)
    om, m_f, m_b = bidir(lambda d, s, st: mlstm_scan(s, st), (qm, km, vm, ig[0], lf[0]), (qm, km, vm, ig[1], lf[1]), ml_st[0], ml_st[1])
    out_c = merge_heads(om) * jax.nn.sigmoid(mo.astype(f32))

    qr = heads(rq).astype(f32)
    kr = heads(rk).astype(f32)
    if latent:
        t = jnp.arange(T)
        pos_r = (t // GRID_W).astype(f32)
        pos_c = (t % GRID_W).astype(f32)
        qr = axial_rope(qr, pos_r, pos_c)
        kr = axial_rope(kr, pos_r, pos_c)
    qr = qr * scale
    vr = heads(rv).astype(f32)
    log_g = jax.nn.log_sigmoid(p['ret_decay'][l].astype(f32))
    orr, r_f, r_b = bidir(lambda d, s, st: retention_scan(s, st, log_g[d]), (qr, kr, vr), (qr, kr, vr), ret_st[0], ret_st[1])
    mu = jnp.mean(orr, axis=-1, keepdims=True)
    var = jnp.mean(jnp.square(orr - mu), axis=-1, keepdims=True)
    orr = (orr - mu) * lax.rsqrt(var + EPS) * p['ret_norm_w'][l].astype(f32)
    out_d = merge_heads(orr) * jax.nn.silu(rg.astype(f32))

    y = jnp.concatenate([out_a, out_b, out_c, out_d], axis=-1).astype(h.dtype) @ p['w_out'][l]
    if latent:
        return y, None
    new = {
        'na_k': kn, 'na_v': vn,
        'gla': jnp.stack([ga_f[0], ga_b[0]], axis=1),
        'mC': jnp.stack([m_f[0], m_b[0]], axis=1),
        'mn': jnp.stack([m_f[1], m_b[1]], axis=1),
        'mm': jnp.stack([m_f[2], m_b[2]], axis=1),
        'ret': jnp.stack([r_f[0], r_b[0]], axis=1),
    }
    return y, new


def layer(x, p, l, mod, ctx):
    g = p['norm_w'][l]
    mod = mod.astype(x.dtype)

    def half_ffn(x, j, s):
        h = modulate(rmsnorm(x, g[2 * s]), mod[:, 3 * s], mod[:, 3 * s + 1])
        y = swiglu(h, p['ffn_w1'][l, j], p['ffn_w3'][l, j], p['ffn_w2'][l, j])
        return x + 0.5 * mod[:, 3 * s + 2][:, None] * rmsnorm(y, g[2 * s + 1])

    x = half_ffn(x, 0, 0)
    h = modulate(rmsnorm(x, g[2]), mod[:, 3], mod[:, 4])
    y, new = token_mixing(h, p, l, ctx)
    x = x + mod[:, 5][:, None] * rmsnorm(y, g[3])
    x = half_ffn(x, 1, 2)
    return x, new


def setup_inputs(seed: int = 0) -> dict:
    key = jax.random.key(seed)
    ks = jax.random.split(key, 28)
    f32 = jnp.float32
    H, HD, GW, D = N_HEADS, HEAD_DIM, GROUP_WIDTH, D_MODEL

    def nrm(k, shape, s):
        return jax.random.normal(k, shape, f32) * s

    gamma = 1.0 - 2.0 ** (-5.0 - np.arange(N_HEADS))
    ret_logit = jnp.asarray(np.log(gamma / (1.0 - gamma)), f32)
    return {
        'x_prompt': nrm(ks[0], (BATCH, SEQ, D), 1.0),
        'x_sample': nrm(ks[1], (DEC_BATCH, DEC_SEQ, D), 1.0),
        'cache_na_k': nrm(ks[2], (DEC_BATCH, DEPTH, H, PAST_LEN, HD), 1.0),
        'cache_na_v': nrm(ks[3], (DEC_BATCH, DEPTH, H, PAST_LEN, HD), 1.0),
        'state_gla': nrm(ks[4], (DEC_BATCH, DEPTH, 2, H, HD, HD), 0.5),
        'state_mlstm_C': nrm(ks[5], (DEC_BATCH, DEPTH, 2, H, HD, HD), 0.5),
        'state_mlstm_n': nrm(ks[6], (DEC_BATCH, DEPTH, 2, H, HD), 0.5),
        'state_mlstm_m': nrm(ks[7], (DEC_BATCH, DEPTH, 2, H), 1.0),
        'state_ret': nrm(ks[8], (DEC_BATCH, DEPTH, 2, H, HD, HD), 0.5),
        'c': nrm(ks[9], (DEC_BATCH, D), 1.0),
        'c_ctx': nrm(ks[10], (D,), 1.0),
        'w_mod': nrm(ks[11], (DEPTH, D, N_MOD * D), 0.5 * D ** -0.5),
        'b_mod': nrm(ks[12], (DEPTH, N_MOD * D), 0.02),
        'norm_w': 1.0 + nrm(ks[13], (DEPTH, 6, D), 0.02),
        'ffn_w1': nrm(ks[14], (DEPTH, 2, D, D_FF), D ** -0.5),
        'ffn_w3': nrm(ks[15], (DEPTH, 2, D, D_FF), D ** -0.5),
        'ffn_w2': nrm(ks[16], (DEPTH, 2, D_FF, D), D_FF ** -0.5),
        'w_in': nrm(ks[17], (DEPTH, D, N_IN), D ** -0.5),
        'w_out': nrm(ks[18], (DEPTH, D, D), D ** -0.5),
        'gla_w_a2': nrm(ks[19], (DEPTH, 2, GLA_RANK, GW), GLA_RANK ** -0.5),
        'gla_b_a': nrm(ks[20], (DEPTH, 2, GW), 0.1),
        'gla_norm_w': 1.0 + nrm(ks[21], (DEPTH, HD), 0.02),
        'na_rpb': nrm(ks[22], (DEPTH, H, 2 * NA_KH - 1, 2 * NA_KW - 1), 0.1),
        'mlstm_b_i': nrm(ks[23], (DEPTH, 2, H), 0.1),
        'mlstm_b_f': jnp.linspace(3.0, 6.0, H, dtype=f32) + nrm(ks[24], (DEPTH, 2, H), 0.1),
        'ret_decay': ret_logit + nrm(ks[25], (DEPTH, 2, H), 0.05),
        'ret_norm_w': 1.0 + nrm(ks[26], (DEPTH, HD), 0.02),
    }


def reference(x_prompt, x_sample, cache_na_k, cache_na_v, state_gla, state_mlstm_C, state_mlstm_n, state_mlstm_m, state_ret, c, c_ctx, w_mod, b_mod, norm_w, ffn_w1, ffn_w3, ffn_w2, w_in, w_out, gla_w_a2, gla_b_a, gla_norm_w, na_rpb, mlstm_b_i, mlstm_b_f, ret_decay, ret_norm_w):
    p = {
        'norm_w': norm_w, 'ffn_w1': ffn_w1, 'ffn_w3': ffn_w3, 'ffn_w2': ffn_w2,
        'w_in': w_in, 'w_out': w_out, 'gla_w_a2': gla_w_a2, 'gla_b_a': gla_b_a,
        'gla_norm_w': gla_norm_w, 'na_rpb': na_rpb, 'mlstm_b_i': mlstm_b_i,
        'mlstm_b_f': mlstm_b_f, 'ret_decay': ret_decay, 'ret_norm_w': ret_norm_w,
    }

    def modulation(cvec, l):
        m = jax.nn.silu(cvec) @ w_mod[l] + b_mod[l]
        return m.reshape(cvec.shape[0], N_MOD, D_MODEL)

    xp = x_prompt
    states = []
    for l in range(DEPTH):
        xp, st = layer(xp, p, l, modulation(c_ctx[None], l), None)
        states.append(st)
    new_na_k = jnp.stack([s['na_k'] for s in states], axis=1)
    new_na_v = jnp.stack([s['na_v'] for s in states], axis=1)
    new_gla = jnp.stack([s['gla'] for s in states], axis=1)
    new_mC = jnp.stack([s['mC'] for s in states], axis=1)
    new_mn = jnp.stack([s['mn'] for s in states], axis=1)
    new_mm = jnp.stack([s['mm'] for s in states], axis=1)
    new_ret = jnp.stack([s['ret'] for s in states], axis=1)

    xs = x_sample
    for l in range(DEPTH):
        ctx = {
            'na_k': cache_na_k[:, l], 'na_v': cache_na_v[:, l], 'gla': state_gla[:, l],
            'mC': state_mlstm_C[:, l], 'mn': state_mlstm_n[:, l], 'mm': state_mlstm_m[:, l],
            'ret': state_ret[:, l],
        }
        xs, _ = layer(xs, p, l, modulation(c, l), ctx)

    return (xp, xs, new_na_k, new_na_v, new_gla, new_mC, new_mn, new_mm, new_ret)
```

```python
import functools

import numpy as np
import jax
import jax.numpy as jnp
from jax import lax
from jax.experimental import pallas as pl
from jax.experimental.pallas import tpu as pltpu

D_MODEL = 1024
DEPTH = 2
HEAD_DIM = 64
N_HEADS = 4
GROUP_WIDTH = N_HEADS * HEAD_DIM
N_MOD = 9
GLA_RANK = 16
GLA_TAU = 16.0
CHUNK = 64
GRID_W = 64
NA_KH = 8
NA_KW = 16
ROPE_BASE = 10000.0
EPS = 1e-6
D_FF = 2816

HEADS_PER_GROUP = 2
GROUP_LANES = HEADS_PER_GROUP * HEAD_DIM
N_GROUPS = N_HEADS // HEADS_PER_GROUP
GLA_SUB = 16
SMALL_LANES = 128
MI_OFF = 2 * GLA_RANK
MF_OFF = MI_OFF + 2 * N_HEADS
NEG = -1e30
VMEM_LIMIT = 56 * 1024 * 1024

F32 = jnp.float32
BF16 = jnp.bfloat16
HI = lax.Precision.HIGHEST

_OFF = {}
_o = 0
for _name, _size in (("gq", 256), ("gk", 256), ("gv", 256), ("gg", 256), ("ga", 32), ("nq", 256), ("nk", 256),
                     ("nv", 256), ("mq", 256), ("mk", 256), ("mv", 256), ("mo", 256), ("mi", 8), ("mf", 8),
                     ("rq", 256), ("rk", 256), ("rv", 256), ("rg", 256)):
    _OFF[_name] = _o
    _o += _size
N_IN = _o
_MIXER_COLS = (("gq", "gk", "gv", "gg"), ("nq", "nk", "nv", None), ("mq", "mk", "mv", "mo"), ("rq", "rk", "rv", "rg"))
MAIN_COLS = 4 * N_GROUPS * 4 * GROUP_LANES


def _mm(a, b):
    return jnp.dot(a.astype(BF16), b.astype(BF16), preferred_element_type=F32)


def _mm_nt(a, b):
    return lax.dot_general(a.astype(BF16), b.astype(BF16), (((1,), (1,)), ((), ())), preferred_element_type=F32)


def _mm_tn(a, b):
    return lax.dot_general(a.astype(BF16), b.astype(BF16), (((0,), (0,)), ((), ())), preferred_element_type=F32)


def _mm_hi(a, b):
    return jnp.dot(a, b, precision=HI, preferred_element_type=F32)


def _mm_nt_hi(a, b):
    return lax.dot_general(a, b, (((1,), (1,)), ((), ())), precision=HI, preferred_element_type=F32)


def _log_sigmoid(x):
    return jnp.minimum(x, 0.0) - jnp.log1p(jnp.exp(-jnp.abs(x)))


def _silu(x):
    return x * jax.nn.sigmoid(x)


def _iota(shape, dim):
    return lax.broadcasted_iota(jnp.int32, shape, dim)


def _head_of(idx):
    return lax.shift_right_logical(idx, 6)


def _head_masks():
    lane = _iota((1, GROUP_LANES), 1)
    return [(_head_of(lane) == h).astype(F32) for h in range(HEADS_PER_GROUP)]


def _bd_mask():
    n = GROUP_LANES
    return (_head_of(_iota((n, n), 0)) == _head_of(_iota((n, n), 1))).astype(F32)


def _stack(x, masks):
    return jnp.concatenate([x * m for m in masks], axis=0)


def _unstack(r, masks, n):
    out = r[0:n] * masks[0]
    for h in range(1, len(masks)):
        out = out + r[h * n:(h + 1) * n] * masks[h]
    return out


def _rmsnorm(x, w):
    return x * lax.rsqrt(jnp.mean(x * x, axis=-1, keepdims=True) + EPS) * w


def _tri(reverse):
    i = _iota((CHUNK, CHUNK), 0)
    j = _iota((CHUNK, CHUNK), 1)
    return ((j >= i) if reverse else (j <= i)).astype(F32)


def _mod_kernel(c_ref, w_ref, b_ref, o_ref):
    s = _silu(c_ref[...])
    o_ref[0] = _mm(s, w_ref[0]) + b_ref[0]


def _modulation(cvec, w_mod, b_mod):
    n = N_MOD * D_MODEL
    tn = n // 8
    return pl.pallas_call(
        _mod_kernel,
        grid=(DEPTH, n // tn),
        in_specs=[
            pl.BlockSpec((8, D_MODEL), lambda l, j: (0, 0)),
            pl.BlockSpec((1, D_MODEL, tn), lambda l, j: (l, 0, j)),
            pl.BlockSpec((1, 1, tn), lambda l, j: (l, 0, j)),
        ],
        out_specs=pl.BlockSpec((1, 8, tn), lambda l, j: (l, 0, j)),
        out_shape=jax.ShapeDtypeStruct((DEPTH, 8, n), F32),
        compiler_params=pltpu.CompilerParams(dimension_semantics=("parallel", "parallel"),
                                             vmem_limit_bytes=VMEM_LIMIT),
        name="modulation",
    )(cvec, w_mod, b_mod.reshape(DEPTH, 1, n))


def _ffn_kernel(x_ref, mod_ref, g_ref, w1_ref, w3_ref, w2_ref, o_ref):
    x = x_ref[...]
    mod = mod_ref[0]
    h = _rmsnorm(x, g_ref[0:1]) * (1.0 + mod[1:2]) + mod[0:1]
    hb = h.astype(BF16)
    a = _silu(jnp.dot(hb, w1_ref[...], preferred_element_type=F32)) * jnp.dot(hb, w3_ref[...],
                                                                           preferred_element_type=F32)
    y = jnp.dot(a.astype(BF16), w2_ref[...], preferred_element_type=F32)
    o_ref[...] = x + 0.5 * mod[2:3] * _rmsnorm(y, g_ref[1:2])


def _row_tile(rows_per_batch):
    return min(256, rows_per_batch)


def _mod_index(n_mod, rows_per_batch, tm):
    if n_mod == 1:
        return lambda i: (0, 0, 0)
    per = rows_per_batch // tm
    return lambda i: (i // per, 0, 0)


def _ffn(x, mod3, g2, w1, w3, w2, rows_per_batch):
    rows = x.shape[0]
    tm = _row_tile(rows_per_batch)
    const = lambda i: (0, 0)
    return pl.pallas_call(
        _ffn_kernel,
        grid=(rows // tm,),
        in_specs=[
            pl.BlockSpec((tm, D_MODEL), lambda i: (i, 0)),
            pl.BlockSpec((1, 3, D_MODEL), _mod_index(mod3.shape[0], rows_per_batch, tm)),
            pl.BlockSpec((2, D_MODEL), const),
            pl.BlockSpec((D_MODEL, D_FF), const, pipeline_mode=pl.Buffered(1)),
            pl.BlockSpec((D_MODEL, D_FF), const, pipeline_mode=pl.Buffered(1)),
            pl.BlockSpec((D_FF, D_MODEL), const, pipeline_mode=pl.Buffered(1)),
        ],
        out_specs=pl.BlockSpec((tm, D_MODEL), lambda i: (i, 0)),
        out_shape=jax.ShapeDtypeStruct((rows, D_MODEL), F32),
        compiler_params=pltpu.CompilerParams(dimension_semantics=("parallel",), vmem_limit_bytes=VMEM_LIMIT),
        name="ffn",
    )(x, mod3, g2, w1, w3, w2)


def _inproj_kernel(x_ref, mod_ref, g_ref, w_ref, um_ref, us_ref):
    mod = mod_ref[0]
    h = _rmsnorm(x_ref[...], g_ref[...]) * (1.0 + mod[1:2]) + mod[0:1]
    u = jnp.dot(h.astype(BF16), w_ref[...], preferred_element_type=F32)
    um_ref[...] = u[:, :MAIN_COLS]
    us_ref[...] = u[:, MAIN_COLS:]


def _inproj(x, mod3, g, w_big, rows_per_batch):
    rows = x.shape[0]
    tm = _row_tile(rows_per_batch)
    const = lambda i: (0, 0)
    return pl.pallas_call(
        _inproj_kernel,
        grid=(rows // tm,),
        in_specs=[
            pl.BlockSpec((tm, D_MODEL), lambda i: (i, 0)),
            pl.BlockSpec((1, 3, D_MODEL), _mod_index(mod3.shape[0], rows_per_batch, tm)),
            pl.BlockSpec((1, D_MODEL), const),
            pl.BlockSpec((D_MODEL, MAIN_COLS + SMALL_LANES), const, pipeline_mode=pl.Buffered(1)),
        ],
        out_specs=[pl.BlockSpec((tm, MAIN_COLS), lambda i: (i, 0)),
                   pl.BlockSpec((tm, SMALL_LANES), lambda i: (i, 0))],
        out_shape=[jax.ShapeDtypeStruct((rows, MAIN_COLS), F32),
                   jax.ShapeDtypeStruct((rows, SMALL_LANES), F32)],
        compiler_params=pltpu.CompilerParams(dimension_semantics=("parallel",), vmem_limit_bytes=VMEM_LIMIT),
        name="inproj",
    )(x, mod3, g, w_big)


def _outproj_kernel(x_ref, ma_ref, mb_ref, mc_ref, md_ref, mod_ref, g_ref, w_ref, o_ref):
    mix = jnp.concatenate([ma_ref[...], mb_ref[...], mc_ref[...], md_ref[...]], axis=-1)
    y = jnp.dot(mix.astype(BF16), w_ref[...], preferred_element_type=F32)
    o_ref[...] = x_ref[...] + mod_ref[0][2:3] * _rmsnorm(y, g_ref[...])


def _outproj(x, mixes, mod3, g, w_out, rows_per_batch):
    rows = x.shape[0]
    tm = _row_tile(rows_per_batch)
    const = lambda i: (0, 0)
    mix_spec = pl.BlockSpec((tm, GROUP_WIDTH), lambda i: (i, 0))
    return pl.pallas_call(
        _outproj_kernel,
        grid=(rows // tm,),
        in_specs=[
            pl.BlockSpec((tm, D_MODEL), lambda i: (i, 0)),
            mix_spec, mix_spec, mix_spec, mix_spec,
            pl.BlockSpec((1, 3, D_MODEL), _mod_index(mod3.shape[0], rows_per_batch, tm)),
            pl.BlockSpec((1, D_MODEL), const),
            pl.BlockSpec((D_MODEL, D_MODEL), const, pipeline_mode=pl.Buffered(1)),
        ],
        out_specs=pl.BlockSpec((tm, D_MODEL), lambda i: (i, 0)),
        out_shape=jax.ShapeDtypeStruct((rows, D_MODEL), F32),
        compiler_params=pltpu.CompilerParams(dimension_semantics=("parallel",), vmem_limit_bytes=VMEM_LIMIT),
        name="outproj",
    )(x, *mixes, mod3, g, w_out)


def _mixer_specs(mixer, seq):
    gl = GROUP_LANES
    u_spec = pl.BlockSpec((seq, 4 * gl), lambda b, g: (b, mixer * N_GROUPS + g))
    small_spec = pl.BlockSpec((seq, SMALL_LANES), lambda b, g: (b, 0))
    out_spec = pl.BlockSpec((seq, gl), lambda b, g: (b, g))
    return u_spec, small_spec, out_spec


def _state_spec(rows):
    return pl.BlockSpec((1, 1, 2, rows, GROUP_LANES), lambda b, g: (b, g, 0, 0, 0))


_MIXER_PARAMS = pltpu.CompilerParams(dimension_semantics=("parallel", "parallel"), vmem_limit_bytes=VMEM_LIMIT)


def _chunk_rows(t, n_chunks, reverse):
    c = (n_chunks - 1 - t) if reverse else t
    return pl.ds(pl.multiple_of(c * CHUNK, CHUNK), CHUNK)


def _rope(x, cos, sin_signed):
    lane = _iota(x.shape, 1)
    first = (lane & 31) < 16
    swapped = jnp.where(first, pltpu.roll(x, GROUP_LANES - 16, 1), pltpu.roll(x, 16, 1))
    return x * cos + swapped * sin_signed


def _ret_kernel(dec_ref, nw_ref, u_ref, *rest, seq, rope, has_state, emit_state):
    rest = list(rest)
    cos_ref, sin_ref = (rest.pop(0), rest.pop(0)) if rope else (None, None)
    s0_ref = rest.pop(0) if has_state else None
    o_ref = rest.pop(0)
    st_ref = rest.pop(0) if emit_state else None
    acc_ref = rest.pop(0)

    gl, L, G = GROUP_LANES, CHUNK, HEADS_PER_GROUP
    n_chunks = seq // L
    g = pl.program_id(1)
    masks = _head_masks()
    bd = _bd_mask()
    bd_avg = bd * (1.0 / HEAD_DIM)
    scale = HEAD_DIM ** -0.5
    ri = _iota((L, gl), 0).astype(F32)
    si = _iota((G * L, L), 0)
    sj = _iota((G * L, L), 1)
    i_idx = si & (L - 1)
    hrow = _head_of(_iota((G * L, 1), 0))

    for d in (0, 1):
        reverse = d == 1
        raw_row = masks[0] * dec_ref[d, g * G]
        raw_col = jnp.where(hrow == 0, dec_ref[d, g * G], 0.0)
        for h in range(1, G):
            raw_row = raw_row + masks[h] * dec_ref[d, g * G + h]
            raw_col = jnp.where(hrow == h, dec_ref[d, g * G + h], raw_col)
        lg_row = _log_sigmoid(raw_row)
        lg_col = _log_sigmoid(raw_col)
        diff = ((sj - i_idx) if reverse else (i_idx - sj)).astype(F32)
        dmat = jnp.where(diff >= 0, jnp.exp(jnp.maximum(diff, 0.0) * lg_col), 0.0)
        qdec = jnp.exp(((L - ri) if reverse else (ri + 1.0)) * lg_row)
        kdec = jnp.exp((ri if reverse else (L - 1.0 - ri)) * lg_row)
        cdec = jnp.exp(float(L) * lg_row)
        s_init = s0_ref[0, 0, d] if has_state else jnp.zeros((gl, gl), F32)

        def body(t, S, reverse=reverse, dmat=dmat, qdec=qdec, kdec=kdec, cdec=cdec):
            rows = _chunk_rows(t, n_chunks, reverse)
            q = u_ref[rows, 0:gl]
            k = u_ref[rows, gl:2 * gl]
            v = u_ref[rows, 2 * gl:3 * gl]
            if rope:
                cos = cos_ref[rows, :]
                sin = sin_ref[rows, :]
                q = _rope(q, cos, sin)
                k = _rope(k, cos, sin)
            q = q * scale
            att = _mm_nt(_stack(q, masks), k) * dmat
            o = _unstack(_mm(att, v), masks, L) + _mm(q * qdec, S)
            S = cdec * S + bd * _mm_tn(k * kdec, v)
            if not reverse:
                acc_ref[rows, :] = o
            else:
                o = acc_ref[rows, :] + o
                mu = _mm_hi(o, bd_avg)
                dev = o - mu
                var = _mm_hi(dev * dev, bd_avg)
                y = dev * lax.rsqrt(var + EPS) * nw_ref[...]
                o_ref[rows, :] = y * _silu(u_ref[rows, 3 * gl:4 * gl])
            return S

        s_fin = lax.fori_loop(0, n_chunks, body, s_init)
        if emit_state:
            st_ref[0, 0, d] = s_fin


def _ret_tables(seq):
    t = np.arange(seq)
    quarter = HEAD_DIM // 4
    inv = (ROPE_BASE ** (-np.arange(quarter, dtype=np.float32) / quarter)).astype(np.float32)
    ang_r = (t // GRID_W).astype(np.float32)[:, None] * inv
    ang_c = (t % GRID_W).astype(np.float32)[:, None] * inv
    cos = np.concatenate([np.cos(ang_r), np.cos(ang_r), np.cos(ang_c), np.cos(ang_c)], axis=1)
    sin = np.concatenate([-np.sin(ang_r), np.sin(ang_r), -np.sin(ang_c), np.sin(ang_c)], axis=1)
    tile = lambda a: jnp.asarray(np.tile(a.astype(np.float32), (1, HEADS_PER_GROUP)))
    return tile(cos), tile(sin)


def _retention(u_main, ret_decay, norm_w, batch, seq, state):
    latent = state is not None
    u_spec, _, out_spec = _mixer_specs(3, seq)
    gl = GROUP_LANES
    const2 = lambda b, g: (0, 0)
    in_specs = [pl.BlockSpec(memory_space=pltpu.SMEM), pl.BlockSpec((1, gl), const2), u_spec]
    args = [ret_decay, jnp.tile(norm_w.reshape(1, HEAD_DIM), (1, HEADS_PER_GROUP)), u_main]
    if latent:
        cos, sin = _ret_tables(seq)
        in_specs += [pl.BlockSpec((seq, gl), const2), pl.BlockSpec((seq, gl), const2), _state_spec(gl)]
        args += [cos, sin, state]
    out_shape = [jax.ShapeDtypeStruct((batch * seq, GROUP_WIDTH), F32)]
    out_specs = [out_spec]
    if not latent:
        out_shape.append(jax.ShapeDtypeStruct((batch, N_GROUPS, 2, gl, gl), F32))
        out_specs.append(_state_spec(gl))
    res = pl.pallas_call(
        functools.partial(_ret_kernel, seq=seq, rope=latent, has_state=latent, emit_state=not latent),
        grid=(batch, N_GROUPS),
        in_specs=in_specs,
        out_specs=out_specs,
        out_shape=out_shape,
        scratch_shapes=[pltpu.VMEM((seq, gl), F32)],
        compiler_params=_MIXER_PARAMS,
        name="retention",
    )(*args)
    return res[0], (None if latent else res[1])


def _gla_kernel(wa_ref, ba_ref, nw_ref, u_ref, us_ref, *rest, seq, has_state, emit_state):
    rest = list(rest)
    s0_ref = rest.pop(0) if has_state else None
    o_ref = rest.pop(0)
    st_ref = rest.pop(0) if emit_state else None
    acc_ref = rest.pop(0)

    gl, L, G, SB = GROUP_LANES, CHUNK, HEADS_PER_GROUP, GLA_SUB
    n_chunks = seq // L
    n_sub = L // SB
    masks = _head_masks()
    bd = _bd_mask()
    bd_avg = bd * (1.0 / HEAD_DIM)
    scale = HEAD_DIM ** -0.5
    pair_row = _iota((SB * SB, gl), 0)
    pi = lax.shift_right_logical(pair_row, 4)
    pj = pair_row & (SB - 1)

    for d in (0, 1):
        reverse = d == 1
        tri = _tri(reverse)
        pair_ok = ((pj >= pi) if reverse else (pj <= pi)).astype(F32)
        s_init = s0_ref[0, 0, d] if has_state else jnp.zeros((gl, gl), F32)

        def body(t, ST, d=d, reverse=reverse, tri=tri, pair_ok=pair_ok):
            rows = _chunk_rows(t, n_chunks, reverse)
            a_pre = _mm(us_ref[rows, :], wa_ref[d]) + ba_ref[d]
            la = _log_sigmoid(a_pre) * (1.0 / GLA_TAU)
            b = _mm_hi(tri, la)
            btot = b[0:1] if reverse else b[L - 1:L]
            q = u_ref[rows, 0:gl] * scale
            k = u_ref[rows, gl:2 * gl]
            v = u_ref[rows, 2 * gl:3 * gl]
            o_inter = _mm_nt(q * jnp.exp(b), ST)
            ST_new = ST * jnp.exp(btot) + bd * _mm_tn(v, k * jnp.exp(btot - b))

            blocks = []
            for i_blk in range(n_sub):
                lo = i_blk * SB
                qi, ki, vi, bi = q[lo:lo + SB], k[lo:lo + SB], v[lo:lo + SB], b[lo:lo + SB]
                prods = [qi[i:i + 1] * ki * jnp.exp(jnp.minimum(bi[i:i + 1] - bi, 0.0)) for i in range(SB)]
                att = _mm(jnp.concatenate(prods, axis=0), bd)
                weighted = att * jnp.concatenate([vi] * SB, axis=0) * pair_ok
                o_blk = jnp.sum(weighted.reshape(SB, SB, gl), axis=1)
                if reverse and i_blk < n_sub - 1:
                    ref = bi[SB - 1:SB]
                    ko, vo, bo = k[lo + SB:], v[lo + SB:], b[lo + SB:]
                elif (not reverse) and i_blk > 0:
                    ref = bi[0:1]
                    ko, vo, bo = k[:lo], v[:lo], b[:lo]
                else:
                    ref = None
                if ref is not None:
                    qd = qi * jnp.exp(bi - ref)
                    kd = ko * jnp.exp(ref - bo)
                    att_o = _mm_nt(_stack(qd, masks), kd)
                    o_blk = o_blk + _unstack(_mm(att_o, vo), masks, SB)
                blocks.append(o_blk)
            o = o_inter + jnp.concatenate(blocks, axis=0)

            if not reverse:
                acc_ref[rows, :] = o
            else:
                o = acc_ref[rows, :] + o
                ms = _mm_hi(o * o, bd_avg)
                y = o * lax.rsqrt(ms + EPS) * nw_ref[...]
                o_ref[rows, :] = y * _silu(u_ref[rows, 3 * gl:4 * gl])
            return ST_new

        s_fin = lax.fori_loop(0, n_chunks, body, s_init)
        if emit_state:
            st_ref[0, 0, d] = s_fin


def _gla(u_main, u_small, wa_pad, b_a, norm_w, batch, seq, state):
    latent = state is not None
    u_spec, small_spec, out_spec = _mixer_specs(0, seq)
    gl = GROUP_LANES
    in_specs = [
        pl.BlockSpec((2, SMALL_LANES, gl), lambda b, g: (0, 0, g)),
        pl.BlockSpec((2, 1, gl), lambda b, g: (0, 0, g)),
        pl.BlockSpec((1, gl), lambda b, g: (0, 0)),
        u_spec, small_spec,
    ]
    args = [wa_pad, b_a.reshape(2, 1, GROUP_WIDTH),
            jnp.tile(norm_w.reshape(1, HEAD_DIM), (1, HEADS_PER_GROUP)), u_main, u_small]
    if latent:
        in_specs.append(_state_spec(gl))
        args.append(state)
    out_shape = [jax.ShapeDtypeStruct((batch * seq, GROUP_WIDTH), F32)]
    out_specs = [out_spec]
    if not latent:
        out_shape.append(jax.ShapeDtypeStruct((batch, N_GROUPS, 2, gl, gl), F32))
        out_specs.append(_state_spec(gl))
    res = pl.pallas_call(
        functools.partial(_gla_kernel, seq=seq, has_state=latent, emit_state=not latent),
        grid=(batch, N_GROUPS),
        in_specs=in_specs,
        out_specs=out_specs,
        out_shape=out_shape,
        scratch_shapes=[pltpu.VMEM((seq, gl), F32)],
        compiler_params=_MIXER_PARAMS,
        name="gla",
    )(*args)
    return res[0], (None if latent else res[1])


def _mlstm_kernel(ei_ref, ef_ref, bi_ref, bf_ref, u_ref, us_ref, *rest, seq, has_state, emit_state):
    rest = list(rest)
    if has_state:
        c0_ref, n0_ref, m0_ref = rest.pop(0), rest.pop(0), rest.pop(0)
    o_ref = rest.pop(0)
    if emit_state:
        cst_ref, nst_ref, mst_ref = rest.pop(0), rest.pop(0), rest.pop(0)
    acc_ref = rest.pop(0)

    gl, L, G = GROUP_LANES, CHUNK, HEADS_PER_GROUP
    n_chunks = seq // L
    masks = _head_masks()
    bd = _bd_mask()
    scale = HEAD_DIM ** -0.5
    si = _iota((G * L, L), 0) & (L - 1)
    sj = _iota((G * L, L), 1)
    sel = ((_iota((8, gl), 1) == _iota((8, gl), 0) * HEAD_DIM)).astype(F32)
    last = 0

    for d in (0, 1):
        reverse = d == 1
        tri = _tri(reverse)
        causal = (sj >= si) if reverse else (sj <= si)
        last = 0 if reverse else L - 1
        if has_state:
            init = (c0_ref[0, 0, d], n0_ref[0, 0, d], m0_ref[0, 0, d])
        else:
            init = (jnp.zeros((gl, gl), F32), jnp.zeros((1, gl), F32), jnp.zeros((1, gl), F32))

        def body(t, carry, d=d, reverse=reverse, tri=tri, causal=causal, last=last):
            C, n_row, m_row = carry
            rows = _chunk_rows(t, n_chunks, reverse)
            us = us_ref[rows, :]
            ig = _mm_hi(us, ei_ref[d, 0]) + bi_ref[d, 0]
            fg = _log_sigmoid(_mm_hi(us, ef_ref[d, 0]) + bf_ref[d, 0])
            F = _mm_hi(tri, fg)
            f_tot = F[last:last + 1]
            g_rows = _mm_nt_hi(sel, ig - F)
            f_col = jnp.concatenate([F[:, h * HEAD_DIM:h * HEAD_DIM + 1] for h in range(G)], axis=0)
            m_col = jnp.concatenate(
                [jnp.broadcast_to(m_row[:, h * HEAD_DIM:h * HEAD_DIM + 1], (L, 1)) for h in range(G)], axis=0)
            logw = jnp.concatenate(
                [F[:, h * HEAD_DIM:h * HEAD_DIM + 1] + g_rows[h:h + 1, :] for h in range(G)], axis=0)
            logw = jnp.where(causal, logw, NEG)
            log_inter = f_col + m_col
            m_i = jnp.maximum(log_inter, jnp.max(logw, axis=1, keepdims=True))
            w = jnp.exp(logw - m_i)
            w_inter = jnp.exp(log_inter - m_i)
            q = u_ref[rows, 0:gl] * scale
            k = u_ref[rows, gl:2 * gl]
            v = u_ref[rows, 2 * gl:3 * gl]
            qs = _stack(q, masks)
            s = _mm_nt(qs, k) * w
            num = _mm(s, v) + w_inter * _mm(qs, C)
            den = jnp.sum(s, axis=1, keepdims=True) + w_inter * jnp.sum(qs * n_row, axis=1, keepdims=True)
            hc = num / jnp.maximum(jnp.abs(den), jnp.exp(-m_i))
            o = _unstack(hc, masks, L)

            m_new = m_i[last:last + 1] * masks[0]
            for h in range(1, G):
                m_new = m_new + m_i[h * L + last:h * L + last + 1] * masks[h]
            w_prev = jnp.exp(f_tot + m_row - m_new)
            kw = k * jnp.exp(f_tot - F + ig - m_new)
            C = w_prev * C + bd * _mm_tn(kw, v)
            n_row = w_prev * n_row + jnp.sum(kw, axis=0, keepdims=True)

            if not reverse:
                acc_ref[rows, :] = o
            else:
                o = acc_ref[rows, :] + o
                o_ref[rows, :] = o * jax.nn.sigmoid(u_ref[rows, 3 * gl:4 * gl])
            return C, n_row, m_new

        c_fin, n_fin, m_fin = lax.fori_loop(0, n_chunks, body, init)
        if emit_state:
            cst_ref[0, 0, d] = c_fin
            nst_ref[0, 0, d] = n_fin
            mst_ref[0, 0, d] = m_fin


def _gate_expanders(offset):
    e = np.zeros((2, N_GROUPS, SMALL_LANES, GROUP_LANES), np.float32)
    for d in range(2):
        for g in range(N_GROUPS):
            for h in range(HEADS_PER_GROUP):
                src = offset + d * N_HEADS + g * HEADS_PER_GROUP + h
                e[d, g, src, h * HEAD_DIM:(h + 1) * HEAD_DIM] = 1.0
    return jnp.asarray(e)


def _gate_bias(b):
    return jnp.repeat(b.astype(F32), HEAD_DIM, axis=1).reshape(2, N_GROUPS, 1, GROUP_LANES)


def _mlstm(u_main, u_small, b_i, b_f, batch, seq, state):
    latent = state is not None
    u_spec, small_spec, out_spec = _mixer_specs(2, seq)
    gl = GROUP_LANES
    exp_spec = pl.BlockSpec((2, 1, SMALL_LANES, gl), lambda b, g: (0, g, 0, 0))
    bias_spec = pl.BlockSpec((2, 1, 1, gl), lambda b, g: (0, g, 0, 0))
    in_specs = [exp_spec, exp_spec, bias_spec, bias_spec, u_spec, small_spec]
    args = [_gate_expanders(MI_OFF), _gate_expanders(MF_OFF), _gate_bias(b_i), _gate_bias(b_f), u_main, u_small]
    if latent:
        in_specs += [_state_spec(gl), _state_spec(1), _state_spec(1)]
        args += list(state)
    out_shape = [jax.ShapeDtypeStruct((batch * seq, GROUP_WIDTH), F32)]
    out_specs = [out_spec]
    if not latent:
        out_shape += [jax.ShapeDtypeStruct((batch, N_GROUPS, 2, gl, gl), F32),
                      jax.ShapeDtypeStruct((batch, N_GROUPS, 2, 1, gl), F32),
                      jax.ShapeDtypeStruct((batch, N_GROUPS, 2, 1, gl), F32)]
        out_specs += [_state_spec(gl), _state_spec(1), _state_spec(1)]
    res = pl.pallas_call(
        functools.partial(_mlstm_kernel, seq=seq, has_state=latent, emit_state=not latent),
        grid=(batch, N_GROUPS),
        in_specs=in_specs,
        out_specs=out_specs,
        out_shape=out_shape,
        scratch_shapes=[pltpu.VMEM((seq, gl), F32)],
        compiler_params=_MIXER_PARAMS,
        name="mlstm",
    )(*args)
    return res[0], (None if latent else tuple(res[1:]))


def _na_ctx_kernel(u_ref, o_ref, *, seq):
    gl = GROUP_LANES
    masks = _head_masks()
    q = u_ref[:, 0:gl] * (HEAD_DIM ** -0.5)
    k = u_ref[:, gl:2 * gl]
    v = u_ref[:, 2 * gl:3 * gl]
    s = _mm_nt(_stack(q, masks), k)
    p = jnp.exp(s - jnp.max(s, axis=1, keepdims=True))
    o = _mm(p, v) / jnp.sum(p, axis=1, keepdims=True)
    o_ref[...] = _unstack(o, masks, seq)


def _na_context(u_main, batch, seq):
    u_spec, _, out_spec = _mixer_specs(1, seq)
    return pl.pallas_call(
        functools.partial(_na_ctx_kernel, seq=seq),
        grid=(batch, N_GROUPS),
        in_specs=[u_spec],
        out_specs=out_spec,
        out_shape=jax.ShapeDtypeStruct((batch * seq, GROUP_WIDTH), F32),
        compiler_params=_MIXER_PARAMS,
        name="na_context",
    )(u_main)


def _na_lat_kernel(tab_ref, kc_ref, vc_ref, u_ref, o_ref, *, seq):
    gl = GROUP_LANES
    grid_rows = seq // GRID_W
    win = NA_KH * GRID_W
    masks = _head_masks()
    scale = HEAD_DIM ** -0.5
    kc = kc_ref[0, 0]
    vc = vc_ref[0, 0]

    def body(r, carry):
        ks = jnp.clip(r - NA_KH // 2, 0, grid_rows - NA_KH)
        q_rows = pl.ds(pl.multiple_of(r * GRID_W, GRID_W), GRID_W)
        k_rows = pl.ds(pl.multiple_of(ks * GRID_W, GRID_W), win)
        qs = _stack(u_ref[q_rows, 0:gl] * scale, masks)
        s_loc = _mm_nt(qs, u_ref[k_rows, gl:2 * gl]) + tab_ref[0, ks - r + NA_KH - 1]
        s_ctx = _mm_nt(qs, kc)
        m = jnp.maximum(jnp.max(s_loc, axis=1, keepdims=True), jnp.max(s_ctx, axis=1, keepdims=True))
        p_loc = jnp.exp(s_loc - m)
        p_ctx = jnp.exp(s_ctx - m)
        den = jnp.sum(p_loc, axis=1, keepdims=True) + jnp.sum(p_ctx, axis=1, keepdims=True)
        o = (_mm(p_loc, u_ref[k_rows, 2 * gl:3 * gl]) + _mm(p_ctx, vc)) / den
        o_ref[q_rows, :] = _unstack(o, masks, GRID_W)
        return carry

    lax.fori_loop(0, grid_rows, body, 0)


def _na_bias_table(rpb):
    c = np.arange(GRID_W)
    rel = c[None, :] - c[:, None]
    cs = np.clip(c - NA_KW // 2, 0, GRID_W - NA_KW)
    valid = (c[None, :] >= cs[:, None]) & (c[None, :] < cs[:, None] + NA_KW)
    dcol = np.clip(rel, -(NA_KW - 1), NA_KW - 1) + NA_KW - 1
    bmat = jnp.where(valid[None, None], rpb.astype(F32)[:, :, dcol], NEG)
    tabs = jnp.stack([bmat[:, s:s + NA_KH] for s in range(NA_KH)], axis=1)
    tabs = tabs.transpose(0, 1, 3, 2, 4).reshape(N_GROUPS, HEADS_PER_GROUP, NA_KH, GRID_W, NA_KH * GRID_W)
    return tabs.transpose(0, 2, 1, 3, 4).reshape(N_GROUPS, NA_KH, HEADS_PER_GROUP * GRID_W, NA_KH * GRID_W)


def _heads_to_lanes(t):
    b, _, s, _ = t.shape
    t = t.reshape(b, N_GROUPS, HEADS_PER_GROUP, s, HEAD_DIM).transpose(0, 1, 3, 2, 4)
    return t.reshape(b, N_GROUPS, s, GROUP_LANES)


def _na_latent(u_main, rpb, k_ctx, v_ctx, batch, seq):
    u_spec, _, out_spec = _mixer_specs(1, seq)
    gl = GROUP_LANES
    past = k_ctx.shape[2]
    tab = _na_bias_table(rpb)
    ctx_spec = pl.BlockSpec((1, 1, past, gl), lambda b, g: (b, g, 0, 0))
    return pl.pallas_call(
        functools.partial(_na_lat_kernel, seq=seq),
        grid=(batch, N_GROUPS),
        in_specs=[pl.BlockSpec((1,) + tab.shape[1:], lambda b, g: (g, 0, 0, 0)), ctx_spec, ctx_spec, u_spec],
        out_specs=out_spec,
        out_shape=jax.ShapeDtypeStruct((batch * seq, GROUP_WIDTH), F32),
        compiler_params=_MIXER_PARAMS,
        name="na_latent",
    )(tab, _heads_to_lanes(k_ctx.astype(F32)), _heads_to_lanes(v_ctx.astype(F32)), u_main)


def _to_block_diag(s):
    b = s.shape[0]
    G = HEADS_PER_GROUP
    s = s.astype(F32).reshape(b, 2, N_GROUPS, G, HEAD_DIM, 1, HEAD_DIM)
    eye = jnp.eye(G, dtype=F32).reshape(1, 1, 1, G, 1, G, 1)
    return (s * eye).reshape(b, 2, N_GROUPS, GROUP_LANES, GROUP_LANES).transpose(0, 2, 1, 3, 4)


def _from_block_diag(bd):
    b = bd.shape[0]
    G = HEADS_PER_GROUP
    x = bd.reshape(b, N_GROUPS, 2, G, HEAD_DIM, G, HEAD_DIM)
    x = jnp.stack([x[:, :, :, h, :, h, :] for h in range(G)], axis=3)
    return x.transpose(0, 2, 1, 3, 4, 5).reshape(b, 2, N_HEADS, HEAD_DIM, HEAD_DIM)


def _rows_to_lanes(v):
    b = v.shape[0]
    return v.astype(F32).reshape(b, 2, N_GROUPS, 1, GROUP_LANES).transpose(0, 2, 1, 3, 4)


def _lanes_to_rows(v):
    b = v.shape[0]
    return v.transpose(0, 2, 1, 3, 4).reshape(b, 2, N_HEADS, HEAD_DIM)


def _swap_last(s):
    return jnp.swapaxes(s, -1, -2)


def _in_weight(w_in_l):
    gl = GROUP_LANES
    zeros = jnp.zeros((D_MODEL, gl), w_in_l.dtype)
    cols = []
    for names in _MIXER_COLS:
        for g in range(N_GROUPS):
            for name in names:
                cols.append(zeros if name is None else w_in_l[:, _OFF[name] + g * gl:_OFF[name] + (g + 1) * gl])
    cols.append(w_in_l[:, _OFF["ga"]:_OFF["ga"] + 2 * GLA_RANK])
    cols.append(w_in_l[:, _OFF["mi"]:_OFF["mi"] + 2 * N_HEADS])
    cols.append(w_in_l[:, _OFF["mf"]:_OFF["mf"] + 2 * N_HEADS])
    cols.append(jnp.zeros((D_MODEL, SMALL_LANES - MF_OFF - 2 * N_HEADS), w_in_l.dtype))
    return jnp.concatenate(cols, axis=1).astype(BF16)


def _gla_gate_weight(w_a2_l):
    out = jnp.zeros((2, SMALL_LANES, GROUP_WIDTH), F32)
    for d in range(2):
        out = out.at[d, d * GLA_RANK:(d + 1) * GLA_RANK].set(w_a2_l[d].astype(F32))
    return out


def _layer(x, mod, p, batch, seq, ctx):
    latent = ctx is not None
    x = _ffn(x, mod[:, 0:3], p["norm_w"][0:2], p["w1"][0], p["w3"][0], p["w2"][0], seq)
    u_main, u_small = _inproj(x, mod[:, 3:6], p["norm_w"][2:3], p["w_in"], seq)
    mix_a, st_gla = _gla(u_main, u_small, p["wa_pad"], p["gla_b_a"], p["gla_norm_w"], batch, seq,
                         _swap_last(_to_block_diag(ctx["gla"])) if latent else None)
    if latent:
        mix_b = _na_latent(u_main, p["na_rpb"], ctx["na_k"], ctx["na_v"], batch, seq)
    else:
        mix_b = _na_context(u_main, batch, seq)
    mix_c, st_ml = _mlstm(u_main, u_small, p["mlstm_b_i"], p["mlstm_b_f"], batch, seq,
                          (_to_block_diag(ctx["mC"]), _rows_to_lanes(ctx["mn"]),
                           _rows_to_lanes(jnp.repeat(ctx["mm"][..., None], HEAD_DIM, axis=-1))) if latent else None)
    mix_d, st_ret = _retention(u_main, p["ret_decay"], p["ret_norm_w"], batch, seq,
                               _to_block_diag(ctx["ret"]) if latent else None)
    x = _outproj(x, (mix_a, mix_b, mix_c, mix_d), mod[:, 3:6], p["norm_w"][3:4], p["w_out"], seq)
    x = _ffn(x, mod[:, 6:9], p["norm_w"][4:6], p["w1"][1], p["w3"][1], p["w2"][1], seq)
    if latent:
        return x, None
    gl = GROUP_LANES
    na = u_main.reshape(batch, seq, 4 * N_GROUPS, 4, HEADS_PER_GROUP, HEAD_DIM)[:, :, N_GROUPS:2 * N_GROUPS]
    to_heads = lambda t: t.transpose(0, 2, 3, 1, 4).reshape(batch, N_HEADS, seq, HEAD_DIM)
    new = {
        "na_k": to_heads(na[:, :, :, 1]),
        "na_v": to_heads(na[:, :, :, 2]),
        "gla": _swap_last(_from_block_diag(st_gla)),
        "mC": _from_block_diag(st_ml[0]),
        "mn": _lanes_to_rows(st_ml[1]),
        "mm": _lanes_to_rows(st_ml[2])[..., 0],
        "ret": _from_block_diag(st_ret),
    }
    return x, new


def kernel(x_prompt, x_sample, cache_na_k, cache_na_v, state_gla, state_mlstm_C, state_mlstm_n, state_mlstm_m,
           state_ret, c, c_ctx, w_mod, b_mod, norm_w, ffn_w1, ffn_w3, ffn_w2, w_in, w_out, gla_w_a2, gla_b_a,
           gla_norm_w, na_rpb, mlstm_b_i, mlstm_b_f, ret_decay, ret_norm_w):
    batch, seq, _ = x_prompt.shape
    dec_batch, dec_seq, _ = x_sample.shape

    cvec = jnp.zeros((8, D_MODEL), F32).at[0].set(c_ctx).at[1:1 + dec_batch].set(c)
    mods = _modulation(cvec, w_mod, b_mod).reshape(DEPTH, 8, N_MOD, D_MODEL)

    params = []
    for l in range(DEPTH):
        params.append({
            "norm_w": norm_w[l],
            "w1": ffn_w1[l].astype(BF16), "w3": ffn_w3[l].astype(BF16), "w2": ffn_w2[l].astype(BF16),
            "w_in": _in_weight(w_in[l]), "w_out": w_out[l].astype(BF16),
            "wa_pad": _gla_gate_weight(gla_w_a2[l]), "gla_b_a": gla_b_a[l], "gla_norm_w": gla_norm_w[l],
            "na_rpb": na_rpb[l], "mlstm_b_i": mlstm_b_i[l], "mlstm_b_f": mlstm_b_f[l],
            "ret_decay": ret_decay[l], "ret_norm_w": ret_norm_w[l],
        })

    xp = x_prompt.reshape(batch * seq, D_MODEL)
    states = []
    for l in range(DEPTH):
        xp, st = _layer(xp, mods[l, 0:1], params[l], batch, seq, None)
        states.append(st)

    xs = x_sample.reshape(dec_batch * dec_seq, D_MODEL)
    for l in range(DEPTH):
        ctx = {"na_k": cache_na_k[:, l], "na_v": cache_na_v[:, l], "gla": state_gla[:, l],
               "mC": state_mlstm_C[:, l], "mn": state_mlstm_n[:, l], "mm": state_mlstm_m[:, l],
               "ret": state_ret[:, l]}
        xs, _ = _layer(xs, mods[l, 1:1 + dec_batch], params[l], dec_batch, dec_seq, ctx)

    stack = lambda name: jnp.stack([s[name] for s in states], axis=1)
    return (xp.reshape(batch, seq, D_MODEL), xs.reshape(dec_batch, dec_seq, D_MODEL),
            stack("na_k"), stack("na_v"), stack("gla"), stack("mC"), stack("mn"), stack("mm"), stack("ret"))
```

```python
import functools

import numpy as np
import jax
import jax.numpy as jnp
from jax import lax
from jax.experimental import pallas as pl
from jax.experimental.pallas import tpu as pltpu

D_MODEL = 1024
DEPTH = 2
HEAD_DIM = 64
N_HEADS = 4
GROUP_WIDTH = N_HEADS * HEAD_DIM
N_MOD = 9
GLA_RANK = 16
GLA_TAU = 16.0
CHUNK = 64
GRID_W = 64
NA_KH = 8
NA_KW = 16
ROPE_BASE = 10000.0
EPS = 1e-6
D_FF = 2816

HEADS_PER_GROUP = 2
GROUP_LANES = HEADS_PER_GROUP * HEAD_DIM
N_GROUPS = N_HEADS // HEADS_PER_GROUP
GLA_SUB = 16
PRE_ROWS = 256
CHUNKS_IN_FLIGHT = 4
SMALL_LANES = 128
MI_OFF = 2 * GLA_RANK
MF_OFF = MI_OFF + 2 * N_HEADS
NEG = -1e30
VMEM_LIMIT = 56 * 1024 * 1024

F32 = jnp.float32
BF16 = jnp.bfloat16
HI = lax.Precision.HIGHEST

_OFF = {}
_o = 0
for _name, _size in (("gq", 256), ("gk", 256), ("gv", 256), ("gg", 256), ("ga", 32), ("nq", 256), ("nk", 256),
                     ("nv", 256), ("mq", 256), ("mk", 256), ("mv", 256), ("mo", 256), ("mi", 8), ("mf", 8),
                     ("rq", 256), ("rk", 256), ("rv", 256), ("rg", 256)):
    _OFF[_name] = _o
    _o += _size
N_IN = _o
_MIXER_COLS = (("gq", "gk", "gv", "gg"), ("nq", "nk", "nv", None), ("mq", "mk", "mv", "mo"), ("rq", "rk", "rv", "rg"))
MAIN_COLS = 4 * N_GROUPS * 4 * GROUP_LANES


def _mm(a, b):
    return jnp.dot(a.astype(BF16), b.astype(BF16), preferred_element_type=F32)


def _mm_nt(a, b):
    return lax.dot_general(a.astype(BF16), b.astype(BF16), (((1,), (1,)), ((), ())), preferred_element_type=F32)


def _mm_tn(a, b):
    return lax.dot_general(a.astype(BF16), b.astype(BF16), (((0,), (0,)), ((), ())), preferred_element_type=F32)


def _mm_hi(a, b):
    return jnp.dot(a, b, precision=HI, preferred_element_type=F32)


def _mm_nt_hi(a, b):
    return lax.dot_general(a, b, (((1,), (1,)), ((), ())), precision=HI, preferred_element_type=F32)


def _log_sigmoid(x):
    return jnp.minimum(x, 0.0) - jnp.log1p(jnp.exp(-jnp.abs(x)))


def _silu(x):
    return x * jax.nn.sigmoid(x)


def _iota(shape, dim):
    return lax.broadcasted_iota(jnp.int32, shape, dim)


def _head_of(idx):
    return lax.shift_right_logical(idx, 6)


def _head_masks():
    lane = _iota((1, GROUP_LANES), 1)
    return [(_head_of(lane) == h).astype(F32) for h in range(HEADS_PER_GROUP)]


def _bd_mask():
    n = GROUP_LANES
    return (_head_of(_iota((n, n), 0)) == _head_of(_iota((n, n), 1))).astype(F32)


def _stack(x, masks):
    return jnp.concatenate([x * m for m in masks], axis=0)


def _unstack(r, masks, n):
    out = r[0:n] * masks[0]
    for h in range(1, len(masks)):
        out = out + r[h * n:(h + 1) * n] * masks[h]
    return out


def _rmsnorm(x, w):
    return x * lax.rsqrt(jnp.mean(x * x, axis=-1, keepdims=True) + EPS) * w


def _split_lanes(x, lane_in_head, at, fill, fill_at):
    hi = x.astype(BF16).astype(F32)
    rest = x - hi
    mid = rest.astype(BF16).astype(F32)
    lo = rest - mid
    out = jnp.where(lane_in_head == at, hi, jnp.where(lane_in_head == at + 1, mid,
                                                      jnp.where(lane_in_head == at + 2, lo, 0.0)))
    is_fill = (lane_in_head >= fill_at) & (lane_in_head < fill_at + 3)
    return jnp.where(is_fill, fill, out)


def _mod_kernel(c_ref, w_ref, b_ref, o_ref):
    s = _silu(c_ref[...])
    o_ref[0] = _mm(s, w_ref[0]) + b_ref[0]


def _modulation(cvec, w_mod, b_mod):
    n = N_MOD * D_MODEL
    tn = n // 8
    return pl.pallas_call(
        _mod_kernel,
        grid=(DEPTH, n // tn),
        in_specs=[
            pl.BlockSpec((8, D_MODEL), lambda l, j: (0, 0)),
            pl.BlockSpec((1, D_MODEL, tn), lambda l, j: (l, 0, j)),
            pl.BlockSpec((1, 1, tn), lambda l, j: (l, 0, j)),
        ],
        out_specs=pl.BlockSpec((1, 8, tn), lambda l, j: (l, 0, j)),
        out_shape=jax.ShapeDtypeStruct((DEPTH, 8, n), F32),
        compiler_params=pltpu.CompilerParams(dimension_semantics=("parallel", "parallel"),
                                             vmem_limit_bytes=VMEM_LIMIT),
        name="modulation",
    )(cvec, w_mod, b_mod.reshape(DEPTH, 1, n))


def _ffn_kernel(x_ref, mod_ref, g_ref, w1_ref, w3_ref, w2_ref, o_ref):
    x = x_ref[...]
    mod = mod_ref[0]
    h = _rmsnorm(x, g_ref[0:1]) * (1.0 + mod[1:2]) + mod[0:1]
    hb = h.astype(BF16)
    a = _silu(jnp.dot(hb, w1_ref[...], preferred_element_type=F32)) * jnp.dot(hb, w3_ref[...],
                                                                           preferred_element_type=F32)
    y = jnp.dot(a.astype(BF16), w2_ref[...], preferred_element_type=F32)
    o_ref[...] = x + 0.5 * mod[2:3] * _rmsnorm(y, g_ref[1:2])


def _row_tile(rows_per_batch):
    return min(256, rows_per_batch)


def _mod_index(n_mod, rows_per_batch, tm):
    if n_mod == 1:
        return lambda i: (0, 0, 0)
    per = rows_per_batch // tm
    return lambda i: (i // per, 0, 0)


def _ffn(x, mod3, g2, w1, w3, w2, rows_per_batch):
    rows = x.shape[0]
    tm = _row_tile(rows_per_batch)
    const = lambda i: (0, 0)
    return pl.pallas_call(
        _ffn_kernel,
        grid=(rows // tm,),
        in_specs=[
            pl.BlockSpec((tm, D_MODEL), lambda i: (i, 0)),
            pl.BlockSpec((1, 3, D_MODEL), _mod_index(mod3.shape[0], rows_per_batch, tm)),
            pl.BlockSpec((2, D_MODEL), const),
            pl.BlockSpec((D_MODEL, D_FF), const, pipeline_mode=pl.Buffered(1)),
            pl.BlockSpec((D_MODEL, D_FF), const, pipeline_mode=pl.Buffered(1)),
            pl.BlockSpec((D_FF, D_MODEL), const, pipeline_mode=pl.Buffered(1)),
        ],
        out_specs=pl.BlockSpec((tm, D_MODEL), lambda i: (i, 0)),
        out_shape=jax.ShapeDtypeStruct((rows, D_MODEL), F32),
        compiler_params=pltpu.CompilerParams(dimension_semantics=("parallel",), vmem_limit_bytes=VMEM_LIMIT),
        name="ffn",
    )(x, mod3, g2, w1, w3, w2)


def _inproj_kernel(x_ref, mod_ref, g_ref, w_ref, um_ref, us_ref):
    mod = mod_ref[0]
    h = _rmsnorm(x_ref[...], g_ref[...]) * (1.0 + mod[1:2]) + mod[0:1]
    u = jnp.dot(h.astype(BF16), w_ref[...], preferred_element_type=F32)
    um_ref[...] = u[:, :MAIN_COLS]
    us_ref[...] = u[:, MAIN_COLS:]


def _inproj(x, mod3, g, w_big, rows_per_batch):
    rows = x.shape[0]
    tm = _row_tile(rows_per_batch)
    const = lambda i: (0, 0)
    return pl.pallas_call(
        _inproj_kernel,
        grid=(rows // tm,),
        in_specs=[
            pl.BlockSpec((tm, D_MODEL), lambda i: (i, 0)),
            pl.BlockSpec((1, 3, D_MODEL), _mod_index(mod3.shape[0], rows_per_batch, tm)),
            pl.BlockSpec((1, D_MODEL), const),
            pl.BlockSpec((D_MODEL, MAIN_COLS + SMALL_LANES), const, pipeline_mode=pl.Buffered(1)),
        ],
        out_specs=[pl.BlockSpec((tm, MAIN_COLS), lambda i: (i, 0)),
                   pl.BlockSpec((tm, SMALL_LANES), lambda i: (i, 0))],
        out_shape=[jax.ShapeDtypeStruct((rows, MAIN_COLS), F32),
                   jax.ShapeDtypeStruct((rows, SMALL_LANES), F32)],
        compiler_params=pltpu.CompilerParams(dimension_semantics=("parallel",), vmem_limit_bytes=VMEM_LIMIT),
        name="inproj",
    )(x, mod3, g, w_big)


def _outproj_kernel(x_ref, ma_ref, mb_ref, mc_ref, md_ref, mod_ref, g_ref, w_ref, o_ref):
    mix = jnp.concatenate([ma_ref[...], mb_ref[...], mc_ref[...], md_ref[...]], axis=-1)
    y = jnp.dot(mix.astype(BF16), w_ref[...], preferred_element_type=F32)
    o_ref[...] = x_ref[...] + mod_ref[0][2:3] * _rmsnorm(y, g_ref[...])


def _outproj(x, mixes, mod3, g, w_out, rows_per_batch):
    rows = x.shape[0]
    tm = _row_tile(rows_per_batch)
    const = lambda i: (0, 0)
    mix_spec = pl.BlockSpec((tm, GROUP_WIDTH), lambda i: (i, 0))
    return pl.pallas_call(
        _outproj_kernel,
        grid=(rows // tm,),
        in_specs=[
            pl.BlockSpec((tm, D_MODEL), lambda i: (i, 0)),
            mix_spec, mix_spec, mix_spec, mix_spec,
            pl.BlockSpec((1, 3, D_MODEL), _mod_index(mod3.shape[0], rows_per_batch, tm)),
            pl.BlockSpec((1, D_MODEL), const),
            pl.BlockSpec((D_MODEL, D_MODEL), const, pipeline_mode=pl.Buffered(1)),
        ],
        out_specs=pl.BlockSpec((tm, D_MODEL), lambda i: (i, 0)),
        out_shape=jax.ShapeDtypeStruct((rows, D_MODEL), F32),
        compiler_params=pltpu.CompilerParams(dimension_semantics=("parallel",), vmem_limit_bytes=VMEM_LIMIT),
        name="outproj",
    )(x, *mixes, mod3, g, w_out)


def _mixer_specs(mixer, seq):
    gl = GROUP_LANES
    u_spec = pl.BlockSpec((seq, 4 * gl), lambda b, g: (b, mixer * N_GROUPS + g))
    small_spec = pl.BlockSpec((seq, SMALL_LANES), lambda b, g: (b, 0))
    out_spec = pl.BlockSpec((seq, gl), lambda b, g: (b, g))
    return u_spec, small_spec, out_spec


def _state_spec(rows):
    return pl.BlockSpec((1, 1, 2, rows, GROUP_LANES), lambda b, g: (b, g, 0, 0, 0))


_MIXER_PARAMS = pltpu.CompilerParams(dimension_semantics=("parallel", "parallel"), vmem_limit_bytes=VMEM_LIMIT)


def _chunk_rows(c):
    return pl.ds(pl.multiple_of(c * CHUNK, CHUNK), CHUNK)


def _scan_order(t, n_chunks, reverse):
    return (n_chunks - 1 - t) if reverse else t


def _tile_row(row):
    return jnp.broadcast_to(row, (8, row.shape[1]))


def _edge_row(ref, c, reverse):
    if reverse:
        return ref[pl.ds(pl.multiple_of(c * CHUNK, 8), 8), :][0:1]
    return ref[pl.ds(pl.multiple_of(c * CHUNK + CHUNK - 8, 8), 8), :][7:8]


def _chunk_scan(x, reverse, op):
    n = x.shape[0]
    row = _iota(x.shape, 0) & (CHUNK - 1)
    s = 1
    while s < CHUNK:
        if reverse:
            shifted = pltpu.roll(x, n - s, 0)
            ok = row < CHUNK - s
        else:
            shifted = pltpu.roll(x, s, 0)
            ok = row >= s
        x = jnp.where(ok, op(x, shifted), x)
        s *= 2
    return x


def _row_loop(seq, fn):
    blk = min(seq, PRE_ROWS)

    def body(i, c):
        fn(pl.ds(pl.multiple_of(i * blk, blk), blk))
        return c

    if seq == blk:
        fn(pl.ds(0, blk))
    else:
        lax.fori_loop(0, seq // blk, body, 0)


def _chunk_loop(n_chunks, make_stream):
    k = min(n_chunks, CHUNKS_IN_FLIGHT)

    def body(t, carry):
        _interleave([make_stream(t * k + j) for j in range(k)])
        return carry

    if n_chunks == k:
        body(0, 0)
    else:
        lax.fori_loop(0, n_chunks // k, body, 0)


def _interleave(streams):
    results = [None] * len(streams)
    live = []
    for i, s in enumerate(streams):
        if hasattr(s, "send"):
            live.append(i)
        else:
            results[i] = s
    while live:
        for i in list(live):
            try:
                next(streams[i])
            except StopIteration as done:
                results[i] = done.value
                live.remove(i)
    return results


def _state_scan(n_chunks, reverse, s_ref, decay_ref, init):
    def body(t, s):
        c = _scan_order(t, n_chunks, reverse)
        inc = s_ref[c]
        s_ref[c] = s
        return decay_ref[c][0:1] * s + inc

    return lax.fori_loop(0, n_chunks, body, init)


def _finalize(seq, acc_ref, o_ref, fn):
    def blk(rows):
        o_ref[rows, :] = fn(rows, acc_ref[rows, :] + o_ref[rows, :])

    _row_loop(seq, blk)


def _rope(x, cos, sin_signed):
    lane = _iota(x.shape, 1)
    first = (lane & 31) < 16
    swapped = jnp.where(first, pltpu.roll(x, GROUP_LANES - 16, 1), pltpu.roll(x, 16, 1))
    return x * cos + swapped * sin_signed


def _ret_kernel(dec_ref, nw_ref, u_ref, *rest, seq, rope, has_state, emit_state):
    rest = list(rest)
    cos_ref, sin_ref = (rest.pop(0), rest.pop(0)) if rope else (None, None)
    s0_ref = rest.pop(0) if has_state else None
    o_ref = rest.pop(0)
    st_ref = rest.pop(0) if emit_state else None
    acc_ref, q_ref, k_ref, s_ref, cdec_ref = rest

    gl, L, G = GROUP_LANES, CHUNK, HEADS_PER_GROUP
    n_chunks = seq // L
    g = pl.program_id(1)
    masks = _head_masks()
    bd = _bd_mask()
    bd_avg = bd * (1.0 / HEAD_DIM)
    scale = HEAD_DIM ** -0.5
    ri = _iota((L, gl), 0).astype(F32)
    si = _iota((G * L, L), 0)
    sj = _iota((G * L, L), 1)
    i_idx = si & (L - 1)
    hrow = _head_of(_iota((G * L, 1), 0))

    def prepare(rows):
        q = u_ref[rows, 0:gl]
        k = u_ref[rows, gl:2 * gl]
        if rope:
            cos = cos_ref[rows, :]
            sin = sin_ref[rows, :]
            q = _rope(q, cos, sin)
            k = _rope(k, cos, sin)
        q_ref[rows, :] = q * scale
        k_ref[rows, :] = k

    _row_loop(seq, prepare)

    for d in (0, 1):
        reverse = d == 1
        raw_row = masks[0] * dec_ref[d, g * G]
        raw_col = jnp.where(hrow == 0, dec_ref[d, g * G], 0.0)
        for h in range(1, G):
            raw_row = raw_row + masks[h] * dec_ref[d, g * G + h]
            raw_col = jnp.where(hrow == h, dec_ref[d, g * G + h], raw_col)
        lg_row = _log_sigmoid(raw_row)
        lg_col = _log_sigmoid(raw_col)
        diff = ((sj - i_idx) if reverse else (i_idx - sj)).astype(F32)
        dmat = jnp.where(diff >= 0, jnp.exp(jnp.maximum(diff, 0.0) * lg_col), 0.0)
        qdec = jnp.exp(((L - ri) if reverse else (ri + 1.0)) * lg_row)
        kdec = jnp.exp((ri if reverse else (L - 1.0 - ri)) * lg_row)
        cdec_tile = _tile_row(jnp.exp(float(L) * lg_row))
        out_ref = o_ref if reverse else acc_ref

        def increments(c, kdec=kdec, cdec_tile=cdec_tile):
            rows = _chunk_rows(c)
            inc = _mm_tn(k_ref[rows, :] * kdec, u_ref[rows, 2 * gl:3 * gl])
            yield
            s_ref[c] = bd * inc
            cdec_ref[c] = cdec_tile

        _chunk_loop(n_chunks, increments)
        s_fin = _state_scan(n_chunks, reverse, s_ref, cdec_ref,
                            s0_ref[0, 0, d] if has_state else jnp.zeros((gl, gl), F32))
        if emit_state:
            st_ref[0, 0, d] = s_fin

        def outputs(c, dmat=dmat, qdec=qdec, out_ref=out_ref):
            rows = _chunk_rows(c)
            q = q_ref[rows, :]
            v = u_ref[rows, 2 * gl:3 * gl]
            att = _mm_nt(_stack(q, masks), k_ref[rows, :])
            inter = _mm(q * qdec, s_ref[c])
            yield
            intra = _mm(att * dmat, v)
            yield
            out_ref[rows, :] = _unstack(intra, masks, L) + inter

        _chunk_loop(n_chunks, outputs)

    def finalize(rows, o):
        mu = _mm_hi(o, bd_avg)
        dev = o - mu
        var = _mm_hi(dev * dev, bd_avg)
        return dev * lax.rsqrt(var + EPS) * nw_ref[...] * _silu(u_ref[rows, 3 * gl:4 * gl])

    _finalize(seq, acc_ref, o_ref, finalize)


def _ret_tables(seq):
    t = np.arange(seq)
    quarter = HEAD_DIM // 4
    inv = (ROPE_BASE ** (-np.arange(quarter, dtype=np.float32) / quarter)).astype(np.float32)
    ang_r = (t // GRID_W).astype(np.float32)[:, None] * inv
    ang_c = (t % GRID_W).astype(np.float32)[:, None] * inv
    cos = np.concatenate([np.cos(ang_r), np.cos(ang_r), np.cos(ang_c), np.cos(ang_c)], axis=1)
    sin = np.concatenate([-np.sin(ang_r), np.sin(ang_r), -np.sin(ang_c), np.sin(ang_c)], axis=1)
    tile = lambda a: jnp.asarray(np.tile(a.astype(np.float32), (1, HEADS_PER_GROUP)))
    return tile(cos), tile(sin)


def _retention(u_main, ret_decay, norm_w, batch, seq, state):
    latent = state is not None
    u_spec, _, out_spec = _mixer_specs(3, seq)
    gl = GROUP_LANES
    const2 = lambda b, g: (0, 0)
    in_specs = [pl.BlockSpec(memory_space=pltpu.SMEM), pl.BlockSpec((1, gl), const2), u_spec]
    args = [ret_decay, jnp.tile(norm_w.reshape(1, HEAD_DIM), (1, HEADS_PER_GROUP)), u_main]
    if latent:
        cos, sin = _ret_tables(seq)
        in_specs += [pl.BlockSpec((seq, gl), const2), pl.BlockSpec((seq, gl), const2), _state_spec(gl)]
        args += [cos, sin, state]
    out_shape = [jax.ShapeDtypeStruct((batch * seq, GROUP_WIDTH), F32)]
    out_specs = [out_spec]
    if not latent:
        out_shape.append(jax.ShapeDtypeStruct((batch, N_GROUPS, 2, gl, gl), F32))
        out_specs.append(_state_spec(gl))
    res = pl.pallas_call(
        functools.partial(_ret_kernel, seq=seq, rope=latent, has_state=latent, emit_state=not latent),
        grid=(batch, N_GROUPS),
        in_specs=in_specs,
        out_specs=out_specs,
        out_shape=out_shape,
        scratch_shapes=[pltpu.VMEM((seq, gl), F32), pltpu.VMEM((seq, gl), F32), pltpu.VMEM((seq, gl), F32),
                        pltpu.VMEM((seq // CHUNK, gl, gl), F32), pltpu.VMEM((seq // CHUNK, 8, gl), F32)],
        compiler_params=_MIXER_PARAMS,
        name="retention",
    )(*args)
    return res[0], (None if latent else res[1])


def _gla_kernel(wa_ref, ba_ref, nw_ref, u_ref, us_ref, *rest, seq, has_state, emit_state):
    rest = list(rest)
    s0_ref = rest.pop(0) if has_state else None
    o_ref = rest.pop(0)
    st_ref = rest.pop(0) if emit_state else None
    acc_ref, b_ref, s_ref, dec_ref = rest

    gl, L, G, SB = GROUP_LANES, CHUNK, HEADS_PER_GROUP, GLA_SUB
    n_chunks = seq // L
    n_sub = L // SB
    masks = _head_masks()
    bd = _bd_mask()
    bd_avg = bd * (1.0 / HEAD_DIM)
    scale = HEAD_DIM ** -0.5
    pair_row = _iota((SB * SB, gl), 0)
    pi = lax.shift_right_logical(pair_row, 4)
    pj = pair_row & (SB - 1)

    for d in (0, 1):
        reverse = d == 1
        pair_ok = ((pj >= pi) if reverse else (pj <= pi)).astype(F32)
        out_ref = o_ref if reverse else acc_ref

        def log_decay(rows, d=d, reverse=reverse):
            a_pre = _mm(us_ref[rows, :], wa_ref[d]) + ba_ref[d]
            la = _log_sigmoid(a_pre) * (1.0 / GLA_TAU)
            b_ref[rows, :] = _chunk_scan(la, reverse, jnp.add)

        _row_loop(seq, log_decay)

        def increments(c, reverse=reverse):
            rows = _chunk_rows(c)
            b = b_ref[rows, :]
            btot = b[0:1] if reverse else b[L - 1:L]
            inc = _mm_tn(u_ref[rows, 2 * gl:3 * gl], u_ref[rows, gl:2 * gl] * jnp.exp(btot - b))
            yield
            s_ref[c] = bd * inc
            dec_ref[c] = _tile_row(jnp.exp(btot))

        _chunk_loop(n_chunks, increments)
        s_fin = _state_scan(n_chunks, reverse, s_ref, dec_ref,
                            s0_ref[0, 0, d] if has_state else jnp.zeros((gl, gl), F32))
        if emit_state:
            st_ref[0, 0, d] = s_fin

        def outputs(c, reverse=reverse, pair_ok=pair_ok, out_ref=out_ref):
            rows = _chunk_rows(c)
            b = b_ref[rows, :]
            q = u_ref[rows, 0:gl] * scale
            k = u_ref[rows, gl:2 * gl]
            v = u_ref[rows, 2 * gl:3 * gl]
            o_inter = _mm_nt(q * jnp.exp(b), s_ref[c])
            diag, off = [], []
            for i_blk in range(n_sub):
                lo = i_blk * SB
                qi, ki, bi = q[lo:lo + SB], k[lo:lo + SB], b[lo:lo + SB]
                prods = [qi[i:i + 1] * ki * jnp.exp(jnp.minimum(bi[i:i + 1] - bi, 0.0)) for i in range(SB)]
                diag.append(_mm(jnp.concatenate(prods, axis=0), bd))
                if reverse and i_blk < n_sub - 1:
                    ref, other = bi[SB - 1:SB], slice(lo + SB, L)
                elif (not reverse) and i_blk > 0:
                    ref, other = bi[0:1], slice(0, lo)
                else:
                    off.append(None)
                    continue
                qd = qi * jnp.exp(bi - ref)
                kd = k[other] * jnp.exp(ref - b[other])
                off.append((_mm_nt(_stack(qd, masks), kd), other))
            yield
            off = [None if a is None else _mm(a[0], v[a[1]]) for a in off]
            yield
            blocks = []
            for i_blk in range(n_sub):
                vi = v[i_blk * SB:(i_blk + 1) * SB]
                weighted = diag[i_blk] * jnp.concatenate([vi] * SB, axis=0) * pair_ok
                o_blk = jnp.sum(weighted.reshape(SB, SB, gl), axis=1)
                if off[i_blk] is not None:
                    o_blk = o_blk + _unstack(off[i_blk], masks, SB)
                blocks.append(o_blk)
            out_ref[rows, :] = o_inter + jnp.concatenate(blocks, axis=0)

        _chunk_loop(n_chunks, outputs)

    def finalize(rows, o):
        ms = _mm_hi(o * o, bd_avg)
        return o * lax.rsqrt(ms + EPS) * nw_ref[...] * _silu(u_ref[rows, 3 * gl:4 * gl])

    _finalize(seq, acc_ref, o_ref, finalize)


def _gla(u_main, u_small, wa_pad, b_a, norm_w, batch, seq, state):
    latent = state is not None
    u_spec, small_spec, out_spec = _mixer_specs(0, seq)
    gl = GROUP_LANES
    in_specs = [
        pl.BlockSpec((2, SMALL_LANES, gl), lambda b, g: (0, 0, g)),
        pl.BlockSpec((2, 1, gl), lambda b, g: (0, 0, g)),
        pl.BlockSpec((1, gl), lambda b, g: (0, 0)),
        u_spec, small_spec,
    ]
    args = [wa_pad, b_a.reshape(2, 1, GROUP_WIDTH),
            jnp.tile(norm_w.reshape(1, HEAD_DIM), (1, HEADS_PER_GROUP)), u_main, u_small]
    if latent:
        in_specs.append(_state_spec(gl))
        args.append(state)
    out_shape = [jax.ShapeDtypeStruct((batch * seq, GROUP_WIDTH), F32)]
    out_specs = [out_spec]
    if not latent:
        out_shape.append(jax.ShapeDtypeStruct((batch, N_GROUPS, 2, gl, gl), F32))
        out_specs.append(_state_spec(gl))
    res = pl.pallas_call(
        functools.partial(_gla_kernel, seq=seq, has_state=latent, emit_state=not latent),
        grid=(batch, N_GROUPS),
        in_specs=in_specs,
        out_specs=out_specs,
        out_shape=out_shape,
        scratch_shapes=[pltpu.VMEM((seq, gl), F32), pltpu.VMEM((seq, gl), F32),
                        pltpu.VMEM((seq // CHUNK, gl, gl), F32), pltpu.VMEM((seq // CHUNK, 8, gl), F32)],
        compiler_params=_MIXER_PARAMS,
        name="gla",
    )(*args)
    return res[0], (None if latent else res[1])


def _mlstm_kernel(ei_ref, ef_ref, bi_ref, bf_ref, u_ref, us_ref, *rest, seq, has_state, emit_state):
    rest = list(rest)
    if has_state:
        c0_ref, n0_ref, m0_ref = rest.pop(0), rest.pop(0), rest.pop(0)
    o_ref = rest.pop(0)
    if emit_state:
        cst_ref, nst_ref, mst_ref = rest.pop(0), rest.pop(0), rest.pop(0)
    acc_ref, f_ref, x_ref, p_ref, s_ref, nu_ref, wp_ref, min_ref, mout_ref = rest

    gl, L, G = GROUP_LANES, CHUNK, HEADS_PER_GROUP
    n_chunks = seq // L
    masks = _head_masks()
    bd = _bd_mask()
    scale = HEAD_DIM ** -0.5
    si = _iota((G * L, L), 0) & (L - 1)
    sj = _iota((G * L, L), 1)
    lane_in_head = _iota((L, gl), 1) & (HEAD_DIM - 1)

    for d in (0, 1):
        reverse = d == 1
        causal = (sj >= si) if reverse else (sj <= si)
        last = 0 if reverse else L - 1
        out_ref = o_ref if reverse else acc_ref

        def gates(rows, d=d, reverse=reverse):
            us = us_ref[rows, :]
            ig = _mm_hi(us, ei_ref[d, 0]) + bi_ref[d, 0]
            fg = _log_sigmoid(_mm_hi(us, ef_ref[d, 0]) + bf_ref[d, 0])
            f_sum = _chunk_scan(fg, reverse, jnp.add)
            x = ig - f_sum
            f_ref[rows, :] = f_sum
            x_ref[rows, :] = x
            p_ref[rows, :] = _chunk_scan(x, reverse, jnp.maximum)

        _row_loop(seq, gates)

        def stabiliser(t, m_row, reverse=reverse):
            c = _scan_order(t, n_chunks, reverse)
            m_new = _edge_row(f_ref, c, reverse) + jnp.maximum(m_row, _edge_row(p_ref, c, reverse))
            min_ref[c] = _tile_row(m_row)
            mout_ref[c] = _tile_row(m_new)
            return m_new

        m_fin = lax.fori_loop(0, n_chunks, stabiliser,
                              m0_ref[0, 0, d] if has_state else jnp.zeros((1, gl), F32))

        def increments(c, last=last):
            rows = _chunk_rows(c)
            f_tot = f_ref[rows, :][last:last + 1]
            m_in = min_ref[c][0:1]
            m_out = mout_ref[c][0:1]
            kw = u_ref[rows, gl:2 * gl] * jnp.exp(f_tot + x_ref[rows, :] - m_out)
            inc = _mm_tn(kw, u_ref[rows, 2 * gl:3 * gl])
            yield
            s_ref[c] = bd * inc
            nu_ref[c] = _tile_row(jnp.sum(kw, axis=0, keepdims=True))
            wp_ref[c] = _tile_row(jnp.exp(f_tot + m_in - m_out))

        _chunk_loop(n_chunks, increments)

        def memory(t, carry, reverse=reverse):
            c_mat, n_row = carry
            c = _scan_order(t, n_chunks, reverse)
            inc = s_ref[c]
            nu = nu_ref[c][0:1]
            wp = wp_ref[c][0:1]
            s_ref[c] = c_mat
            nu_ref[c] = _tile_row(n_row)
            return wp * c_mat + inc, wp * n_row + nu

        if has_state:
            init = (c0_ref[0, 0, d], n0_ref[0, 0, d])
        else:
            init = (jnp.zeros((gl, gl), F32), jnp.zeros((1, gl), F32))
        c_fin, n_fin = lax.fori_loop(0, n_chunks, memory, init)
        if emit_state:
            cst_ref[0, 0, d], nst_ref[0, 0, d], mst_ref[0, 0, d] = c_fin, n_fin, m_fin

        def outputs(c, causal=causal, out_ref=out_ref):
            rows = _chunk_rows(c)
            x = x_ref[rows, :]
            m_in = min_ref[c][0:1]
            q = u_ref[rows, 0:gl] * scale
            k = u_ref[rows, gl:2 * gl]
            v = u_ref[rows, 2 * gl:3 * gl]
            m_full = jnp.maximum(m_in, p_ref[rows, :])
            lhs = _split_lanes(-m_full, lane_in_head, 0, 1.0, 3)
            rhs = _split_lanes(x, lane_in_head, 3, 1.0, 0)
            logw = _mm_nt(_stack(lhs, masks), rhs)
            qk = _mm_nt(_stack(q, masks), k)
            inter = _mm(q, s_ref[c])
            q_n = _mm(q * nu_ref[c][0:1], bd)
            yield
            s = qk * jnp.exp(jnp.where(causal, logw, NEG))
            s_hi = s.astype(BF16)
            ones = jnp.ones((L, gl), BF16)
            r = jnp.dot(s_hi, jnp.concatenate([v.astype(BF16), ones], axis=1), preferred_element_type=F32)
            sum_lo = jnp.dot((s - s_hi.astype(F32)).astype(BF16), ones, preferred_element_type=F32)
            yield
            w_inter = jnp.exp(m_in - m_full)
            num = _unstack(r[:, :gl], masks, L) + w_inter * inter
            den = _unstack(r[:, gl:] + sum_lo, masks, L) + w_inter * q_n
            out_ref[rows, :] = num / jnp.maximum(jnp.abs(den), jnp.exp(-(f_ref[rows, :] + m_full)))

        _chunk_loop(n_chunks, outputs)

    def finalize(rows, o):
        return o * jax.nn.sigmoid(u_ref[rows, 3 * gl:4 * gl])

    _finalize(seq, acc_ref, o_ref, finalize)


def _gate_expanders(offset):
    e = np.zeros((2, N_GROUPS, SMALL_LANES, GROUP_LANES), np.float32)
    for d in range(2):
        for g in range(N_GROUPS):
            for h in range(HEADS_PER_GROUP):
                src = offset + d * N_HEADS + g * HEADS_PER_GROUP + h
                e[d, g, src, h * HEAD_DIM:(h + 1) * HEAD_DIM] = 1.0
    return jnp.asarray(e)


def _gate_bias(b):
    return jnp.repeat(b.astype(F32), HEAD_DIM, axis=1).reshape(2, N_GROUPS, 1, GROUP_LANES)


def _mlstm(u_main, u_small, b_i, b_f, batch, seq, state):
    latent = state is not None
    u_spec, small_spec, out_spec = _mixer_specs(2, seq)
    gl = GROUP_LANES
    exp_spec = pl.BlockSpec((2, 1, SMALL_LANES, gl), lambda b, g: (0, g, 0, 0))
    bias_spec = pl.BlockSpec((2, 1, 1, gl), lambda b, g: (0, g, 0, 0))
    in_specs = [exp_spec, exp_spec, bias_spec, bias_spec, u_spec, small_spec]
    args = [_gate_expanders(MI_OFF), _gate_expanders(MF_OFF), _gate_bias(b_i), _gate_bias(b_f), u_main, u_small]
    if latent:
        in_specs += [_state_spec(gl), _state_spec(1), _state_spec(1)]
        args += list(state)
    out_shape = [jax.ShapeDtypeStruct((batch * seq, GROUP_WIDTH), F32)]
    out_specs = [out_spec]
    if not latent:
        out_shape += [jax.ShapeDtypeStruct((batch, N_GROUPS, 2, gl, gl), F32),
                      jax.ShapeDtypeStruct((batch, N_GROUPS, 2, 1, gl), F32),
                      jax.ShapeDtypeStruct((batch, N_GROUPS, 2, 1, gl), F32)]
        out_specs += [_state_spec(gl), _state_spec(1), _state_spec(1)]
    res = pl.pallas_call(
        functools.partial(_mlstm_kernel, seq=seq, has_state=latent, emit_state=not latent),
        grid=(batch, N_GROUPS),
        in_specs=in_specs,
        out_specs=out_specs,
        out_shape=out_shape,
        scratch_shapes=[pltpu.VMEM((seq, gl), F32)] * 4 + [pltpu.VMEM((seq // CHUNK, gl, gl), F32)]
        + [pltpu.VMEM((seq // CHUNK, 8, gl), F32)] * 4,
        compiler_params=_MIXER_PARAMS,
        name="mlstm",
    )(*args)
    return res[0], (None if latent else tuple(res[1:]))


def _na_ctx_kernel(u_ref, o_ref, *, seq):
    gl = GROUP_LANES
    masks = _head_masks()
    q = u_ref[:, 0:gl] * (HEAD_DIM ** -0.5)
    k = u_ref[:, gl:2 * gl]
    v = u_ref[:, 2 * gl:3 * gl]
    s = _mm_nt(_stack(q, masks), k)
    p = jnp.exp(s - jnp.max(s, axis=1, keepdims=True))
    o = _mm(p, v) / jnp.sum(p, axis=1, keepdims=True)
    o_ref[...] = _unstack(o, masks, seq)


def _na_context(u_main, batch, seq):
    u_spec, _, out_spec = _mixer_specs(1, seq)
    return pl.pallas_call(
        functools.partial(_na_ctx_kernel, seq=seq),
        grid=(batch, N_GROUPS),
        in_specs=[u_spec],
        out_specs=out_spec,
        out_shape=jax.ShapeDtypeStruct((batch * seq, GROUP_WIDTH), F32),
        compiler_params=_MIXER_PARAMS,
        name="na_context",
    )(u_main)


def _na_lat_kernel(tab_ref, kc_ref, vc_ref, u_ref, o_ref, *, seq):
    gl = GROUP_LANES
    grid_rows = seq // GRID_W
    win = NA_KH * GRID_W
    masks = _head_masks()
    scale = HEAD_DIM ** -0.5
    kc = kc_ref[0, 0]
    vc = vc_ref[0, 0]

    def body(r, carry):
        ks = jnp.clip(r - NA_KH // 2, 0, grid_rows - NA_KH)
        q_rows = pl.ds(pl.multiple_of(r * GRID_W, GRID_W), GRID_W)
        k_rows = pl.ds(pl.multiple_of(ks * GRID_W, GRID_W), win)
        qs = _stack(u_ref[q_rows, 0:gl] * scale, masks)
        s_loc = _mm_nt(qs, u_ref[k_rows, gl:2 * gl]) + tab_ref[0, ks - r + NA_KH - 1]
        s_ctx = _mm_nt(qs, kc)
        m = jnp.maximum(jnp.max(s_loc, axis=1, keepdims=True), jnp.max(s_ctx, axis=1, keepdims=True))
        p_loc = jnp.exp(s_loc - m)
        p_ctx = jnp.exp(s_ctx - m)
        den = jnp.sum(p_loc, axis=1, keepdims=True) + jnp.sum(p_ctx, axis=1, keepdims=True)
        o = (_mm(p_loc, u_ref[k_rows, 2 * gl:3 * gl]) + _mm(p_ctx, vc)) / den
        o_ref[q_rows, :] = _unstack(o, masks, GRID_W)
        return carry

    lax.fori_loop(0, grid_rows, body, 0)


def _na_bias_table(rpb):
    c = np.arange(GRID_W)
    rel = c[None, :] - c[:, None]
    cs = np.clip(c - NA_KW // 2, 0, GRID_W - NA_KW)
    valid = (c[None, :] >= cs[:, None]) & (c[None, :] < cs[:, None] + NA_KW)
    dcol = np.clip(rel, -(NA_KW - 1), NA_KW - 1) + NA_KW - 1
    bmat = jnp.where(valid[None, None], rpb.astype(F32)[:, :, dcol], NEG)
    tabs = jnp.stack([bmat[:, s:s + NA_KH] for s in range(NA_KH)], axis=1)
    tabs = tabs.transpose(0, 1, 3, 2, 4).reshape(N_GROUPS, HEADS_PER_GROUP, NA_KH, GRID_W, NA_KH * GRID_W)
    return tabs.transpose(0, 2, 1, 3, 4).reshape(N_GROUPS, NA_KH, HEADS_PER_GROUP * GRID_W, NA_KH * GRID_W)


def _heads_to_lanes(t):
    b, _, s, _ = t.shape
    t = t.reshape(b, N_GROUPS, HEADS_PER_GROUP, s, HEAD_DIM).transpose(0, 1, 3, 2, 4)
    return t.reshape(b, N_GROUPS, s, GROUP_LANES)


def _na_latent(u_main, rpb, k_ctx, v_ctx, batch, seq):
    u_spec, _, out_spec = _mixer_specs(1, seq)
    gl = GROUP_LANES
    past = k_ctx.shape[2]
    tab = _na_bias_table(rpb)
    ctx_spec = pl.BlockSpec((1, 1, past, gl), lambda b, g: (b, g, 0, 0))
    return pl.pallas_call(
        functools.partial(_na_lat_kernel, seq=seq),
        grid=(batch, N_GROUPS),
        in_specs=[pl.BlockSpec((1,) + tab.shape[1:], lambda b, g: (g, 0, 0, 0)), ctx_spec, ctx_spec, u_spec],
        out_specs=out_spec,
        out_shape=jax.ShapeDtypeStruct((batch * seq, GROUP_WIDTH), F32),
        compiler_params=_MIXER_PARAMS,
        name="na_latent",
    )(tab, _heads_to_lanes(k_ctx.astype(F32)), _heads_to_lanes(v_ctx.astype(F32)), u_main)


def _to_block_diag(s):
    b = s.shape[0]
    G = HEADS_PER_GROUP
    s = s.astype(F32).reshape(b, 2, N_GROUPS, G, HEAD_DIM, 1, HEAD_DIM)
    eye = jnp.eye(G, dtype=F32).reshape(1, 1, 1, G, 1, G, 1)
    return (s * eye).reshape(b, 2, N_GROUPS, GROUP_LANES, GROUP_LANES).transpose(0, 2, 1, 3, 4)


def _from_block_diag(bd):
    b = bd.shape[0]
    G = HEADS_PER_GROUP
    x = bd.reshape(b, N_GROUPS, 2, G, HEAD_DIM, G, HEAD_DIM)
    x = jnp.stack([x[:, :, :, h, :, h, :] for h in range(G)], axis=3)
    return x.transpose(0, 2, 1, 3, 4, 5).reshape(b, 2, N_HEADS, HEAD_DIM, HEAD_DIM)


def _rows_to_lanes(v):
    b = v.shape[0]
    return v.astype(F32).reshape(b, 2, N_GROUPS, 1, GROUP_LANES).transpose(0, 2, 1, 3, 4)


def _lanes_to_rows(v):
    b = v.shape[0]
    return v.transpose(0, 2, 1, 3, 4).reshape(b, 2, N_HEADS, HEAD_DIM)


def _swap_last(s):
    return jnp.swapaxes(s, -1, -2)


def _in_weight(w_in_l):
    gl = GROUP_LANES
    zeros = jnp.zeros((D_MODEL, gl), w_in_l.dtype)
    cols = []
    for names in _MIXER_COLS:
        for g in range(N_GROUPS):
            for name in names:
                cols.append(zeros if name is None else w_in_l[:, _OFF[name] + g * gl:_OFF[name] + (g + 1) * gl])
    cols.append(w_in_l[:, _OFF["ga"]:_OFF["ga"] + 2 * GLA_RANK])
    cols.append(w_in_l[:, _OFF["mi"]:_OFF["mi"] + 2 * N_HEADS])
    cols.append(w_in_l[:, _OFF["mf"]:_OFF["mf"] + 2 * N_HEADS])
    cols.append(jnp.zeros((D_MODEL, SMALL_LANES - MF_OFF - 2 * N_HEADS), w_in_l.dtype))
    return jnp.concatenate(cols, axis=1).astype(BF16)


def _gla_gate_weight(w_a2_l):
    out = jnp.zeros((2, SMALL_LANES, GROUP_WIDTH), F32)
    for d in range(2):
        out = out.at[d, d * GLA_RANK:(d + 1) * GLA_RANK].set(w_a2_l[d].astype(F32))
    return out


def _layer(x, mod, p, batch, seq, ctx):
    latent = ctx is not None
    x = _ffn(x, mod[:, 0:3], p["norm_w"][0:2], p["w1"][0], p["w3"][0], p["w2"][0], seq)
    u_main, u_small = _inproj(x, mod[:, 3:6], p["norm_w"][2:3], p["w_in"], seq)
    mix_a, st_gla = _gla(u_main, u_small, p["wa_pad"], p["gla_b_a"], p["gla_norm_w"], batch, seq,
                         _swap_last(_to_block_diag(ctx["gla"])) if latent else None)
    if latent:
        mix_b = _na_latent(u_main, p["na_rpb"], ctx["na_k"], ctx["na_v"], batch, seq)
    else:
        mix_b = _na_context(u_main, batch, seq)
    mix_c, st_ml = _mlstm(u_main, u_small, p["mlstm_b_i"], p["mlstm_b_f"], batch, seq,
                          (_to_block_diag(ctx["mC"]), _rows_to_lanes(ctx["mn"]),
                           _rows_to_lanes(jnp.repeat(ctx["mm"][..., None], HEAD_DIM, axis=-1))) if latent else None)
    mix_d, st_ret = _retention(u_main, p["ret_decay"], p["ret_norm_w"], batch, seq,
                               _to_block_diag(ctx["ret"]) if latent else None)
    x = _outproj(x, (mix_a, mix_b, mix_c, mix_d), mod[:, 3:6], p["norm_w"][3:4], p["w_out"], seq)
    x = _ffn(x, mod[:, 6:9], p["norm_w"][4:6], p["w1"][1], p["w3"][1], p["w2"][1], seq)
    if latent:
        return x, None
    gl = GROUP_LANES
    na = u_main.reshape(batch, seq, 4 * N_GROUPS, 4, HEADS_PER_GROUP, HEAD_DIM)[:, :, N_GROUPS:2 * N_GROUPS]
    to_heads = lambda t: t.transpose(0, 2, 3, 1, 4).reshape(batch, N_HEADS, seq, HEAD_DIM)
    new = {
        "na_k": to_heads(na[:, :, :, 1]),
        "na_v": to_heads(na[:, :, :, 2]),
        "gla": _swap_last(_from_block_diag(st_gla)),
        "mC": _from_block_diag(st_ml[0]),
        "mn": _lanes_to_rows(st_ml[1]),
        "mm": _lanes_to_rows(st_ml[2])[..., 0],
        "ret": _from_block_diag(st_ret),
    }
    return x, new


def kernel(x_prompt, x_sample, cache_na_k, cache_na_v, state_gla, state_mlstm_C, state_mlstm_n, state_mlstm_m,
           state_ret, c, c_ctx, w_mod, b_mod, norm_w, ffn_w1, ffn_w3, ffn_w2, w_in, w_out, gla_w_a2, gla_b_a,
           gla_norm_w, na_rpb, mlstm_b_i, mlstm_b_f, ret_decay, ret_norm_w):
    batch, seq, _ = x_prompt.shape
    dec_batch, dec_seq, _ = x_sample.shape

    cvec = jnp.zeros((8, D_MODEL), F32).at[0].set(c_ctx).at[1:1 + dec_batch].set(c)
    mods = _modulation(cvec, w_mod, b_mod).reshape(DEPTH, 8, N_MOD, D_MODEL)

    params = []
    for l in range(DEPTH):
        params.append({
            "norm_w": norm_w[l],
            "w1": ffn_w1[l].astype(BF16), "w3": ffn_w3[l].astype(BF16), "w2": ffn_w2[l].astype(BF16),
            "w_in": _in_weight(w_in[l]), "w_out": w_out[l].astype(BF16),
            "wa_pad": _gla_gate_weight(gla_w_a2[l]), "gla_b_a": gla_b_a[l], "gla_norm_w": gla_norm_w[l],
            "na_rpb": na_rpb[l], "mlstm_b_i": mlstm_b_i[l], "mlstm_b_f": mlstm_b_f[l],
            "ret_decay": ret_decay[l], "ret_norm_w": ret_norm_w[l],
        })

    xp = x_prompt.reshape(batch * seq, D_MODEL)
    states = []
    for l in range(DEPTH):
        xp, st = _layer(xp, mods[l, 0:1], params[l], batch, seq, None)
        states.append(st)

    xs = x_sample.reshape(dec_batch * dec_seq, D_MODEL)
    for l in range(DEPTH):
        ctx = {"na_k": cache_na_k[:, l], "na_v": cache_na_v[:, l], "gla": state_gla[:, l],
               "mC": state_mlstm_C[:, l], "mn": state_mlstm_n[:, l], "mm": state_mlstm_m[:, l],
               "ret": state_ret[:, l]}
        xs, _ = _layer(xs, mods[l, 1:1 + dec_batch], params[l], dec_batch, dec_seq, ctx)

    stack = lambda name: jnp.stack([s[name] for s in states], axis=1)
    return (xp.reshape(batch, seq, D_MODEL), xs.reshape(dec_batch, dec_seq, D_MODEL),
            stack("na_k"), stack("na_v"), stack("gla"), stack("mC"), stack("mn"), stack("mm"), stack("ret"))
```

```python
import functools

import numpy as np
import jax
import jax.numpy as jnp
from jax import lax
from jax.experimental import pallas as pl
from jax.experimental.pallas import tpu as pltpu

D_MODEL = 1024
DEPTH = 2
HEAD_DIM = 64
N_HEADS = 4
GROUP_WIDTH = N_HEADS * HEAD_DIM
N_MOD = 9
GLA_RANK = 16
GLA_TAU = 16.0
CHUNK = 64
GRID_W = 64
NA_KH = 8
NA_KW = 16
ROPE_BASE = 10000.0
EPS = 1e-6
D_FF = 2816

HEADS_PER_GROUP = 2
GROUP_LANES = HEADS_PER_GROUP * HEAD_DIM
N_GROUPS = N_HEADS // HEADS_PER_GROUP
GLA_SUB = 16
GLA_SAFE_LOG2 = 64.0
PRE_ROWS = 128
ROWS_IN_FLIGHT = 2
LOG2E = 1.4426950408889634
CHUNKS_IN_FLIGHT = 4
SMALL_LANES = 128
MI_OFF = 2 * GLA_RANK
MF_OFF = MI_OFF + 2 * N_HEADS
NEG = -1e30
VMEM_LIMIT = 56 * 1024 * 1024

F32 = jnp.float32
BF16 = jnp.bfloat16
HI = lax.Precision.HIGHEST

_OFF = {}
_o = 0
for _name, _size in (("gq", 256), ("gk", 256), ("gv", 256), ("gg", 256), ("ga", 32), ("nq", 256), ("nk", 256),
                     ("nv", 256), ("mq", 256), ("mk", 256), ("mv", 256), ("mo", 256), ("mi", 8), ("mf", 8),
                     ("rq", 256), ("rk", 256), ("rv", 256), ("rg", 256)):
    _OFF[_name] = _o
    _o += _size
N_IN = _o
_MIXER_COLS = (("gq", "gk", "gv", "gg"), ("nq", "nk", "nv", None), ("mq", "mk", "mv", "mo"), ("rq", "rk", "rv", "rg"))
MAIN_COLS = 4 * N_GROUPS * 4 * GROUP_LANES


def _mm(a, b):
    return jnp.dot(a.astype(BF16), b.astype(BF16), preferred_element_type=F32)


def _mm_nt(a, b):
    return lax.dot_general(a.astype(BF16), b.astype(BF16), (((1,), (1,)), ((), ())), preferred_element_type=F32)


def _mm_tn(a, b):
    return lax.dot_general(a.astype(BF16), b.astype(BF16), (((0,), (0,)), ((), ())), preferred_element_type=F32)


def _mm_split(x, w):
    hi = x.astype(BF16)
    rest = x - hi.astype(F32)
    mid = rest.astype(BF16)
    lo = (rest - mid.astype(F32)).astype(BF16)
    wb = w.astype(BF16)
    return jnp.dot(jnp.concatenate([hi, mid, lo], axis=1), jnp.concatenate([wb, wb, wb], axis=0),
                   preferred_element_type=F32)


def _log_sigmoid(x):
    return jnp.minimum(x, 0.0) - jnp.log1p(jnp.exp(-jnp.abs(x)))


def _silu(x):
    return x * jax.nn.sigmoid(x)


def _iota(shape, dim):
    return lax.broadcasted_iota(jnp.int32, shape, dim)


def _head_of(idx):
    return lax.shift_right_logical(idx, 6)


def _head_masks():
    lane = _iota((1, GROUP_LANES), 1)
    return [(_head_of(lane) == h).astype(F32) for h in range(HEADS_PER_GROUP)]


def _bd_mask():
    n = GROUP_LANES
    return (_head_of(_iota((n, n), 0)) == _head_of(_iota((n, n), 1))).astype(F32)


def _stack(x, masks):
    return jnp.concatenate([x * m for m in masks], axis=0)


def _unstack(r, masks, n):
    out = r[0:n] * masks[0]
    for h in range(1, len(masks)):
        out = out + r[h * n:(h + 1) * n] * masks[h]
    return out


def _rmsnorm(x, w):
    return x * lax.rsqrt(jnp.mean(x * x, axis=-1, keepdims=True) + EPS) * w


def _split_lanes(x, lane_in_head, at, fill, fill_at):
    hi = x.astype(BF16).astype(F32)
    rest = x - hi
    mid = rest.astype(BF16).astype(F32)
    lo = rest - mid
    out = jnp.where(lane_in_head == at, hi, jnp.where(lane_in_head == at + 1, mid,
                                                      jnp.where(lane_in_head == at + 2, lo, 0.0)))
    is_fill = (lane_in_head >= fill_at) & (lane_in_head < fill_at + 3)
    return jnp.where(is_fill, fill, out)


def _mod_kernel(c_ref, w_ref, b_ref, o_ref):
    s = _silu(c_ref[...])
    o_ref[0] = _mm(s, w_ref[0]) + b_ref[0]


def _modulation(cvec, w_mod, b_mod):
    n = N_MOD * D_MODEL
    tn = n // 8
    return pl.pallas_call(
        _mod_kernel,
        grid=(DEPTH, n // tn),
        in_specs=[
            pl.BlockSpec((8, D_MODEL), lambda l, j: (0, 0)),
            pl.BlockSpec((1, D_MODEL, tn), lambda l, j: (l, 0, j)),
            pl.BlockSpec((1, 1, tn), lambda l, j: (l, 0, j)),
        ],
        out_specs=pl.BlockSpec((1, 8, tn), lambda l, j: (l, 0, j)),
        out_shape=jax.ShapeDtypeStruct((DEPTH, 8, n), F32),
        compiler_params=pltpu.CompilerParams(dimension_semantics=("parallel", "parallel"),
                                             vmem_limit_bytes=VMEM_LIMIT),
        name="modulation",
    )(cvec, w_mod, b_mod.reshape(DEPTH, 1, n))


def _ffn_kernel(x_ref, mod_ref, g_ref, w1_ref, w3_ref, w2_ref, o_ref):
    x = x_ref[...]
    mod = mod_ref[0]
    h = _rmsnorm(x, g_ref[0:1]) * (1.0 + mod[1:2]) + mod[0:1]
    hb = h.astype(BF16)
    a = _silu(jnp.dot(hb, w1_ref[...], preferred_element_type=F32)) * jnp.dot(hb, w3_ref[...],
                                                                           preferred_element_type=F32)
    y = jnp.dot(a.astype(BF16), w2_ref[...], preferred_element_type=F32)
    o_ref[...] = x + 0.5 * mod[2:3] * _rmsnorm(y, g_ref[1:2])


def _row_tile(rows_per_batch):
    return min(256, rows_per_batch)


def _mod_index(n_mod, rows_per_batch, tm):
    if n_mod == 1:
        return lambda i: (0, 0, 0)
    per = rows_per_batch // tm
    return lambda i: (i // per, 0, 0)


def _ffn(x, mod3, g2, w1, w3, w2, which, rows_per_batch):
    rows = x.shape[0]
    tm = _row_tile(rows_per_batch)
    const = lambda i: (0, 0)
    pick = lambda i: which + (0, 0)
    return pl.pallas_call(
        _ffn_kernel,
        grid=(rows // tm,),
        in_specs=[
            pl.BlockSpec((tm, D_MODEL), lambda i: (i, 0)),
            pl.BlockSpec((1, 3, D_MODEL), _mod_index(mod3.shape[0], rows_per_batch, tm)),
            pl.BlockSpec((2, D_MODEL), const),
            pl.BlockSpec((None, None, D_MODEL, D_FF), pick, pipeline_mode=pl.Buffered(1)),
            pl.BlockSpec((None, None, D_MODEL, D_FF), pick, pipeline_mode=pl.Buffered(1)),
            pl.BlockSpec((None, None, D_FF, D_MODEL), pick, pipeline_mode=pl.Buffered(1)),
        ],
        out_specs=pl.BlockSpec((tm, D_MODEL), lambda i: (i, 0)),
        out_shape=jax.ShapeDtypeStruct((rows, D_MODEL), F32),
        compiler_params=pltpu.CompilerParams(dimension_semantics=("parallel",), vmem_limit_bytes=VMEM_LIMIT),
        name="ffn",
    )(x, mod3, g2, w1, w3, w2)


def _inproj_kernel(x_ref, mod_ref, g_ref, w_ref, um_ref, us_ref):
    mod = mod_ref[0]
    h = _rmsnorm(x_ref[...], g_ref[...]) * (1.0 + mod[1:2]) + mod[0:1]
    u = jnp.dot(h.astype(BF16), w_ref[...], preferred_element_type=F32)
    um_ref[...] = u[:, :MAIN_COLS]
    us_ref[...] = u[:, MAIN_COLS:]


def _inproj(x, mod3, g, w_big, layer, rows_per_batch):
    rows = x.shape[0]
    tm = _row_tile(rows_per_batch)
    const = lambda i: (0, 0)
    return pl.pallas_call(
        _inproj_kernel,
        grid=(rows // tm,),
        in_specs=[
            pl.BlockSpec((tm, D_MODEL), lambda i: (i, 0)),
            pl.BlockSpec((1, 3, D_MODEL), _mod_index(mod3.shape[0], rows_per_batch, tm)),
            pl.BlockSpec((1, D_MODEL), const),
            pl.BlockSpec((None, D_MODEL, MAIN_COLS + SMALL_LANES), lambda i: (layer, 0, 0),
                         pipeline_mode=pl.Buffered(1)),
        ],
        out_specs=[pl.BlockSpec((tm, MAIN_COLS), lambda i: (i, 0)),
                   pl.BlockSpec((tm, SMALL_LANES), lambda i: (i, 0))],
        out_shape=[jax.ShapeDtypeStruct((rows, MAIN_COLS), F32),
                   jax.ShapeDtypeStruct((rows, SMALL_LANES), F32)],
        compiler_params=pltpu.CompilerParams(dimension_semantics=("parallel",), vmem_limit_bytes=VMEM_LIMIT),
        name="inproj",
    )(x, mod3, g, w_big)


def _outproj_kernel(x_ref, ma_ref, mb_ref, mc_ref, md_ref, mod_ref, g_ref, w_ref, o_ref):
    mix = jnp.concatenate([ma_ref[...], mb_ref[...], mc_ref[...], md_ref[...]], axis=-1)
    y = jnp.dot(mix.astype(BF16), w_ref[...], preferred_element_type=F32)
    o_ref[...] = x_ref[...] + mod_ref[0][2:3] * _rmsnorm(y, g_ref[...])


def _outproj(x, mixes, mod3, g, w_out, layer, rows_per_batch):
    rows = x.shape[0]
    tm = _row_tile(rows_per_batch)
    const = lambda i: (0, 0)
    mix_spec = pl.BlockSpec((tm, GROUP_WIDTH), lambda i: (i, 0))
    return pl.pallas_call(
        _outproj_kernel,
        grid=(rows // tm,),
        in_specs=[
            pl.BlockSpec((tm, D_MODEL), lambda i: (i, 0)),
            mix_spec, mix_spec, mix_spec, mix_spec,
            pl.BlockSpec((1, 3, D_MODEL), _mod_index(mod3.shape[0], rows_per_batch, tm)),
            pl.BlockSpec((1, D_MODEL), const),
            pl.BlockSpec((None, D_MODEL, D_MODEL), lambda i: (layer, 0, 0), pipeline_mode=pl.Buffered(1)),
        ],
        out_specs=pl.BlockSpec((tm, D_MODEL), lambda i: (i, 0)),
        out_shape=jax.ShapeDtypeStruct((rows, D_MODEL), F32),
        compiler_params=pltpu.CompilerParams(dimension_semantics=("parallel",), vmem_limit_bytes=VMEM_LIMIT),
        name="outproj",
    )(x, *mixes, mod3, g, w_out)


def _mixer_specs(mixer, seq):
    gl = GROUP_LANES
    u_spec = pl.BlockSpec((seq, 4 * gl), lambda b, g: (b, mixer * N_GROUPS + g))
    small_spec = pl.BlockSpec((seq, SMALL_LANES), lambda b, g: (b, 0))
    out_spec = pl.BlockSpec((seq, gl), lambda b, g: (b, g))
    return u_spec, small_spec, out_spec


def _state_spec(rows):
    return pl.BlockSpec((1, 1, 2, rows, GROUP_LANES), lambda b, g: (b, g, 0, 0, 0))


def _head_state_shape(batch):
    return jax.ShapeDtypeStruct((batch, 2, N_HEADS, HEAD_DIM, HEAD_DIM), F32)


_HEAD_STATE_SPEC = pl.BlockSpec((1, 2, HEADS_PER_GROUP, HEAD_DIM, HEAD_DIM), lambda b, g: (b, 0, g, 0, 0))


def _store_head_blocks(ref, d, mat):
    for h in range(HEADS_PER_GROUP):
        lo = h * HEAD_DIM
        ref[0, d, h] = mat[lo:lo + HEAD_DIM, lo:lo + HEAD_DIM]


_MIXER_PARAMS = pltpu.CompilerParams(dimension_semantics=("parallel", "parallel"), vmem_limit_bytes=VMEM_LIMIT)


def _chunk_rows(c):
    return pl.ds(pl.multiple_of(c * CHUNK, CHUNK), CHUNK)


def _scan_order(t, n_chunks, reverse):
    return (n_chunks - 1 - t) if reverse else t


def _tile_row(row):
    return jnp.broadcast_to(row, (8, row.shape[1]))


def _edge_row(ref, c, reverse):
    if reverse:
        return ref[pl.ds(pl.multiple_of(c * CHUNK, 8), 8), :][0:1]
    return ref[pl.ds(pl.multiple_of(c * CHUNK + CHUNK - 8, 8), 8), :][7:8]


def _chunk_scan(x, reverse, op):
    n = x.shape[0]
    row = _iota(x.shape, 0) & (CHUNK - 1)
    s = 1
    while s < CHUNK:
        if reverse:
            shifted = pltpu.roll(x, n - s, 0)
            ok = row < CHUNK - s
        else:
            shifted = pltpu.roll(x, s, 0)
            ok = row >= s
        x = jnp.where(ok, op(x, shifted), x)
        s *= 2
    return x


def _row_loop(seq, fn):
    blk = min(seq, PRE_ROWS)
    k = min(seq // blk, ROWS_IN_FLIGHT)

    def body(i, c):
        _interleave([fn(pl.ds(pl.multiple_of((i * k + j) * blk, blk), blk)) for j in range(k)])
        return c

    if seq == blk * k:
        _interleave([fn(pl.ds(j * blk, blk)) for j in range(k)])
    else:
        lax.fori_loop(0, seq // (blk * k), body, 0)


def _chunk_loop(n_chunks, make_stream):
    k = min(n_chunks, CHUNKS_IN_FLIGHT)

    def body(t, carry):
        _interleave([make_stream(t * k + j) for j in range(k)])
        return carry

    if n_chunks == k:
        body(0, 0)
    else:
        lax.fori_loop(0, n_chunks // k, body, 0)


def _interleave(streams):
    results = [None] * len(streams)
    live = []
    for i, s in enumerate(streams):
        if hasattr(s, "send"):
            live.append(i)
        else:
            results[i] = s
    while live:
        for i in list(live):
            try:
                next(streams[i])
            except StopIteration as done:
                results[i] = done.value
                live.remove(i)
    return results


def _state_scan(n_chunks, reverse, s_ref, decay_ref, init):
    def body(t, s):
        c = _scan_order(t, n_chunks, reverse)
        inc = s_ref[c]
        s_ref[c] = s
        return decay_ref[c][0:1] * s + inc

    return lax.fori_loop(0, n_chunks, body, init)


def _finalize(seq, acc_ref, o_ref, fn):
    def blk(rows):
        res = fn(rows, acc_ref[rows, :] + o_ref[rows, :])
        if hasattr(res, "send"):
            res = yield from res
        o_ref[rows, :] = res

    _row_loop(seq, blk)


def _rope(x, cos, sin_signed):
    lane = _iota(x.shape, 1)
    first = (lane & 31) < 16
    swapped = jnp.where(first, pltpu.roll(x, GROUP_LANES - 16, 1), pltpu.roll(x, 16, 1))
    return x * cos + swapped * sin_signed


def _ret_kernel(dec_ref, nw_ref, u_ref, *rest, seq, rope, has_state, emit_state):
    rest = list(rest)
    cos_ref, sin_ref = (rest.pop(0), rest.pop(0)) if rope else (None, None)
    s0_ref = rest.pop(0) if has_state else None
    o_ref = rest.pop(0)
    st_ref = rest.pop(0) if emit_state else None
    acc_ref, q_ref, k_ref, s_ref, cdec_ref = rest

    gl, L, G = GROUP_LANES, CHUNK, HEADS_PER_GROUP
    n_chunks = seq // L
    g = pl.program_id(1)
    masks = _head_masks()
    bd = _bd_mask()
    bd_avg = bd * (1.0 / HEAD_DIM)
    scale = HEAD_DIM ** -0.5
    ri = _iota((L, gl), 0).astype(F32)
    si = _iota((G * L, L), 0)
    sj = _iota((G * L, L), 1)
    i_idx = si & (L - 1)
    hrow = _head_of(_iota((G * L, 1), 0))

    def prepare(rows):
        q = u_ref[rows, 0:gl]
        k = u_ref[rows, gl:2 * gl]
        if rope:
            cos = cos_ref[rows, :]
            sin = sin_ref[rows, :]
            q = _rope(q, cos, sin)
            k = _rope(k, cos, sin)
        q_ref[rows, :] = q * scale
        k_ref[rows, :] = k

    _row_loop(seq, prepare)

    for d in (0, 1):
        reverse = d == 1
        raw_row = masks[0] * dec_ref[d, g * G]
        raw_col = jnp.where(hrow == 0, dec_ref[d, g * G], 0.0)
        for h in range(1, G):
            raw_row = raw_row + masks[h] * dec_ref[d, g * G + h]
            raw_col = jnp.where(hrow == h, dec_ref[d, g * G + h], raw_col)
        lg_row = _log_sigmoid(raw_row)
        lg_col = _log_sigmoid(raw_col)
        diff = ((sj - i_idx) if reverse else (i_idx - sj)).astype(F32)
        dmat = jnp.where(diff >= 0, jnp.exp(jnp.maximum(diff, 0.0) * lg_col), 0.0)
        qdec = jnp.exp(((L - ri) if reverse else (ri + 1.0)) * lg_row)
        kdec = jnp.exp((ri if reverse else (L - 1.0 - ri)) * lg_row)
        cdec_tile = _tile_row(jnp.exp(float(L) * lg_row))
        out_ref = o_ref if reverse else acc_ref

        def increments(c, kdec=kdec, cdec_tile=cdec_tile):
            rows = _chunk_rows(c)
            inc = _mm_tn(k_ref[rows, :] * kdec, u_ref[rows, 2 * gl:3 * gl])
            yield
            s_ref[c] = bd * inc
            cdec_ref[c] = cdec_tile

        _chunk_loop(n_chunks, increments)
        s_fin = _state_scan(n_chunks, reverse, s_ref, cdec_ref,
                            s0_ref[0, 0, d] if has_state else jnp.zeros((gl, gl), F32))
        if emit_state:
            _store_head_blocks(st_ref, d, s_fin)

        def outputs(c, dmat=dmat, qdec=qdec, out_ref=out_ref):
            rows = _chunk_rows(c)
            q = q_ref[rows, :]
            v = u_ref[rows, 2 * gl:3 * gl]
            att = _mm_nt(_stack(q, masks), k_ref[rows, :])
            inter = _mm(q * qdec, s_ref[c])
            yield
            intra = _mm(att * dmat, v)
            yield
            out_ref[rows, :] = _unstack(intra, masks, L) + inter

        _chunk_loop(n_chunks, outputs)

    def finalize(rows, o):
        mu = _mm_split(o, bd_avg)
        yield
        dev = o - mu
        var = _mm_split(dev * dev, bd_avg)
        yield
        return dev * lax.rsqrt(var + EPS) * nw_ref[...] * _silu(u_ref[rows, 3 * gl:4 * gl])

    _finalize(seq, acc_ref, o_ref, finalize)


def _ret_tables(seq):
    t = np.arange(seq)
    quarter = HEAD_DIM // 4
    inv = (ROPE_BASE ** (-np.arange(quarter, dtype=np.float32) / quarter)).astype(np.float32)
    ang_r = (t // GRID_W).astype(np.float32)[:, None] * inv
    ang_c = (t % GRID_W).astype(np.float32)[:, None] * inv
    cos = np.concatenate([np.cos(ang_r), np.cos(ang_r), np.cos(ang_c), np.cos(ang_c)], axis=1)
    sin = np.concatenate([-np.sin(ang_r), np.sin(ang_r), -np.sin(ang_c), np.sin(ang_c)], axis=1)
    tile = lambda a: jnp.asarray(np.tile(a.astype(np.float32), (1, HEADS_PER_GROUP)))
    return tile(cos), tile(sin)


def _retention(u_main, ret_decay, norm_w, batch, seq, state):
    latent = state is not None
    u_spec, _, out_spec = _mixer_specs(3, seq)
    gl = GROUP_LANES
    const2 = lambda b, g: (0, 0)
    in_specs = [pl.BlockSpec(memory_space=pltpu.SMEM), pl.BlockSpec((1, gl), const2), u_spec]
    args = [ret_decay, jnp.tile(norm_w.reshape(1, HEAD_DIM), (1, HEADS_PER_GROUP)), u_main]
    if latent:
        cos, sin = _ret_tables(seq)
        in_specs += [pl.BlockSpec((seq, gl), const2), pl.BlockSpec((seq, gl), const2), _state_spec(gl)]
        args += [cos, sin, state]
    out_shape = [jax.ShapeDtypeStruct((batch * seq, GROUP_WIDTH), F32)]
    out_specs = [out_spec]
    if not latent:
        out_shape.append(_head_state_shape(batch))
        out_specs.append(_HEAD_STATE_SPEC)
    res = pl.pallas_call(
        functools.partial(_ret_kernel, seq=seq, rope=latent, has_state=latent, emit_state=not latent),
        grid=(batch, N_GROUPS),
        in_specs=in_specs,
        out_specs=out_specs,
        out_shape=out_shape,
        scratch_shapes=[pltpu.VMEM((seq, gl), F32), pltpu.VMEM((seq, gl), F32), pltpu.VMEM((seq, gl), F32),
                        pltpu.VMEM((seq // CHUNK, gl, gl), F32), pltpu.VMEM((seq // CHUNK, 8, gl), F32)],
        compiler_params=_MIXER_PARAMS,
        name="retention",
    )(*args)
    return res[0], (None if latent else res[1])


def _gla_kernel(wa_ref, ba_ref, nw_ref, u_ref, us_ref, *rest, seq, has_state, emit_state):
    rest = list(rest)
    s0_ref = rest.pop(0) if has_state else None
    o_ref = rest.pop(0)
    st_ref = rest.pop(0) if emit_state else None
    acc_ref, b_ref, s_ref, dec_ref, steep_ref = rest

    gl, L, G, SB = GROUP_LANES, CHUNK, HEADS_PER_GROUP, GLA_SUB
    n_chunks = seq // L
    n_sub = L // SB
    masks = _head_masks()
    bd = _bd_mask()
    bd_avg = bd * (1.0 / HEAD_DIM)
    scale = HEAD_DIM ** -0.5
    half = SB // 2
    pair_row = _iota((SB * SB, gl), 0)
    pi = lax.shift_right_logical(pair_row, 4)
    pj = pair_row & (SB - 1)
    pair_sum = (lax.shift_right_logical(_iota((SB, SB * SB), 1), 4) == _iota((SB, SB * SB), 0)).astype(BF16)
    zeros_half = jnp.zeros((half, gl), F32)

    for d in (0, 1):
        reverse = d == 1
        pair_ok = (pj >= pi) if reverse else (pj <= pi)
        out_ref = o_ref if reverse else acc_ref

        steep_ref[...] = jnp.zeros((8, gl), F32)

        def log_decay(rows, d=d, reverse=reverse):
            a_pre = _mm(us_ref[rows, :], wa_ref[d]) + ba_ref[d]
            yield
            la = _log_sigmoid(a_pre) * (LOG2E / GLA_TAU)
            b_ref[rows, :] = _chunk_scan(la, reverse, jnp.add)
            steep_ref[...] = jnp.maximum(steep_ref[...], jnp.max((-la).reshape(-1, 8, gl), axis=0))

        _row_loop(seq, log_decay)
        factorise_all = jnp.max(steep_ref[...]) * (SB - 1) <= GLA_SAFE_LOG2

        def increments(c, reverse=reverse):
            rows = _chunk_rows(c)
            b = b_ref[rows, :]
            btot = b[0:1] if reverse else b[L - 1:L]
            inc = _mm_tn(u_ref[rows, 2 * gl:3 * gl], u_ref[rows, gl:2 * gl] * jnp.exp2(btot - b))
            yield
            s_ref[c] = bd * inc
            dec_ref[c] = _tile_row(jnp.exp2(btot))

        _chunk_loop(n_chunks, increments)
        s_fin = _state_scan(n_chunks, reverse, s_ref, dec_ref,
                            s0_ref[0, 0, d] if has_state else jnp.zeros((gl, gl), F32))
        if emit_state:
            _store_head_blocks(st_ref, d, s_fin.T)

        def outputs_factorised(c, reverse=reverse, out_ref=out_ref):
            rows = _chunk_rows(c)
            b = b_ref[rows, :]
            q = u_ref[rows, 0:gl] * scale
            k = u_ref[rows, gl:2 * gl]
            v = u_ref[rows, 2 * gl:3 * gl]
            o_inter = _mm_nt(q * jnp.exp2(b), s_ref[c])
            atts = []
            for i_blk in range(n_sub):
                lo = i_blk * SB
                bi = b[lo:lo + SB]
                ref, keys = (bi[SB - 1:SB], slice(lo, L)) if reverse else (bi[0:1], slice(0, lo + SB))
                qd = q[lo:lo + SB] * jnp.exp2(bi - ref)
                kd = k[keys] * jnp.exp2(ref - b[keys])
                att = _mm_nt(_stack(qd, masks), kd)
                n_keys = keys.stop - keys.start
                qrow = _iota((G * SB, n_keys), 0) & (SB - 1)
                kcol = _iota((G * SB, n_keys), 1)
                ok = (kcol >= qrow) if reverse else (kcol <= qrow + lo)
                atts.append((jnp.where(ok, att, 0.0), keys))
            yield
            outs = [_mm(a, v[keys]) for a, keys in atts]
            yield
            out_ref[rows, :] = o_inter + jnp.concatenate([_unstack(o, masks, SB) for o in outs], axis=0)

        def outputs_direct(c, reverse=reverse, pair_ok=pair_ok, out_ref=out_ref):
            rows = _chunk_rows(c)
            b = b_ref[rows, :]
            q = u_ref[rows, 0:gl] * scale
            k = u_ref[rows, gl:2 * gl]
            v = u_ref[rows, 2 * gl:3 * gl]
            o_inter = _mm_nt(q * jnp.exp2(b), s_ref[c])
            diag, off = [], []
            for i_blk in range(n_sub):
                lo = i_blk * SB
                qi, ki, bi = q[lo:lo + SB], k[lo:lo + SB], b[lo:lo + SB]
                prods = []
                for i in range(SB):
                    if reverse and i >= half:
                        part = slice(half, SB)
                    elif (not reverse) and i < half:
                        part = slice(0, half)
                    else:
                        part = slice(0, SB)
                    p = (qi[i:i + 1] * ki[part]) * jnp.exp2(bi[i:i + 1] - bi[part])
                    if part.start == half:
                        prods += [zeros_half, p]
                    elif part.stop == half:
                        prods += [p, zeros_half]
                    else:
                        prods.append(p)
                diag.append(_mm(jnp.concatenate(prods, axis=0), bd))
                if reverse and i_blk < n_sub - 1:
                    ref, other = bi[SB - 1:SB], slice(lo + SB, L)
                elif (not reverse) and i_blk > 0:
                    ref, other = bi[0:1], slice(0, lo)
                else:
                    off.append(None)
                    continue
                qd = qi * jnp.exp2(bi - ref)
                kd = k[other] * jnp.exp2(ref - b[other])
                off.append((_mm_nt(_stack(qd, masks), kd), other))
            yield
            off = [None if a is None else _mm(a[0], v[a[1]]) for a in off]
            for i_blk in range(n_sub):
                vi = v[i_blk * SB:(i_blk + 1) * SB]
                weighted = jnp.where(pair_ok, diag[i_blk] * jnp.concatenate([vi] * SB, axis=0), 0.0)
                diag[i_blk] = jnp.dot(pair_sum, weighted.astype(BF16), preferred_element_type=F32)
            yield
            blocks = []
            for i_blk in range(n_sub):
                o_blk = diag[i_blk]
                if off[i_blk] is not None:
                    o_blk = o_blk + _unstack(off[i_blk], masks, SB)
                blocks.append(o_blk)
            out_ref[rows, :] = o_inter + jnp.concatenate(blocks, axis=0)

        @pl.when(factorise_all)
        def _():
            _chunk_loop(n_chunks, outputs_factorised)

        @pl.when(jnp.logical_not(factorise_all))
        def _():
            _chunk_loop(n_chunks, outputs_direct)

    def finalize(rows, o):
        ms = _mm_split(o * o, bd_avg)
        yield
        return o * lax.rsqrt(ms + EPS) * nw_ref[...] * _silu(u_ref[rows, 3 * gl:4 * gl])

    _finalize(seq, acc_ref, o_ref, finalize)


def _gla(u_main, u_small, wa_pad, b_a, norm_w, batch, seq, state):
    latent = state is not None
    u_spec, small_spec, out_spec = _mixer_specs(0, seq)
    gl = GROUP_LANES
    in_specs = [
        pl.BlockSpec((2, SMALL_LANES, gl), lambda b, g: (0, 0, g)),
        pl.BlockSpec((2, 1, gl), lambda b, g: (0, 0, g)),
        pl.BlockSpec((1, gl), lambda b, g: (0, 0)),
        u_spec, small_spec,
    ]
    args = [wa_pad, b_a.reshape(2, 1, GROUP_WIDTH),
            jnp.tile(norm_w.reshape(1, HEAD_DIM), (1, HEADS_PER_GROUP)), u_main, u_small]
    if latent:
        in_specs.append(_state_spec(gl))
        args.append(state)
    out_shape = [jax.ShapeDtypeStruct((batch * seq, GROUP_WIDTH), F32)]
    out_specs = [out_spec]
    if not latent:
        out_shape.append(_head_state_shape(batch))
        out_specs.append(_HEAD_STATE_SPEC)
    res = pl.pallas_call(
        functools.partial(_gla_kernel, seq=seq, has_state=latent, emit_state=not latent),
        grid=(batch, N_GROUPS),
        in_specs=in_specs,
        out_specs=out_specs,
        out_shape=out_shape,
        scratch_shapes=[pltpu.VMEM((seq, gl), F32), pltpu.VMEM((seq, gl), F32),
                        pltpu.VMEM((seq // CHUNK, gl, gl), F32), pltpu.VMEM((seq // CHUNK, 8, gl), F32),
                        pltpu.VMEM((8, gl), F32)],
        compiler_params=_MIXER_PARAMS,
        name="gla",
    )(*args)
    return res[0], (None if latent else res[1])


def _mlstm_kernel(ei_ref, ef_ref, bi_ref, bf_ref, u_ref, us_ref, *rest, seq, has_state, emit_state):
    rest = list(rest)
    if has_state:
        c0_ref, n0_ref, m0_ref = rest.pop(0), rest.pop(0), rest.pop(0)
    o_ref = rest.pop(0)
    if emit_state:
        cst_ref, nst_ref, mst_ref = rest.pop(0), rest.pop(0), rest.pop(0)
    acc_ref, f_ref, x_ref, p_ref, s_ref, nu_ref, wp_ref, min_ref, mout_ref = rest

    gl, L, G = GROUP_LANES, CHUNK, HEADS_PER_GROUP
    n_chunks = seq // L
    masks = _head_masks()
    bd = _bd_mask()
    scale = HEAD_DIM ** -0.5
    si = _iota((G * L, L), 0) & (L - 1)
    sj = _iota((G * L, L), 1)
    lane_in_head = _iota((L, gl), 1) & (HEAD_DIM - 1)

    for d in (0, 1):
        reverse = d == 1
        causal = (sj >= si) if reverse else (sj <= si)
        last = 0 if reverse else L - 1
        out_ref = o_ref if reverse else acc_ref

        def gates(rows, d=d, reverse=reverse):
            us = us_ref[rows, :]
            ig = _mm_split(us, ei_ref[d, 0]) + bi_ref[d, 0]
            fg = _mm_split(us, ef_ref[d, 0]) + bf_ref[d, 0]
            yield
            f_sum = _chunk_scan(_log_sigmoid(fg), reverse, jnp.add)
            x = ig - f_sum
            f_ref[rows, :] = f_sum
            x_ref[rows, :] = x
            p_ref[rows, :] = _chunk_scan(x, reverse, jnp.maximum)

        _row_loop(seq, gates)

        def stabiliser(t, m_row, reverse=reverse):
            c = _scan_order(t, n_chunks, reverse)
            m_new = _edge_row(f_ref, c, reverse) + jnp.maximum(m_row, _edge_row(p_ref, c, reverse))
            min_ref[c] = _tile_row(m_row)
            mout_ref[c] = _tile_row(m_new)
            return m_new

        m_fin = lax.fori_loop(0, n_chunks, stabiliser,
                              m0_ref[0, 0, d] if has_state else jnp.zeros((1, gl), F32))

        def increments(c, last=last):
            rows = _chunk_rows(c)
            f_tot = f_ref[rows, :][last:last + 1]
            m_in = min_ref[c][0:1]
            m_out = mout_ref[c][0:1]
            kw = u_ref[rows, gl:2 * gl] * jnp.exp(f_tot + x_ref[rows, :] - m_out)
            inc = _mm_tn(kw, u_ref[rows, 2 * gl:3 * gl])
            yield
            s_ref[c] = bd * inc
            nu_ref[c] = _tile_row(jnp.sum(kw, axis=0, keepdims=True))
            wp_ref[c] = _tile_row(jnp.exp(f_tot + m_in - m_out))

        _chunk_loop(n_chunks, increments)

        def memory(t, carry, reverse=reverse):
            c_mat, n_row = carry
            c = _scan_order(t, n_chunks, reverse)
            inc = s_ref[c]
            nu = nu_ref[c][0:1]
            wp = wp_ref[c][0:1]
            s_ref[c] = c_mat
            nu_ref[c] = _tile_row(n_row)
            return wp * c_mat + inc, wp * n_row + nu

        if has_state:
            init = (c0_ref[0, 0, d], n0_ref[0, 0, d])
        else:
            init = (jnp.zeros((gl, gl), F32), jnp.zeros((1, gl), F32))
        c_fin, n_fin = lax.fori_loop(0, n_chunks, memory, init)
        if emit_state:
            _store_head_blocks(cst_ref, d, c_fin)
            nst_ref[0, 0, d], mst_ref[0, 0, d] = n_fin, m_fin

        def outputs(c, causal=causal, out_ref=out_ref):
            rows = _chunk_rows(c)
            x = x_ref[rows, :]
            m_in = min_ref[c][0:1]
            q = u_ref[rows, 0:gl] * scale
            k = u_ref[rows, gl:2 * gl]
            v = u_ref[rows, 2 * gl:3 * gl]
            m_full = jnp.maximum(m_in, p_ref[rows, :])
            lhs = _split_lanes(-m_full, lane_in_head, 0, 1.0, 3)
            rhs = _split_lanes(x, lane_in_head, 3, 1.0, 0)
            logw = _mm_nt(_stack(lhs, masks), rhs)
            qk = _mm_nt(_stack(q, masks), k)
            inter = _mm(q, s_ref[c])
            q_n = _mm_split(q * nu_ref[c][0:1], bd)
            yield
            s = qk * jnp.exp(jnp.where(causal, logw, NEG))
            s_hi = s.astype(BF16)
            ones = jnp.ones((L, gl), BF16)
            r = jnp.dot(s_hi, jnp.concatenate([v.astype(BF16), ones], axis=1), preferred_element_type=F32)
            sum_lo = jnp.dot((s - s_hi.astype(F32)).astype(BF16), ones, preferred_element_type=F32)
            yield
            w_inter = jnp.exp(m_in - m_full)
            num = _unstack(r[:, :gl], masks, L) + w_inter * inter
            den = _unstack(r[:, gl:] + sum_lo, masks, L) + w_inter * q_n
            out_ref[rows, :] = num / jnp.maximum(jnp.abs(den), jnp.exp(-(f_ref[rows, :] + m_full)))

        _chunk_loop(n_chunks, outputs)

    def finalize(rows, o):
        return o * jax.nn.sigmoid(u_ref[rows, 3 * gl:4 * gl])

    _finalize(seq, acc_ref, o_ref, finalize)


def _gate_expanders(offset):
    e = np.zeros((2, N_GROUPS, SMALL_LANES, GROUP_LANES), np.float32)
    for d in range(2):
        for g in range(N_GROUPS):
            for h in range(HEADS_PER_GROUP):
                src = offset + d * N_HEADS + g * HEADS_PER_GROUP + h
                e[d, g, src, h * HEAD_DIM:(h + 1) * HEAD_DIM] = 1.0
    return jnp.asarray(e)


def _gate_bias(b):
    return jnp.repeat(b.astype(F32), HEAD_DIM, axis=1).reshape(2, N_GROUPS, 1, GROUP_LANES)


def _mlstm(u_main, u_small, b_i, b_f, batch, seq, state):
    latent = state is not None
    u_spec, small_spec, out_spec = _mixer_specs(2, seq)
    gl = GROUP_LANES
    exp_spec = pl.BlockSpec((2, 1, SMALL_LANES, gl), lambda b, g: (0, g, 0, 0))
    bias_spec = pl.BlockSpec((2, 1, 1, gl), lambda b, g: (0, g, 0, 0))
    in_specs = [exp_spec, exp_spec, bias_spec, bias_spec, u_spec, small_spec]
    args = [_gate_expanders(MI_OFF), _gate_expanders(MF_OFF), _gate_bias(b_i), _gate_bias(b_f), u_main, u_small]
    if latent:
        in_specs += [_state_spec(gl), _state_spec(1), _state_spec(1)]
        args += list(state)
    out_shape = [jax.ShapeDtypeStruct((batch * seq, GROUP_WIDTH), F32)]
    out_specs = [out_spec]
    if not latent:
        out_shape += [_head_state_shape(batch),
                      jax.ShapeDtypeStruct((batch, N_GROUPS, 2, 1, gl), F32),
                      jax.ShapeDtypeStruct((batch, N_GROUPS, 2, 1, gl), F32)]
        out_specs += [_HEAD_STATE_SPEC, _state_spec(1), _state_spec(1)]
    res = pl.pallas_call(
        functools.partial(_mlstm_kernel, seq=seq, has_state=latent, emit_state=not latent),
        grid=(batch, N_GROUPS),
        in_specs=in_specs,
        out_specs=out_specs,
        out_shape=out_shape,
        scratch_shapes=[pltpu.VMEM((seq, gl), F32)] * 4 + [pltpu.VMEM((seq // CHUNK, gl, gl), F32)]
        + [pltpu.VMEM((seq // CHUNK, 8, gl), F32)] * 4,
        compiler_params=_MIXER_PARAMS,
        name="mlstm",
    )(*args)
    return res[0], (None if latent else tuple(res[1:]))


def _na_ctx_kernel(u_ref, o_ref, ko_ref, vo_ref, *, seq):
    gl = GROUP_LANES
    masks = _head_masks()
    q = u_ref[:, 0:gl] * (HEAD_DIM ** -0.5)
    k = u_ref[:, gl:2 * gl]
    v = u_ref[:, 2 * gl:3 * gl]
    s = _mm_nt(_stack(q, masks), k)
    p = jnp.exp(s - jnp.max(s, axis=1, keepdims=True))
    o = _mm(p, v) / jnp.sum(p, axis=1, keepdims=True)
    o_ref[...] = _unstack(o, masks, seq)
    for h in range(HEADS_PER_GROUP):
        ko_ref[0, h] = k[:, h * HEAD_DIM:(h + 1) * HEAD_DIM]
        vo_ref[0, h] = v[:, h * HEAD_DIM:(h + 1) * HEAD_DIM]


def _na_context(u_main, batch, seq):
    u_spec, _, out_spec = _mixer_specs(1, seq)
    kv_shape = jax.ShapeDtypeStruct((batch, N_HEADS, seq, HEAD_DIM), F32)
    kv_spec = pl.BlockSpec((1, HEADS_PER_GROUP, seq, HEAD_DIM), lambda b, g: (b, g, 0, 0))
    return pl.pallas_call(
        functools.partial(_na_ctx_kernel, seq=seq),
        grid=(batch, N_GROUPS),
        in_specs=[u_spec],
        out_specs=[out_spec, kv_spec, kv_spec],
        out_shape=[jax.ShapeDtypeStruct((batch * seq, GROUP_WIDTH), F32), kv_shape, kv_shape],
        compiler_params=_MIXER_PARAMS,
        name="na_context",
    )(u_main)


def _na_lat_kernel(tab_ref, kc_ref, vc_ref, u_ref, o_ref, *, seq):
    gl = GROUP_LANES
    grid_rows = seq // GRID_W
    win = NA_KH * GRID_W
    masks = _head_masks()
    scale = HEAD_DIM ** -0.5
    kc = kc_ref[0, 0]
    vc = vc_ref[0, 0]

    def body(r, carry):
        ks = jnp.clip(r - NA_KH // 2, 0, grid_rows - NA_KH)
        q_rows = pl.ds(pl.multiple_of(r * GRID_W, GRID_W), GRID_W)
        k_rows = pl.ds(pl.multiple_of(ks * GRID_W, GRID_W), win)
        qs = _stack(u_ref[q_rows, 0:gl] * scale, masks)
        s_loc = _mm_nt(qs, u_ref[k_rows, gl:2 * gl]) + tab_ref[0, ks - r + NA_KH - 1]
        s_ctx = _mm_nt(qs, kc)
        m = jnp.maximum(jnp.max(s_loc, axis=1, keepdims=True), jnp.max(s_ctx, axis=1, keepdims=True))
        p_loc = jnp.exp(s_loc - m)
        p_ctx = jnp.exp(s_ctx - m)
        den = jnp.sum(p_loc, axis=1, keepdims=True) + jnp.sum(p_ctx, axis=1, keepdims=True)
        o = (_mm(p_loc, u_ref[k_rows, 2 * gl:3 * gl]) + _mm(p_ctx, vc)) / den
        o_ref[q_rows, :] = _unstack(o, masks, GRID_W)
        return carry

    lax.fori_loop(0, grid_rows, body, 0)


def _na_bias_table(rpb):
    c = np.arange(GRID_W)
    rel = c[None, :] - c[:, None]
    cs = np.clip(c - NA_KW // 2, 0, GRID_W - NA_KW)
    valid = (c[None, :] >= cs[:, None]) & (c[None, :] < cs[:, None] + NA_KW)
    dcol = np.clip(rel, -(NA_KW - 1), NA_KW - 1) + NA_KW - 1
    bmat = jnp.where(valid[None, None], rpb.astype(F32)[:, :, dcol], NEG)
    tabs = jnp.stack([bmat[:, s:s + NA_KH] for s in range(NA_KH)], axis=1)
    tabs = tabs.transpose(0, 1, 3, 2, 4).reshape(N_GROUPS, HEADS_PER_GROUP, NA_KH, GRID_W, NA_KH * GRID_W)
    return tabs.transpose(0, 2, 1, 3, 4).reshape(N_GROUPS, NA_KH, HEADS_PER_GROUP * GRID_W, NA_KH * GRID_W)


def _heads_to_lanes(t):
    b, _, s, _ = t.shape
    t = t.reshape(b, N_GROUPS, HEADS_PER_GROUP, s, HEAD_DIM).transpose(0, 1, 3, 2, 4)
    return t.reshape(b, N_GROUPS, s, GROUP_LANES)


def _na_latent(u_main, rpb, k_ctx, v_ctx, batch, seq):
    u_spec, _, out_spec = _mixer_specs(1, seq)
    gl = GROUP_LANES
    past = k_ctx.shape[2]
    tab = _na_bias_table(rpb)
    ctx_spec = pl.BlockSpec((1, 1, past, gl), lambda b, g: (b, g, 0, 0))
    return pl.pallas_call(
        functools.partial(_na_lat_kernel, seq=seq),
        grid=(batch, N_GROUPS),
        in_specs=[pl.BlockSpec((1,) + tab.shape[1:], lambda b, g: (g, 0, 0, 0)), ctx_spec, ctx_spec, u_spec],
        out_specs=out_spec,
        out_shape=jax.ShapeDtypeStruct((batch * seq, GROUP_WIDTH), F32),
        compiler_params=_MIXER_PARAMS,
        name="na_latent",
    )(tab, _heads_to_lanes(k_ctx.astype(F32)), _heads_to_lanes(v_ctx.astype(F32)), u_main)


def _to_block_diag(s):
    b = s.shape[0]
    G = HEADS_PER_GROUP
    s = s.astype(F32).reshape(b, 2, N_GROUPS, G, HEAD_DIM, 1, HEAD_DIM)
    eye = jnp.eye(G, dtype=F32).reshape(1, 1, 1, G, 1, G, 1)
    return (s * eye).reshape(b, 2, N_GROUPS, GROUP_LANES, GROUP_LANES).transpose(0, 2, 1, 3, 4)


def _rows_to_lanes(v):
    b = v.shape[0]
    return v.astype(F32).reshape(b, 2, N_GROUPS, 1, GROUP_LANES).transpose(0, 2, 1, 3, 4)


def _lanes_to_rows(v):
    b = v.shape[0]
    return v.transpose(0, 2, 1, 3, 4).reshape(b, 2, N_HEADS, HEAD_DIM)


def _swap_last(s):
    return jnp.swapaxes(s, -1, -2)


def _in_weight(w_in):
    gl = GROUP_LANES
    w_in = w_in.astype(BF16)
    zeros = jnp.zeros((DEPTH, D_MODEL, gl), BF16)
    cols = []
    for names in _MIXER_COLS:
        for g in range(N_GROUPS):
            for name in names:
                cols.append(zeros if name is None else w_in[..., _OFF[name] + g * gl:_OFF[name] + (g + 1) * gl])
    cols.append(w_in[..., _OFF["ga"]:_OFF["ga"] + 2 * GLA_RANK])
    cols.append(w_in[..., _OFF["mi"]:_OFF["mi"] + 2 * N_HEADS])
    cols.append(w_in[..., _OFF["mf"]:_OFF["mf"] + 2 * N_HEADS])
    cols.append(jnp.zeros((DEPTH, D_MODEL, SMALL_LANES - MF_OFF - 2 * N_HEADS), BF16))
    return jnp.concatenate(cols, axis=-1)


def _gla_gate_weight(w_a2_l):
    out = jnp.zeros((2, SMALL_LANES, GROUP_WIDTH), F32)
    for d in range(2):
        out = out.at[d, d * GLA_RANK:(d + 1) * GLA_RANK].set(w_a2_l[d].astype(F32))
    return out


def _layer(x, mod, p, batch, seq, ctx):
    latent = ctx is not None
    l = p["layer"]
    x = _ffn(x, mod[:, 0:3], p["norm_w"][0:2], p["w1"], p["w3"], p["w2"], (l, 0), seq)
    u_main, u_small = _inproj(x, mod[:, 3:6], p["norm_w"][2:3], p["w_in"], l, seq)
    mix_a, st_gla = _gla(u_main, u_small, p["wa_pad"], p["gla_b_a"], p["gla_norm_w"], batch, seq,
                         _swap_last(_to_block_diag(ctx["gla"])) if latent else None)
    if latent:
        mix_b = _na_latent(u_main, p["na_rpb"], ctx["na_k"], ctx["na_v"], batch, seq)
    else:
        mix_b, na_k, na_v = _na_context(u_main, batch, seq)
    mix_c, st_ml = _mlstm(u_main, u_small, p["mlstm_b_i"], p["mlstm_b_f"], batch, seq,
                          (_to_block_diag(ctx["mC"]), _rows_to_lanes(ctx["mn"]),
                           _rows_to_lanes(jnp.repeat(ctx["mm"][..., None], HEAD_DIM, axis=-1))) if latent else None)
    mix_d, st_ret = _retention(u_main, p["ret_decay"], p["ret_norm_w"], batch, seq,
                               _to_block_diag(ctx["ret"]) if latent else None)
    x = _outproj(x, (mix_a, mix_b, mix_c, mix_d), mod[:, 3:6], p["norm_w"][3:4], p["w_out"], l, seq)
    x = _ffn(x, mod[:, 6:9], p["norm_w"][4:6], p["w1"], p["w3"], p["w2"], (l, 1), seq)
    if latent:
        return x, None
    new = {
        "na_k": na_k, "na_v": na_v, "gla": st_gla, "mC": st_ml[0],
        "mn": _lanes_to_rows(st_ml[1]), "mm": _lanes_to_rows(st_ml[2])[..., 0], "ret": st_ret,
    }
    return x, new


def kernel(x_prompt, x_sample, cache_na_k, cache_na_v, state_gla, state_mlstm_C, state_mlstm_n, state_mlstm_m,
           state_ret, c, c_ctx, w_mod, b_mod, norm_w, ffn_w1, ffn_w3, ffn_w2, w_in, w_out, gla_w_a2, gla_b_a,
           gla_norm_w, na_rpb, mlstm_b_i, mlstm_b_f, ret_decay, ret_norm_w):
    batch, seq, _ = x_prompt.shape
    dec_batch, dec_seq, _ = x_sample.shape

    cvec = jnp.zeros((8, D_MODEL), F32).at[0].set(c_ctx).at[1:1 + dec_batch].set(c)
    mods = _modulation(cvec, w_mod, b_mod).reshape(DEPTH, 8, N_MOD, D_MODEL)

    w1, w3, w2 = ffn_w1.astype(BF16), ffn_w3.astype(BF16), ffn_w2.astype(BF16)
    w_in_all, w_out_all = _in_weight(w_in), w_out.astype(BF16)
    params = []
    for l in range(DEPTH):
        params.append({
            "layer": l, "norm_w": norm_w[l],
            "w1": w1, "w3": w3, "w2": w2, "w_in": w_in_all, "w_out": w_out_all,
            "wa_pad": _gla_gate_weight(gla_w_a2[l]), "gla_b_a": gla_b_a[l], "gla_norm_w": gla_norm_w[l],
            "na_rpb": na_rpb[l], "mlstm_b_i": mlstm_b_i[l], "mlstm_b_f": mlstm_b_f[l],
            "ret_decay": ret_decay[l], "ret_norm_w": ret_norm_w[l],
        })

    xp = x_prompt.reshape(batch * seq, D_MODEL)
    states = []
    for l in range(DEPTH):
        xp, st = _layer(xp, mods[l, 0:1], params[l], batch, seq, None)
        states.append(st)

    xs = x_sample.reshape(dec_batch * dec_seq, D_MODEL)
    for l in range(DEPTH):
        ctx = {"na_k": cache_na_k[:, l], "na_v": cache_na_v[:, l], "gla": state_gla[:, l],
               "mC": state_mlstm_C[:, l], "mn": state_mlstm_n[:, l], "mm": state_mlstm_m[:, l],
               "ret": state_ret[:, l]}
        xs, _ = _layer(xs, mods[l, 1:1 + dec_batch], params[l], dec_batch, dec_seq, ctx)

    stack = lambda name: jnp.stack([s[name] for s in states], axis=1)
    return (xp.reshape(batch, seq, D_MODEL), xs.reshape(dec_batch, dec_seq, D_MODEL),
            stack("na_k"), stack("na_v"), stack("gla"), stack("mC"), stack("mn"), stack("mm"), stack("ret"))
```

```python
import functools

import numpy as np
import jax
import jax.numpy as jnp
from jax import lax
from jax.experimental import pallas as pl
from jax.experimental.pallas import tpu as pltpu

D_MODEL = 1024
DEPTH = 2
HEAD_DIM = 64
N_HEADS = 4
GROUP_WIDTH = N_HEADS * HEAD_DIM
N_MOD = 9
GLA_RANK = 16
GLA_TAU = 16.0
CHUNK = 64
GRID_W = 64
NA_KH = 8
NA_KW = 16
ROPE_BASE = 10000.0
EPS = 1e-6
D_FF = 2816

HEADS_PER_GROUP = 2
GROUP_LANES = HEADS_PER_GROUP * HEAD_DIM
N_GROUPS = N_HEADS // HEADS_PER_GROUP
GLA_SUB = 16
GLA_SAFE_LOG2 = 64.0
ROW_TILE_IN = 256
ROW_TILE_OUT = 512
ROW_PARTS = 2
PRE_ROWS = 128
ROWS_IN_FLIGHT = 2
NA_ROWS_IN_FLIGHT = 4
LOG2E = 1.4426950408889634
CHUNKS_IN_FLIGHT = 8
SMALL_LANES = 128
MI_OFF = 2 * GLA_RANK
MF_OFF = MI_OFF + 2 * N_HEADS
NEG = -1e30
VMEM_LIMIT = 56 * 1024 * 1024

F32 = jnp.float32
BF16 = jnp.bfloat16
HI = lax.Precision.HIGHEST

_OFF = {}
_o = 0
for _name, _size in (("gq", 256), ("gk", 256), ("gv", 256), ("gg", 256), ("ga", 32), ("nq", 256), ("nk", 256),
                     ("nv", 256), ("mq", 256), ("mk", 256), ("mv", 256), ("mo", 256), ("mi", 8), ("mf", 8),
                     ("rq", 256), ("rk", 256), ("rv", 256), ("rg", 256)):
    _OFF[_name] = _o
    _o += _size
N_IN = _o
_MIXER_COLS = (("gq", "gk", "gv", "gg"), ("nq", "nk", "nv", None), ("mq", "mk", "mv", "mo"), ("rq", "rk", "rv", "rg"))
MAIN_COLS = 4 * N_GROUPS * 4 * GROUP_LANES


def _mm(a, b):
    return jnp.dot(a.astype(BF16), b.astype(BF16), preferred_element_type=F32)


def _mm_nt(a, b):
    return lax.dot_general(a.astype(BF16), b.astype(BF16), (((1,), (1,)), ((), ())), preferred_element_type=F32)


def _mm_tn(a, b):
    return lax.dot_general(a.astype(BF16), b.astype(BF16), (((0,), (0,)), ((), ())), preferred_element_type=F32)


def _mm_split(x, w):
    hi = x.astype(BF16)
    rest = x - hi.astype(F32)
    mid = rest.astype(BF16)
    lo = (rest - mid.astype(F32)).astype(BF16)
    wb = w.astype(BF16)
    return jnp.dot(jnp.concatenate([hi, mid, lo], axis=1), jnp.concatenate([wb, wb, wb], axis=0),
                   preferred_element_type=F32)


def _log_sigmoid(x):
    return jnp.minimum(x, 0.0) - jnp.log1p(jnp.exp(-jnp.abs(x)))


def _silu(x):
    return x * jax.nn.sigmoid(x)


def _iota(shape, dim):
    return lax.broadcasted_iota(jnp.int32, shape, dim)


def _head_of(idx):
    return lax.shift_right_logical(idx, 6)


def _head_masks():
    lane = _iota((1, GROUP_LANES), 1)
    return [(_head_of(lane) == h).astype(F32) for h in range(HEADS_PER_GROUP)]


def _bd_mask():
    n = GROUP_LANES
    return (_head_of(_iota((n, n), 0)) == _head_of(_iota((n, n), 1))).astype(F32)


def _stack(x, masks):
    return jnp.concatenate([x * m for m in masks], axis=0)


def _unstack(r, masks, n):
    out = r[0:n] * masks[0]
    for h in range(1, len(masks)):
        out = out + r[h * n:(h + 1) * n] * masks[h]
    return out


def _rmsnorm(x, w):
    return x * lax.rsqrt(jnp.mean(x * x, axis=-1, keepdims=True) + EPS) * w


def _split_lanes(x, lane_in_head, at, fill, fill_at):
    hi = x.astype(BF16).astype(F32)
    rest = x - hi
    mid = rest.astype(BF16).astype(F32)
    lo = rest - mid
    out = jnp.where(lane_in_head == at, hi, jnp.where(lane_in_head == at + 1, mid,
                                                      jnp.where(lane_in_head == at + 2, lo, 0.0)))
    is_fill = (lane_in_head >= fill_at) & (lane_in_head < fill_at + 3)
    return jnp.where(is_fill, fill, out)


def _mod_kernel(c_ref, w_ref, b_ref, o_ref):
    s = _silu(c_ref[...])
    o_ref[0] = _mm(s, w_ref[0]) + b_ref[0]


def _modulation(cvec, w_mod, b_mod):
    n = N_MOD * D_MODEL
    tn = n // 8
    return pl.pallas_call(
        _mod_kernel,
        grid=(DEPTH, n // tn),
        in_specs=[
            pl.BlockSpec((8, D_MODEL), lambda l, j: (0, 0)),
            pl.BlockSpec((1, D_MODEL, tn), lambda l, j: (l, 0, j)),
            pl.BlockSpec((1, 1, tn), lambda l, j: (l, 0, j)),
        ],
        out_specs=pl.BlockSpec((1, 8, tn), lambda l, j: (l, 0, j)),
        out_shape=jax.ShapeDtypeStruct((DEPTH, 8, n), F32),
        compiler_params=pltpu.CompilerParams(dimension_semantics=("parallel", "parallel"),
                                             vmem_limit_bytes=VMEM_LIMIT),
        name="modulation",
    )(cvec, w_mod, b_mod.reshape(DEPTH, 1, n))


def _half_ffn(x, mod, g, w1_ref, w3_ref, w2_ref):
    h = _rmsnorm(x, g[0:1]) * (1.0 + mod[1:2]) + mod[0:1]
    hb = h.astype(BF16)
    h1 = jnp.dot(hb, w1_ref[...], preferred_element_type=F32)
    h3 = jnp.dot(hb, w3_ref[...], preferred_element_type=F32)
    yield
    y = jnp.dot((_silu(h1) * h3).astype(BF16), w2_ref[...], preferred_element_type=F32)
    yield
    return x + 0.5 * mod[2:3] * _rmsnorm(y, g[1:2])


def _row_parts(tm):
    part = tm // ROW_PARTS
    return [slice(j * part, (j + 1) * part) for j in range(ROW_PARTS)]


def _ffn_inproj_kernel(x_ref, mod_ref, g_ref, w1_ref, w3_ref, w2_ref, win_ref, x_out_ref, um_ref, us_ref):
    mod = mod_ref[0]
    g = g_ref[...]

    def part(rows):
        x = yield from _half_ffn(x_ref[rows, :], mod[0:3], g[0:2], w1_ref, w3_ref, w2_ref)
        x_out_ref[rows, :] = x
        h = _rmsnorm(x, g[2:3]) * (1.0 + mod[4:5]) + mod[3:4]
        u = jnp.dot(h.astype(BF16), win_ref[...], preferred_element_type=F32)
        yield
        um_ref[rows, :] = u[:, :MAIN_COLS]
        us_ref[rows, :] = u[:, MAIN_COLS:]

    _interleave([part(rows) for rows in _row_parts(x_ref.shape[0])])


def _outproj_ffn_kernel(x_ref, ma_ref, mb_ref, mc_ref, md_ref, mod_ref, g_ref, wo_ref, w1_ref, w3_ref, w2_ref,
                        o_ref):
    mod = mod_ref[0]
    g = g_ref[...]

    def part(rows):
        mix = jnp.concatenate([ma_ref[rows, :], mb_ref[rows, :], mc_ref[rows, :], md_ref[rows, :]], axis=-1)
        y = jnp.dot(mix.astype(BF16), wo_ref[...], preferred_element_type=F32)
        yield
        x = x_ref[rows, :] + mod[2:3] * _rmsnorm(y, g[0:1])
        o_ref[rows, :] = yield from _half_ffn(x, mod[3:6], g[1:3], w1_ref, w3_ref, w2_ref)

    _interleave([part(rows) for rows in _row_parts(x_ref.shape[0])])


def _row_tile(n_mod, rows_per_batch, tile):
    return tile if n_mod == 1 else min(tile, rows_per_batch)


def _mod_index(n_mod, rows_per_batch, tm):
    if n_mod == 1:
        return lambda i: (0, 0, 0)
    per = rows_per_batch // tm
    return lambda i: (i // per, 0, 0)


_ROW_PARAMS = pltpu.CompilerParams(dimension_semantics=("parallel",), vmem_limit_bytes=VMEM_LIMIT)


def _ffn_weight_specs(layer, half):
    pick = lambda i: (layer, half, 0, 0)
    return [pl.BlockSpec((None, None, D_MODEL, D_FF), pick, pipeline_mode=pl.Buffered(1)),
            pl.BlockSpec((None, None, D_MODEL, D_FF), pick, pipeline_mode=pl.Buffered(1)),
            pl.BlockSpec((None, None, D_FF, D_MODEL), pick, pipeline_mode=pl.Buffered(1))]


def _ffn_inproj(x, mod6, g3, w1, w3, w2, w_big, layer, rows_per_batch):
    rows = x.shape[0]
    tm = _row_tile(mod6.shape[0], rows_per_batch, ROW_TILE_IN)
    row_spec = lambda width: pl.BlockSpec((tm, width), lambda i: (i, 0))
    return pl.pallas_call(
        _ffn_inproj_kernel,
        grid=(rows // tm,),
        in_specs=[
            row_spec(D_MODEL),
            pl.BlockSpec((1, 6, D_MODEL), _mod_index(mod6.shape[0], rows_per_batch, tm)),
            pl.BlockSpec((3, D_MODEL), lambda i: (0, 0)),
            *_ffn_weight_specs(layer, 0),
            pl.BlockSpec((None, D_MODEL, MAIN_COLS + SMALL_LANES), lambda i: (layer, 0, 0),
                         pipeline_mode=pl.Buffered(1)),
        ],
        out_specs=[row_spec(D_MODEL), row_spec(MAIN_COLS), row_spec(SMALL_LANES)],
        out_shape=[jax.ShapeDtypeStruct((rows, D_MODEL), F32), jax.ShapeDtypeStruct((rows, MAIN_COLS), F32),
                   jax.ShapeDtypeStruct((rows, SMALL_LANES), F32)],
        compiler_params=_ROW_PARAMS,
        name="ffn_inproj",
    )(x, mod6, g3, w1, w3, w2, w_big)


def _outproj_ffn(x, mixes, mod6, g3, w_out, w1, w3, w2, layer, rows_per_batch):
    rows = x.shape[0]
    tm = _row_tile(mod6.shape[0], rows_per_batch, ROW_TILE_OUT)
    row_spec = lambda width: pl.BlockSpec((tm, width), lambda i: (i, 0))
    return pl.pallas_call(
        _outproj_ffn_kernel,
        grid=(rows // tm,),
        in_specs=[
            row_spec(D_MODEL), *[row_spec(GROUP_WIDTH)] * 4,
            pl.BlockSpec((1, 6, D_MODEL), _mod_index(mod6.shape[0], rows_per_batch, tm)),
            pl.BlockSpec((3, D_MODEL), lambda i: (0, 0)),
            pl.BlockSpec((None, D_MODEL, D_MODEL), lambda i: (layer, 0, 0), pipeline_mode=pl.Buffered(1)),
            *_ffn_weight_specs(layer, 1),
        ],
        out_specs=row_spec(D_MODEL),
        out_shape=jax.ShapeDtypeStruct((rows, D_MODEL), F32),
        compiler_params=_ROW_PARAMS,
        name="outproj_ffn",
    )(x, *mixes, mod6, g3, w_out, w1, w3, w2)


def _mixer_specs(mixer, seq):
    gl = GROUP_LANES
    u_spec = pl.BlockSpec((seq, 4 * gl), lambda b, g: (b, mixer * N_GROUPS + g))
    small_spec = pl.BlockSpec((seq, SMALL_LANES), lambda b, g: (b, 0))
    out_spec = pl.BlockSpec((seq, gl), lambda b, g: (b, g))
    return u_spec, small_spec, out_spec


def _state_spec(rows):
    return pl.BlockSpec((1, 1, 2, rows, GROUP_LANES), lambda b, g: (b, g, 0, 0, 0))


def _head_state_shape(batch):
    return jax.ShapeDtypeStruct((batch, 2, N_HEADS, HEAD_DIM, HEAD_DIM), F32)


_HEAD_STATE_SPEC = pl.BlockSpec((1, 2, HEADS_PER_GROUP, HEAD_DIM, HEAD_DIM), lambda b, g: (b, 0, g, 0, 0))


def _store_head_blocks(ref, d, mat):
    for h in range(HEADS_PER_GROUP):
        lo = h * HEAD_DIM
        ref[0, d, h] = mat[lo:lo + HEAD_DIM, lo:lo + HEAD_DIM]


_MIXER_PARAMS = pltpu.CompilerParams(dimension_semantics=("parallel", "parallel"), vmem_limit_bytes=VMEM_LIMIT)


def _chunk_rows(c):
    return pl.ds(pl.multiple_of(c * CHUNK, CHUNK), CHUNK)


def _scan_order(t, n_chunks, reverse):
    return (n_chunks - 1 - t) if reverse else t


def _tile_row(row):
    return jnp.broadcast_to(row, (8, row.shape[1]))


def _edge_row(ref, c, reverse):
    if reverse:
        return ref[pl.ds(pl.multiple_of(c * CHUNK, 8), 8), :][0:1]
    return ref[pl.ds(pl.multiple_of(c * CHUNK + CHUNK - 8, 8), 8), :][7:8]


def _chunk_scan(x, reverse, op):
    n = x.shape[0]
    row = _iota(x.shape, 0) & (CHUNK - 1)
    s = 1
    while s < CHUNK:
        if reverse:
            shifted = pltpu.roll(x, n - s, 0)
            ok = row < CHUNK - s
        else:
            shifted = pltpu.roll(x, s, 0)
            ok = row >= s
        x = jnp.where(ok, op(x, shifted), x)
        s *= 2
    return x


def _row_loop(seq, fn):
    blk = min(seq, PRE_ROWS)
    k = min(seq // blk, ROWS_IN_FLIGHT)

    def body(i, c):
        _interleave([fn(pl.ds(pl.multiple_of((i * k + j) * blk, blk), blk)) for j in range(k)])
        return c

    if seq == blk * k:
        _interleave([fn(pl.ds(j * blk, blk)) for j in range(k)])
    else:
        lax.fori_loop(0, seq // (blk * k), body, 0)


def _chunk_loop(n_chunks, make_stream):
    k = min(n_chunks, CHUNKS_IN_FLIGHT)

    def body(t, carry):
        _interleave([make_stream(t * k + j) for j in range(k)])
        return carry

    if n_chunks == k:
        body(0, 0)
    else:
        lax.fori_loop(0, n_chunks // k, body, 0)


def _interleave(streams):
    results = [None] * len(streams)
    live = []
    for i, s in enumerate(streams):
        if hasattr(s, "send"):
            live.append(i)
        else:
            results[i] = s
    while live:
        for i in list(live):
            try:
                next(streams[i])
            except StopIteration as done:
                results[i] = done.value
                live.remove(i)
    return results


def _state_scan(n_chunks, reverse, s_ref, decay_ref, init):
    def body(t, s):
        c = _scan_order(t, n_chunks, reverse)
        inc = s_ref[c]
        s_ref[c] = s
        return decay_ref[c][0:1] * s + inc

    return lax.fori_loop(0, n_chunks, body, init)


def _finalize(seq, acc_ref, o_ref, fn):
    def blk(rows):
        res = fn(rows, acc_ref[rows, :] + o_ref[rows, :])
        if hasattr(res, "send"):
            res = yield from res
        o_ref[rows, :] = res

    _row_loop(seq, blk)


def _rope(x, cos, sin_signed):
    lane = _iota(x.shape, 1)
    first = (lane & 31) < 16
    swapped = jnp.where(first, pltpu.roll(x, GROUP_LANES - 16, 1), pltpu.roll(x, 16, 1))
    return x * cos + swapped * sin_signed


def _ret_kernel(dec_ref, nw_ref, u_ref, *rest, seq, rope, has_state, emit_state):
    rest = list(rest)
    cos_ref, sin_ref = (rest.pop(0), rest.pop(0)) if rope else (None, None)
    s0_ref = rest.pop(0) if has_state else None
    o_ref = rest.pop(0)
    st_ref = rest.pop(0) if emit_state else None
    acc_ref, q_ref, k_ref, s_ref, cdec_ref = rest

    gl, L, G = GROUP_LANES, CHUNK, HEADS_PER_GROUP
    n_chunks = seq // L
    g = pl.program_id(1)
    masks = _head_masks()
    bd = _bd_mask()
    bd_avg = bd * (1.0 / HEAD_DIM)
    scale = HEAD_DIM ** -0.5
    ri = _iota((L, gl), 0).astype(F32)
    si = _iota((G * L, L), 0)
    sj = _iota((G * L, L), 1)
    i_idx = si & (L - 1)
    hrow = _head_of(_iota((G * L, 1), 0))

    def prepare(rows):
        q = u_ref[rows, 0:gl]
        k = u_ref[rows, gl:2 * gl]
        if rope:
            cos = cos_ref[rows, :]
            sin = sin_ref[rows, :]
            q = _rope(q, cos, sin)
            k = _rope(k, cos, sin)
        q_ref[rows, :] = q * scale
        k_ref[rows, :] = k

    _row_loop(seq, prepare)

    for d in (0, 1):
        reverse = d == 1
        raw_row = masks[0] * dec_ref[d, g * G]
        raw_col = jnp.where(hrow == 0, dec_ref[d, g * G], 0.0)
        for h in range(1, G):
            raw_row = raw_row + masks[h] * dec_ref[d, g * G + h]
            raw_col = jnp.where(hrow == h, dec_ref[d, g * G + h], raw_col)
        lg_row = _log_sigmoid(raw_row)
        lg_col = _log_sigmoid(raw_col)
        diff = ((sj - i_idx) if reverse else (i_idx - sj)).astype(F32)
        dmat = jnp.where(diff >= 0, jnp.exp(jnp.maximum(diff, 0.0) * lg_col), 0.0)
        qdec = jnp.exp(((L - ri) if reverse else (ri + 1.0)) * lg_row)
        kdec = jnp.exp((ri if reverse else (L - 1.0 - ri)) * lg_row)
        cdec_tile = _tile_row(jnp.exp(float(L) * lg_row))
        out_ref = o_ref if reverse else acc_ref

        def increments(c, kdec=kdec, cdec_tile=cdec_tile):
            rows = _chunk_rows(c)
            inc = _mm_tn(k_ref[rows, :] * kdec, u_ref[rows, 2 * gl:3 * gl])
            yield
            s_ref[c] = bd * inc
            cdec_ref[c] = cdec_tile

        _chunk_loop(n_chunks, increments)
        s_fin = _state_scan(n_chunks, reverse, s_ref, cdec_ref,
                            s0_ref[0, 0, d] if has_state else jnp.zeros((gl, gl), F32))
        if emit_state:
            _store_head_blocks(st_ref, d, s_fin)

        def outputs(c, dmat=dmat, qdec=qdec, out_ref=out_ref):
            rows = _chunk_rows(c)
            q = q_ref[rows, :]
            v = u_ref[rows, 2 * gl:3 * gl]
            att = _mm_nt(_stack(q, masks), k_ref[rows, :])
            inter = _mm(q * qdec, s_ref[c])
            yield
            intra = _mm(att * dmat, v)
            yield
            out_ref[rows, :] = _unstack(intra, masks, L) + inter

        _chunk_loop(n_chunks, outputs)

    def finalize(rows, o):
        mu = _mm_split(o, bd_avg)
        yield
        dev = o - mu
        var = _mm_split(dev * dev, bd_avg)
        yield
        return dev * lax.rsqrt(var + EPS) * nw_ref[...] * _silu(u_ref[rows, 3 * gl:4 * gl])

    _finalize(seq, acc_ref, o_ref, finalize)


def _ret_tables(seq):
    t = np.arange(seq)
    quarter = HEAD_DIM // 4
    inv = (ROPE_BASE ** (-np.arange(quarter, dtype=np.float32) / quarter)).astype(np.float32)
    ang_r = (t // GRID_W).astype(np.float32)[:, None] * inv
    ang_c = (t % GRID_W).astype(np.float32)[:, None] * inv
    cos = np.concatenate([np.cos(ang_r), np.cos(ang_r), np.cos(ang_c), np.cos(ang_c)], axis=1)
    sin = np.concatenate([-np.sin(ang_r), np.sin(ang_r), -np.sin(ang_c), np.sin(ang_c)], axis=1)
    tile = lambda a: jnp.asarray(np.tile(a.astype(np.float32), (1, HEADS_PER_GROUP)))
    return tile(cos), tile(sin)


def _retention(u_main, ret_decay, norm_w, batch, seq, state):
    latent = state is not None
    u_spec, _, out_spec = _mixer_specs(3, seq)
    gl = GROUP_LANES
    const2 = lambda b, g: (0, 0)
    in_specs = [pl.BlockSpec(memory_space=pltpu.SMEM), pl.BlockSpec((1, gl), const2), u_spec]
    args = [ret_decay, jnp.tile(norm_w.reshape(1, HEAD_DIM), (1, HEADS_PER_GROUP)), u_main]
    if latent:
        cos, sin = _ret_tables(seq)
        in_specs += [pl.BlockSpec((seq, gl), const2), pl.BlockSpec((seq, gl), const2), _state_spec(gl)]
        args += [cos, sin, state]
    out_shape = [jax.ShapeDtypeStruct((batch * seq, GROUP_WIDTH), F32)]
    out_specs = [out_spec]
    if not latent:
        out_shape.append(_head_state_shape(batch))
        out_specs.append(_HEAD_STATE_SPEC)
    res = pl.pallas_call(
        functools.partial(_ret_kernel, seq=seq, rope=latent, has_state=latent, emit_state=not latent),
        grid=(batch, N_GROUPS),
        in_specs=in_specs,
        out_specs=out_specs,
        out_shape=out_shape,
        scratch_shapes=[pltpu.VMEM((seq, gl), F32), pltpu.VMEM((seq, gl), F32), pltpu.VMEM((seq, gl), F32),
                        pltpu.VMEM((seq // CHUNK, gl, gl), F32), pltpu.VMEM((seq // CHUNK, 8, gl), F32)],
        compiler_params=_MIXER_PARAMS,
        name="retention",
    )(*args)
    return res[0], (None if latent else res[1])


def _gla_kernel(wa_ref, ba_ref, nw_ref, u_ref, us_ref, *rest, seq, has_state, emit_state):
    rest = list(rest)
    s0_ref = rest.pop(0) if has_state else None
    o_ref = rest.pop(0)
    st_ref = rest.pop(0) if emit_state else None
    acc_ref, b_ref, s_ref, dec_ref, steep_ref = rest

    gl, L, G, SB = GROUP_LANES, CHUNK, HEADS_PER_GROUP, GLA_SUB
    n_chunks = seq // L
    n_sub = L // SB
    masks = _head_masks()
    bd = _bd_mask()
    bd_avg = bd * (1.0 / HEAD_DIM)
    scale = HEAD_DIM ** -0.5
    half = SB // 2
    pair_row = _iota((SB * SB, gl), 0)
    pi = lax.shift_right_logical(pair_row, 4)
    pj = pair_row & (SB - 1)
    pair_sum = (lax.shift_right_logical(_iota((SB, SB * SB), 1), 4) == _iota((SB, SB * SB), 0)).astype(BF16)
    zeros_half = jnp.zeros((half, gl), F32)

    for d in (0, 1):
        reverse = d == 1
        pair_ok = (pj >= pi) if reverse else (pj <= pi)
        out_ref = o_ref if reverse else acc_ref

        steep_ref[...] = jnp.zeros((8, gl), F32)

        def log_decay(rows, d=d, reverse=reverse):
            a_pre = _mm(us_ref[rows, :], wa_ref[d]) + ba_ref[d]
            yield
            la = _log_sigmoid(a_pre) * (LOG2E / GLA_TAU)
            b_ref[rows, :] = _chunk_scan(la, reverse, jnp.add)
            steep_ref[...] = jnp.maximum(steep_ref[...], jnp.max((-la).reshape(-1, 8, gl), axis=0))

        _row_loop(seq, log_decay)
        factorise_all = jnp.max(steep_ref[...]) * (SB - 1) <= GLA_SAFE_LOG2

        def increments(c, reverse=reverse):
            rows = _chunk_rows(c)
            b = b_ref[rows, :]
            btot = b[0:1] if reverse else b[L - 1:L]
            inc = _mm_tn(u_ref[rows, 2 * gl:3 * gl], u_ref[rows, gl:2 * gl] * jnp.exp2(btot - b))
            yield
            s_ref[c] = bd * inc
            dec_ref[c] = _tile_row(jnp.exp2(btot))

        _chunk_loop(n_chunks, increments)
        s_fin = _state_scan(n_chunks, reverse, s_ref, dec_ref,
                            s0_ref[0, 0, d] if has_state else jnp.zeros((gl, gl), F32))
        if emit_state:
            _store_head_blocks(st_ref, d, s_fin.T)

        def outputs_factorised(c, reverse=reverse, out_ref=out_ref):
            rows = _chunk_rows(c)
            b = b_ref[rows, :]
            q = u_ref[rows, 0:gl] * scale
            k = u_ref[rows, gl:2 * gl]
            v = u_ref[rows, 2 * gl:3 * gl]
            o_inter = _mm_nt(q * jnp.exp2(b), s_ref[c])
            atts = []
            for i_blk in range(n_sub):
                lo = i_blk * SB
                bi = b[lo:lo + SB]
                ref, keys = (bi[SB - 1:SB], slice(lo, L)) if reverse else (bi[0:1], slice(0, lo + SB))
                qd = q[lo:lo + SB] * jnp.exp2(bi - ref)
                kd = k[keys] * jnp.exp2(ref - b[keys])
                att = _mm_nt(_stack(qd, masks), kd)
                n_keys = keys.stop - keys.start
                qrow = _iota((G * SB, n_keys), 0) & (SB - 1)
                kcol = _iota((G * SB, n_keys), 1)
                ok = (kcol >= qrow) if reverse else (kcol <= qrow + lo)
                atts.append((jnp.where(ok, att, 0.0), keys))
            yield
            outs = [_mm(a, v[keys]) for a, keys in atts]
            yield
            out_ref[rows, :] = o_inter + jnp.concatenate([_unstack(o, masks, SB) for o in outs], axis=0)

        def outputs_direct(c, reverse=reverse, pair_ok=pair_ok, out_ref=out_ref):
            rows = _chunk_rows(c)
            b = b_ref[rows, :]
            q = u_ref[rows, 0:gl] * scale
            k = u_ref[rows, gl:2 * gl]
            v = u_ref[rows, 2 * gl:3 * gl]
            o_inter = _mm_nt(q * jnp.exp2(b), s_ref[c])
            diag, off = [], []
            for i_blk in range(n_sub):
                lo = i_blk * SB
                qi, ki, bi = q[lo:lo + SB], k[lo:lo + SB], b[lo:lo + SB]
                prods = []
                for i in range(SB):
                    if reverse and i >= half:
                        part = slice(half, SB)
                    elif (not reverse) and i < half:
                        part = slice(0, half)
                    else:
                        part = slice(0, SB)
                    p = (qi[i:i + 1] * ki[part]) * jnp.exp2(bi[i:i + 1] - bi[part])
                    if part.start == half:
                        prods += [zeros_half, p]
                    elif part.stop == half:
                        prods += [p, zeros_half]
                    else:
                        prods.append(p)
                diag.append(_mm(jnp.concatenate(prods, axis=0), bd))
                if reverse and i_blk < n_sub - 1:
                    ref, other = bi[SB - 1:SB], slice(lo + SB, L)
                elif (not reverse) and i_blk > 0:
                    ref, other = bi[0:1], slice(0, lo)
                else:
                    off.append(None)
                    continue
                qd = qi * jnp.exp2(bi - ref)
                kd = k[other] * jnp.exp2(ref - b[other])
                off.append((_mm_nt(_stack(qd, masks), kd), other))
            yield
            off = [None if a is None else _mm(a[0], v[a[1]]) for a in off]
            for i_blk in range(n_sub):
                vi = v[i_blk * SB:(i_blk + 1) * SB]
                weighted = jnp.where(pair_ok, diag[i_blk] * jnp.concatenate([vi] * SB, axis=0), 0.0)
                diag[i_blk] = jnp.dot(pair_sum, weighted.astype(BF16), preferred_element_type=F32)
            yield
            blocks = []
            for i_blk in range(n_sub):
                o_blk = diag[i_blk]
                if off[i_blk] is not None:
                    o_blk = o_blk + _unstack(off[i_blk], masks, SB)
                blocks.append(o_blk)
            out_ref[rows, :] = o_inter + jnp.concatenate(blocks, axis=0)

        @pl.when(factorise_all)
        def _():
            _chunk_loop(n_chunks, outputs_factorised)

        @pl.when(jnp.logical_not(factorise_all))
        def _():
            _chunk_loop(n_chunks, outputs_direct)

    def finalize(rows, o):
        ms = _mm_split(o * o, bd_avg)
        yield
        return o * lax.rsqrt(ms + EPS) * nw_ref[...] * _silu(u_ref[rows, 3 * gl:4 * gl])

    _finalize(seq, acc_ref, o_ref, finalize)


def _gla(u_main, u_small, wa_pad, b_a, norm_w, batch, seq, state):
    latent = state is not None
    u_spec, small_spec, out_spec = _mixer_specs(0, seq)
    gl = GROUP_LANES
    in_specs = [
        pl.BlockSpec((2, SMALL_LANES, gl), lambda b, g: (0, 0, g)),
        pl.BlockSpec((2, 1, gl), lambda b, g: (0, 0, g)),
        pl.BlockSpec((1, gl), lambda b, g: (0, 0)),
        u_spec, small_spec,
    ]
    args = [wa_pad, b_a.reshape(2, 1, GROUP_WIDTH),
            jnp.tile(norm_w.reshape(1, HEAD_DIM), (1, HEADS_PER_GROUP)), u_main, u_small]
    if latent:
        in_specs.append(_state_spec(gl))
        args.append(state)
    out_shape = [jax.ShapeDtypeStruct((batch * seq, GROUP_WIDTH), F32)]
    out_specs = [out_spec]
    if not latent:
        out_shape.append(_head_state_shape(batch))
        out_specs.append(_HEAD_STATE_SPEC)
    res = pl.pallas_call(
        functools.partial(_gla_kernel, seq=seq, has_state=latent, emit_state=not latent),
        grid=(batch, N_GROUPS),
        in_specs=in_specs,
        out_specs=out_specs,
        out_shape=out_shape,
        scratch_shapes=[pltpu.VMEM((seq, gl), F32), pltpu.VMEM((seq, gl), F32),
                        pltpu.VMEM((seq // CHUNK, gl, gl), F32), pltpu.VMEM((seq // CHUNK, 8, gl), F32),
                        pltpu.VMEM((8, gl), F32)],
        compiler_params=_MIXER_PARAMS,
        name="gla",
    )(*args)
    return res[0], (None if latent else res[1])


def _mlstm_kernel(ei_ref, ef_ref, bi_ref, bf_ref, u_ref, us_ref, *rest, seq, has_state, emit_state):
    rest = list(rest)
    if has_state:
        c0_ref, n0_ref, m0_ref = rest.pop(0), rest.pop(0), rest.pop(0)
    o_ref = rest.pop(0)
    if emit_state:
        cst_ref, nst_ref, mst_ref = rest.pop(0), rest.pop(0), rest.pop(0)
    acc_ref, f_ref, x_ref, p_ref, s_ref, nu_ref, wp_ref, min_ref, mout_ref = rest

    gl, L, G = GROUP_LANES, CHUNK, HEADS_PER_GROUP
    n_chunks = seq // L
    masks = _head_masks()
    bd = _bd_mask()
    scale = HEAD_DIM ** -0.5
    si = _iota((G * L, L), 0) & (L - 1)
    sj = _iota((G * L, L), 1)
    lane_in_head = _iota((L, gl), 1) & (HEAD_DIM - 1)

    for d in (0, 1):
        reverse = d == 1
        causal = (sj >= si) if reverse else (sj <= si)
        last = 0 if reverse else L - 1
        out_ref = o_ref if reverse else acc_ref

        def gates(rows, d=d, reverse=reverse):
            us = us_ref[rows, :]
            ig = _mm_split(us, ei_ref[d, 0]) + bi_ref[d, 0]
            fg = _mm_split(us, ef_ref[d, 0]) + bf_ref[d, 0]
            yield
            f_sum = _chunk_scan(_log_sigmoid(fg), reverse, jnp.add)
            x = ig - f_sum
            f_ref[rows, :] = f_sum
            x_ref[rows, :] = x
            p_ref[rows, :] = _chunk_scan(x, reverse, jnp.maximum)

        _row_loop(seq, gates)

        def stabiliser(t, m_row, reverse=reverse):
            c = _scan_order(t, n_chunks, reverse)
            m_new = _edge_row(f_ref, c, reverse) + jnp.maximum(m_row, _edge_row(p_ref, c, reverse))
            min_ref[c] = _tile_row(m_row)
            mout_ref[c] = _tile_row(m_new)
            return m_new

        m_fin = lax.fori_loop(0, n_chunks, stabiliser,
                              m0_ref[0, 0, d] if has_state else jnp.zeros((1, gl), F32))

        def increments(c, last=last):
            rows = _chunk_rows(c)
            f_tot = f_ref[rows, :][last:last + 1]
            m_in = min_ref[c][0:1]
            m_out = mout_ref[c][0:1]
            kw = u_ref[rows, gl:2 * gl] * jnp.exp(f_tot + x_ref[rows, :] - m_out)
            inc = _mm_tn(kw, u_ref[rows, 2 * gl:3 * gl])
            yield
            s_ref[c] = bd * inc
            nu_ref[c] = _tile_row(jnp.sum(kw, axis=0, keepdims=True))
            wp_ref[c] = _tile_row(jnp.exp(f_tot + m_in - m_out))

        _chunk_loop(n_chunks, increments)

        def memory(t, carry, reverse=reverse):
            c_mat, n_row = carry
            c = _scan_order(t, n_chunks, reverse)
            inc = s_ref[c]
            nu = nu_ref[c][0:1]
            wp = wp_ref[c][0:1]
            s_ref[c] = c_mat
            nu_ref[c] = _tile_row(n_row)
            return wp * c_mat + inc, wp * n_row + nu

        if has_state:
            init = (c0_ref[0, 0, d], n0_ref[0, 0, d])
        else:
            init = (jnp.zeros((gl, gl), F32), jnp.zeros((1, gl), F32))
        c_fin, n_fin = lax.fori_loop(0, n_chunks, memory, init)
        if emit_state:
            _store_head_blocks(cst_ref, d, c_fin)
            nst_ref[0, 0, d], mst_ref[0, 0, d] = n_fin, m_fin

        def outputs(c, causal=causal, out_ref=out_ref):
            rows = _chunk_rows(c)
            x = x_ref[rows, :]
            m_in = min_ref[c][0:1]
            q = u_ref[rows, 0:gl] * scale
            k = u_ref[rows, gl:2 * gl]
            v = u_ref[rows, 2 * gl:3 * gl]
            m_full = jnp.maximum(m_in, p_ref[rows, :])
            lhs = _split_lanes(-m_full, lane_in_head, 0, 1.0, 3)
            rhs = _split_lanes(x, lane_in_head, 3, 1.0, 0)
            logw = _mm_nt(_stack(lhs, masks), rhs)
            qk = _mm_nt(_stack(q, masks), k)
            inter = _mm(q, s_ref[c])
            q_n = _mm_split(q * nu_ref[c][0:1], bd)
            yield
            s = qk * jnp.exp(jnp.where(causal, logw, NEG))
            s_hi = s.astype(BF16)
            ones = jnp.ones((L, gl), BF16)
            r = jnp.dot(s_hi, jnp.concatenate([v.astype(BF16), ones], axis=1), preferred_element_type=F32)
            sum_lo = jnp.dot((s - s_hi.astype(F32)).astype(BF16), ones, preferred_element_type=F32)
            yield
            w_inter = jnp.exp(m_in - m_full)
            num = _unstack(r[:, :gl], masks, L) + w_inter * inter
            den = _unstack(r[:, gl:] + sum_lo, masks, L) + w_inter * q_n
            out_ref[rows, :] = num / jnp.maximum(jnp.abs(den), jnp.exp(-(f_ref[rows, :] + m_full)))

        _chunk_loop(n_chunks, outputs)

    def finalize(rows, o):
        return o * jax.nn.sigmoid(u_ref[rows, 3 * gl:4 * gl])

    _finalize(seq, acc_ref, o_ref, finalize)


def _gate_expanders(offset):
    e = np.zeros((2, N_GROUPS, SMALL_LANES, GROUP_LANES), np.float32)
    for d in range(2):
        for g in range(N_GROUPS):
            for h in range(HEADS_PER_GROUP):
                src = offset + d * N_HEADS + g * HEADS_PER_GROUP + h
                e[d, g, src, h * HEAD_DIM:(h + 1) * HEAD_DIM] = 1.0
    return jnp.asarray(e)


def _gate_bias(b):
    return jnp.repeat(b.astype(F32), HEAD_DIM, axis=1).reshape(2, N_GROUPS, 1, GROUP_LANES)


def _mlstm(u_main, u_small, b_i, b_f, batch, seq, state):
    latent = state is not None
    u_spec, small_spec, out_spec = _mixer_specs(2, seq)
    gl = GROUP_LANES
    exp_spec = pl.BlockSpec((2, 1, SMALL_LANES, gl), lambda b, g: (0, g, 0, 0))
    bias_spec = pl.BlockSpec((2, 1, 1, gl), lambda b, g: (0, g, 0, 0))
    in_specs = [exp_spec, exp_spec, bias_spec, bias_spec, u_spec, small_spec]
    args = [_gate_expanders(MI_OFF), _gate_expanders(MF_OFF), _gate_bias(b_i), _gate_bias(b_f), u_main, u_small]
    if latent:
        in_specs += [_state_spec(gl), _state_spec(1), _state_spec(1)]
        args += list(state)
    out_shape = [jax.ShapeDtypeStruct((batch * seq, GROUP_WIDTH), F32)]
    out_specs = [out_spec]
    if not latent:
        out_shape += [_head_state_shape(batch),
                      jax.ShapeDtypeStruct((batch, N_GROUPS, 2, 1, gl), F32),
                      jax.ShapeDtypeStruct((batch, N_GROUPS, 2, 1, gl), F32)]
        out_specs += [_HEAD_STATE_SPEC, _state_spec(1), _state_spec(1)]
    res = pl.pallas_call(
        functools.partial(_mlstm_kernel, seq=seq, has_state=latent, emit_state=not latent),
        grid=(batch, N_GROUPS),
        in_specs=in_specs,
        out_specs=out_specs,
        out_shape=out_shape,
        scratch_shapes=[pltpu.VMEM((seq, gl), F32)] * 4 + [pltpu.VMEM((seq // CHUNK, gl, gl), F32)]
        + [pltpu.VMEM((seq // CHUNK, 8, gl), F32)] * 4,
        compiler_params=_MIXER_PARAMS,
        name="mlstm",
    )(*args)
    return res[0], (None if latent else tuple(res[1:]))


def _na_ctx_kernel(u_ref, o_ref, ko_ref, vo_ref, *, seq):
    gl = GROUP_LANES
    masks = _head_masks()
    q = u_ref[:, 0:gl] * (HEAD_DIM ** -0.5)
    k = u_ref[:, gl:2 * gl]
    v = u_ref[:, 2 * gl:3 * gl]
    s = _mm_nt(_stack(q, masks), k)
    p = jnp.exp(s - jnp.max(s, axis=1, keepdims=True))
    o = _mm(p, v) / jnp.sum(p, axis=1, keepdims=True)
    o_ref[...] = _unstack(o, masks, seq)
    for h in range(HEADS_PER_GROUP):
        ko_ref[0, h] = k[:, h * HEAD_DIM:(h + 1) * HEAD_DIM]
        vo_ref[0, h] = v[:, h * HEAD_DIM:(h + 1) * HEAD_DIM]


def _na_context(u_main, batch, seq):
    u_spec, _, out_spec = _mixer_specs(1, seq)
    kv_shape = jax.ShapeDtypeStruct((batch, N_HEADS, seq, HEAD_DIM), F32)
    kv_spec = pl.BlockSpec((1, HEADS_PER_GROUP, seq, HEAD_DIM), lambda b, g: (b, g, 0, 0))
    return pl.pallas_call(
        functools.partial(_na_ctx_kernel, seq=seq),
        grid=(batch, N_GROUPS),
        in_specs=[u_spec],
        out_specs=[out_spec, kv_spec, kv_spec],
        out_shape=[jax.ShapeDtypeStruct((batch * seq, GROUP_WIDTH), F32), kv_shape, kv_shape],
        compiler_params=_MIXER_PARAMS,
        name="na_context",
    )(u_main)


def _na_lat_kernel(tab_ref, kc_ref, vc_ref, u_ref, o_ref, *, seq):
    gl = GROUP_LANES
    grid_rows = seq // GRID_W
    win = NA_KH * GRID_W
    masks = _head_masks()
    scale = HEAD_DIM ** -0.5
    kc = kc_ref[0, 0]
    vc = vc_ref[0, 0]

    def grid_row(r):
        ks = jnp.clip(r - NA_KH // 2, 0, grid_rows - NA_KH)
        q_rows = pl.ds(pl.multiple_of(r * GRID_W, GRID_W), GRID_W)
        k_rows = pl.ds(pl.multiple_of(ks * GRID_W, GRID_W), win)
        qs = _stack(u_ref[q_rows, 0:gl] * scale, masks)
        s_loc = _mm_nt(qs, u_ref[k_rows, gl:2 * gl])
        s_ctx = _mm_nt(qs, kc)
        yield
        s_loc = s_loc + tab_ref[0, ks - r + NA_KH - 1]
        m = jnp.maximum(jnp.max(s_loc, axis=1, keepdims=True), jnp.max(s_ctx, axis=1, keepdims=True))
        p_loc = jnp.exp(s_loc - m)
        p_ctx = jnp.exp(s_ctx - m)
        den = jnp.sum(p_loc, axis=1, keepdims=True) + jnp.sum(p_ctx, axis=1, keepdims=True)
        o = _mm(p_loc, u_ref[k_rows, 2 * gl:3 * gl]) + _mm(p_ctx, vc)
        yield
        o_ref[q_rows, :] = _unstack(o / den, masks, GRID_W)

    def body(t, carry):
        _interleave([grid_row(t * NA_ROWS_IN_FLIGHT + j) for j in range(NA_ROWS_IN_FLIGHT)])
        return carry

    lax.fori_loop(0, grid_rows // NA_ROWS_IN_FLIGHT, body, 0)


def _na_bias_table(rpb):
    c = np.arange(GRID_W)
    rel = c[None, :] - c[:, None]
    cs = np.clip(c - NA_KW // 2, 0, GRID_W - NA_KW)
    valid = (c[None, :] >= cs[:, None]) & (c[None, :] < cs[:, None] + NA_KW)
    dcol = np.clip(rel, -(NA_KW - 1), NA_KW - 1) + NA_KW - 1
    bmat = jnp.where(valid[None, None], rpb.astype(F32)[:, :, dcol], NEG)
    tabs = jnp.stack([bmat[:, s:s + NA_KH] for s in range(NA_KH)], axis=1)
    tabs = tabs.transpose(0, 1, 3, 2, 4).reshape(N_GROUPS, HEADS_PER_GROUP, NA_KH, GRID_W, NA_KH * GRID_W)
    return tabs.transpose(0, 2, 1, 3, 4).reshape(N_GROUPS, NA_KH, HEADS_PER_GROUP * GRID_W, NA_KH * GRID_W)


def _heads_to_lanes(t):
    b, _, s, _ = t.shape
    t = t.reshape(b, N_GROUPS, HEADS_PER_GROUP, s, HEAD_DIM).transpose(0, 1, 3, 2, 4)
    return t.reshape(b, N_GROUPS, s, GROUP_LANES)


def _na_latent(u_main, rpb, k_ctx, v_ctx, batch, seq):
    u_spec, _, out_spec = _mixer_specs(1, seq)
    gl = GROUP_LANES
    past = k_ctx.shape[2]
    tab = _na_bias_table(rpb)
    ctx_spec = pl.BlockSpec((1, 1, past, gl), lambda b, g: (b, g, 0, 0))
    return pl.pallas_call(
        functools.partial(_na_lat_kernel, seq=seq),
        grid=(batch, N_GROUPS),
        in_specs=[pl.BlockSpec((1,) + tab.shape[1:], lambda b, g: (g, 0, 0, 0)), ctx_spec, ctx_spec, u_spec],
        out_specs=out_spec,
        out_shape=jax.ShapeDtypeStruct((batch * seq, GROUP_WIDTH), F32),
        compiler_params=_MIXER_PARAMS,
        name="na_latent",
    )(tab, _heads_to_lanes(k_ctx.astype(F32)), _heads_to_lanes(v_ctx.astype(F32)), u_main)


def _to_block_diag(s):
    b = s.shape[0]
    G = HEADS_PER_GROUP
    s = s.astype(F32).reshape(b, 2, N_GROUPS, G, HEAD_DIM, 1, HEAD_DIM)
    eye = jnp.eye(G, dtype=F32).reshape(1, 1, 1, G, 1, G, 1)
    return (s * eye).reshape(b, 2, N_GROUPS, GROUP_LANES, GROUP_LANES).transpose(0, 2, 1, 3, 4)


def _rows_to_lanes(v):
    b = v.shape[0]
    return v.astype(F32).reshape(b, 2, N_GROUPS, 1, GROUP_LANES).transpose(0, 2, 1, 3, 4)


def _lanes_to_rows(v):
    b = v.shape[0]
    return v.transpose(0, 2, 1, 3, 4).reshape(b, 2, N_HEADS, HEAD_DIM)


def _swap_last(s):
    return jnp.swapaxes(s, -1, -2)


def _in_weight(w_in):
    gl = GROUP_LANES
    w_in = w_in.astype(BF16)
    zeros = jnp.zeros((DEPTH, D_MODEL, gl), BF16)
    cols = []
    for names in _MIXER_COLS:
        for g in range(N_GROUPS):
            for name in names:
                cols.append(zeros if name is None else w_in[..., _OFF[name] + g * gl:_OFF[name] + (g + 1) * gl])
    cols.append(w_in[..., _OFF["ga"]:_OFF["ga"] + 2 * GLA_RANK])
    cols.append(w_in[..., _OFF["mi"]:_OFF["mi"] + 2 * N_HEADS])
    cols.append(w_in[..., _OFF["mf"]:_OFF["mf"] + 2 * N_HEADS])
    cols.append(jnp.zeros((DEPTH, D_MODEL, SMALL_LANES - MF_OFF - 2 * N_HEADS), BF16))
    return jnp.concatenate(cols, axis=-1)


def _gla_gate_weight(w_a2_l):
    out = jnp.zeros((2, SMALL_LANES, GROUP_WIDTH), F32)
    for d in range(2):
        out = out.at[d, d * GLA_RANK:(d + 1) * GLA_RANK].set(w_a2_l[d].astype(F32))
    return out


def _layer(x, mod, p, batch, seq, ctx):
    latent = ctx is not None
    l = p["layer"]
    x, u_main, u_small = _ffn_inproj(x, mod[:, 0:6], p["norm_w"][0:3], p["w1"], p["w3"], p["w2"], p["w_in"], l, seq)
    mix_a, st_gla = _gla(u_main, u_small, p["wa_pad"], p["gla_b_a"], p["gla_norm_w"], batch, seq,
                         _swap_last(_to_block_diag(ctx["gla"])) if latent else None)
    if latent:
        mix_b = _na_latent(u_main, p["na_rpb"], ctx["na_k"], ctx["na_v"], batch, seq)
    else:
        mix_b, na_k, na_v = _na_context(u_main, batch, seq)
    mix_c, st_ml = _mlstm(u_main, u_small, p["mlstm_b_i"], p["mlstm_b_f"], batch, seq,
                          (_to_block_diag(ctx["mC"]), _rows_to_lanes(ctx["mn"]),
                           _rows_to_lanes(jnp.repeat(ctx["mm"][..., None], HEAD_DIM, axis=-1))) if latent else None)
    mix_d, st_ret = _retention(u_main, p["ret_decay"], p["ret_norm_w"], batch, seq,
                               _to_block_diag(ctx["ret"]) if latent else None)
    x = _outproj_ffn(x, (mix_a, mix_b, mix_c, mix_d), mod[:, 3:9], p["norm_w"][3:6], p["w_out"],
                     p["w1"], p["w3"], p["w2"], l, seq)
    if latent:
        return x, None
    new = {
        "na_k": na_k, "na_v": na_v, "gla": st_gla, "mC": st_ml[0],
        "mn": _lanes_to_rows(st_ml[1]), "mm": _lanes_to_rows(st_ml[2])[..., 0], "ret": st_ret,
    }
    return x, new


def kernel(x_prompt, x_sample, cache_na_k, cache_na_v, state_gla, state_mlstm_C, state_mlstm_n, state_mlstm_m,
           state_ret, c, c_ctx, w_mod, b_mod, norm_w, ffn_w1, ffn_w3, ffn_w2, w_in, w_out, gla_w_a2, gla_b_a,
           gla_norm_w, na_rpb, mlstm_b_i, mlstm_b_f, ret_decay, ret_norm_w):
    batch, seq, _ = x_prompt.shape
    dec_batch, dec_seq, _ = x_sample.shape

    cvec = jnp.zeros((8, D_MODEL), F32).at[0].set(c_ctx).at[1:1 + dec_batch].set(c)
    mods = _modulation(cvec, w_mod, b_mod).reshape(DEPTH, 8, N_MOD, D_MODEL)

    w1, w3, w2 = ffn_w1.astype(BF16), ffn_w3.astype(BF16), ffn_w2.astype(BF16)
    w_in_all, w_out_all = _in_weight(w_in), w_out.astype(BF16)
    params = []
    for l in range(DEPTH):
        params.append({
            "layer": l, "norm_w": norm_w[l],
            "w1": w1, "w3": w3, "w2": w2, "w_in": w_in_all, "w_out": w_out_all,
            "wa_pad": _gla_gate_weight(gla_w_a2[l]), "gla_b_a": gla_b_a[l], "gla_norm_w": gla_norm_w[l],
            "na_rpb": na_rpb[l], "mlstm_b_i": mlstm_b_i[l], "mlstm_b_f": mlstm_b_f[l],
            "ret_decay": ret_decay[l], "ret_norm_w": ret_norm_w[l],
        })

    xp = x_prompt.reshape(batch * seq, D_MODEL)
    states = []
    for l in range(DEPTH):
        xp, st = _layer(xp, mods[l, 0:1], params[l], batch, seq, None)
        states.append(st)

    xs = x_sample.reshape(dec_batch * dec_seq, D_MODEL)
    for l in range(DEPTH):
        ctx = {"na_k": cache_na_k[:, l], "na_v": cache_na_v[:, l], "gla": state_gla[:, l],
               "mC": state_mlstm_C[:, l], "mn": state_mlstm_n[:, l], "mm": state_mlstm_m[:, l],
               "ret": state_ret[:, l]}
        xs, _ = _layer(xs, mods[l, 1:1 + dec_batch], params[l], dec_batch, dec_seq, ctx)

    stack = lambda name: jnp.stack([s[name] for s in states], axis=1)
    return (xp.reshape(batch, seq, D_MODEL), xs.reshape(dec_batch, dec_seq, D_MODEL),
            stack("na_k"), stack("na_v"), stack("gla"), stack("mC"), stack("mn"), stack("mm"), stack("ret"))
```

```python
import functools

import numpy as np
import jax
import jax.numpy as jnp
from jax import lax
from jax.experimental import pallas as pl
from jax.experimental.pallas import tpu as pltpu

D_MODEL = 1024
DEPTH = 2
HEAD_DIM = 64
N_HEADS = 4
GROUP_WIDTH = N_HEADS * HEAD_DIM
N_MOD = 9
GLA_RANK = 16
GLA_TAU = 16.0
CHUNK = 64
GRID_W = 64
NA_KH = 8
NA_KW = 16
ROPE_BASE = 10000.0
EPS = 1e-6
D_FF = 2816

HEADS_PER_GROUP = 2
GROUP_LANES = HEADS_PER_GROUP * HEAD_DIM
N_GROUPS = N_HEADS // HEADS_PER_GROUP
GLA_SUB = 16
GLA_SAFE_LOG2 = 64.0
ROW_TILE_IN = 256
ROW_TILE_OUT = 512
ROW_PARTS = 2
CTX_SEQS_PER_STEP = 4
PRE_ROWS = 128
ROWS_IN_FLIGHT = 2
NA_ROWS_IN_FLIGHT = 4
LOG2E = 1.4426950408889634
CHUNKS_IN_FLIGHT = 8
SMALL_LANES = 128
MI_OFF = 2 * GLA_RANK
MF_OFF = MI_OFF + 2 * N_HEADS
NEG = -1e30
VMEM_LIMIT = 56 * 1024 * 1024

F32 = jnp.float32
BF16 = jnp.bfloat16
HI = lax.Precision.HIGHEST

_OFF = {}
_o = 0
for _name, _size in (("gq", 256), ("gk", 256), ("gv", 256), ("gg", 256), ("ga", 32), ("nq", 256), ("nk", 256),
                     ("nv", 256), ("mq", 256), ("mk", 256), ("mv", 256), ("mo", 256), ("mi", 8), ("mf", 8),
                     ("rq", 256), ("rk", 256), ("rv", 256), ("rg", 256)):
    _OFF[_name] = _o
    _o += _size
N_IN = _o
_MIXER_COLS = (("gq", "gk", "gv", "gg"), ("nq", "nk", "nv", None), ("mq", "mk", "mv", "mo"), ("rq", "rk", "rv", "rg"))
MAIN_COLS = 4 * N_GROUPS * 4 * GROUP_LANES


def _mm(a, b):
    return jnp.dot(a.astype(BF16), b.astype(BF16), preferred_element_type=F32)


def _mm_nt(a, b):
    return lax.dot_general(a.astype(BF16), b.astype(BF16), (((1,), (1,)), ((), ())), preferred_element_type=F32)


def _mm_tn(a, b):
    return lax.dot_general(a.astype(BF16), b.astype(BF16), (((0,), (0,)), ((), ())), preferred_element_type=F32)


def _mm_split(x, w):
    hi = x.astype(BF16)
    rest = x - hi.astype(F32)
    mid = rest.astype(BF16)
    lo = (rest - mid.astype(F32)).astype(BF16)
    wb = w.astype(BF16)
    return jnp.dot(jnp.concatenate([hi, mid, lo], axis=1), jnp.concatenate([wb, wb, wb], axis=0),
                   preferred_element_type=F32)


def _log_sigmoid(x):
    return jnp.minimum(x, 0.0) - jnp.log1p(jnp.exp(-jnp.abs(x)))


def _silu(x):
    return x * jax.nn.sigmoid(x)


def _iota(shape, dim):
    return lax.broadcasted_iota(jnp.int32, shape, dim)


def _head_of(idx):
    return lax.shift_right_logical(idx, 6)


def _head_masks():
    lane = _iota((1, GROUP_LANES), 1)
    return [(_head_of(lane) == h).astype(F32) for h in range(HEADS_PER_GROUP)]


def _bd_mask():
    n = GROUP_LANES
    return (_head_of(_iota((n, n), 0)) == _head_of(_iota((n, n), 1))).astype(F32)


def _stack(x, masks):
    return jnp.concatenate([x * m for m in masks], axis=0)


def _unstack(r, masks, n):
    out = r[0:n] * masks[0]
    for h in range(1, len(masks)):
        out = out + r[h * n:(h + 1) * n] * masks[h]
    return out


def _rmsnorm(x, w):
    return x * lax.rsqrt(jnp.mean(x * x, axis=-1, keepdims=True) + EPS) * w


def _split_lanes(x, lane_in_head, at, fill, fill_at):
    hi = x.astype(BF16).astype(F32)
    rest = x - hi
    mid = rest.astype(BF16).astype(F32)
    lo = rest - mid
    out = jnp.where(lane_in_head == at, hi, jnp.where(lane_in_head == at + 1, mid,
                                                      jnp.where(lane_in_head == at + 2, lo, 0.0)))
    is_fill = (lane_in_head >= fill_at) & (lane_in_head < fill_at + 3)
    return jnp.where(is_fill, fill, out)


def _mod_kernel(c_ref, w_ref, b_ref, o_ref):
    s = _silu(c_ref[...])
    o_ref[0] = _mm(s, w_ref[0]) + b_ref[0]


def _modulation(cvec, w_mod, b_mod):
    n = N_MOD * D_MODEL
    tn = n // 8
    return pl.pallas_call(
        _mod_kernel,
        grid=(DEPTH, n // tn),
        in_specs=[
            pl.BlockSpec((8, D_MODEL), lambda l, j: (0, 0)),
            pl.BlockSpec((1, D_MODEL, tn), lambda l, j: (l, 0, j)),
            pl.BlockSpec((1, 1, tn), lambda l, j: (l, 0, j)),
        ],
        out_specs=pl.BlockSpec((1, 8, tn), lambda l, j: (l, 0, j)),
        out_shape=jax.ShapeDtypeStruct((DEPTH, 8, n), F32),
        compiler_params=pltpu.CompilerParams(dimension_semantics=("parallel", "parallel"),
                                             vmem_limit_bytes=VMEM_LIMIT),
        name="modulation",
    )(cvec, w_mod, b_mod.reshape(DEPTH, 1, n))


def _half_ffn(x, mod, g, w1_ref, w3_ref, w2_ref):
    h = _rmsnorm(x, g[0:1]) * (1.0 + mod[1:2]) + mod[0:1]
    hb = h.astype(BF16)
    h1 = jnp.dot(hb, w1_ref[...], preferred_element_type=F32)
    h3 = jnp.dot(hb, w3_ref[...], preferred_element_type=F32)
    yield
    y = jnp.dot((_silu(h1) * h3).astype(BF16), w2_ref[...], preferred_element_type=F32)
    yield
    return x + 0.5 * mod[2:3] * _rmsnorm(y, g[1:2])


def _row_parts(tm):
    part = tm // ROW_PARTS
    return [slice(j * part, (j + 1) * part) for j in range(ROW_PARTS)]


def _ffn_inproj_kernel(x_ref, mod_ref, g_ref, w1_ref, w3_ref, w2_ref, win_ref, x_out_ref, um_ref, us_ref):
    mod = mod_ref[0]
    g = g_ref[...]

    def part(rows):
        x = yield from _half_ffn(x_ref[rows, :], mod[0:3], g[0:2], w1_ref, w3_ref, w2_ref)
        x_out_ref[rows, :] = x
        h = _rmsnorm(x, g[2:3]) * (1.0 + mod[4:5]) + mod[3:4]
        u = jnp.dot(h.astype(BF16), win_ref[...], preferred_element_type=F32)
        yield
        um_ref[rows, :] = u[:, :MAIN_COLS]
        us_ref[rows, :] = u[:, MAIN_COLS:]

    _interleave([part(rows) for rows in _row_parts(x_ref.shape[0])])


def _outproj_ffn_kernel(x_ref, ma_ref, mb_ref, mc_ref, md_ref, mod_ref, g_ref, wo_ref, w1_ref, w3_ref, w2_ref,
                        o_ref):
    mod = mod_ref[0]
    g = g_ref[...]

    def part(rows):
        mix = jnp.concatenate([ma_ref[rows, :], mb_ref[rows, :], mc_ref[rows, :], md_ref[rows, :]], axis=-1)
        y = jnp.dot(mix.astype(BF16), wo_ref[...], preferred_element_type=F32)
        yield
        x = x_ref[rows, :] + mod[2:3] * _rmsnorm(y, g[0:1])
        o_ref[rows, :] = yield from _half_ffn(x, mod[3:6], g[1:3], w1_ref, w3_ref, w2_ref)

    _interleave([part(rows) for rows in _row_parts(x_ref.shape[0])])


def _row_tile(n_mod, rows_per_batch, tile):
    return tile if n_mod == 1 else min(tile, rows_per_batch)


def _mod_index(n_mod, rows_per_batch, tm):
    if n_mod == 1:
        return lambda i: (0, 0, 0)
    per = rows_per_batch // tm
    return lambda i: (i // per, 0, 0)


_ROW_PARAMS = pltpu.CompilerParams(dimension_semantics=("parallel",), vmem_limit_bytes=VMEM_LIMIT)


def _ffn_weight_specs(layer, half):
    pick = lambda i: (layer, half, 0, 0)
    return [pl.BlockSpec((None, None, D_MODEL, D_FF), pick, pipeline_mode=pl.Buffered(1)),
            pl.BlockSpec((None, None, D_MODEL, D_FF), pick, pipeline_mode=pl.Buffered(1)),
            pl.BlockSpec((None, None, D_FF, D_MODEL), pick, pipeline_mode=pl.Buffered(1))]


def _ffn_inproj(x, mod6, g3, w1, w3, w2, w_big, layer, rows_per_batch):
    rows = x.shape[0]
    tm = _row_tile(mod6.shape[0], rows_per_batch, ROW_TILE_IN)
    row_spec = lambda width: pl.BlockSpec((tm, width), lambda i: (i, 0))
    return pl.pallas_call(
        _ffn_inproj_kernel,
        grid=(rows // tm,),
        in_specs=[
            row_spec(D_MODEL),
            pl.BlockSpec((1, 6, D_MODEL), _mod_index(mod6.shape[0], rows_per_batch, tm)),
            pl.BlockSpec((3, D_MODEL), lambda i: (0, 0)),
            *_ffn_weight_specs(layer, 0),
            pl.BlockSpec((None, D_MODEL, MAIN_COLS + SMALL_LANES), lambda i: (layer, 0, 0),
                         pipeline_mode=pl.Buffered(1)),
        ],
        out_specs=[row_spec(D_MODEL), row_spec(MAIN_COLS), row_spec(SMALL_LANES)],
        out_shape=[jax.ShapeDtypeStruct((rows, D_MODEL), F32), jax.ShapeDtypeStruct((rows, MAIN_COLS), F32),
                   jax.ShapeDtypeStruct((rows, SMALL_LANES), F32)],
        compiler_params=_ROW_PARAMS,
        name="ffn_inproj",
    )(x, mod6, g3, w1, w3, w2, w_big)


def _outproj_ffn(x, mixes, mod6, g3, w_out, w1, w3, w2, layer, rows_per_batch):
    rows = x.shape[0]
    tm = _row_tile(mod6.shape[0], rows_per_batch, ROW_TILE_OUT)
    row_spec = lambda width: pl.BlockSpec((tm, width), lambda i: (i, 0))
    return pl.pallas_call(
        _outproj_ffn_kernel,
        grid=(rows // tm,),
        in_specs=[
            row_spec(D_MODEL), *[row_spec(GROUP_WIDTH)] * 4,
            pl.BlockSpec((1, 6, D_MODEL), _mod_index(mod6.shape[0], rows_per_batch, tm)),
            pl.BlockSpec((3, D_MODEL), lambda i: (0, 0)),
            pl.BlockSpec((None, D_MODEL, D_MODEL), lambda i: (layer, 0, 0), pipeline_mode=pl.Buffered(1)),
            *_ffn_weight_specs(layer, 1),
        ],
        out_specs=row_spec(D_MODEL),
        out_shape=jax.ShapeDtypeStruct((rows, D_MODEL), F32),
        compiler_params=_ROW_PARAMS,
        name="outproj_ffn",
    )(x, *mixes, mod6, g3, w_out, w1, w3, w2)


def _seqs_per_step(latent):
    return 1 if latent else CTX_SEQS_PER_STEP


def _mixer_specs(mixer, seq, nb):
    gl = GROUP_LANES
    u_spec = pl.BlockSpec((nb * seq, 4 * gl), lambda b, g: (b, mixer * N_GROUPS + g))
    small_spec = pl.BlockSpec((nb * seq, SMALL_LANES), lambda b, g: (b, 0))
    out_spec = pl.BlockSpec((nb * seq, gl), lambda b, g: (b, g))
    return u_spec, small_spec, out_spec


def _state_spec(rows, nb=1):
    return pl.BlockSpec((nb, 1, 2, rows, GROUP_LANES), lambda b, g: (b, g, 0, 0, 0))


def _head_state_shape(batch):
    return jax.ShapeDtypeStruct((batch, 2, N_HEADS, HEAD_DIM, HEAD_DIM), F32)


def _head_state_spec(nb):
    return pl.BlockSpec((nb, 2, HEADS_PER_GROUP, HEAD_DIM, HEAD_DIM), lambda b, g: (b, 0, g, 0, 0))


def _store_head_blocks(ref, s, d, mat):
    for h in range(HEADS_PER_GROUP):
        lo = h * HEAD_DIM
        ref[s, d, h] = mat[lo:lo + HEAD_DIM, lo:lo + HEAD_DIM]


_MIXER_PARAMS = pltpu.CompilerParams(dimension_semantics=("parallel", "parallel"), vmem_limit_bytes=VMEM_LIMIT)


def _chunk_rows(c):
    return pl.ds(pl.multiple_of(c * CHUNK, CHUNK), CHUNK)


def _scan_order(t, n_chunks, reverse):
    return (n_chunks - 1 - t) if reverse else t


def _tile_row(row):
    return jnp.broadcast_to(row, (8, row.shape[1]))


def _edge_row(ref, c, reverse):
    if reverse:
        return ref[pl.ds(pl.multiple_of(c * CHUNK, 8), 8), :][0:1]
    return ref[pl.ds(pl.multiple_of(c * CHUNK + CHUNK - 8, 8), 8), :][7:8]


def _chunk_scan(x, reverse, op):
    n = x.shape[0]
    row = _iota(x.shape, 0) & (CHUNK - 1)
    s = 1
    while s < CHUNK:
        if reverse:
            shifted = pltpu.roll(x, n - s, 0)
            ok = row < CHUNK - s
        else:
            shifted = pltpu.roll(x, s, 0)
            ok = row >= s
        x = jnp.where(ok, op(x, shifted), x)
        s *= 2
    return x


def _row_loop(seq, fn):
    blk = min(seq, PRE_ROWS)
    k = min(seq // blk, ROWS_IN_FLIGHT)

    def body(i, c):
        _interleave([fn(pl.ds(pl.multiple_of((i * k + j) * blk, blk), blk)) for j in range(k)])
        return c

    if seq == blk * k:
        _interleave([fn(pl.ds(j * blk, blk)) for j in range(k)])
    else:
        lax.fori_loop(0, seq // (blk * k), body, 0)


def _chunk_loop(n_chunks, make_stream):
    k = min(n_chunks, CHUNKS_IN_FLIGHT)

    def body(t, carry):
        _interleave([make_stream(t * k + j) for j in range(k)])
        return carry

    if n_chunks == k:
        body(0, 0)
    else:
        lax.fori_loop(0, n_chunks // k, body, 0)


def _interleave(streams):
    results = [None] * len(streams)
    live = []
    for i, s in enumerate(streams):
        if hasattr(s, "send"):
            live.append(i)
        else:
            results[i] = s
    while live:
        for i in list(live):
            try:
                next(streams[i])
            except StopIteration as done:
                results[i] = done.value
                live.remove(i)
    return results


def _state_scan(n_chunks, reverse, s_ref, decay_ref, init, base):
    def body(t, s):
        c = base + _scan_order(t, n_chunks, reverse)
        inc = s_ref[c]
        s_ref[c] = s
        return decay_ref[c][0:1] * s + inc

    return lax.fori_loop(0, n_chunks, body, init)


def _finalize(seq, acc_ref, o_ref, fn):
    def blk(rows):
        res = fn(rows, acc_ref[rows, :] + o_ref[rows, :])
        if hasattr(res, "send"):
            res = yield from res
        o_ref[rows, :] = res

    _row_loop(seq, blk)


def _rope(x, cos, sin_signed):
    lane = _iota(x.shape, 1)
    first = (lane & 31) < 16
    swapped = jnp.where(first, pltpu.roll(x, GROUP_LANES - 16, 1), pltpu.roll(x, 16, 1))
    return x * cos + swapped * sin_signed


def _ret_kernel(dec_ref, nw_ref, u_ref, *rest, seq, nb, rope, has_state, emit_state):
    rest = list(rest)
    cos_ref, sin_ref = (rest.pop(0), rest.pop(0)) if rope else (None, None)
    s0_ref = rest.pop(0) if has_state else None
    o_ref = rest.pop(0)
    st_ref = rest.pop(0) if emit_state else None
    acc_ref, q_ref, k_ref, s_ref, cdec_ref = rest

    gl, L, G = GROUP_LANES, CHUNK, HEADS_PER_GROUP
    n_chunks = seq // L
    g = pl.program_id(1)
    masks = _head_masks()
    bd = _bd_mask()
    bd_avg = bd * (1.0 / HEAD_DIM)
    scale = HEAD_DIM ** -0.5
    ri = _iota((L, gl), 0).astype(F32)
    si = _iota((G * L, L), 0)
    sj = _iota((G * L, L), 1)
    i_idx = si & (L - 1)
    hrow = _head_of(_iota((G * L, 1), 0))

    def prepare(rows):
        q = u_ref[rows, 0:gl]
        k = u_ref[rows, gl:2 * gl]
        if rope:
            cos = cos_ref[rows, :]
            sin = sin_ref[rows, :]
            q = _rope(q, cos, sin)
            k = _rope(k, cos, sin)
        q_ref[rows, :] = q * scale
        k_ref[rows, :] = k

    _row_loop(nb * seq, prepare)

    for d in (0, 1):
        reverse = d == 1
        raw_row = masks[0] * dec_ref[d, g * G]
        raw_col = jnp.where(hrow == 0, dec_ref[d, g * G], 0.0)
        for h in range(1, G):
            raw_row = raw_row + masks[h] * dec_ref[d, g * G + h]
            raw_col = jnp.where(hrow == h, dec_ref[d, g * G + h], raw_col)
        lg_row = _log_sigmoid(raw_row)
        lg_col = _log_sigmoid(raw_col)
        diff = ((sj - i_idx) if reverse else (i_idx - sj)).astype(F32)
        dmat = jnp.where(diff >= 0, jnp.exp(jnp.maximum(diff, 0.0) * lg_col), 0.0)
        qdec = jnp.exp(((L - ri) if reverse else (ri + 1.0)) * lg_row)
        kdec = jnp.exp((ri if reverse else (L - 1.0 - ri)) * lg_row)
        cdec_tile = _tile_row(jnp.exp(float(L) * lg_row))
        out_ref = o_ref if reverse else acc_ref

        def increments(c, kdec=kdec, cdec_tile=cdec_tile):
            rows = _chunk_rows(c)
            inc = _mm_tn(k_ref[rows, :] * kdec, u_ref[rows, 2 * gl:3 * gl])
            yield
            s_ref[c] = bd * inc
            cdec_ref[c] = cdec_tile

        _chunk_loop(nb * n_chunks, increments)
        for s in range(nb):
            s_fin = _state_scan(n_chunks, reverse, s_ref, cdec_ref,
                                s0_ref[s, 0, d] if has_state else jnp.zeros((gl, gl), F32), s * n_chunks)
            if emit_state:
                _store_head_blocks(st_ref, s, d, s_fin)

        def outputs(c, dmat=dmat, qdec=qdec, out_ref=out_ref):
            rows = _chunk_rows(c)
            q = q_ref[rows, :]
            v = u_ref[rows, 2 * gl:3 * gl]
            att = _mm_nt(_stack(q, masks), k_ref[rows, :])
            inter = _mm(q * qdec, s_ref[c])
            yield
            intra = _mm(att * dmat, v)
            yield
            out_ref[rows, :] = _unstack(intra, masks, L) + inter

        _chunk_loop(nb * n_chunks, outputs)

    def finalize(rows, o):
        mu = _mm_split(o, bd_avg)
        yield
        dev = o - mu
        var = _mm_split(dev * dev, bd_avg)
        yield
        return dev * lax.rsqrt(var + EPS) * nw_ref[...] * _silu(u_ref[rows, 3 * gl:4 * gl])

    _finalize(nb * seq, acc_ref, o_ref, finalize)


def _ret_tables(seq):
    t = np.arange(seq)
    quarter = HEAD_DIM // 4
    inv = (ROPE_BASE ** (-np.arange(quarter, dtype=np.float32) / quarter)).astype(np.float32)
    ang_r = (t // GRID_W).astype(np.float32)[:, None] * inv
    ang_c = (t % GRID_W).astype(np.float32)[:, None] * inv
    cos = np.concatenate([np.cos(ang_r), np.cos(ang_r), np.cos(ang_c), np.cos(ang_c)], axis=1)
    sin = np.concatenate([-np.sin(ang_r), np.sin(ang_r), -np.sin(ang_c), np.sin(ang_c)], axis=1)
    tile = lambda a: jnp.asarray(np.tile(a.astype(np.float32), (1, HEADS_PER_GROUP)))
    return tile(cos), tile(sin)


def _retention(u_main, ret_decay, norm_w, batch, seq, state):
    latent = state is not None
    nb = _seqs_per_step(latent)
    u_spec, _, out_spec = _mixer_specs(3, seq, nb)
    gl = GROUP_LANES
    rows, chunks = nb * seq, nb * seq // CHUNK
    const2 = lambda b, g: (0, 0)
    in_specs = [pl.BlockSpec(memory_space=pltpu.SMEM), pl.BlockSpec((1, gl), const2), u_spec]
    args = [ret_decay, jnp.tile(norm_w.reshape(1, HEAD_DIM), (1, HEADS_PER_GROUP)), u_main]
    if latent:
        cos, sin = _ret_tables(seq)
        in_specs += [pl.BlockSpec((seq, gl), const2), pl.BlockSpec((seq, gl), const2), _state_spec(gl)]
        args += [cos, sin, state]
    out_shape = [jax.ShapeDtypeStruct((batch * seq, GROUP_WIDTH), F32)]
    out_specs = [out_spec]
    if not latent:
        out_shape.append(_head_state_shape(batch))
        out_specs.append(_head_state_spec(nb))
    res = pl.pallas_call(
        functools.partial(_ret_kernel, seq=seq, nb=nb, rope=latent, has_state=latent, emit_state=not latent),
        grid=(batch // nb, N_GROUPS),
        in_specs=in_specs,
        out_specs=out_specs,
        out_shape=out_shape,
        scratch_shapes=[pltpu.VMEM((rows, gl), F32), pltpu.VMEM((rows, gl), F32), pltpu.VMEM((rows, gl), F32),
                        pltpu.VMEM((chunks, gl, gl), F32), pltpu.VMEM((chunks, 8, gl), F32)],
        compiler_params=_MIXER_PARAMS,
        name="retention",
    )(*args)
    return res[0], (None if latent else res[1])


def _gla_kernel(wa_ref, ba_ref, nw_ref, u_ref, us_ref, *rest, seq, nb, has_state, emit_state):
    rest = list(rest)
    s0_ref = rest.pop(0) if has_state else None
    o_ref = rest.pop(0)
    st_ref = rest.pop(0) if emit_state else None
    acc_ref, b_ref, s_ref, dec_ref, steep_ref = rest

    gl, L, G, SB = GROUP_LANES, CHUNK, HEADS_PER_GROUP, GLA_SUB
    n_chunks = seq // L
    n_sub = L // SB
    masks = _head_masks()
    bd = _bd_mask()
    bd_avg = bd * (1.0 / HEAD_DIM)
    scale = HEAD_DIM ** -0.5
    half = SB // 2
    pair_row = _iota((SB * SB, gl), 0)
    pi = lax.shift_right_logical(pair_row, 4)
    pj = pair_row & (SB - 1)
    pair_sum = (lax.shift_right_logical(_iota((SB, SB * SB), 1), 4) == _iota((SB, SB * SB), 0)).astype(BF16)
    zeros_half = jnp.zeros((half, gl), F32)

    for d in (0, 1):
        reverse = d == 1
        pair_ok = (pj >= pi) if reverse else (pj <= pi)
        out_ref = o_ref if reverse else acc_ref

        steep_ref[...] = jnp.zeros((8, gl), F32)

        def log_decay(rows, d=d, reverse=reverse):
            a_pre = _mm(us_ref[rows, :], wa_ref[d]) + ba_ref[d]
            yield
            la = _log_sigmoid(a_pre) * (LOG2E / GLA_TAU)
            b_ref[rows, :] = _chunk_scan(la, reverse, jnp.add)
            steep_ref[...] = jnp.maximum(steep_ref[...], jnp.max((-la).reshape(-1, 8, gl), axis=0))

        _row_loop(nb * seq, log_decay)
        factorise_all = jnp.max(steep_ref[...]) * (SB - 1) <= GLA_SAFE_LOG2

        def increments(c, reverse=reverse):
            rows = _chunk_rows(c)
            b = b_ref[rows, :]
            btot = b[0:1] if reverse else b[L - 1:L]
            inc = _mm_tn(u_ref[rows, 2 * gl:3 * gl], u_ref[rows, gl:2 * gl] * jnp.exp2(btot - b))
            yield
            s_ref[c] = bd * inc
            dec_ref[c] = _tile_row(jnp.exp2(btot))

        _chunk_loop(nb * n_chunks, increments)
        for s in range(nb):
            s_fin = _state_scan(n_chunks, reverse, s_ref, dec_ref,
                                s0_ref[s, 0, d] if has_state else jnp.zeros((gl, gl), F32), s * n_chunks)
            if emit_state:
                _store_head_blocks(st_ref, s, d, s_fin.T)

        def outputs_factorised(c, reverse=reverse, out_ref=out_ref):
            rows = _chunk_rows(c)
            b = b_ref[rows, :]
            q = u_ref[rows, 0:gl] * scale
            k = u_ref[rows, gl:2 * gl]
            v = u_ref[rows, 2 * gl:3 * gl]
            o_inter = _mm_nt(q * jnp.exp2(b), s_ref[c])
            atts = []
            for i_blk in range(n_sub):
                lo = i_blk * SB
                bi = b[lo:lo + SB]
                ref, keys = (bi[SB - 1:SB], slice(lo, L)) if reverse else (bi[0:1], slice(0, lo + SB))
                qd = q[lo:lo + SB] * jnp.exp2(bi - ref)
                kd = k[keys] * jnp.exp2(ref - b[keys])
                att = _mm_nt(_stack(qd, masks), kd)
                n_keys = keys.stop - keys.start
                qrow = _iota((G * SB, n_keys), 0) & (SB - 1)
                kcol = _iota((G * SB, n_keys), 1)
                ok = (kcol >= qrow) if reverse else (kcol <= qrow + lo)
                atts.append((jnp.where(ok, att, 0.0), keys))
            yield
            outs = [_mm(a, v[keys]) for a, keys in atts]
            yield
            out_ref[rows, :] = o_inter + jnp.concatenate([_unstack(o, masks, SB) for o in outs], axis=0)

        def outputs_direct(c, reverse=reverse, pair_ok=pair_ok, out_ref=out_ref):
            rows = _chunk_rows(c)
            b = b_ref[rows, :]
            q = u_ref[rows, 0:gl] * scale
            k = u_ref[rows, gl:2 * gl]
            v = u_ref[rows, 2 * gl:3 * gl]
            o_inter = _mm_nt(q * jnp.exp2(b), s_ref[c])
            diag, off = [], []
            for i_blk in range(n_sub):
                lo = i_blk * SB
                qi, ki, bi = q[lo:lo + SB], k[lo:lo + SB], b[lo:lo + SB]
                prods = []
                for i in range(SB):
                    if reverse and i >= half:
                        part = slice(half, SB)
                    elif (not reverse) and i < half:
                        part = slice(0, half)
                    else:
                        part = slice(0, SB)
                    p = (qi[i:i + 1] * ki[part]) * jnp.exp2(bi[i:i + 1] - bi[part])
                    if part.start == half:
                        prods += [zeros_half, p]
                    elif part.stop == half:
                        prods += [p, zeros_half]
                    else:
                        prods.append(p)
                diag.append(_mm(jnp.concatenate(prods, axis=0), bd))
                if reverse and i_blk < n_sub - 1:
                    ref, other = bi[SB - 1:SB], slice(lo + SB, L)
                elif (not reverse) and i_blk > 0:
                    ref, other = bi[0:1], slice(0, lo)
                else:
                    off.append(None)
                    continue
                qd = qi * jnp.exp2(bi - ref)
                kd = k[other] * jnp.exp2(ref - b[other])
                off.append((_mm_nt(_stack(qd, masks), kd), other))
            yield
            off = [None if a is None else _mm(a[0], v[a[1]]) for a in off]
            for i_blk in range(n_sub):
                vi = v[i_blk * SB:(i_blk + 1) * SB]
                weighted = jnp.where(pair_ok, diag[i_blk] * jnp.concatenate([vi] * SB, axis=0), 0.0)
                diag[i_blk] = jnp.dot(pair_sum, weighted.astype(BF16), preferred_element_type=F32)
            yield
            blocks = []
            for i_blk in range(n_sub):
                o_blk = diag[i_blk]
                if off[i_blk] is not None:
                    o_blk = o_blk + _unstack(off[i_blk], masks, SB)
                blocks.append(o_blk)
            out_ref[rows, :] = o_inter + jnp.concatenate(blocks, axis=0)

        @pl.when(factorise_all)
        def _():
            _chunk_loop(nb * n_chunks, outputs_factorised)

        @pl.when(jnp.logical_not(factorise_all))
        def _():
            _chunk_loop(nb * n_chunks, outputs_direct)

    def finalize(rows, o):
        ms = _mm_split(o * o, bd_avg)
        yield
        return o * lax.rsqrt(ms + EPS) * nw_ref[...] * _silu(u_ref[rows, 3 * gl:4 * gl])

    _finalize(nb * seq, acc_ref, o_ref, finalize)


def _gla(u_main, u_small, wa_pad, b_a, norm_w, batch, seq, state):
    latent = state is not None
    nb = _seqs_per_step(latent)
    u_spec, small_spec, out_spec = _mixer_specs(0, seq, nb)
    gl = GROUP_LANES
    rows, chunks = nb * seq, nb * seq // CHUNK
    in_specs = [
        pl.BlockSpec((2, SMALL_LANES, gl), lambda b, g: (0, 0, g)),
        pl.BlockSpec((2, 1, gl), lambda b, g: (0, 0, g)),
        pl.BlockSpec((1, gl), lambda b, g: (0, 0)),
        u_spec, small_spec,
    ]
    args = [wa_pad, b_a.reshape(2, 1, GROUP_WIDTH),
            jnp.tile(norm_w.reshape(1, HEAD_DIM), (1, HEADS_PER_GROUP)), u_main, u_small]
    if latent:
        in_specs.append(_state_spec(gl))
        args.append(state)
    out_shape = [jax.ShapeDtypeStruct((batch * seq, GROUP_WIDTH), F32)]
    out_specs = [out_spec]
    if not latent:
        out_shape.append(_head_state_shape(batch))
        out_specs.append(_head_state_spec(nb))
    res = pl.pallas_call(
        functools.partial(_gla_kernel, seq=seq, nb=nb, has_state=latent, emit_state=not latent),
        grid=(batch // nb, N_GROUPS),
        in_specs=in_specs,
        out_specs=out_specs,
        out_shape=out_shape,
        scratch_shapes=[pltpu.VMEM((rows, gl), F32), pltpu.VMEM((rows, gl), F32),
                        pltpu.VMEM((chunks, gl, gl), F32), pltpu.VMEM((chunks, 8, gl), F32),
                        pltpu.VMEM((8, gl), F32)],
        compiler_params=_MIXER_PARAMS,
        name="gla",
    )(*args)
    return res[0], (None if latent else res[1])


def _mlstm_kernel(ei_ref, ef_ref, bi_ref, bf_ref, u_ref, us_ref, *rest, seq, nb, has_state, emit_state):
    rest = list(rest)
    if has_state:
        c0_ref, n0_ref, m0_ref = rest.pop(0), rest.pop(0), rest.pop(0)
    o_ref = rest.pop(0)
    if emit_state:
        cst_ref, nst_ref, mst_ref = rest.pop(0), rest.pop(0), rest.pop(0)
    acc_ref, f_ref, x_ref, p_ref, s_ref, nu_ref, wp_ref, min_ref, mout_ref = rest

    gl, L, G = GROUP_LANES, CHUNK, HEADS_PER_GROUP
    n_chunks = seq // L
    masks = _head_masks()
    bd = _bd_mask()
    scale = HEAD_DIM ** -0.5
    si = _iota((G * L, L), 0) & (L - 1)
    sj = _iota((G * L, L), 1)
    lane_in_head = _iota((L, gl), 1) & (HEAD_DIM - 1)

    for d in (0, 1):
        reverse = d == 1
        causal = (sj >= si) if reverse else (sj <= si)
        last = 0 if reverse else L - 1
        out_ref = o_ref if reverse else acc_ref

        def gates(rows, d=d, reverse=reverse):
            us = us_ref[rows, :]
            ig = _mm_split(us, ei_ref[d, 0]) + bi_ref[d, 0]
            fg = _mm_split(us, ef_ref[d, 0]) + bf_ref[d, 0]
            yield
            f_sum = _chunk_scan(_log_sigmoid(fg), reverse, jnp.add)
            x = ig - f_sum
            f_ref[rows, :] = f_sum
            x_ref[rows, :] = x
            p_ref[rows, :] = _chunk_scan(x, reverse, jnp.maximum)

        _row_loop(nb * seq, gates)

        def stabiliser(t, m_row, base, reverse=reverse):
            c = base + _scan_order(t, n_chunks, reverse)
            m_new = _edge_row(f_ref, c, reverse) + jnp.maximum(m_row, _edge_row(p_ref, c, reverse))
            min_ref[c] = _tile_row(m_row)
            mout_ref[c] = _tile_row(m_new)
            return m_new

        m_fin = [lax.fori_loop(0, n_chunks, functools.partial(stabiliser, base=s * n_chunks),
                               m0_ref[s, 0, d] if has_state else jnp.zeros((1, gl), F32)) for s in range(nb)]

        def increments(c, last=last):
            rows = _chunk_rows(c)
            f_tot = f_ref[rows, :][last:last + 1]
            m_in = min_ref[c][0:1]
            m_out = mout_ref[c][0:1]
            kw = u_ref[rows, gl:2 * gl] * jnp.exp(f_tot + x_ref[rows, :] - m_out)
            inc = _mm_tn(kw, u_ref[rows, 2 * gl:3 * gl])
            yield
            s_ref[c] = bd * inc
            nu_ref[c] = _tile_row(jnp.sum(kw, axis=0, keepdims=True))
            wp_ref[c] = _tile_row(jnp.exp(f_tot + m_in - m_out))

        _chunk_loop(nb * n_chunks, increments)

        def memory(t, carry, base, reverse=reverse):
            c_mat, n_row = carry
            c = base + _scan_order(t, n_chunks, reverse)
            inc = s_ref[c]
            nu = nu_ref[c][0:1]
            wp = wp_ref[c][0:1]
            s_ref[c] = c_mat
            nu_ref[c] = _tile_row(n_row)
            return wp * c_mat + inc, wp * n_row + nu

        for s in range(nb):
            if has_state:
                init = (c0_ref[s, 0, d], n0_ref[s, 0, d])
            else:
                init = (jnp.zeros((gl, gl), F32), jnp.zeros((1, gl), F32))
            c_fin, n_fin = lax.fori_loop(0, n_chunks, functools.partial(memory, base=s * n_chunks), init)
            if emit_state:
                _store_head_blocks(cst_ref, s, d, c_fin)
                nst_ref[s, 0, d], mst_ref[s, 0, d] = n_fin, m_fin[s]

        def outputs(c, causal=causal, out_ref=out_ref):
            rows = _chunk_rows(c)
            x = x_ref[rows, :]
            m_in = min_ref[c][0:1]
            q = u_ref[rows, 0:gl] * scale
            k = u_ref[rows, gl:2 * gl]
            v = u_ref[rows, 2 * gl:3 * gl]
            m_full = jnp.maximum(m_in, p_ref[rows, :])
            lhs = _split_lanes(-m_full, lane_in_head, 0, 1.0, 3)
            rhs = _split_lanes(x, lane_in_head, 3, 1.0, 0)
            logw = _mm_nt(_stack(lhs, masks), rhs)
            qk = _mm_nt(_stack(q, masks), k)
            inter = _mm(q, s_ref[c])
            q_n = _mm_split(q * nu_ref[c][0:1], bd)
            yield
            s = qk * jnp.exp(jnp.where(causal, logw, NEG))
            s_hi = s.astype(BF16)
            ones = jnp.ones((L, gl), BF16)
            r = jnp.dot(s_hi, jnp.concatenate([v.astype(BF16), ones], axis=1), preferred_element_type=F32)
            sum_lo = jnp.dot((s - s_hi.astype(F32)).astype(BF16), ones, preferred_element_type=F32)
            yield
            w_inter = jnp.exp(m_in - m_full)
            num = _unstack(r[:, :gl], masks, L) + w_inter * inter
            den = _unstack(r[:, gl:] + sum_lo, masks, L) + w_inter * q_n
            out_ref[rows, :] = num / jnp.maximum(jnp.abs(den), jnp.exp(-(f_ref[rows, :] + m_full)))

        _chunk_loop(nb * n_chunks, outputs)

    def finalize(rows, o):
        return o * jax.nn.sigmoid(u_ref[rows, 3 * gl:4 * gl])

    _finalize(nb * seq, acc_ref, o_ref, finalize)


def _gate_expanders(offset):
    e = np.zeros((2, N_GROUPS, SMALL_LANES, GROUP_LANES), np.float32)
    for d in range(2):
        for g in range(N_GROUPS):
            for h in range(HEADS_PER_GROUP):
                src = offset + d * N_HEADS + g * HEADS_PER_GROUP + h
                e[d, g, src, h * HEAD_DIM:(h + 1) * HEAD_DIM] = 1.0
    return jnp.asarray(e)


def _gate_bias(b):
    return jnp.repeat(b.astype(F32), HEAD_DIM, axis=1).reshape(2, N_GROUPS, 1, GROUP_LANES)


def _mlstm(u_main, u_small, b_i, b_f, batch, seq, state):
    latent = state is not None
    nb = _seqs_per_step(latent)
    u_spec, small_spec, out_spec = _mixer_specs(2, seq, nb)
    gl = GROUP_LANES
    rows, chunks = nb * seq, nb * seq // CHUNK
    exp_spec = pl.BlockSpec((2, 1, SMALL_LANES, gl), lambda b, g: (0, g, 0, 0))
    bias_spec = pl.BlockSpec((2, 1, 1, gl), lambda b, g: (0, g, 0, 0))
    in_specs = [exp_spec, exp_spec, bias_spec, bias_spec, u_spec, small_spec]
    args = [_gate_expanders(MI_OFF), _gate_expanders(MF_OFF), _gate_bias(b_i), _gate_bias(b_f), u_main, u_small]
    if latent:
        in_specs += [_state_spec(gl), _state_spec(1), _state_spec(1)]
        args += list(state)
    out_shape = [jax.ShapeDtypeStruct((batch * seq, GROUP_WIDTH), F32)]
    out_specs = [out_spec]
    if not latent:
        out_shape += [_head_state_shape(batch),
                      jax.ShapeDtypeStruct((batch, N_GROUPS, 2, 1, gl), F32),
                      jax.ShapeDtypeStruct((batch, N_GROUPS, 2, 1, gl), F32)]
        out_specs += [_head_state_spec(nb), _state_spec(1, nb), _state_spec(1, nb)]
    res = pl.pallas_call(
        functools.partial(_mlstm_kernel, seq=seq, nb=nb, has_state=latent, emit_state=not latent),
        grid=(batch // nb, N_GROUPS),
        in_specs=in_specs,
        out_specs=out_specs,
        out_shape=out_shape,
        scratch_shapes=[pltpu.VMEM((rows, gl), F32)] * 4 + [pltpu.VMEM((chunks, gl, gl), F32)]
        + [pltpu.VMEM((chunks, 8, gl), F32)] * 4,
        compiler_params=_MIXER_PARAMS,
        name="mlstm",
    )(*args)
    return res[0], (None if latent else tuple(res[1:]))


def _na_ctx_kernel(u_ref, o_ref, ko_ref, vo_ref, *, seq, nb):
    gl = GROUP_LANES
    masks = _head_masks()

    def sequence(s):
        rows = slice(s * seq, (s + 1) * seq)
        q = u_ref[rows, 0:gl] * (HEAD_DIM ** -0.5)
        k = u_ref[rows, gl:2 * gl]
        v = u_ref[rows, 2 * gl:3 * gl]
        sc = _mm_nt(_stack(q, masks), k)
        yield
        p = jnp.exp(sc - jnp.max(sc, axis=1, keepdims=True))
        o = _mm(p, v)
        yield
        o_ref[rows, :] = _unstack(o / jnp.sum(p, axis=1, keepdims=True), masks, seq)
        for h in range(HEADS_PER_GROUP):
            ko_ref[s, h] = k[:, h * HEAD_DIM:(h + 1) * HEAD_DIM]
            vo_ref[s, h] = v[:, h * HEAD_DIM:(h + 1) * HEAD_DIM]

    _interleave([sequence(s) for s in range(nb)])


def _na_context(u_main, batch, seq):
    nb = _seqs_per_step(False)
    u_spec, _, out_spec = _mixer_specs(1, seq, nb)
    kv_shape = jax.ShapeDtypeStruct((batch, N_HEADS, seq, HEAD_DIM), F32)
    kv_spec = pl.BlockSpec((nb, HEADS_PER_GROUP, seq, HEAD_DIM), lambda b, g: (b, g, 0, 0))
    return pl.pallas_call(
        functools.partial(_na_ctx_kernel, seq=seq, nb=nb),
        grid=(batch // nb, N_GROUPS),
        in_specs=[u_spec],
        out_specs=[out_spec, kv_spec, kv_spec],
        out_shape=[jax.ShapeDtypeStruct((batch * seq, GROUP_WIDTH), F32), kv_shape, kv_shape],
        compiler_params=_MIXER_PARAMS,
        name="na_context",
    )(u_main)


def _na_lat_kernel(tab_ref, kc_ref, vc_ref, u_ref, o_ref, *, seq):
    gl = GROUP_LANES
    grid_rows = seq // GRID_W
    win = NA_KH * GRID_W
    masks = _head_masks()
    scale = HEAD_DIM ** -0.5
    kc = kc_ref[0, 0]
    vc = vc_ref[0, 0]

    def grid_row(r):
        ks = jnp.clip(r - NA_KH // 2, 0, grid_rows - NA_KH)
        q_rows = pl.ds(pl.multiple_of(r * GRID_W, GRID_W), GRID_W)
        k_rows = pl.ds(pl.multiple_of(ks * GRID_W, GRID_W), win)
        qs = _stack(u_ref[q_rows, 0:gl] * scale, masks)
        s_loc = _mm_nt(qs, u_ref[k_rows, gl:2 * gl])
        s_ctx = _mm_nt(qs, kc)
        yield
        s_loc = s_loc + tab_ref[0, ks - r + NA_KH - 1]
        m = jnp.maximum(jnp.max(s_loc, axis=1, keepdims=True), jnp.max(s_ctx, axis=1, keepdims=True))
        p_loc = jnp.exp(s_loc - m)
        p_ctx = jnp.exp(s_ctx - m)
        den = jnp.sum(p_loc, axis=1, keepdims=True) + jnp.sum(p_ctx, axis=1, keepdims=True)
        o = _mm(p_loc, u_ref[k_rows, 2 * gl:3 * gl]) + _mm(p_ctx, vc)
        yield
        o_ref[q_rows, :] = _unstack(o / den, masks, GRID_W)

    def body(t, carry):
        _interleave([grid_row(t * NA_ROWS_IN_FLIGHT + j) for j in range(NA_ROWS_IN_FLIGHT)])
        return carry

    lax.fori_loop(0, grid_rows // NA_ROWS_IN_FLIGHT, body, 0)


def _na_bias_table(rpb):
    c = np.arange(GRID_W)
    rel = c[None, :] - c[:, None]
    cs = np.clip(c - NA_KW // 2, 0, GRID_W - NA_KW)
    valid = (c[None, :] >= cs[:, None]) & (c[None, :] < cs[:, None] + NA_KW)
    dcol = np.clip(rel, -(NA_KW - 1), NA_KW - 1) + NA_KW - 1
    bmat = jnp.where(valid[None, None], rpb.astype(F32)[:, :, dcol], NEG)
    tabs = jnp.stack([bmat[:, s:s + NA_KH] for s in range(NA_KH)], axis=1)
    tabs = tabs.transpose(0, 1, 3, 2, 4).reshape(N_GROUPS, HEADS_PER_GROUP, NA_KH, GRID_W, NA_KH * GRID_W)
    return tabs.transpose(0, 2, 1, 3, 4).reshape(N_GROUPS, NA_KH, HEADS_PER_GROUP * GRID_W, NA_KH * GRID_W)


def _heads_to_lanes(t):
    b, _, s, _ = t.shape
    t = t.reshape(b, N_GROUPS, HEADS_PER_GROUP, s, HEAD_DIM).transpose(0, 1, 3, 2, 4)
    return t.reshape(b, N_GROUPS, s, GROUP_LANES)


def _na_latent(u_main, rpb, k_ctx, v_ctx, batch, seq):
    u_spec, _, out_spec = _mixer_specs(1, seq, 1)
    gl = GROUP_LANES
    past = k_ctx.shape[2]
    tab = _na_bias_table(rpb)
    ctx_spec = pl.BlockSpec((1, 1, past, gl), lambda b, g: (b, g, 0, 0))
    return pl.pallas_call(
        functools.partial(_na_lat_kernel, seq=seq),
        grid=(batch, N_GROUPS),
        in_specs=[pl.BlockSpec((1,) + tab.shape[1:], lambda b, g: (g, 0, 0, 0)), ctx_spec, ctx_spec, u_spec],
        out_specs=out_spec,
        out_shape=jax.ShapeDtypeStruct((batch * seq, GROUP_WIDTH), F32),
        compiler_params=_MIXER_PARAMS,
        name="na_latent",
    )(tab, _heads_to_lanes(k_ctx.astype(F32)), _heads_to_lanes(v_ctx.astype(F32)), u_main)


def _to_block_diag(s):
    b = s.shape[0]
    G = HEADS_PER_GROUP
    s = s.astype(F32).reshape(b, 2, N_GROUPS, G, HEAD_DIM, 1, HEAD_DIM)
    eye = jnp.eye(G, dtype=F32).reshape(1, 1, 1, G, 1, G, 1)
    return (s * eye).reshape(b, 2, N_GROUPS, GROUP_LANES, GROUP_LANES).transpose(0, 2, 1, 3, 4)


def _rows_to_lanes(v):
    b = v.shape[0]
    return v.astype(F32).reshape(b, 2, N_GROUPS, 1, GROUP_LANES).transpose(0, 2, 1, 3, 4)


def _lanes_to_rows(v):
    b = v.shape[0]
    return v.transpose(0, 2, 1, 3, 4).reshape(b, 2, N_HEADS, HEAD_DIM)


def _swap_last(s):
    return jnp.swapaxes(s, -1, -2)


def _in_weight(w_in):
    gl = GROUP_LANES
    w_in = w_in.astype(BF16)
    zeros = jnp.zeros((DEPTH, D_MODEL, gl), BF16)
    cols = []
    for names in _MIXER_COLS:
        for g in range(N_GROUPS):
            for name in names:
                cols.append(zeros if name is None else w_in[..., _OFF[name] + g * gl:_OFF[name] + (g + 1) * gl])
    cols.append(w_in[..., _OFF["ga"]:_OFF["ga"] + 2 * GLA_RANK])
    cols.append(w_in[..., _OFF["mi"]:_OFF["mi"] + 2 * N_HEADS])
    cols.append(w_in[..., _OFF["mf"]:_OFF["mf"] + 2 * N_HEADS])
    cols.append(jnp.zeros((DEPTH, D_MODEL, SMALL_LANES - MF_OFF - 2 * N_HEADS), BF16))
    return jnp.concatenate(cols, axis=-1)


def _gla_gate_weight(w_a2_l):
    out = jnp.zeros((2, SMALL_LANES, GROUP_WIDTH), F32)
    for d in range(2):
        out = out.at[d, d * GLA_RANK:(d + 1) * GLA_RANK].set(w_a2_l[d].astype(F32))
    return out


def _layer(x, mod, p, batch, seq, ctx):
    latent = ctx is not None
    l = p["layer"]
    x, u_main, u_small = _ffn_inproj(x, mod[:, 0:6], p["norm_w"][0:3], p["w1"], p["w3"], p["w2"], p["w_in"], l, seq)
    mix_a, st_gla = _gla(u_main, u_small, p["wa_pad"], p["gla_b_a"], p["gla_norm_w"], batch, seq,
                         _swap_last(_to_block_diag(ctx["gla"])) if latent else None)
    if latent:
        mix_b = _na_latent(u_main, p["na_rpb"], ctx["na_k"], ctx["na_v"], batch, seq)
    else:
        mix_b, na_k, na_v = _na_context(u_main, batch, seq)
    mix_c, st_ml = _mlstm(u_main, u_small, p["mlstm_b_i"], p["mlstm_b_f"], batch, seq,
                          (_to_block_diag(ctx["mC"]), _rows_to_lanes(ctx["mn"]),
                           _rows_to_lanes(jnp.repeat(ctx["mm"][..., None], HEAD_DIM, axis=-1))) if latent else None)
    mix_d, st_ret = _retention(u_main, p["ret_decay"], p["ret_norm_w"], batch, seq,
                               _to_block_diag(ctx["ret"]) if latent else None)
    x = _outproj_ffn(x, (mix_a, mix_b, mix_c, mix_d), mod[:, 3:9], p["norm_w"][3:6], p["w_out"],
                     p["w1"], p["w3"], p["w2"], l, seq)
    if latent:
        return x, None
    new = {
        "na_k": na_k, "na_v": na_v, "gla": st_gla, "mC": st_ml[0],
        "mn": _lanes_to_rows(st_ml[1]), "mm": _lanes_to_rows(st_ml[2])[..., 0], "ret": st_ret,
    }
    return x, new


def kernel(x_prompt, x_sample, cache_na_k, cache_na_v, state_gla, state_mlstm_C, state_mlstm_n, state_mlstm_m,
           state_ret, c, c_ctx, w_mod, b_mod, norm_w, ffn_w1, ffn_w3, ffn_w2, w_in, w_out, gla_w_a2, gla_b_a,
           gla_norm_w, na_rpb, mlstm_b_i, mlstm_b_f, ret_decay, ret_norm_w):
    batch, seq, _ = x_prompt.shape
    dec_batch, dec_seq, _ = x_sample.shape

    cvec = jnp.zeros((8, D_MODEL), F32).at[0].set(c_ctx).at[1:1 + dec_batch].set(c)
    mods = _modulation(cvec, w_mod, b_mod).reshape(DEPTH, 8, N_MOD, D_MODEL)

    w1, w3, w2 = ffn_w1.astype(BF16), ffn_w3.astype(BF16), ffn_w2.astype(BF16)
    w_in_all, w_out_all = _in_weight(w_in), w_out.astype(BF16)
    params = []
    for l in range(DEPTH):
        params.append({
            "layer": l, "norm_w": norm_w[l],
            "w1": w1, "w3": w3, "w2": w2, "w_in": w_in_all, "w_out": w_out_all,
            "wa_pad": _gla_gate_weight(gla_w_a2[l]), "gla_b_a": gla_b_a[l], "gla_norm_w": gla_norm_w[l],
            "na_rpb": na_rpb[l], "mlstm_b_i": mlstm_b_i[l], "mlstm_b_f": mlstm_b_f[l],
            "ret_decay": ret_decay[l], "ret_norm_w": ret_norm_w[l],
        })

    xp = x_prompt.reshape(batch * seq, D_MODEL)
    states = []
    for l in range(DEPTH):
        xp, st = _layer(xp, mods[l, 0:1], params[l], batch, seq, None)
        states.append(st)

    xs = x_sample.reshape(dec_batch * dec_seq, D_MODEL)
    for l in range(DEPTH):
        ctx = {"na_k": cache_na_k[:, l], "na_v": cache_na_v[:, l], "gla": state_gla[:, l],
               "mC": state_mlstm_C[:, l], "mn": state_mlstm_n[:, l], "mm": state_mlstm_m[:, l],
               "ret": state_ret[:, l]}
        xs, _ = _layer(xs, mods[l, 1:1 + dec_batch], params[l], dec_batch, dec_seq, ctx)

    stack = lambda name: jnp.stack([s[name] for s in states], axis=1)
    return (xp.reshape(batch, seq, D_MODEL), xs.reshape(dec_batch, dec_seq, D_MODEL),
            stack("na_k"), stack("na_v"), stack("gla"), stack("mC"), stack("mn"), stack("mm"), stack("ret"))
```

```python
import functools

import numpy as np
import jax
import jax.numpy as jnp
from jax import lax
from jax.experimental import pallas as pl
from jax.experimental.pallas import tpu as pltpu

D_MODEL = 1024
DEPTH = 2
HEAD_DIM = 64
N_HEADS = 4
GROUP_WIDTH = N_HEADS * HEAD_DIM
N_MOD = 9
GLA_RANK = 16
GLA_TAU = 16.0
CHUNK = 64
GRID_W = 64
NA_KH = 8
NA_KW = 16
ROPE_BASE = 10000.0
EPS = 1e-6
D_FF = 2816

HEADS_PER_GROUP = 2
GROUP_LANES = HEADS_PER_GROUP * HEAD_DIM
N_GROUPS = N_HEADS // HEADS_PER_GROUP
GLA_SUB = 16
GLA_SAFE_LOG2 = 64.0
ROW_TILE_IN = 256
ROW_TILE_OUT = 512
ROW_PARTS = 2
CTX_SEQS_PER_STEP = 4
PRE_ROWS = 128
ROWS_IN_FLIGHT = 2
NA_ROWS_IN_FLIGHT = 4
LOG2E = 1.4426950408889634
CHUNKS_IN_FLIGHT = 8
SMALL_LANES = 128
MI_OFF = 2 * GLA_RANK
MF_OFF = MI_OFF + 2 * N_HEADS
MP_OFF = MF_OFF + 2 * N_HEADS
NEG = -1e30
VMEM_LIMIT = 56 * 1024 * 1024

F32 = jnp.float32
BF16 = jnp.bfloat16
HI = lax.Precision.HIGHEST

_OFF = {}
_o = 0
for _name, _size in (("gq", 256), ("gk", 256), ("gv", 256), ("gg", 256), ("ga", 32), ("nq", 256), ("nk", 256),
                     ("nv", 256), ("mq", 256), ("mk", 256), ("mv", 256), ("mo", 256), ("mi", 8), ("mf", 8),
                     ("rq", 256), ("rk", 256), ("rv", 256), ("rg", 256)):
    _OFF[_name] = _o
    _o += _size
N_IN = _o
_MIXER_COLS = (("gq", "gk", "gv", "gg"), ("nq", "nk", "nv", None), ("mq", "mk", "mv", "mo"), ("rq", "rk", "rv", "rg"))
MAIN_COLS = 4 * N_GROUPS * 4 * GROUP_LANES


def _mm(a, b):
    return jnp.dot(a.astype(BF16), b.astype(BF16), preferred_element_type=F32)


def _mm_nt(a, b):
    return lax.dot_general(a.astype(BF16), b.astype(BF16), (((1,), (1,)), ((), ())), preferred_element_type=F32)


def _mm_tn(a, b):
    return lax.dot_general(a.astype(BF16), b.astype(BF16), (((0,), (0,)), ((), ())), preferred_element_type=F32)


def _split3(x):
    hi = x.astype(BF16)
    rest = x - hi.astype(F32)
    mid = rest.astype(BF16)
    lo = (rest - mid.astype(F32)).astype(BF16)
    return jnp.concatenate([hi, mid, lo], axis=1)


def _dot_split(pieces, w):
    wb = w.astype(BF16)
    return jnp.dot(pieces, jnp.concatenate([wb, wb, wb], axis=0), preferred_element_type=F32)


def _mm_split(x, w):
    return _dot_split(_split3(x), w)


def _log_sigmoid(x):
    return jnp.minimum(x, 0.0) - jnp.log1p(jnp.exp(-jnp.abs(x)))


def _silu(x):
    return x * jax.nn.sigmoid(x)


def _iota(shape, dim):
    return lax.broadcasted_iota(jnp.int32, shape, dim)


def _head_of(idx):
    return lax.shift_right_logical(idx, 6)


def _head_masks():
    lane = _iota((1, GROUP_LANES), 1)
    return [(_head_of(lane) == h).astype(F32) for h in range(HEADS_PER_GROUP)]


def _bd_mask():
    n = GROUP_LANES
    return (_head_of(_iota((n, n), 0)) == _head_of(_iota((n, n), 1))).astype(F32)


def _stack(x, masks):
    return jnp.concatenate([x * m for m in masks], axis=0)


def _unstack(r, masks, n):
    out = r[0:n] * masks[0]
    for h in range(1, len(masks)):
        out = out + r[h * n:(h + 1) * n] * masks[h]
    return out


def _rmsnorm(x, w):
    return x * lax.rsqrt(jnp.mean(x * x, axis=-1, keepdims=True) + EPS) * w


def _split_lanes(x, lane_in_head, at, fill, fill_at):
    hi = x.astype(BF16).astype(F32)
    rest = x - hi
    mid = rest.astype(BF16).astype(F32)
    lo = rest - mid
    out = jnp.where(lane_in_head == at, hi, jnp.where(lane_in_head == at + 1, mid,
                                                      jnp.where(lane_in_head == at + 2, lo, 0.0)))
    is_fill = (lane_in_head >= fill_at) & (lane_in_head < fill_at + 3)
    return jnp.where(is_fill, fill, out)


def _mod_kernel(c_ref, w_ref, b_ref, o_ref):
    s = _silu(c_ref[...])
    o_ref[0] = _mm(s, w_ref[0]) + b_ref[0]


def _modulation(cvec, w_mod, b_mod):
    n = N_MOD * D_MODEL
    tn = n // 8
    return pl.pallas_call(
        _mod_kernel,
        grid=(DEPTH, n // tn),
        in_specs=[
            pl.BlockSpec((8, D_MODEL), lambda l, j: (0, 0)),
            pl.BlockSpec((1, D_MODEL, tn), lambda l, j: (l, 0, j)),
            pl.BlockSpec((1, 1, tn), lambda l, j: (l, 0, j)),
        ],
        out_specs=pl.BlockSpec((1, 8, tn), lambda l, j: (l, 0, j)),
        out_shape=jax.ShapeDtypeStruct((DEPTH, 8, n), F32),
        compiler_params=pltpu.CompilerParams(dimension_semantics=("parallel", "parallel"),
                                             vmem_limit_bytes=VMEM_LIMIT),
        name="modulation",
    )(cvec, w_mod, b_mod.reshape(DEPTH, 1, n))


def _half_ffn(x, mod, g, w1_ref, w3_ref, w2_ref):
    h = _rmsnorm(x, g[0:1]) * (1.0 + mod[1:2]) + mod[0:1]
    hb = h.astype(BF16)
    h1 = jnp.dot(hb, w1_ref[...], preferred_element_type=F32)
    h3 = jnp.dot(hb, w3_ref[...], preferred_element_type=F32)
    yield
    y = jnp.dot((_silu(h1) * h3).astype(BF16), w2_ref[...], preferred_element_type=F32)
    yield
    return x + 0.5 * mod[2:3] * _rmsnorm(y, g[1:2])


def _row_parts(tm):
    part = tm // ROW_PARTS
    return [slice(j * part, (j + 1) * part) for j in range(ROW_PARTS)]


def _mlstm_gate_scans(us, gate_bias):
    lane = _iota(us.shape, 1)
    field = 2 * N_HEADS
    fwd = (lane & (field - 1)) < N_HEADS
    pre = us + gate_bias
    lf = jnp.where((lane >= MF_OFF) & (lane < MF_OFF + field), _log_sigmoid(pre), 0.0)
    f = jnp.where(fwd, _chunk_scan(lf, False, jnp.add), _chunk_scan(lf, True, jnp.add))
    x = pre - pltpu.roll(f, SMALL_LANES - (MF_OFF - MI_OFF), 1)
    p = jnp.where(fwd, _chunk_scan(x, False, jnp.maximum), _chunk_scan(x, True, jnp.maximum))
    p = pltpu.roll(p, MP_OFF - MI_OFF, 1)
    return jnp.where(lane < MI_OFF, us,
                     jnp.where(lane < MF_OFF, x, jnp.where(lane < MP_OFF, f, jnp.where(lane < MP_OFF + field, p, 0.0))))


def _gla_log_decay(u_small, wa, ba):
    a_pre = _mm(u_small, wa) + ba
    la = _log_sigmoid(a_pre) * (LOG2E / GLA_TAU)
    return jnp.concatenate([_chunk_scan(la[:, :GROUP_WIDTH], False, jnp.add),
                            _chunk_scan(la[:, GROUP_WIDTH:], True, jnp.add)], axis=1)


def _ffn_inproj_kernel(x_ref, mod_ref, g_ref, gb_ref, wa_ref, ba_ref, w1_ref, w3_ref, w2_ref, win_ref,
                       x_out_ref, um_ref, us_ref, ub_ref):
    mod = mod_ref[0]
    g = g_ref[...]

    def part(rows):
        x = yield from _half_ffn(x_ref[rows, :], mod[0:3], g[0:2], w1_ref, w3_ref, w2_ref)
        x_out_ref[rows, :] = x
        hb = (_rmsnorm(x, g[2:3]) * (1.0 + mod[4:5]) + mod[3:4]).astype(BF16)
        u_small = jnp.dot(hb, win_ref[:, MAIN_COLS:], preferred_element_type=F32)
        u_main = jnp.dot(hb, win_ref[:, :MAIN_COLS], preferred_element_type=F32)
        us_ref[rows, :] = _mlstm_gate_scans(u_small, gb_ref[...])
        ub_ref[rows, :] = _gla_log_decay(u_small, wa_ref[...], ba_ref[...])
        yield
        um_ref[rows, :] = u_main

    _interleave([part(rows) for rows in _row_parts(x_ref.shape[0])])


def _outproj_ffn_kernel(x_ref, ma_ref, mb_ref, mc_ref, md_ref, mod_ref, g_ref, wo_ref, w1_ref, w3_ref, w2_ref,
                        o_ref):
    mod = mod_ref[0]
    g = g_ref[...]

    def part(rows):
        mix = jnp.concatenate([ma_ref[rows, :], mb_ref[rows, :], mc_ref[rows, :], md_ref[rows, :]], axis=-1)
        y = jnp.dot(mix.astype(BF16), wo_ref[...], preferred_element_type=F32)
        yield
        x = x_ref[rows, :] + mod[2:3] * _rmsnorm(y, g[0:1])
        o_ref[rows, :] = yield from _half_ffn(x, mod[3:6], g[1:3], w1_ref, w3_ref, w2_ref)

    _interleave([part(rows) for rows in _row_parts(x_ref.shape[0])])


def _row_tile(n_mod, rows_per_batch, tile):
    return tile if n_mod == 1 else min(tile, rows_per_batch)


def _mod_index(n_mod, rows_per_batch, tm):
    if n_mod == 1:
        return lambda i: (0, 0, 0)
    per = rows_per_batch // tm
    return lambda i: (i // per, 0, 0)


_ROW_PARAMS = pltpu.CompilerParams(dimension_semantics=("parallel",), vmem_limit_bytes=VMEM_LIMIT)


def _ffn_weight_specs(layer, half):
    pick = lambda i: (layer, half, 0, 0)
    return [pl.BlockSpec((None, None, D_MODEL, D_FF), pick, pipeline_mode=pl.Buffered(1)),
            pl.BlockSpec((None, None, D_MODEL, D_FF), pick, pipeline_mode=pl.Buffered(1)),
            pl.BlockSpec((None, None, D_FF, D_MODEL), pick, pipeline_mode=pl.Buffered(1))]


def _ffn_inproj(x, mod6, g3, gate_bias, gla_wa, gla_ba, w1, w3, w2, w_big, layer, rows_per_batch):
    rows = x.shape[0]
    tm = _row_tile(mod6.shape[0], rows_per_batch, ROW_TILE_IN)
    row_spec = lambda width: pl.BlockSpec((tm, width), lambda i: (i, 0))
    const = lambda i: (0, 0)
    widths = (D_MODEL, MAIN_COLS, SMALL_LANES, 2 * GROUP_WIDTH)
    return pl.pallas_call(
        _ffn_inproj_kernel,
        grid=(rows // tm,),
        in_specs=[
            row_spec(D_MODEL),
            pl.BlockSpec((1, 6, D_MODEL), _mod_index(mod6.shape[0], rows_per_batch, tm)),
            pl.BlockSpec((3, D_MODEL), const),
            pl.BlockSpec((1, SMALL_LANES), const),
            pl.BlockSpec((SMALL_LANES, 2 * GROUP_WIDTH), const),
            pl.BlockSpec((1, 2 * GROUP_WIDTH), const),
            *_ffn_weight_specs(layer, 0),
            pl.BlockSpec((None, D_MODEL, MAIN_COLS + SMALL_LANES), lambda i: (layer, 0, 0),
                         pipeline_mode=pl.Buffered(1)),
        ],
        out_specs=[row_spec(w) for w in widths],
        out_shape=[jax.ShapeDtypeStruct((rows, w), F32) for w in widths],
        compiler_params=_ROW_PARAMS,
        name="ffn_inproj",
    )(x, mod6, g3, gate_bias, gla_wa, gla_ba, w1, w3, w2, w_big)


def _outproj_ffn(x, mixes, mod6, g3, w_out, w1, w3, w2, layer, rows_per_batch):
    rows = x.shape[0]
    tm = _row_tile(mod6.shape[0], rows_per_batch, ROW_TILE_OUT)
    row_spec = lambda width: pl.BlockSpec((tm, width), lambda i: (i, 0))
    return pl.pallas_call(
        _outproj_ffn_kernel,
        grid=(rows // tm,),
        in_specs=[
            row_spec(D_MODEL), *[row_spec(GROUP_WIDTH)] * 4,
            pl.BlockSpec((1, 6, D_MODEL), _mod_index(mod6.shape[0], rows_per_batch, tm)),
            pl.BlockSpec((3, D_MODEL), lambda i: (0, 0)),
            pl.BlockSpec((None, D_MODEL, D_MODEL), lambda i: (layer, 0, 0), pipeline_mode=pl.Buffered(1)),
            *_ffn_weight_specs(layer, 1),
        ],
        out_specs=row_spec(D_MODEL),
        out_shape=jax.ShapeDtypeStruct((rows, D_MODEL), F32),
        compiler_params=_ROW_PARAMS,
        name="outproj_ffn",
    )(x, *mixes, mod6, g3, w_out, w1, w3, w2)


def _seqs_per_step(latent):
    return 1 if latent else CTX_SEQS_PER_STEP


def _mixer_specs(mixer, seq, nb):
    gl = GROUP_LANES
    u_spec = pl.BlockSpec((nb * seq, 4 * gl), lambda b, g: (b, mixer * N_GROUPS + g))
    small_spec = pl.BlockSpec((nb * seq, SMALL_LANES), lambda b, g: (b, 0))
    out_spec = pl.BlockSpec((nb * seq, gl), lambda b, g: (b, g))
    return u_spec, small_spec, out_spec


def _state_spec(rows, nb=1):
    return pl.BlockSpec((nb, 1, 2, rows, GROUP_LANES), lambda b, g: (b, g, 0, 0, 0))


def _head_state_shape(batch):
    return jax.ShapeDtypeStruct((batch, 2, N_HEADS, HEAD_DIM, HEAD_DIM), F32)


def _head_state_spec(nb):
    return pl.BlockSpec((nb, 2, HEADS_PER_GROUP, HEAD_DIM, HEAD_DIM), lambda b, g: (b, 0, g, 0, 0))


def _store_head_blocks(ref, s, d, mat):
    for h in range(HEADS_PER_GROUP):
        lo = h * HEAD_DIM
        ref[s, d, h] = mat[lo:lo + HEAD_DIM, lo:lo + HEAD_DIM]


_MIXER_PARAMS = pltpu.CompilerParams(dimension_semantics=("parallel", "parallel"), vmem_limit_bytes=VMEM_LIMIT)


def _chunk_rows(c):
    return pl.ds(pl.multiple_of(c * CHUNK, CHUNK), CHUNK)


def _scan_order(t, n_chunks, reverse):
    return (n_chunks - 1 - t) if reverse else t


def _tile_row(row):
    return jnp.broadcast_to(row, (8, row.shape[1]))


def _edge_row(ref, c, reverse):
    if reverse:
        return ref[pl.ds(pl.multiple_of(c * CHUNK, 8), 8), :][0:1]
    return ref[pl.ds(pl.multiple_of(c * CHUNK + CHUNK - 8, 8), 8), :][7:8]


def _chunk_scan(x, reverse, op):
    n = x.shape[0]
    row = _iota(x.shape, 0) & (CHUNK - 1)
    s = 1
    while s < CHUNK:
        if reverse:
            shifted = pltpu.roll(x, n - s, 0)
            ok = row < CHUNK - s
        else:
            shifted = pltpu.roll(x, s, 0)
            ok = row >= s
        x = jnp.where(ok, op(x, shifted), x)
        s *= 2
    return x


def _row_loop(seq, fn):
    blk = min(seq, PRE_ROWS)
    k = min(seq // blk, ROWS_IN_FLIGHT)

    def body(i, c):
        _interleave([fn(pl.ds(pl.multiple_of((i * k + j) * blk, blk), blk)) for j in range(k)])
        return c

    if seq == blk * k:
        _interleave([fn(pl.ds(j * blk, blk)) for j in range(k)])
    else:
        lax.fori_loop(0, seq // (blk * k), body, 0)


def _chunk_loop(n_chunks, make_stream):
    k = min(n_chunks, CHUNKS_IN_FLIGHT)

    def body(t, carry):
        _interleave([make_stream(t * k + j) for j in range(k)])
        return carry

    if n_chunks == k:
        body(0, 0)
    else:
        lax.fori_loop(0, n_chunks // k, body, 0)


def _interleave(streams):
    results = [None] * len(streams)
    live = []
    for i, s in enumerate(streams):
        if hasattr(s, "send"):
            live.append(i)
        else:
            results[i] = s
    while live:
        for i in list(live):
            try:
                next(streams[i])
            except StopIteration as done:
                results[i] = done.value
                live.remove(i)
    return results


def _state_scan(n_chunks, reverse, s_ref, decay_ref, init, base):
    def body(t, s):
        c = base + _scan_order(t, n_chunks, reverse)
        inc = s_ref[c]
        s_ref[c] = s
        return decay_ref[c][0:1] * s + inc

    return lax.fori_loop(0, n_chunks, body, init)


def _finalize(seq, acc_ref, o_ref, fn):
    def blk(rows):
        res = fn(rows, acc_ref[rows, :] + o_ref[rows, :])
        if hasattr(res, "send"):
            res = yield from res
        o_ref[rows, :] = res

    _row_loop(seq, blk)


def _rope(x, cos, sin_signed):
    lane = _iota(x.shape, 1)
    first = (lane & 31) < 16
    swapped = jnp.where(first, pltpu.roll(x, GROUP_LANES - 16, 1), pltpu.roll(x, 16, 1))
    return x * cos + swapped * sin_signed


def _ret_kernel(dec_ref, nw_ref, u_ref, *rest, seq, nb, rope, has_state, emit_state):
    rest = list(rest)
    cos_ref, sin_ref = (rest.pop(0), rest.pop(0)) if rope else (None, None)
    s0_ref = rest.pop(0) if has_state else None
    o_ref = rest.pop(0)
    st_ref = rest.pop(0) if emit_state else None
    acc_ref, q_ref, k_ref, s_ref, cdec_ref = rest

    gl, L, G = GROUP_LANES, CHUNK, HEADS_PER_GROUP
    n_chunks = seq // L
    g = pl.program_id(1)
    masks = _head_masks()
    bd = _bd_mask()
    bd_avg = bd * (1.0 / HEAD_DIM)
    scale = HEAD_DIM ** -0.5
    ri = _iota((L, gl), 0).astype(F32)
    si = _iota((G * L, L), 0)
    sj = _iota((G * L, L), 1)
    i_idx = si & (L - 1)
    hrow = _head_of(_iota((G * L, 1), 0))

    def prepare(rows):
        q = u_ref[rows, 0:gl]
        k = u_ref[rows, gl:2 * gl]
        if rope:
            cos = cos_ref[rows, :]
            sin = sin_ref[rows, :]
            q = _rope(q, cos, sin)
            k = _rope(k, cos, sin)
        q_ref[rows, :] = q * scale
        k_ref[rows, :] = k

    _row_loop(nb * seq, prepare)

    for d in (0, 1):
        reverse = d == 1
        raw_row = masks[0] * dec_ref[d, g * G]
        raw_col = jnp.where(hrow == 0, dec_ref[d, g * G], 0.0)
        for h in range(1, G):
            raw_row = raw_row + masks[h] * dec_ref[d, g * G + h]
            raw_col = jnp.where(hrow == h, dec_ref[d, g * G + h], raw_col)
        lg_row = _log_sigmoid(raw_row)
        lg_col = _log_sigmoid(raw_col)
        diff = ((sj - i_idx) if reverse else (i_idx - sj)).astype(F32)
        dmat = jnp.where(diff >= 0, jnp.exp(jnp.maximum(diff, 0.0) * lg_col), 0.0)
        qdec = jnp.exp(((L - ri) if reverse else (ri + 1.0)) * lg_row)
        kdec = jnp.exp((ri if reverse else (L - 1.0 - ri)) * lg_row)
        cdec_tile = _tile_row(jnp.exp(float(L) * lg_row))
        out_ref = o_ref if reverse else acc_ref

        def increments(c, kdec=kdec, cdec_tile=cdec_tile):
            rows = _chunk_rows(c)
            inc = _mm_tn(k_ref[rows, :] * kdec, u_ref[rows, 2 * gl:3 * gl])
            yield
            s_ref[c] = bd * inc
            cdec_ref[c] = cdec_tile

        _chunk_loop(nb * n_chunks, increments)
        for s in range(nb):
            s_fin = _state_scan(n_chunks, reverse, s_ref, cdec_ref,
                                s0_ref[s, 0, d] if has_state else jnp.zeros((gl, gl), F32), s * n_chunks)
            if emit_state:
                _store_head_blocks(st_ref, s, d, s_fin)

        def outputs(c, dmat=dmat, qdec=qdec, out_ref=out_ref):
            rows = _chunk_rows(c)
            q = q_ref[rows, :]
            v = u_ref[rows, 2 * gl:3 * gl]
            att = _mm_nt(_stack(q, masks), k_ref[rows, :])
            inter = _mm(q * qdec, s_ref[c])
            yield
            intra = _mm(att * dmat, v)
            yield
            out_ref[rows, :] = _unstack(intra, masks, L) + inter

        _chunk_loop(nb * n_chunks, outputs)

    def finalize(rows, o):
        mu = _mm_split(o, bd_avg)
        yield
        dev = o - mu
        var = _mm_split(dev * dev, bd_avg)
        yield
        return dev * lax.rsqrt(var + EPS) * nw_ref[...] * _silu(u_ref[rows, 3 * gl:4 * gl])

    _finalize(nb * seq, acc_ref, o_ref, finalize)


def _ret_tables(seq):
    t = np.arange(seq)
    quarter = HEAD_DIM // 4
    inv = (ROPE_BASE ** (-np.arange(quarter, dtype=np.float32) / quarter)).astype(np.float32)
    ang_r = (t // GRID_W).astype(np.float32)[:, None] * inv
    ang_c = (t % GRID_W).astype(np.float32)[:, None] * inv
    cos = np.concatenate([np.cos(ang_r), np.cos(ang_r), np.cos(ang_c), np.cos(ang_c)], axis=1)
    sin = np.concatenate([-np.sin(ang_r), np.sin(ang_r), -np.sin(ang_c), np.sin(ang_c)], axis=1)
    tile = lambda a: jnp.asarray(np.tile(a.astype(np.float32), (1, HEADS_PER_GROUP)))
    return tile(cos), tile(sin)


def _retention(u_main, ret_decay, norm_w, batch, seq, state):
    latent = state is not None
    nb = _seqs_per_step(latent)
    u_spec, _, out_spec = _mixer_specs(3, seq, nb)
    gl = GROUP_LANES
    rows, chunks = nb * seq, nb * seq // CHUNK
    const2 = lambda b, g: (0, 0)
    in_specs = [pl.BlockSpec(memory_space=pltpu.SMEM), pl.BlockSpec((1, gl), const2), u_spec]
    args = [ret_decay, jnp.tile(norm_w.reshape(1, HEAD_DIM), (1, HEADS_PER_GROUP)), u_main]
    if latent:
        cos, sin = _ret_tables(seq)
        in_specs += [pl.BlockSpec((seq, gl), const2), pl.BlockSpec((seq, gl), const2), _state_spec(gl)]
        args += [cos, sin, state]
    out_shape = [jax.ShapeDtypeStruct((batch * seq, GROUP_WIDTH), F32)]
    out_specs = [out_spec]
    if not latent:
        out_shape.append(_head_state_shape(batch))
        out_specs.append(_head_state_spec(nb))
    res = pl.pallas_call(
        functools.partial(_ret_kernel, seq=seq, nb=nb, rope=latent, has_state=latent, emit_state=not latent),
        grid=(batch // nb, N_GROUPS),
        in_specs=in_specs,
        out_specs=out_specs,
        out_shape=out_shape,
        scratch_shapes=[pltpu.VMEM((rows, gl), F32), pltpu.VMEM((rows, gl), F32), pltpu.VMEM((rows, gl), F32),
                        pltpu.VMEM((chunks, gl, gl), F32), pltpu.VMEM((chunks, 8, gl), F32)],
        compiler_params=_MIXER_PARAMS,
        name="retention",
    )(*args)
    return res[0], (None if latent else res[1])


def _gla_kernel(nw_ref, u_ref, bf_ref, bb_ref, *rest, seq, nb, has_state, emit_state):
    rest = list(rest)
    s0_ref = rest.pop(0) if has_state else None
    o_ref = rest.pop(0)
    st_ref = rest.pop(0) if emit_state else None
    acc_ref, s_ref, dec_ref, steep_ref = rest

    gl, L, G, SB = GROUP_LANES, CHUNK, HEADS_PER_GROUP, GLA_SUB
    n_chunks = seq // L
    n_sub = L // SB
    masks = _head_masks()
    bd = _bd_mask()
    bd_avg = bd * (1.0 / HEAD_DIM)
    scale = HEAD_DIM ** -0.5
    half = SB // 2
    pair_row = _iota((SB * SB, gl), 0)
    pi = lax.shift_right_logical(pair_row, 4)
    pj = pair_row & (SB - 1)
    pair_sum = (lax.shift_right_logical(_iota((SB, SB * SB), 1), 4) == _iota((SB, SB * SB), 0)).astype(BF16)
    zeros_half = jnp.zeros((half, gl), F32)

    for d in (0, 1):
        reverse = d == 1
        pair_ok = (pj >= pi) if reverse else (pj <= pi)
        out_ref = o_ref if reverse else acc_ref

        b_ref = bb_ref if reverse else bf_ref
        steep_ref[...] = jnp.zeros((8, gl), F32)

        def increments(c, reverse=reverse, b_ref=b_ref):
            rows = _chunk_rows(c)
            b = b_ref[rows, :]
            btot = b[0:1] if reverse else b[L - 1:L]
            inc = _mm_tn(u_ref[rows, 2 * gl:3 * gl], u_ref[rows, gl:2 * gl] * jnp.exp2(btot - b))
            yield
            s_ref[c] = bd * inc
            dec_ref[c] = _tile_row(jnp.exp2(btot))
            steep_ref[...] = jnp.maximum(steep_ref[...], _tile_row(-btot))

        _chunk_loop(nb * n_chunks, increments)
        factorise_all = jnp.max(steep_ref[...]) <= GLA_SAFE_LOG2
        for s in range(nb):
            s_fin = _state_scan(n_chunks, reverse, s_ref, dec_ref,
                                s0_ref[s, 0, d] if has_state else jnp.zeros((gl, gl), F32), s * n_chunks)
            if emit_state:
                _store_head_blocks(st_ref, s, d, s_fin.T)

        def outputs_factorised(c, reverse=reverse, out_ref=out_ref, b_ref=b_ref):
            rows = _chunk_rows(c)
            b = b_ref[rows, :]
            q = u_ref[rows, 0:gl] * scale
            k = u_ref[rows, gl:2 * gl]
            v = u_ref[rows, 2 * gl:3 * gl]
            o_inter = _mm_nt(q * jnp.exp2(b), s_ref[c])
            atts = []
            for i_blk in range(n_sub):
                lo = i_blk * SB
                bi = b[lo:lo + SB]
                ref, keys = (bi[SB - 1:SB], slice(lo, L)) if reverse else (bi[0:1], slice(0, lo + SB))
                qd = q[lo:lo + SB] * jnp.exp2(bi - ref)
                kd = k[keys] * jnp.exp2(ref - b[keys])
                att = _mm_nt(_stack(qd, masks), kd)
                n_keys = keys.stop - keys.start
                qrow = _iota((G * SB, n_keys), 0) & (SB - 1)
                kcol = _iota((G * SB, n_keys), 1)
                ok = (kcol >= qrow) if reverse else (kcol <= qrow + lo)
                atts.append((jnp.where(ok, att, 0.0), keys))
            yield
            outs = [_mm(a, v[keys]) for a, keys in atts]
            yield
            out_ref[rows, :] = o_inter + jnp.concatenate([_unstack(o, masks, SB) for o in outs], axis=0)

        def outputs_direct(c, reverse=reverse, pair_ok=pair_ok, out_ref=out_ref, b_ref=b_ref):
            rows = _chunk_rows(c)
            b = b_ref[rows, :]
            q = u_ref[rows, 0:gl] * scale
            k = u_ref[rows, gl:2 * gl]
            v = u_ref[rows, 2 * gl:3 * gl]
            o_inter = _mm_nt(q * jnp.exp2(b), s_ref[c])
            diag, off = [], []
            for i_blk in range(n_sub):
                lo = i_blk * SB
                qi, ki, bi = q[lo:lo + SB], k[lo:lo + SB], b[lo:lo + SB]
                prods = []
                for i in range(SB):
                    if reverse and i >= half:
                        part = slice(half, SB)
                    elif (not reverse) and i < half:
                        part = slice(0, half)
                    else:
                        part = slice(0, SB)
                    p = (qi[i:i + 1] * ki[part]) * jnp.exp2(bi[i:i + 1] - bi[part])
                    if part.start == half:
                        prods += [zeros_half, p]
                    elif part.stop == half:
                        prods += [p, zeros_half]
                    else:
                        prods.append(p)
                diag.append(_mm(jnp.concatenate(prods, axis=0), bd))
                if reverse and i_blk < n_sub - 1:
                    ref, other = bi[SB - 1:SB], slice(lo + SB, L)
                elif (not reverse) and i_blk > 0:
                    ref, other = bi[0:1], slice(0, lo)
                else:
                    off.append(None)
                    continue
                qd = qi * jnp.exp2(bi - ref)
                kd = k[other] * jnp.exp2(ref - b[other])
                off.append((_mm_nt(_stack(qd, masks), kd), other))
            yield
            off = [None if a is None else _mm(a[0], v[a[1]]) for a in off]
            for i_blk in range(n_sub):
                vi = v[i_blk * SB:(i_blk + 1) * SB]
                weighted = jnp.where(pair_ok, diag[i_blk] * jnp.concatenate([vi] * SB, axis=0), 0.0)
                diag[i_blk] = jnp.dot(pair_sum, weighted.astype(BF16), preferred_element_type=F32)
            yield
            blocks = []
            for i_blk in range(n_sub):
                o_blk = diag[i_blk]
                if off[i_blk] is not None:
                    o_blk = o_blk + _unstack(off[i_blk], masks, SB)
                blocks.append(o_blk)
            out_ref[rows, :] = o_inter + jnp.concatenate(blocks, axis=0)

        @pl.when(factorise_all)
        def _():
            _chunk_loop(nb * n_chunks, outputs_factorised)

        @pl.when(jnp.logical_not(factorise_all))
        def _():
            _chunk_loop(nb * n_chunks, outputs_direct)

    def finalize(rows, o):
        ms = _mm_split(o * o, bd_avg)
        yield
        return o * lax.rsqrt(ms + EPS) * nw_ref[...] * _silu(u_ref[rows, 3 * gl:4 * gl])

    _finalize(nb * seq, acc_ref, o_ref, finalize)


def _gla(u_main, log_decay, norm_w, batch, seq, state):
    latent = state is not None
    nb = _seqs_per_step(latent)
    u_spec, _, out_spec = _mixer_specs(0, seq, nb)
    gl = GROUP_LANES
    rows, chunks = nb * seq, nb * seq // CHUNK
    in_specs = [
        pl.BlockSpec((1, gl), lambda b, g: (0, 0)),
        u_spec,
        pl.BlockSpec((rows, gl), lambda b, g: (b, g)),
        pl.BlockSpec((rows, gl), lambda b, g: (b, N_GROUPS + g)),
    ]
    args = [jnp.tile(norm_w.reshape(1, HEAD_DIM), (1, HEADS_PER_GROUP)), u_main, log_decay, log_decay]
    if latent:
        in_specs.append(_state_spec(gl))
        args.append(state)
    out_shape = [jax.ShapeDtypeStruct((batch * seq, GROUP_WIDTH), F32)]
    out_specs = [out_spec]
    if not latent:
        out_shape.append(_head_state_shape(batch))
        out_specs.append(_head_state_spec(nb))
    res = pl.pallas_call(
        functools.partial(_gla_kernel, seq=seq, nb=nb, has_state=latent, emit_state=not latent),
        grid=(batch // nb, N_GROUPS),
        in_specs=in_specs,
        out_specs=out_specs,
        out_shape=out_shape,
        scratch_shapes=[pltpu.VMEM((rows, gl), F32),
                        pltpu.VMEM((chunks, gl, gl), F32), pltpu.VMEM((chunks, 8, gl), F32),
                        pltpu.VMEM((8, gl), F32)],
        compiler_params=_MIXER_PARAMS,
        name="gla",
    )(*args)
    return res[0], (None if latent else res[1])


def _mlstm_kernel(e_ref, u_ref, us_ref, *rest, seq, nb, has_state, emit_state):
    rest = list(rest)
    if has_state:
        c0_ref, n0_ref, m0_ref = rest.pop(0), rest.pop(0), rest.pop(0)
    o_ref = rest.pop(0)
    if emit_state:
        cst_ref, nst_ref, mst_ref = rest.pop(0), rest.pop(0), rest.pop(0)
    acc_ref, f_ref, x_ref, p_ref, s_ref, nu_ref, wp_ref, min_ref, mout_ref = rest

    gl, L, G = GROUP_LANES, CHUNK, HEADS_PER_GROUP
    n_chunks = seq // L
    masks = _head_masks()
    bd = _bd_mask()
    scale = HEAD_DIM ** -0.5
    si = _iota((G * L, L), 0) & (L - 1)
    sj = _iota((G * L, L), 1)
    lane_in_head = _iota((L, gl), 1) & (HEAD_DIM - 1)

    for d in (0, 1):
        reverse = d == 1
        causal = (sj >= si) if reverse else (sj <= si)
        last = 0 if reverse else L - 1
        out_ref = o_ref if reverse else acc_ref

        def gates(rows, d=d):
            r = _mm_split(us_ref[rows, :], e_ref[d, 0])
            yield
            x_ref[rows, :] = r[:, 0:gl]
            f_ref[rows, :] = r[:, gl:2 * gl]
            p_ref[rows, :] = r[:, 2 * gl:3 * gl]

        _row_loop(nb * seq, gates)

        def stabiliser(t, m_row, base, reverse=reverse):
            c = base + _scan_order(t, n_chunks, reverse)
            m_new = _edge_row(f_ref, c, reverse) + jnp.maximum(m_row, _edge_row(p_ref, c, reverse))
            min_ref[c] = _tile_row(m_row)
            mout_ref[c] = _tile_row(m_new)
            return m_new

        m_fin = [lax.fori_loop(0, n_chunks, functools.partial(stabiliser, base=s * n_chunks),
                               m0_ref[s, 0, d] if has_state else jnp.zeros((1, gl), F32)) for s in range(nb)]

        def increments(c, last=last):
            rows = _chunk_rows(c)
            f_tot = f_ref[rows, :][last:last + 1]
            m_in = min_ref[c][0:1]
            m_out = mout_ref[c][0:1]
            kw = u_ref[rows, gl:2 * gl] * jnp.exp(f_tot + x_ref[rows, :] - m_out)
            inc = _mm_tn(kw, u_ref[rows, 2 * gl:3 * gl])
            yield
            s_ref[c] = bd * inc
            nu_ref[c] = _tile_row(jnp.sum(kw, axis=0, keepdims=True))
            wp_ref[c] = _tile_row(jnp.exp(f_tot + m_in - m_out))

        _chunk_loop(nb * n_chunks, increments)

        def memory(t, carry, base, reverse=reverse):
            c_mat, n_row = carry
            c = base + _scan_order(t, n_chunks, reverse)
            inc = s_ref[c]
            nu = nu_ref[c][0:1]
            wp = wp_ref[c][0:1]
            s_ref[c] = c_mat
            nu_ref[c] = _tile_row(n_row)
            return wp * c_mat + inc, wp * n_row + nu

        for s in range(nb):
            if has_state:
                init = (c0_ref[s, 0, d], n0_ref[s, 0, d])
            else:
                init = (jnp.zeros((gl, gl), F32), jnp.zeros((1, gl), F32))
            c_fin, n_fin = lax.fori_loop(0, n_chunks, functools.partial(memory, base=s * n_chunks), init)
            if emit_state:
                _store_head_blocks(cst_ref, s, d, c_fin)
                nst_ref[s, 0, d], mst_ref[s, 0, d] = n_fin, m_fin[s]

        def outputs(c, causal=causal, out_ref=out_ref):
            rows = _chunk_rows(c)
            x = x_ref[rows, :]
            m_in = min_ref[c][0:1]
            q = u_ref[rows, 0:gl] * scale
            k = u_ref[rows, gl:2 * gl]
            v = u_ref[rows, 2 * gl:3 * gl]
            m_full = jnp.maximum(m_in, p_ref[rows, :])
            lhs = _split_lanes(-m_full, lane_in_head, 0, 1.0, 3)
            rhs = _split_lanes(x, lane_in_head, 3, 1.0, 0)
            logw = _mm_nt(_stack(lhs, masks), rhs)
            qk = _mm_nt(_stack(q, masks), k)
            inter = _mm(q, s_ref[c])
            q_n = _mm_split(q * nu_ref[c][0:1], bd)
            yield
            s = qk * jnp.exp(jnp.where(causal, logw, NEG))
            s_hi = s.astype(BF16)
            ones = jnp.ones((L, gl), BF16)
            r = jnp.dot(s_hi, jnp.concatenate([v.astype(BF16), ones], axis=1), preferred_element_type=F32)
            sum_lo = jnp.dot((s - s_hi.astype(F32)).astype(BF16), ones, preferred_element_type=F32)
            yield
            w_inter = jnp.exp(m_in - m_full)
            num = _unstack(r[:, :gl], masks, L) + w_inter * inter
            den = _unstack(r[:, gl:] + sum_lo, masks, L) + w_inter * q_n
            out_ref[rows, :] = num / jnp.maximum(jnp.abs(den), jnp.exp(-(f_ref[rows, :] + m_full)))

        _chunk_loop(nb * n_chunks, outputs)

    def finalize(rows, o):
        return o * jax.nn.sigmoid(u_ref[rows, 3 * gl:4 * gl])

    _finalize(nb * seq, acc_ref, o_ref, finalize)


def _gate_expanders(offset):
    e = np.zeros((2, N_GROUPS, SMALL_LANES, GROUP_LANES), np.float32)
    for d in range(2):
        for g in range(N_GROUPS):
            for h in range(HEADS_PER_GROUP):
                src = offset + d * N_HEADS + g * HEADS_PER_GROUP + h
                e[d, g, src, h * HEAD_DIM:(h + 1) * HEAD_DIM] = 1.0
    return jnp.asarray(e)


def _gate_bias_row(b_i, b_f):
    row = jnp.zeros((1, SMALL_LANES), F32)
    row = row.at[0, MI_OFF:MI_OFF + 2 * N_HEADS].set(b_i.astype(F32).reshape(-1))
    return row.at[0, MF_OFF:MF_OFF + 2 * N_HEADS].set(b_f.astype(F32).reshape(-1))


def _mlstm(u_main, u_small, batch, seq, state):
    latent = state is not None
    nb = _seqs_per_step(latent)
    u_spec, small_spec, out_spec = _mixer_specs(2, seq, nb)
    gl = GROUP_LANES
    rows, chunks = nb * seq, nb * seq // CHUNK
    expand = jnp.concatenate([_gate_expanders(MI_OFF), _gate_expanders(MF_OFF), _gate_expanders(MP_OFF)], axis=-1)
    in_specs = [pl.BlockSpec((2, 1, SMALL_LANES, 3 * gl), lambda b, g: (0, g, 0, 0)), u_spec, small_spec]
    args = [expand, u_main, u_small]
    if latent:
        in_specs += [_state_spec(gl), _state_spec(1), _state_spec(1)]
        args += list(state)
    out_shape = [jax.ShapeDtypeStruct((batch * seq, GROUP_WIDTH), F32)]
    out_specs = [out_spec]
    if not latent:
        out_shape += [_head_state_shape(batch),
                      jax.ShapeDtypeStruct((batch, N_GROUPS, 2, 1, gl), F32),
                      jax.ShapeDtypeStruct((batch, N_GROUPS, 2, 1, gl), F32)]
        out_specs += [_head_state_spec(nb), _state_spec(1, nb), _state_spec(1, nb)]
    res = pl.pallas_call(
        functools.partial(_mlstm_kernel, seq=seq, nb=nb, has_state=latent, emit_state=not latent),
        grid=(batch // nb, N_GROUPS),
        in_specs=in_specs,
        out_specs=out_specs,
        out_shape=out_shape,
        scratch_shapes=[pltpu.VMEM((rows, gl), F32)] * 4 + [pltpu.VMEM((chunks, gl, gl), F32)]
        + [pltpu.VMEM((chunks, 8, gl), F32)] * 4,
        compiler_params=_MIXER_PARAMS,
        name="mlstm",
    )(*args)
    return res[0], (None if latent else tuple(res[1:]))


def _na_ctx_kernel(u_ref, o_ref, ko_ref, vo_ref, *, seq, nb):
    gl = GROUP_LANES
    masks = _head_masks()

    def sequence(s):
        rows = slice(s * seq, (s + 1) * seq)
        q = u_ref[rows, 0:gl] * (HEAD_DIM ** -0.5)
        k = u_ref[rows, gl:2 * gl]
        v = u_ref[rows, 2 * gl:3 * gl]
        sc = _mm_nt(_stack(q, masks), k)
        yield
        p = jnp.exp(sc - jnp.max(sc, axis=1, keepdims=True))
        o = _mm(p, v)
        yield
        o_ref[rows, :] = _unstack(o / jnp.sum(p, axis=1, keepdims=True), masks, seq)
        for h in range(HEADS_PER_GROUP):
            ko_ref[s, h] = k[:, h * HEAD_DIM:(h + 1) * HEAD_DIM]
            vo_ref[s, h] = v[:, h * HEAD_DIM:(h + 1) * HEAD_DIM]

    _interleave([sequence(s) for s in range(nb)])


def _na_context(u_main, batch, seq):
    nb = _seqs_per_step(False)
    u_spec, _, out_spec = _mixer_specs(1, seq, nb)
    kv_shape = jax.ShapeDtypeStruct((batch, N_HEADS, seq, HEAD_DIM), F32)
    kv_spec = pl.BlockSpec((nb, HEADS_PER_GROUP, seq, HEAD_DIM), lambda b, g: (b, g, 0, 0))
    return pl.pallas_call(
        functools.partial(_na_ctx_kernel, seq=seq, nb=nb),
        grid=(batch // nb, N_GROUPS),
        in_specs=[u_spec],
        out_specs=[out_spec, kv_spec, kv_spec],
        out_shape=[jax.ShapeDtypeStruct((batch * seq, GROUP_WIDTH), F32), kv_shape, kv_shape],
        compiler_params=_MIXER_PARAMS,
        name="na_context",
    )(u_main)


def _na_lat_kernel(tab_ref, kc_ref, vc_ref, u_ref, o_ref, *, seq):
    gl = GROUP_LANES
    grid_rows = seq // GRID_W
    win = NA_KH * GRID_W
    masks = _head_masks()
    scale = HEAD_DIM ** -0.5
    kc = kc_ref[0, 0]
    vc = vc_ref[0, 0]

    def grid_row(r):
        ks = jnp.clip(r - NA_KH // 2, 0, grid_rows - NA_KH)
        q_rows = pl.ds(pl.multiple_of(r * GRID_W, GRID_W), GRID_W)
        k_rows = pl.ds(pl.multiple_of(ks * GRID_W, GRID_W), win)
        qs = _stack(u_ref[q_rows, 0:gl] * scale, masks)
        s_loc = _mm_nt(qs, u_ref[k_rows, gl:2 * gl])
        s_ctx = _mm_nt(qs, kc)
        yield
        s_loc = s_loc + tab_ref[0, ks - r + NA_KH - 1]
        m = jnp.maximum(jnp.max(s_loc, axis=1, keepdims=True), jnp.max(s_ctx, axis=1, keepdims=True))
        p_loc = jnp.exp(s_loc - m)
        p_ctx = jnp.exp(s_ctx - m)
        den = jnp.sum(p_loc, axis=1, keepdims=True) + jnp.sum(p_ctx, axis=1, keepdims=True)
        o = _mm(p_loc, u_ref[k_rows, 2 * gl:3 * gl]) + _mm(p_ctx, vc)
        yield
        o_ref[q_rows, :] = _unstack(o / den, masks, GRID_W)

    def body(t, carry):
        _interleave([grid_row(t * NA_ROWS_IN_FLIGHT + j) for j in range(NA_ROWS_IN_FLIGHT)])
        return carry

    lax.fori_loop(0, grid_rows // NA_ROWS_IN_FLIGHT, body, 0)


def _na_bias_table(rpb):
    c = np.arange(GRID_W)
    rel = c[None, :] - c[:, None]
    cs = np.clip(c - NA_KW // 2, 0, GRID_W - NA_KW)
    valid = (c[None, :] >= cs[:, None]) & (c[None, :] < cs[:, None] + NA_KW)
    dcol = np.clip(rel, -(NA_KW - 1), NA_KW - 1) + NA_KW - 1
    bmat = jnp.where(valid[None, None], rpb.astype(F32)[:, :, dcol], NEG)
    tabs = jnp.stack([bmat[:, s:s + NA_KH] for s in range(NA_KH)], axis=1)
    tabs = tabs.transpose(0, 1, 3, 2, 4).reshape(N_GROUPS, HEADS_PER_GROUP, NA_KH, GRID_W, NA_KH * GRID_W)
    return tabs.transpose(0, 2, 1, 3, 4).reshape(N_GROUPS, NA_KH, HEADS_PER_GROUP * GRID_W, NA_KH * GRID_W)


def _heads_to_lanes(t):
    b, _, s, _ = t.shape
    t = t.reshape(b, N_GROUPS, HEADS_PER_GROUP, s, HEAD_DIM).transpose(0, 1, 3, 2, 4)
    return t.reshape(b, N_GROUPS, s, GROUP_LANES)


def _na_latent(u_main, rpb, k_ctx, v_ctx, batch, seq):
    u_spec, _, out_spec = _mixer_specs(1, seq, 1)
    gl = GROUP_LANES
    past = k_ctx.shape[2]
    tab = _na_bias_table(rpb)
    ctx_spec = pl.BlockSpec((1, 1, past, gl), lambda b, g: (b, g, 0, 0))
    return pl.pallas_call(
        functools.partial(_na_lat_kernel, seq=seq),
        grid=(batch, N_GROUPS),
        in_specs=[pl.BlockSpec((1,) + tab.shape[1:], lambda b, g: (g, 0, 0, 0)), ctx_spec, ctx_spec, u_spec],
        out_specs=out_spec,
        out_shape=jax.ShapeDtypeStruct((batch * seq, GROUP_WIDTH), F32),
        compiler_params=_MIXER_PARAMS,
        name="na_latent",
    )(tab, _heads_to_lanes(k_ctx.astype(F32)), _heads_to_lanes(v_ctx.astype(F32)), u_main)


def _to_block_diag(s):
    b = s.shape[0]
    G = HEADS_PER_GROUP
    s = s.astype(F32).reshape(b, 2, N_GROUPS, G, HEAD_DIM, 1, HEAD_DIM)
    eye = jnp.eye(G, dtype=F32).reshape(1, 1, 1, G, 1, G, 1)
    return (s * eye).reshape(b, 2, N_GROUPS, GROUP_LANES, GROUP_LANES).transpose(0, 2, 1, 3, 4)


def _rows_to_lanes(v):
    b = v.shape[0]
    return v.astype(F32).reshape(b, 2, N_GROUPS, 1, GROUP_LANES).transpose(0, 2, 1, 3, 4)


def _lanes_to_rows(v):
    b = v.shape[0]
    return v.transpose(0, 2, 1, 3, 4).reshape(b, 2, N_HEADS, HEAD_DIM)


def _swap_last(s):
    return jnp.swapaxes(s, -1, -2)


def _in_weight(w_in):
    gl = GROUP_LANES
    w_in = w_in.astype(BF16)
    zeros = jnp.zeros((DEPTH, D_MODEL, gl), BF16)
    cols = []
    for names in _MIXER_COLS:
        for g in range(N_GROUPS):
            for name in names:
                cols.append(zeros if name is None else w_in[..., _OFF[name] + g * gl:_OFF[name] + (g + 1) * gl])
    cols.append(w_in[..., _OFF["ga"]:_OFF["ga"] + 2 * GLA_RANK])
    cols.append(w_in[..., _OFF["mi"]:_OFF["mi"] + 2 * N_HEADS])
    cols.append(w_in[..., _OFF["mf"]:_OFF["mf"] + 2 * N_HEADS])
    cols.append(jnp.zeros((DEPTH, D_MODEL, SMALL_LANES - MF_OFF - 2 * N_HEADS), BF16))
    return jnp.concatenate(cols, axis=-1)


def _gla_gate_weight(w_a2_l):
    out = jnp.zeros((SMALL_LANES, 2 * GROUP_WIDTH), F32)
    for d in range(2):
        out = out.at[d * GLA_RANK:(d + 1) * GLA_RANK, d * GROUP_WIDTH:(d + 1) * GROUP_WIDTH].set(
            w_a2_l[d].astype(F32))
    return out.astype(BF16)


def _layer(x, mod, p, batch, seq, ctx):
    latent = ctx is not None
    l = p["layer"]
    x, u_main, u_small, log_decay = _ffn_inproj(x, mod[:, 0:6], p["norm_w"][0:3], p["gate_bias"], p["gla_wa"],
                                                p["gla_ba"], p["w1"], p["w3"], p["w2"], p["w_in"], l, seq)
    mix_a, st_gla = _gla(u_main, log_decay, p["gla_norm_w"], batch, seq,
                         _swap_last(_to_block_diag(ctx["gla"])) if latent else None)
    if latent:
        mix_b = _na_latent(u_main, p["na_rpb"], ctx["na_k"], ctx["na_v"], batch, seq)
    else:
        mix_b, na_k, na_v = _na_context(u_main, batch, seq)
    mix_c, st_ml = _mlstm(u_main, u_small, batch, seq,
                          (_to_block_diag(ctx["mC"]), _rows_to_lanes(ctx["mn"]),
                           _rows_to_lanes(jnp.repeat(ctx["mm"][..., None], HEAD_DIM, axis=-1))) if latent else None)
    mix_d, st_ret = _retention(u_main, p["ret_decay"], p["ret_norm_w"], batch, seq,
                               _to_block_diag(ctx["ret"]) if latent else None)
    x = _outproj_ffn(x, (mix_a, mix_b, mix_c, mix_d), mod[:, 3:9], p["norm_w"][3:6], p["w_out"],
                     p["w1"], p["w3"], p["w2"], l, seq)
    if latent:
        return x, None
    new = {
        "na_k": na_k, "na_v": na_v, "gla": st_gla, "mC": st_ml[0],
        "mn": _lanes_to_rows(st_ml[1]), "mm": _lanes_to_rows(st_ml[2])[..., 0], "ret": st_ret,
    }
    return x, new


def kernel(x_prompt, x_sample, cache_na_k, cache_na_v, state_gla, state_mlstm_C, state_mlstm_n, state_mlstm_m,
           state_ret, c, c_ctx, w_mod, b_mod, norm_w, ffn_w1, ffn_w3, ffn_w2, w_in, w_out, gla_w_a2, gla_b_a,
           gla_norm_w, na_rpb, mlstm_b_i, mlstm_b_f, ret_decay, ret_norm_w):
    batch, seq, _ = x_prompt.shape
    dec_batch, dec_seq, _ = x_sample.shape

    cvec = jnp.zeros((8, D_MODEL), F32).at[0].set(c_ctx).at[1:1 + dec_batch].set(c)
    mods = _modulation(cvec, w_mod, b_mod).reshape(DEPTH, 8, N_MOD, D_MODEL)

    w1, w3, w2 = ffn_w1.astype(BF16), ffn_w3.astype(BF16), ffn_w2.astype(BF16)
    w_in_all, w_out_all = _in_weight(w_in), w_out.astype(BF16)
    params = []
    for l in range(DEPTH):
        params.append({
            "layer": l, "norm_w": norm_w[l],
            "w1": w1, "w3": w3, "w2": w2, "w_in": w_in_all, "w_out": w_out_all,
            "gla_wa": _gla_gate_weight(gla_w_a2[l]), "gla_ba": gla_b_a[l].astype(F32).reshape(1, 2 * GROUP_WIDTH),
            "gla_norm_w": gla_norm_w[l],
            "na_rpb": na_rpb[l], "gate_bias": _gate_bias_row(mlstm_b_i[l], mlstm_b_f[l]),
            "ret_decay": ret_decay[l], "ret_norm_w": ret_norm_w[l],
        })

    xp = x_prompt.reshape(batch * seq, D_MODEL)
    states = []
    for l in range(DEPTH):
        xp, st = _layer(xp, mods[l, 0:1], params[l], batch, seq, None)
        states.append(st)

    xs = x_sample.reshape(dec_batch * dec_seq, D_MODEL)
    for l in range(DEPTH):
        ctx = {"na_k": cache_na_k[:, l], "na_v": cache_na_v[:, l], "gla": state_gla[:, l],
               "mC": state_mlstm_C[:, l], "mn": state_mlstm_n[:, l], "mm": state_mlstm_m[:, l],
               "ret": state_ret[:, l]}
        xs, _ = _layer(xs, mods[l, 1:1 + dec_batch], params[l], dec_batch, dec_seq, ctx)

    stack = lambda name: jnp.stack([s[name] for s in states], axis=1)
    return (xp.reshape(batch, seq, D_MODEL), xs.reshape(dec_batch, dec_seq, D_MODEL),
            stack("na_k"), stack("na_v"), stack("gla"), stack("mC"), stack("mn"), stack("mm"), stack("ret"))
```

```python
import functools

import numpy as np
import jax
import jax.numpy as jnp
from jax import lax
from jax.experimental import pallas as pl
from jax.experimental.pallas import tpu as pltpu

D_MODEL = 1024
DEPTH = 2
HEAD_DIM = 64
N_HEADS = 4
GROUP_WIDTH = N_HEADS * HEAD_DIM
N_MOD = 9
GLA_RANK = 16
GLA_TAU = 16.0
CHUNK = 64
GRID_W = 64
NA_KH = 8
NA_KW = 16
NA_QC = 16
NA_KCB = NA_QC + NA_KW
ROPE_BASE = 10000.0
EPS = 1e-6
D_FF = 2816

HEADS_PER_GROUP = 2
GROUP_LANES = HEADS_PER_GROUP * HEAD_DIM
N_GROUPS = N_HEADS // HEADS_PER_GROUP
GLA_SUB = 16
GLA_SAFE_LOG2 = 64.0
ROW_TILE_IN = 256
ROW_TILE_OUT = 512
ROW_PARTS = 2
CTX_SEQS_PER_STEP = 4
PRE_ROWS = 128
ROWS_IN_FLIGHT = 2
NA_ROWS_IN_FLIGHT = 4
LOG2E = 1.4426950408889634
CHUNKS_IN_FLIGHT = 8
SMALL_LANES = 128
MI_OFF = 2 * GLA_RANK
MF_OFF = MI_OFF + 2 * N_HEADS
MP_OFF = MF_OFF + 2 * N_HEADS
NEG = -1e30
VMEM_LIMIT = 56 * 1024 * 1024

F32 = jnp.float32
BF16 = jnp.bfloat16
HI = lax.Precision.HIGHEST

_OFF = {}
_o = 0
for _name, _size in (("gq", 256), ("gk", 256), ("gv", 256), ("gg", 256), ("ga", 32), ("nq", 256), ("nk", 256),
                     ("nv", 256), ("mq", 256), ("mk", 256), ("mv", 256), ("mo", 256), ("mi", 8), ("mf", 8),
                     ("rq", 256), ("rk", 256), ("rv", 256), ("rg", 256)):
    _OFF[_name] = _o
    _o += _size
N_IN = _o
_MIXER_COLS = (("gq", "gk", "gv", "gg"), ("nq", "nk", "nv", None), ("mq", "mk", "mv", "mo"), ("rq", "rk", "rv", "rg"))
MAIN_COLS = 4 * N_GROUPS * 4 * GROUP_LANES


def _mm(a, b):
    return jnp.dot(a.astype(BF16), b.astype(BF16), preferred_element_type=F32)


def _mm_nt(a, b):
    return lax.dot_general(a.astype(BF16), b.astype(BF16), (((1,), (1,)), ((), ())), preferred_element_type=F32)


def _mm_tn(a, b):
    return lax.dot_general(a.astype(BF16), b.astype(BF16), (((0,), (0,)), ((), ())), preferred_element_type=F32)


def _split3(x):
    hi = x.astype(BF16)
    rest = x - hi.astype(F32)
    mid = rest.astype(BF16)
    lo = (rest - mid.astype(F32)).astype(BF16)
    return jnp.concatenate([hi, mid, lo], axis=1)


def _dot_split(pieces, w):
    wb = w.astype(BF16)
    return jnp.dot(pieces, jnp.concatenate([wb, wb, wb], axis=0), preferred_element_type=F32)


def _mm_split(x, w):
    return _dot_split(_split3(x), w)


def _log_sigmoid(x):
    return jnp.minimum(x, 0.0) - jnp.log1p(jnp.exp(-jnp.abs(x)))


def _silu(x):
    return x * jax.nn.sigmoid(x)


def _iota(shape, dim):
    return lax.broadcasted_iota(jnp.int32, shape, dim)


def _head_of(idx):
    return lax.shift_right_logical(idx, 6)


def _head_masks():
    lane = _iota((1, GROUP_LANES), 1)
    return [(_head_of(lane) == h).astype(F32) for h in range(HEADS_PER_GROUP)]


def _bd_mask():
    n = GROUP_LANES
    return (_head_of(_iota((n, n), 0)) == _head_of(_iota((n, n), 1))).astype(F32)


def _stack(x, masks):
    return jnp.concatenate([x * m for m in masks], axis=0)


def _unstack(r, masks, n):
    out = r[0:n] * masks[0]
    for h in range(1, len(masks)):
        out = out + r[h * n:(h + 1) * n] * masks[h]
    return out


def _rmsnorm(x, w):
    return x * lax.rsqrt(jnp.mean(x * x, axis=-1, keepdims=True) + EPS) * w


def _split_lanes(x, lane_in_head, at, fill, fill_at):
    hi = x.astype(BF16).astype(F32)
    rest = x - hi
    mid = rest.astype(BF16).astype(F32)
    lo = rest - mid
    out = jnp.where(lane_in_head == at, hi, jnp.where(lane_in_head == at + 1, mid,
                                                      jnp.where(lane_in_head == at + 2, lo, 0.0)))
    is_fill = (lane_in_head >= fill_at) & (lane_in_head < fill_at + 3)
    return jnp.where(is_fill, fill, out)


def _mod_kernel(c_ref, w_ref, b_ref, o_ref):
    s = _silu(c_ref[...])
    o_ref[0] = _mm(s, w_ref[0]) + b_ref[0]


def _modulation(cvec, w_mod, b_mod):
    n = N_MOD * D_MODEL
    tn = n // 8
    return pl.pallas_call(
        _mod_kernel,
        grid=(DEPTH, n // tn),
        in_specs=[
            pl.BlockSpec((8, D_MODEL), lambda l, j: (0, 0)),
            pl.BlockSpec((1, D_MODEL, tn), lambda l, j: (l, 0, j)),
            pl.BlockSpec((1, 1, tn), lambda l, j: (l, 0, j)),
        ],
        out_specs=pl.BlockSpec((1, 8, tn), lambda l, j: (l, 0, j)),
        out_shape=jax.ShapeDtypeStruct((DEPTH, 8, n), F32),
        compiler_params=pltpu.CompilerParams(dimension_semantics=("parallel", "parallel"),
                                             vmem_limit_bytes=VMEM_LIMIT),
        name="modulation",
    )(cvec, w_mod, b_mod.reshape(DEPTH, 1, n))


def _half_ffn(x, mod, g, w1_ref, w3_ref, w2_ref):
    h = _rmsnorm(x, g[0:1]) * (1.0 + mod[1:2]) + mod[0:1]
    hb = h.astype(BF16)
    h1 = jnp.dot(hb, w1_ref[...], preferred_element_type=F32)
    h3 = jnp.dot(hb, w3_ref[...], preferred_element_type=F32)
    yield
    y = jnp.dot((_silu(h1) * h3).astype(BF16), w2_ref[...], preferred_element_type=F32)
    yield
    return x + 0.5 * mod[2:3] * _rmsnorm(y, g[1:2])


def _row_parts(tm):
    part = tm // ROW_PARTS
    return [slice(j * part, (j + 1) * part) for j in range(ROW_PARTS)]


def _mlstm_gate_scans(us, gate_bias):
    lane = _iota(us.shape, 1)
    field = 2 * N_HEADS
    fwd = (lane & (field - 1)) < N_HEADS
    pre = us + gate_bias
    lf = jnp.where((lane >= MF_OFF) & (lane < MF_OFF + field), _log_sigmoid(pre), 0.0)
    f = jnp.where(fwd, _chunk_scan(lf, False, jnp.add), _chunk_scan(lf, True, jnp.add))
    x = pre - pltpu.roll(f, SMALL_LANES - (MF_OFF - MI_OFF), 1)
    p = jnp.where(fwd, _chunk_scan(x, False, jnp.maximum), _chunk_scan(x, True, jnp.maximum))
    p = pltpu.roll(p, MP_OFF - MI_OFF, 1)
    return jnp.where(lane < MI_OFF, us,
                     jnp.where(lane < MF_OFF, x, jnp.where(lane < MP_OFF, f, jnp.where(lane < MP_OFF + field, p, 0.0))))


def _gla_gate(u_small, wa, ba):
    return _mm(u_small, wa) + ba


def _gla_log_decay(a_pre):
    la = _log_sigmoid(a_pre) * (LOG2E / GLA_TAU)
    return jnp.concatenate([_chunk_scan(la[:, :GROUP_WIDTH], False, jnp.add),
                            _chunk_scan(la[:, GROUP_WIDTH:], True, jnp.add)], axis=1)


def _ffn_inproj_kernel(x_ref, mod_ref, g_ref, gb_ref, wa_ref, ba_ref, w1_ref, w3_ref, w2_ref, win_ref,
                       x_out_ref, um_ref, us_ref, ub_ref):
    mod = mod_ref[0]
    g = g_ref[...]

    def part(rows):
        x = yield from _half_ffn(x_ref[rows, :], mod[0:3], g[0:2], w1_ref, w3_ref, w2_ref)
        x_out_ref[rows, :] = x
        hb = (_rmsnorm(x, g[2:3]) * (1.0 + mod[4:5]) + mod[3:4]).astype(BF16)
        u_small = jnp.dot(hb, win_ref[:, MAIN_COLS:], preferred_element_type=F32)
        u_main = jnp.dot(hb, win_ref[:, :MAIN_COLS], preferred_element_type=F32)
        us_ref[rows, :] = _mlstm_gate_scans(u_small, gb_ref[...])
        ub_ref[rows, :] = _gla_log_decay(_gla_gate(u_small, wa_ref[...], ba_ref[...]))
        yield
        um_ref[rows, :] = u_main

    _interleave([part(rows) for rows in _row_parts(x_ref.shape[0])])


def _outproj_ffn_kernel(x_ref, ma_ref, mb_ref, mc_ref, md_ref, mod_ref, g_ref, wo_ref, w1_ref, w3_ref, w2_ref,
                        o_ref):
    mod = mod_ref[0]
    g = g_ref[...]

    def part(rows):
        mix = jnp.concatenate([ma_ref[rows, :], mb_ref[rows, :], mc_ref[rows, :], md_ref[rows, :]], axis=-1)
        y = jnp.dot(mix.astype(BF16), wo_ref[...], preferred_element_type=F32)
        yield
        x = x_ref[rows, :] + mod[2:3] * _rmsnorm(y, g[0:1])
        o_ref[rows, :] = yield from _half_ffn(x, mod[3:6], g[1:3], w1_ref, w3_ref, w2_ref)

    _interleave([part(rows) for rows in _row_parts(x_ref.shape[0])])


def _row_tile(n_mod, rows_per_batch, tile):
    return tile if n_mod == 1 else min(tile, rows_per_batch)


def _mod_index(n_mod, rows_per_batch, tm):
    if n_mod == 1:
        return lambda i: (0, 0, 0)
    per = rows_per_batch // tm
    return lambda i: (i // per, 0, 0)


_ROW_PARAMS = pltpu.CompilerParams(dimension_semantics=("parallel",), vmem_limit_bytes=VMEM_LIMIT)


def _ffn_weight_specs(layer, half):
    pick = lambda i: (layer, half, 0, 0)
    return [pl.BlockSpec((None, None, D_MODEL, D_FF), pick, pipeline_mode=pl.Buffered(1)),
            pl.BlockSpec((None, None, D_MODEL, D_FF), pick, pipeline_mode=pl.Buffered(1)),
            pl.BlockSpec((None, None, D_FF, D_MODEL), pick, pipeline_mode=pl.Buffered(1))]


def _ffn_inproj(x, mod6, g3, gate_bias, gla_wa, gla_ba, w1, w3, w2, w_big, layer, rows_per_batch):
    rows = x.shape[0]
    tm = _row_tile(mod6.shape[0], rows_per_batch, ROW_TILE_IN)
    row_spec = lambda width: pl.BlockSpec((tm, width), lambda i: (i, 0))
    const = lambda i: (0, 0)
    widths = (D_MODEL, MAIN_COLS, SMALL_LANES, 2 * GROUP_WIDTH)
    return pl.pallas_call(
        _ffn_inproj_kernel,
        grid=(rows // tm,),
        in_specs=[
            row_spec(D_MODEL),
            pl.BlockSpec((1, 6, D_MODEL), _mod_index(mod6.shape[0], rows_per_batch, tm)),
            pl.BlockSpec((3, D_MODEL), const),
            pl.BlockSpec((1, SMALL_LANES), const),
            pl.BlockSpec((SMALL_LANES, 2 * GROUP_WIDTH), const),
            pl.BlockSpec((1, 2 * GROUP_WIDTH), const),
            *_ffn_weight_specs(layer, 0),
            pl.BlockSpec((None, D_MODEL, MAIN_COLS + SMALL_LANES), lambda i: (layer, 0, 0),
                         pipeline_mode=pl.Buffered(1)),
        ],
        out_specs=[row_spec(w) for w in widths],
        out_shape=[jax.ShapeDtypeStruct((rows, w), F32) for w in widths],
        compiler_params=_ROW_PARAMS,
        name="ffn_inproj",
    )(x, mod6, g3, gate_bias, gla_wa, gla_ba, w1, w3, w2, w_big)


def _outproj_ffn(x, mixes, mod6, g3, w_out, w1, w3, w2, layer, rows_per_batch):
    rows = x.shape[0]
    tm = _row_tile(mod6.shape[0], rows_per_batch, ROW_TILE_OUT)
    row_spec = lambda width: pl.BlockSpec((tm, width), lambda i: (i, 0))
    return pl.pallas_call(
        _outproj_ffn_kernel,
        grid=(rows // tm,),
        in_specs=[
            row_spec(D_MODEL), *[row_spec(GROUP_WIDTH)] * 4,
            pl.BlockSpec((1, 6, D_MODEL), _mod_index(mod6.shape[0], rows_per_batch, tm)),
            pl.BlockSpec((3, D_MODEL), lambda i: (0, 0)),
            pl.BlockSpec((None, D_MODEL, D_MODEL), lambda i: (layer, 0, 0), pipeline_mode=pl.Buffered(1)),
            *_ffn_weight_specs(layer, 1),
        ],
        out_specs=row_spec(D_MODEL),
        out_shape=jax.ShapeDtypeStruct((rows, D_MODEL), F32),
        compiler_params=_ROW_PARAMS,
        name="outproj_ffn",
    )(x, *mixes, mod6, g3, w_out, w1, w3, w2)


def _seqs_per_step(latent):
    return 1 if latent else CTX_SEQS_PER_STEP


def _mixer_specs(mixer, seq, nb):
    gl = GROUP_LANES
    u_spec = pl.BlockSpec((nb * seq, 4 * gl), lambda b, g: (b, mixer * N_GROUPS + g))
    small_spec = pl.BlockSpec((nb * seq, SMALL_LANES), lambda b, g: (b, 0))
    out_spec = pl.BlockSpec((nb * seq, gl), lambda b, g: (b, g))
    return u_spec, small_spec, out_spec


def _state_spec(rows, nb=1):
    return pl.BlockSpec((nb, 1, 2, rows, GROUP_LANES), lambda b, g: (b, g, 0, 0, 0))


def _head_state_shape(batch):
    return jax.ShapeDtypeStruct((batch, 2, N_HEADS, HEAD_DIM, HEAD_DIM), F32)


def _head_state_spec(nb):
    return pl.BlockSpec((nb, 2, HEADS_PER_GROUP, HEAD_DIM, HEAD_DIM), lambda b, g: (b, 0, g, 0, 0))


def _store_head_blocks(ref, s, d, mat):
    for h in range(HEADS_PER_GROUP):
        lo = h * HEAD_DIM
        ref[s, d, h] = mat[lo:lo + HEAD_DIM, lo:lo + HEAD_DIM]


_MIXER_PARAMS = pltpu.CompilerParams(dimension_semantics=("parallel", "parallel"), vmem_limit_bytes=VMEM_LIMIT)


def _chunk_rows(c):
    return pl.ds(pl.multiple_of(c * CHUNK, CHUNK), CHUNK)


def _scan_order(t, n_chunks, reverse):
    return (n_chunks - 1 - t) if reverse else t


def _tile_row(row):
    return jnp.broadcast_to(row, (8, row.shape[1]))


def _edge_row(ref, c, reverse):
    if reverse:
        return ref[pl.ds(pl.multiple_of(c * CHUNK, 8), 8), :][0:1]
    return ref[pl.ds(pl.multiple_of(c * CHUNK + CHUNK - 8, 8), 8), :][7:8]


def _chunk_scan(x, reverse, op):
    n = x.shape[0]
    row = _iota(x.shape, 0) & (CHUNK - 1)
    s = 1
    while s < CHUNK:
        if reverse:
            shifted = pltpu.roll(x, n - s, 0)
            ok = row < CHUNK - s
        else:
            shifted = pltpu.roll(x, s, 0)
            ok = row >= s
        x = jnp.where(ok, op(x, shifted), x)
        s *= 2
    return x


def _row_loop(seq, fn):
    blk = min(seq, PRE_ROWS)
    k = min(seq // blk, ROWS_IN_FLIGHT)

    def body(i, c):
        _interleave([fn(pl.ds(pl.multiple_of((i * k + j) * blk, blk), blk)) for j in range(k)])
        return c

    if seq == blk * k:
        _interleave([fn(pl.ds(j * blk, blk)) for j in range(k)])
    else:
        lax.fori_loop(0, seq // (blk * k), body, 0)


def _chunk_loop(n_chunks, make_stream):
    k = min(n_chunks, CHUNKS_IN_FLIGHT)

    def body(t, carry):
        _interleave([make_stream(t * k + j) for j in range(k)])
        return carry

    if n_chunks == k:
        body(0, 0)
    else:
        lax.fori_loop(0, n_chunks // k, body, 0)


def _interleave(streams):
    results = [None] * len(streams)
    live = []
    for i, s in enumerate(streams):
        if hasattr(s, "send"):
            live.append(i)
        else:
            results[i] = s
    while live:
        for i in list(live):
            try:
                next(streams[i])
            except StopIteration as done:
                results[i] = done.value
                live.remove(i)
    return results


def _state_scan(n_chunks, reverse, s_ref, decay_ref, init, base):
    def body(t, s):
        c = base + _scan_order(t, n_chunks, reverse)
        inc = s_ref[c]
        s_ref[c] = s
        return decay_ref[c][0:1] * s + inc

    return lax.fori_loop(0, n_chunks, body, init)


def _finalize(seq, acc_ref, o_ref, fn):
    def blk(rows):
        res = fn(rows, acc_ref[rows, :] + o_ref[rows, :])
        if hasattr(res, "send"):
            res = yield from res
        o_ref[rows, :] = res

    _row_loop(seq, blk)


def _rope(x, cos, sin_signed):
    lane = _iota(x.shape, 1)
    first = (lane & 31) < 16
    swapped = jnp.where(first, pltpu.roll(x, GROUP_LANES - 16, 1), pltpu.roll(x, 16, 1))
    return x * cos + swapped * sin_signed


def _ret_kernel(dec_ref, nw_ref, u_ref, *rest, seq, nb, rope, has_state, emit_state):
    rest = list(rest)
    cos_ref, sin_ref = (rest.pop(0), rest.pop(0)) if rope else (None, None)
    s0_ref = rest.pop(0) if has_state else None
    o_ref = rest.pop(0)
    st_ref = rest.pop(0) if emit_state else None
    acc_ref, q_ref, k_ref, s_ref, cdec_ref = rest

    gl, L, G = GROUP_LANES, CHUNK, HEADS_PER_GROUP
    n_chunks = seq // L
    g = pl.program_id(1)
    masks = _head_masks()
    bd = _bd_mask()
    bd_avg = bd * (1.0 / HEAD_DIM)
    scale = HEAD_DIM ** -0.5
    ri = _iota((L, gl), 0).astype(F32)
    si = _iota((G * L, L), 0)
    sj = _iota((G * L, L), 1)
    i_idx = si & (L - 1)
    hrow = _head_of(_iota((G * L, 1), 0))

    def prepare(rows):
        q = u_ref[rows, 0:gl]
        k = u_ref[rows, gl:2 * gl]
        if rope:
            cos = cos_ref[rows, :]
            sin = sin_ref[rows, :]
            q = _rope(q, cos, sin)
            k = _rope(k, cos, sin)
        q_ref[rows, :] = q * scale
        k_ref[rows, :] = k

    _row_loop(nb * seq, prepare)

    for d in (0, 1):
        reverse = d == 1
        raw_row = masks[0] * dec_ref[d, g * G]
        raw_col = jnp.where(hrow == 0, dec_ref[d, g * G], 0.0)
        for h in range(1, G):
            raw_row = raw_row + masks[h] * dec_ref[d, g * G + h]
            raw_col = jnp.where(hrow == h, dec_ref[d, g * G + h], raw_col)
        lg_row = _log_sigmoid(raw_row)
        lg_col = _log_sigmoid(raw_col)
        diff = ((sj - i_idx) if reverse else (i_idx - sj)).astype(F32)
        dmat = jnp.where(diff >= 0, jnp.exp(jnp.maximum(diff, 0.0) * lg_col), 0.0)
        qdec = jnp.exp(((L - ri) if reverse else (ri + 1.0)) * lg_row)
        kdec = jnp.exp((ri if reverse else (L - 1.0 - ri)) * lg_row)
        cdec_tile = _tile_row(jnp.exp(float(L) * lg_row))
        out_ref = o_ref if reverse else acc_ref

        def increments(c, kdec=kdec, cdec_tile=cdec_tile):
            rows = _chunk_rows(c)
            inc = _mm_tn(k_ref[rows, :] * kdec, u_ref[rows, 2 * gl:3 * gl])
            yield
            s_ref[c] = bd * inc
            cdec_ref[c] = cdec_tile

        _chunk_loop(nb * n_chunks, increments)
        for s in range(nb):
            s_fin = _state_scan(n_chunks, reverse, s_ref, cdec_ref,
                                s0_ref[s, 0, d] if has_state else jnp.zeros((gl, gl), F32), s * n_chunks)
            if emit_state:
                _store_head_blocks(st_ref, s, d, s_fin)

        def outputs(c, dmat=dmat, qdec=qdec, out_ref=out_ref):
            rows = _chunk_rows(c)
            q = q_ref[rows, :]
            v = u_ref[rows, 2 * gl:3 * gl]
            att = _mm_nt(_stack(q, masks), k_ref[rows, :])
            inter = _mm(q * qdec, s_ref[c])
            yield
            intra = _mm(att * dmat, v)
            yield
            out_ref[rows, :] = _unstack(intra, masks, L) + inter

        _chunk_loop(nb * n_chunks, outputs)

    def finalize(rows, o):
        mu = _mm_split(o, bd_avg)
        yield
        dev = o - mu
        var = _mm_split(dev * dev, bd_avg)
        yield
        return dev * lax.rsqrt(var + EPS) * nw_ref[...] * _silu(u_ref[rows, 3 * gl:4 * gl])

    _finalize(nb * seq, acc_ref, o_ref, finalize)


def _ret_tables(seq):
    t = np.arange(seq)
    quarter = HEAD_DIM // 4
    inv = (ROPE_BASE ** (-np.arange(quarter, dtype=np.float32) / quarter)).astype(np.float32)
    ang_r = (t // GRID_W).astype(np.float32)[:, None] * inv
    ang_c = (t % GRID_W).astype(np.float32)[:, None] * inv
    cos = np.concatenate([np.cos(ang_r), np.cos(ang_r), np.cos(ang_c), np.cos(ang_c)], axis=1)
    sin = np.concatenate([-np.sin(ang_r), np.sin(ang_r), -np.sin(ang_c), np.sin(ang_c)], axis=1)
    tile = lambda a: jnp.asarray(np.tile(a.astype(np.float32), (1, HEADS_PER_GROUP)))
    return tile(cos), tile(sin)


def _retention(u_main, ret_decay, norm_w, batch, seq, state):
    latent = state is not None
    nb = _seqs_per_step(latent)
    u_spec, _, out_spec = _mixer_specs(3, seq, nb)
    gl = GROUP_LANES
    rows, chunks = nb * seq, nb * seq // CHUNK
    const2 = lambda b, g: (0, 0)
    in_specs = [pl.BlockSpec(memory_space=pltpu.SMEM), pl.BlockSpec((1, gl), const2), u_spec]
    args = [ret_decay, jnp.tile(norm_w.reshape(1, HEAD_DIM), (1, HEADS_PER_GROUP)), u_main]
    if latent:
        cos, sin = _ret_tables(seq)
        in_specs += [pl.BlockSpec((seq, gl), const2), pl.BlockSpec((seq, gl), const2), _state_spec(gl)]
        args += [cos, sin, state]
    out_shape = [jax.ShapeDtypeStruct((batch * seq, GROUP_WIDTH), F32)]
    out_specs = [out_spec]
    if not latent:
        out_shape.append(_head_state_shape(batch))
        out_specs.append(_head_state_spec(nb))
    res = pl.pallas_call(
        functools.partial(_ret_kernel, seq=seq, nb=nb, rope=latent, has_state=latent, emit_state=not latent),
        grid=(batch // nb, N_GROUPS),
        in_specs=in_specs,
        out_specs=out_specs,
        out_shape=out_shape,
        scratch_shapes=[pltpu.VMEM((rows, gl), F32), pltpu.VMEM((rows, gl), F32), pltpu.VMEM((rows, gl), F32),
                        pltpu.VMEM((chunks, gl, gl), F32), pltpu.VMEM((chunks, 8, gl), F32)],
        compiler_params=_MIXER_PARAMS,
        name="retention",
    )(*args)
    return res[0], (None if latent else res[1])


def _gla_kernel(nw_ref, u_ref, bf_ref, bb_ref, *rest, seq, nb, has_state, emit_state):
    rest = list(rest)
    s0_ref = rest.pop(0) if has_state else None
    o_ref = rest.pop(0)
    st_ref = rest.pop(0) if emit_state else None
    acc_ref, s_ref, dec_ref, steep_ref = rest

    gl, L, G, SB = GROUP_LANES, CHUNK, HEADS_PER_GROUP, GLA_SUB
    n_chunks = seq // L
    n_sub = L // SB
    masks = _head_masks()
    bd = _bd_mask()
    bd_avg = bd * (1.0 / HEAD_DIM)
    scale = HEAD_DIM ** -0.5
    half = SB // 2
    pair_row = _iota((SB * SB, gl), 0)
    pi = lax.shift_right_logical(pair_row, 4)
    pj = pair_row & (SB - 1)
    pair_sum = (lax.shift_right_logical(_iota((SB, SB * SB), 1), 4) == _iota((SB, SB * SB), 0)).astype(BF16)
    zeros_half = jnp.zeros((half, gl), F32)

    for d in (0, 1):
        reverse = d == 1
        pair_ok = (pj >= pi) if reverse else (pj <= pi)
        out_ref = o_ref if reverse else acc_ref

        b_ref = bb_ref if reverse else bf_ref
        steep_ref[...] = jnp.zeros((8, gl), F32)

        def increments(c, reverse=reverse, b_ref=b_ref):
            rows = _chunk_rows(c)
            b = b_ref[rows, :]
            btot = b[0:1] if reverse else b[L - 1:L]
            inc = _mm_tn(u_ref[rows, 2 * gl:3 * gl], u_ref[rows, gl:2 * gl] * jnp.exp2(btot - b))
            yield
            s_ref[c] = bd * inc
            dec_ref[c] = _tile_row(jnp.exp2(btot))
            steep_ref[...] = jnp.maximum(steep_ref[...], _tile_row(-btot))

        _chunk_loop(nb * n_chunks, increments)
        factorise_all = jnp.max(steep_ref[...]) <= GLA_SAFE_LOG2
        for s in range(nb):
            s_fin = _state_scan(n_chunks, reverse, s_ref, dec_ref,
                                s0_ref[s, 0, d] if has_state else jnp.zeros((gl, gl), F32), s * n_chunks)
            if emit_state:
                _store_head_blocks(st_ref, s, d, s_fin.T)

        def outputs_factorised(c, reverse=reverse, out_ref=out_ref, b_ref=b_ref):
            rows = _chunk_rows(c)
            b = b_ref[rows, :]
            q = u_ref[rows, 0:gl] * scale
            k = u_ref[rows, gl:2 * gl]
            v = u_ref[rows, 2 * gl:3 * gl]
            o_inter = _mm_nt(q * jnp.exp2(b), s_ref[c])
            atts = []
            for i_blk in range(n_sub):
                lo = i_blk * SB
                bi = b[lo:lo + SB]
                ref, keys = (bi[SB - 1:SB], slice(lo, L)) if reverse else (bi[0:1], slice(0, lo + SB))
                qd = q[lo:lo + SB] * jnp.exp2(bi - ref)
                kd = k[keys] * jnp.exp2(ref - b[keys])
                att = _mm_nt(_stack(qd, masks), kd)
                n_keys = keys.stop - keys.start
                qrow = _iota((G * SB, n_keys), 0) & (SB - 1)
                kcol = _iota((G * SB, n_keys), 1)
                ok = (kcol >= qrow) if reverse else (kcol <= qrow + lo)
                atts.append((jnp.where(ok, att, 0.0), keys))
            yield
            outs = [_mm(a, v[keys]) for a, keys in atts]
            yield
            out_ref[rows, :] = o_inter + jnp.concatenate([_unstack(o, masks, SB) for o in outs], axis=0)

        def outputs_direct(c, reverse=reverse, pair_ok=pair_ok, out_ref=out_ref, b_ref=b_ref):
            rows = _chunk_rows(c)
            b = b_ref[rows, :]
            q = u_ref[rows, 0:gl] * scale
            k = u_ref[rows, gl:2 * gl]
            v = u_ref[rows, 2 * gl:3 * gl]
            o_inter = _mm_nt(q * jnp.exp2(b), s_ref[c])
            diag, off = [], []
            for i_blk in range(n_sub):
                lo = i_blk * SB
                qi, ki, bi = q[lo:lo + SB], k[lo:lo + SB], b[lo:lo + SB]
                prods = []
                for i in range(SB):
                    if reverse and i >= half:
                        part = slice(half, SB)
                    elif (not reverse) and i < half:
                        part = slice(0, half)
                    else:
                        part = slice(0, SB)
                    p = (qi[i:i + 1] * ki[part]) * jnp.exp2(bi[i:i + 1] - bi[part])
                    if part.start == half:
                        prods += [zeros_half, p]
                    elif part.stop == half:
                        prods += [p, zeros_half]
                    else:
                        prods.append(p)
                diag.append(_mm(jnp.concatenate(prods, axis=0), bd))
                if reverse and i_blk < n_sub - 1:
                    ref, other = bi[SB - 1:SB], slice(lo + SB, L)
                elif (not reverse) and i_blk > 0:
                    ref, other = bi[0:1], slice(0, lo)
                else:
                    off.append(None)
                    continue
                qd = qi * jnp.exp2(bi - ref)
                kd = k[other] * jnp.exp2(ref - b[other])
                off.append((_mm_nt(_stack(qd, masks), kd), other))
            yield
            off = [None if a is None else _mm(a[0], v[a[1]]) for a in off]
            for i_blk in range(n_sub):
                vi = v[i_blk * SB:(i_blk + 1) * SB]
                weighted = jnp.where(pair_ok, diag[i_blk] * jnp.concatenate([vi] * SB, axis=0), 0.0)
                diag[i_blk] = jnp.dot(pair_sum, weighted.astype(BF16), preferred_element_type=F32)
            yield
            blocks = []
            for i_blk in range(n_sub):
                o_blk = diag[i_blk]
                if off[i_blk] is not None:
                    o_blk = o_blk + _unstack(off[i_blk], masks, SB)
                blocks.append(o_blk)
            out_ref[rows, :] = o_inter + jnp.concatenate(blocks, axis=0)

        @pl.when(factorise_all)
        def _():
            _chunk_loop(nb * n_chunks, outputs_factorised)

        @pl.when(jnp.logical_not(factorise_all))
        def _():
            _chunk_loop(nb * n_chunks, outputs_direct)

    def finalize(rows, o):
        ms = _mm_split(o * o, bd_avg)
        yield
        return o * lax.rsqrt(ms + EPS) * nw_ref[...] * _silu(u_ref[rows, 3 * gl:4 * gl])

    _finalize(nb * seq, acc_ref, o_ref, finalize)


def _gla(u_main, log_decay, norm_w, batch, seq, state):
    latent = state is not None
    nb = _seqs_per_step(latent)
    u_spec, _, out_spec = _mixer_specs(0, seq, nb)
    gl = GROUP_LANES
    rows, chunks = nb * seq, nb * seq // CHUNK
    in_specs = [
        pl.BlockSpec((1, gl), lambda b, g: (0, 0)),
        u_spec,
        pl.BlockSpec((rows, gl), lambda b, g: (b, g)),
        pl.BlockSpec((rows, gl), lambda b, g: (b, N_GROUPS + g)),
    ]
    args = [jnp.tile(norm_w.reshape(1, HEAD_DIM), (1, HEADS_PER_GROUP)), u_main, log_decay, log_decay]
    if latent:
        in_specs.append(_state_spec(gl))
        args.append(state)
    out_shape = [jax.ShapeDtypeStruct((batch * seq, GROUP_WIDTH), F32)]
    out_specs = [out_spec]
    if not latent:
        out_shape.append(_head_state_shape(batch))
        out_specs.append(_head_state_spec(nb))
    res = pl.pallas_call(
        functools.partial(_gla_kernel, seq=seq, nb=nb, has_state=latent, emit_state=not latent),
        grid=(batch // nb, N_GROUPS),
        in_specs=in_specs,
        out_specs=out_specs,
        out_shape=out_shape,
        scratch_shapes=[pltpu.VMEM((rows, gl), F32),
                        pltpu.VMEM((chunks, gl, gl), F32), pltpu.VMEM((chunks, 8, gl), F32),
                        pltpu.VMEM((8, gl), F32)],
        compiler_params=_MIXER_PARAMS,
        name="gla",
    )(*args)
    return res[0], (None if latent else res[1])


def _mlstm_kernel(e_ref, u_ref, us_ref, *rest, seq, nb, has_state, emit_state):
    rest = list(rest)
    if has_state:
        c0_ref, n0_ref, m0_ref = rest.pop(0), rest.pop(0), rest.pop(0)
    o_ref = rest.pop(0)
    if emit_state:
        cst_ref, nst_ref, mst_ref = rest.pop(0), rest.pop(0), rest.pop(0)
    acc_ref, f_ref, x_ref, p_ref, s_ref, nu_ref, wp_ref, min_ref, mout_ref = rest

    gl, L, G = GROUP_LANES, CHUNK, HEADS_PER_GROUP
    n_chunks = seq // L
    masks = _head_masks()
    bd = _bd_mask()
    scale = HEAD_DIM ** -0.5
    si = _iota((G * L, L), 0) & (L - 1)
    sj = _iota((G * L, L), 1)
    lane_in_head = _iota((L, gl), 1) & (HEAD_DIM - 1)

    for d in (0, 1):
        reverse = d == 1
        causal = (sj >= si) if reverse else (sj <= si)
        last = 0 if reverse else L - 1
        out_ref = o_ref if reverse else acc_ref

        def gates(rows, d=d):
            r = _mm_split(us_ref[rows, :], e_ref[d, 0])
            yield
            x_ref[rows, :] = r[:, 0:gl]
            f_ref[rows, :] = r[:, gl:2 * gl]
            p_ref[rows, :] = r[:, 2 * gl:3 * gl]

        _row_loop(nb * seq, gates)

        def stabiliser(t, m_row, base, reverse=reverse):
            c = base + _scan_order(t, n_chunks, reverse)
            m_new = _edge_row(f_ref, c, reverse) + jnp.maximum(m_row, _edge_row(p_ref, c, reverse))
            min_ref[c] = _tile_row(m_row)
            mout_ref[c] = _tile_row(m_new)
            return m_new

        m_fin = [lax.fori_loop(0, n_chunks, functools.partial(stabiliser, base=s * n_chunks),
                               m0_ref[s, 0, d] if has_state else jnp.zeros((1, gl), F32)) for s in range(nb)]

        def increments(c, last=last):
            rows = _chunk_rows(c)
            f_tot = f_ref[rows, :][last:last + 1]
            m_in = min_ref[c][0:1]
            m_out = mout_ref[c][0:1]
            kw = u_ref[rows, gl:2 * gl] * jnp.exp(f_tot + x_ref[rows, :] - m_out)
            inc = _mm_tn(kw, u_ref[rows, 2 * gl:3 * gl])
            yield
            s_ref[c] = bd * inc
            nu_ref[c] = _tile_row(jnp.sum(kw, axis=0, keepdims=True))
            wp_ref[c] = _tile_row(jnp.exp(f_tot + m_in - m_out))

        _chunk_loop(nb * n_chunks, increments)

        def memory(t, carry, base, reverse=reverse):
            c_mat, n_row = carry
            c = base + _scan_order(t, n_chunks, reverse)
            inc = s_ref[c]
            nu = nu_ref[c][0:1]
            wp = wp_ref[c][0:1]
            s_ref[c] = c_mat
            nu_ref[c] = _tile_row(n_row)
            return wp * c_mat + inc, wp * n_row + nu

        for s in range(nb):
            if has_state:
                init = (c0_ref[s, 0, d], n0_ref[s, 0, d])
            else:
                init = (jnp.zeros((gl, gl), F32), jnp.zeros((1, gl), F32))
            c_fin, n_fin = lax.fori_loop(0, n_chunks, functools.partial(memory, base=s * n_chunks), init)
            if emit_state:
                _store_head_blocks(cst_ref, s, d, c_fin)
                nst_ref[s, 0, d], mst_ref[s, 0, d] = n_fin, m_fin[s]

        def outputs(c, causal=causal, out_ref=out_ref):
            rows = _chunk_rows(c)
            x = x_ref[rows, :]
            m_in = min_ref[c][0:1]
            q = u_ref[rows, 0:gl] * scale
            k = u_ref[rows, gl:2 * gl]
            v = u_ref[rows, 2 * gl:3 * gl]
            m_full = jnp.maximum(m_in, p_ref[rows, :])
            lhs = _split_lanes(-m_full, lane_in_head, 0, 1.0, 3)
            rhs = _split_lanes(x, lane_in_head, 3, 1.0, 0)
            logw = _mm_nt(_stack(lhs, masks), rhs)
            qk = _mm_nt(_stack(q, masks), k)
            inter = _mm(q, s_ref[c])
            q_n = _mm_split(q * nu_ref[c][0:1], bd)
            yield
            s = qk * jnp.exp(jnp.where(causal, logw, NEG))
            s_hi = s.astype(BF16)
            ones = jnp.ones((L, gl), BF16)
            r = jnp.dot(s_hi, jnp.concatenate([v.astype(BF16), ones], axis=1), preferred_element_type=F32)
            sum_lo = jnp.dot((s - s_hi.astype(F32)).astype(BF16), ones, preferred_element_type=F32)
            yield
            w_inter = jnp.exp(m_in - m_full)
            num = _unstack(r[:, :gl], masks, L) + w_inter * inter
            den = _unstack(r[:, gl:] + sum_lo, masks, L) + w_inter * q_n
            out_ref[rows, :] = num / jnp.maximum(jnp.abs(den), jnp.exp(-(f_ref[rows, :] + m_full)))

        _chunk_loop(nb * n_chunks, outputs)

    def finalize(rows, o):
        return o * jax.nn.sigmoid(u_ref[rows, 3 * gl:4 * gl])

    _finalize(nb * seq, acc_ref, o_ref, finalize)


def _gate_expanders(offset):
    e = np.zeros((2, N_GROUPS, SMALL_LANES, GROUP_LANES), np.float32)
    for d in range(2):
        for g in range(N_GROUPS):
            for h in range(HEADS_PER_GROUP):
                src = offset + d * N_HEADS + g * HEADS_PER_GROUP + h
                e[d, g, src, h * HEAD_DIM:(h + 1) * HEAD_DIM] = 1.0
    return jnp.asarray(e)


def _gate_bias_row(b_i, b_f):
    row = jnp.zeros((1, SMALL_LANES), F32)
    row = row.at[0, MI_OFF:MI_OFF + 2 * N_HEADS].set(b_i.astype(F32).reshape(-1))
    return row.at[0, MF_OFF:MF_OFF + 2 * N_HEADS].set(b_f.astype(F32).reshape(-1))


def _mlstm(u_main, u_small, batch, seq, state):
    latent = state is not None
    nb = _seqs_per_step(latent)
    u_spec, small_spec, out_spec = _mixer_specs(2, seq, nb)
    gl = GROUP_LANES
    rows, chunks = nb * seq, nb * seq // CHUNK
    expand = jnp.concatenate([_gate_expanders(MI_OFF), _gate_expanders(MF_OFF), _gate_expanders(MP_OFF)], axis=-1)
    in_specs = [pl.BlockSpec((2, 1, SMALL_LANES, 3 * gl), lambda b, g: (0, g, 0, 0)), u_spec, small_spec]
    args = [expand, u_main, u_small]
    if latent:
        in_specs += [_state_spec(gl), _state_spec(1), _state_spec(1)]
        args += list(state)
    out_shape = [jax.ShapeDtypeStruct((batch * seq, GROUP_WIDTH), F32)]
    out_specs = [out_spec]
    if not latent:
        out_shape += [_head_state_shape(batch),
                      jax.ShapeDtypeStruct((batch, N_GROUPS, 2, 1, gl), F32),
                      jax.ShapeDtypeStruct((batch, N_GROUPS, 2, 1, gl), F32)]
        out_specs += [_head_state_spec(nb), _state_spec(1, nb), _state_spec(1, nb)]
    res = pl.pallas_call(
        functools.partial(_mlstm_kernel, seq=seq, nb=nb, has_state=latent, emit_state=not latent),
        grid=(batch // nb, N_GROUPS),
        in_specs=in_specs,
        out_specs=out_specs,
        out_shape=out_shape,
        scratch_shapes=[pltpu.VMEM((rows, gl), F32)] * 4 + [pltpu.VMEM((chunks, gl, gl), F32)]
        + [pltpu.VMEM((chunks, 8, gl), F32)] * 4,
        compiler_params=_MIXER_PARAMS,
        name="mlstm",
    )(*args)
    return res[0], (None if latent else tuple(res[1:]))


def _na_ctx_kernel(u_ref, o_ref, ko_ref, vo_ref, *, seq, nb):
    gl = GROUP_LANES
    masks = _head_masks()

    def sequence(s):
        rows = slice(s * seq, (s + 1) * seq)
        q = u_ref[rows, 0:gl] * (HEAD_DIM ** -0.5)
        k = u_ref[rows, gl:2 * gl]
        v = u_ref[rows, 2 * gl:3 * gl]
        sc = _mm_nt(_stack(q, masks), k)
        yield
        p = jnp.exp(sc - jnp.max(sc, axis=1, keepdims=True))
        o = _mm(p, v)
        yield
        o_ref[rows, :] = _unstack(o / jnp.sum(p, axis=1, keepdims=True), masks, seq)
        for h in range(HEADS_PER_GROUP):
            ko_ref[s, h] = k[:, h * HEAD_DIM:(h + 1) * HEAD_DIM]
            vo_ref[s, h] = v[:, h * HEAD_DIM:(h + 1) * HEAD_DIM]

    _interleave([sequence(s) for s in range(nb)])


def _na_context(u_main, batch, seq):
    nb = _seqs_per_step(False)
    u_spec, _, out_spec = _mixer_specs(1, seq, nb)
    kv_shape = jax.ShapeDtypeStruct((batch, N_HEADS, seq, HEAD_DIM), F32)
    kv_spec = pl.BlockSpec((nb, HEADS_PER_GROUP, seq, HEAD_DIM), lambda b, g: (b, g, 0, 0))
    return pl.pallas_call(
        functools.partial(_na_ctx_kernel, seq=seq, nb=nb),
        grid=(batch // nb, N_GROUPS),
        in_specs=[u_spec],
        out_specs=[out_spec, kv_spec, kv_spec],
        out_shape=[jax.ShapeDtypeStruct((batch * seq, GROUP_WIDTH), F32), kv_shape, kv_shape],
        compiler_params=_MIXER_PARAMS,
        name="na_context",
    )(u_main)


def _na_key_col0(qb):
    return int(np.clip(qb * NA_QC - NA_KW // 2, 0, GRID_W - NA_KCB))


def _na_lat_kernel(tab_ref, kc_ref, vc_ref, u_ref, o_ref, *, seq):
    gl = GROUP_LANES
    grid_rows = seq // GRID_W
    masks = _head_masks()
    scale = HEAD_DIM ** -0.5
    kc = kc_ref[0, 0]
    vc = vc_ref[0, 0]

    def query_block(r, qb):
        ks = jnp.clip(r - NA_KH // 2, 0, grid_rows - NA_KH)
        q_rows = pl.ds(pl.multiple_of(r * GRID_W, GRID_W) + qb * NA_QC, NA_QC)
        k_rows = [pl.ds(pl.multiple_of((ks + j) * GRID_W, GRID_W) + _na_key_col0(qb), NA_KCB) for j in range(NA_KH)]
        qs = _stack(u_ref[q_rows, 0:gl] * scale, masks)
        s_loc = _mm_nt(qs, jnp.concatenate([u_ref[rows, gl:2 * gl] for rows in k_rows], axis=0))
        s_ctx = _mm_nt(qs, kc)
        yield
        s_loc = s_loc + tab_ref[0, qb, ks - r + NA_KH - 1]
        m = jnp.maximum(jnp.max(s_loc, axis=1, keepdims=True), jnp.max(s_ctx, axis=1, keepdims=True))
        p_loc = jnp.exp(s_loc - m)
        p_ctx = jnp.exp(s_ctx - m)
        den = jnp.sum(p_loc, axis=1, keepdims=True) + jnp.sum(p_ctx, axis=1, keepdims=True)
        o = _mm(p_loc, jnp.concatenate([u_ref[rows, 2 * gl:3 * gl] for rows in k_rows], axis=0)) + _mm(p_ctx, vc)
        yield
        o_ref[q_rows, :] = _unstack(o / den, masks, NA_QC)

    def body(t, carry):
        _interleave([query_block(t * NA_ROWS_IN_FLIGHT + j, qb)
                     for j in range(NA_ROWS_IN_FLIGHT) for qb in range(GRID_W // NA_QC)])
        return carry

    lax.fori_loop(0, grid_rows // NA_ROWS_IN_FLIGHT, body, 0)


def _na_bias_table(rpb):
    c = np.arange(GRID_W)
    rel = c[None, :] - c[:, None]
    cs = np.clip(c - NA_KW // 2, 0, GRID_W - NA_KW)
    valid = (c[None, :] >= cs[:, None]) & (c[None, :] < cs[:, None] + NA_KW)
    n_rel = 2 * NA_KW - 1
    pick = (np.arange(n_rel)[:, None, None] == (np.clip(rel, -(NA_KW - 1), NA_KW - 1) + NA_KW - 1)[None])
    bmat = jnp.einsum("hdk,kcz->hdcz", rpb.astype(F32), jnp.asarray(pick, F32), precision=HI)
    bmat = jnp.where(valid[None, None], bmat, NEG)
    blocks = []
    for qb in range(GRID_W // NA_QC):
        sub = bmat[:, :, qb * NA_QC:(qb + 1) * NA_QC, _na_key_col0(qb):_na_key_col0(qb) + NA_KCB]
        tabs = jnp.stack([sub[:, s:s + NA_KH] for s in range(NA_KH)], axis=1)
        tabs = tabs.transpose(0, 1, 3, 2, 4).reshape(N_GROUPS, HEADS_PER_GROUP, NA_KH, NA_QC, NA_KH * NA_KCB)
        blocks.append(tabs.transpose(0, 2, 1, 3, 4).reshape(N_GROUPS, NA_KH, HEADS_PER_GROUP * NA_QC,
                                                             NA_KH * NA_KCB))
    return jnp.stack(blocks, axis=1)


def _heads_to_lanes(t):
    b, _, s, _ = t.shape
    t = t.reshape(b, N_GROUPS, HEADS_PER_GROUP, s, HEAD_DIM).transpose(0, 1, 3, 2, 4)
    return t.reshape(b, N_GROUPS, s, GROUP_LANES)


def _na_latent(u_main, rpb, k_ctx, v_ctx, batch, seq):
    u_spec, _, out_spec = _mixer_specs(1, seq, 1)
    gl = GROUP_LANES
    past = k_ctx.shape[2]
    tab = _na_bias_table(rpb)
    ctx_spec = pl.BlockSpec((1, 1, past, gl), lambda b, g: (b, g, 0, 0))
    return pl.pallas_call(
        functools.partial(_na_lat_kernel, seq=seq),
        grid=(batch, N_GROUPS),
        in_specs=[pl.BlockSpec((1,) + tab.shape[1:], lambda b, g: (g, 0, 0, 0, 0)), ctx_spec, ctx_spec, u_spec],
        out_specs=out_spec,
        out_shape=jax.ShapeDtypeStruct((batch * seq, GROUP_WIDTH), F32),
        compiler_params=_MIXER_PARAMS,
        name="na_latent",
    )(tab, _heads_to_lanes(k_ctx.astype(F32)), _heads_to_lanes(v_ctx.astype(F32)), u_main)


def _to_block_diag(s):
    b = s.shape[0]
    G = HEADS_PER_GROUP
    s = s.astype(F32).reshape(b, 2, N_GROUPS, G, HEAD_DIM, 1, HEAD_DIM)
    eye = jnp.eye(G, dtype=F32).reshape(1, 1, 1, G, 1, G, 1)
    return (s * eye).reshape(b, 2, N_GROUPS, GROUP_LANES, GROUP_LANES).transpose(0, 2, 1, 3, 4)


def _rows_to_lanes(v):
    b = v.shape[0]
    return v.astype(F32).reshape(b, 2, N_GROUPS, 1, GROUP_LANES).transpose(0, 2, 1, 3, 4)


def _lanes_to_rows(v):
    b = v.shape[0]
    return v.transpose(0, 2, 1, 3, 4).reshape(b, 2, N_HEADS, HEAD_DIM)


def _swap_last(s):
    return jnp.swapaxes(s, -1, -2)


def _in_weight(w_in):
    gl = GROUP_LANES
    w_in = w_in.astype(BF16)
    zeros = jnp.zeros((DEPTH, D_MODEL, gl), BF16)
    cols = []
    for names in _MIXER_COLS:
        for g in range(N_GROUPS):
            for name in names:
                cols.append(zeros if name is None else w_in[..., _OFF[name] + g * gl:_OFF[name] + (g + 1) * gl])
    cols.append(w_in[..., _OFF["ga"]:_OFF["ga"] + 2 * GLA_RANK])
    cols.append(w_in[..., _OFF["mi"]:_OFF["mi"] + 2 * N_HEADS])
    cols.append(w_in[..., _OFF["mf"]:_OFF["mf"] + 2 * N_HEADS])
    cols.append(jnp.zeros((DEPTH, D_MODEL, SMALL_LANES - MF_OFF - 2 * N_HEADS), BF16))
    return jnp.concatenate(cols, axis=-1)


def _gla_gate_weight(w_a2_l):
    out = jnp.zeros((SMALL_LANES, 2 * GROUP_WIDTH), F32)
    for d in range(2):
        out = out.at[d * GLA_RANK:(d + 1) * GLA_RANK, d * GROUP_WIDTH:(d + 1) * GROUP_WIDTH].set(
            w_a2_l[d].astype(F32))
    return out.astype(BF16)


def _layer(x, mod, p, batch, seq, ctx):
    latent = ctx is not None
    l = p["layer"]
    x, u_main, u_small, log_decay = _ffn_inproj(x, mod[:, 0:6], p["norm_w"][0:3], p["gate_bias"], p["gla_wa"],
                                                p["gla_ba"], p["w1"], p["w3"], p["w2"], p["w_in"], l, seq)
    mix_a, st_gla = _gla(u_main, log_decay, p["gla_norm_w"], batch, seq,
                         _swap_last(_to_block_diag(ctx["gla"])) if latent else None)
    if latent:
        mix_b = _na_latent(u_main, p["na_rpb"], ctx["na_k"], ctx["na_v"], batch, seq)
    else:
        mix_b, na_k, na_v = _na_context(u_main, batch, seq)
    mix_c, st_ml = _mlstm(u_main, u_small, batch, seq,
                          (_to_block_diag(ctx["mC"]), _rows_to_lanes(ctx["mn"]),
                           _rows_to_lanes(jnp.repeat(ctx["mm"][..., None], HEAD_DIM, axis=-1))) if latent else None)
    mix_d, st_ret = _retention(u_main, p["ret_decay"], p["ret_norm_w"], batch, seq,
                               _to_block_diag(ctx["ret"]) if latent else None)
    x = _outproj_ffn(x, (mix_a, mix_b, mix_c, mix_d), mod[:, 3:9], p["norm_w"][3:6], p["w_out"],
                     p["w1"], p["w3"], p["w2"], l, seq)
    if latent:
        return x, None
    new = {
        "na_k": na_k, "na_v": na_v, "gla": st_gla, "mC": st_ml[0],
        "mn": _lanes_to_rows(st_ml[1]), "mm": _lanes_to_rows(st_ml[2])[..., 0], "ret": st_ret,
    }
    return x, new


def kernel(x_prompt, x_sample, cache_na_k, cache_na_v, state_gla, state_mlstm_C, state_mlstm_n, state_mlstm_m,
           state_ret, c, c_ctx, w_mod, b_mod, norm_w, ffn_w1, ffn_w3, ffn_w2, w_in, w_out, gla_w_a2, gla_b_a,
           gla_norm_w, na_rpb, mlstm_b_i, mlstm_b_f, ret_decay, ret_norm_w):
    batch, seq, _ = x_prompt.shape
    dec_batch, dec_seq, _ = x_sample.shape

    cvec = jnp.zeros((8, D_MODEL), F32).at[0].set(c_ctx).at[1:1 + dec_batch].set(c)
    mods = _modulation(cvec, w_mod, b_mod).reshape(DEPTH, 8, N_MOD, D_MODEL)

    w1, w3, w2 = ffn_w1.astype(BF16), ffn_w3.astype(BF16), ffn_w2.astype(BF16)
    w_in_all, w_out_all = _in_weight(w_in), w_out.astype(BF16)
    params = []
    for l in range(DEPTH):
        params.append({
            "layer": l, "norm_w": norm_w[l],
            "w1": w1, "w3": w3, "w2": w2, "w_in": w_in_all, "w_out": w_out_all,
            "gla_wa": _gla_gate_weight(gla_w_a2[l]), "gla_ba": gla_b_a[l].astype(F32).reshape(1, 2 * GROUP_WIDTH),
            "gla_norm_w": gla_norm_w[l],
            "na_rpb": na_rpb[l], "gate_bias": _gate_bias_row(mlstm_b_i[l], mlstm_b_f[l]),
            "ret_decay": ret_decay[l], "ret_norm_w": ret_norm_w[l],
        })

    xp = x_prompt.reshape(batch * seq, D_MODEL)
    states = []
    for l in range(DEPTH):
        xp, st = _layer(xp, mods[l, 0:1], params[l], batch, seq, None)
        states.append(st)

    xs = x_sample.reshape(dec_batch * dec_seq, D_MODEL)
    for l in range(DEPTH):
        ctx = {"na_k": cache_na_k[:, l], "na_v": cache_na_v[:, l], "gla": state_gla[:, l],
               "mC": state_mlstm_C[:, l], "mn": state_mlstm_n[:, l], "mm": state_mlstm_m[:, l],
               "ret": state_ret[:, l]}
        xs, _ = _layer(xs, mods[l, 1:1 + dec_batch], params[l], dec_batch, dec_seq, ctx)

    stack = lambda name: jnp.stack([s[name] for s in states], axis=1)
    return (xp.reshape(batch, seq, D_MODEL), xs.reshape(dec_batch, dec_seq, D_MODEL),
            stack("na_k"), stack("na_v"), stack("gla"), stack("mC"), stack("mn"), stack("mm"), stack("ret"))
```

```python
import functools

import numpy as np
import jax
import jax.numpy as jnp
from jax import lax
from jax.experimental import pallas as pl
from jax.experimental.pallas import tpu as pltpu

D_MODEL = 1024
DEPTH = 2
HEAD_DIM = 64
N_HEADS = 4
GROUP_WIDTH = N_HEADS * HEAD_DIM
N_MOD = 9
GLA_RANK = 16
GLA_TAU = 16.0
CHUNK = 64
GRID_W = 64
NA_KH = 8
NA_KW = 16
NA_QC = 16
NA_KCB = NA_QC + NA_KW
ROPE_BASE = 10000.0
EPS = 1e-6
D_FF = 2816

HEADS_PER_GROUP = 2
GROUP_LANES = HEADS_PER_GROUP * HEAD_DIM
N_GROUPS = N_HEADS // HEADS_PER_GROUP
GLA_SUB = 16
GLA_SAFE_LOG2 = 64.0
ROW_TILE_IN = 256
ROW_TILE_OUT = 512
ROW_PARTS = 2
CTX_SEQS_PER_STEP = 4
PRE_ROWS = 128
ROWS_IN_FLIGHT = 8
NA_ROWS_IN_FLIGHT = 4
LOG2E = 1.4426950408889634
CHUNKS_IN_FLIGHT = 8
SMALL_LANES = 128
MI_OFF = 2 * GLA_RANK
MF_OFF = MI_OFF + 2 * N_HEADS
MP_OFF = MF_OFF + 2 * N_HEADS
NEG = -1e30
VMEM_LIMIT = 56 * 1024 * 1024

F32 = jnp.float32
BF16 = jnp.bfloat16
HI = lax.Precision.HIGHEST

_OFF = {}
_o = 0
for _name, _size in (("gq", 256), ("gk", 256), ("gv", 256), ("gg", 256), ("ga", 32), ("nq", 256), ("nk", 256),
                     ("nv", 256), ("mq", 256), ("mk", 256), ("mv", 256), ("mo", 256), ("mi", 8), ("mf", 8),
                     ("rq", 256), ("rk", 256), ("rv", 256), ("rg", 256)):
    _OFF[_name] = _o
    _o += _size
N_IN = _o
_MIXER_COLS = (("gq", "gk", "gv", "gg"), ("mq", "mk", "mv", "mo"), ("rq", "rk", "rv", "rg"), ("nq", "nk", "nv"))
MIX_GLA, MIX_MLSTM, MIX_RET, MIX_NA = range(4)
MAIN_COLS = sum(len(names) for names in _MIXER_COLS) * N_GROUPS * GROUP_LANES


def _mm(a, b):
    return jnp.dot(a.astype(BF16), b.astype(BF16), preferred_element_type=F32)


def _mm_nt(a, b):
    return lax.dot_general(a.astype(BF16), b.astype(BF16), (((1,), (1,)), ((), ())), preferred_element_type=F32)


def _mm_tn(a, b):
    return lax.dot_general(a.astype(BF16), b.astype(BF16), (((0,), (0,)), ((), ())), preferred_element_type=F32)


def _split3(x):
    hi = x.astype(BF16)
    rest = x - hi.astype(F32)
    mid = rest.astype(BF16)
    lo = (rest - mid.astype(F32)).astype(BF16)
    return jnp.concatenate([hi, mid, lo], axis=1)


def _dot_split(pieces, w):
    wb = w.astype(BF16)
    return jnp.dot(pieces, jnp.concatenate([wb, wb, wb], axis=0), preferred_element_type=F32)


def _mm_split(x, w):
    return _dot_split(_split3(x), w)


def _log_sigmoid(x):
    return jnp.minimum(x, 0.0) - jnp.log1p(jnp.exp(-jnp.abs(x)))


def _silu(x):
    return x * jax.nn.sigmoid(x)


def _iota(shape, dim):
    return lax.broadcasted_iota(jnp.int32, shape, dim)


def _head_of(idx):
    return lax.shift_right_logical(idx, 6)


def _head_masks():
    lane = _iota((1, GROUP_LANES), 1)
    return [(_head_of(lane) == h).astype(F32) for h in range(HEADS_PER_GROUP)]


def _bd_mask():
    n = GROUP_LANES
    return (_head_of(_iota((n, n), 0)) == _head_of(_iota((n, n), 1))).astype(F32)


def _stack(x, masks):
    return jnp.concatenate([x * m for m in masks], axis=0)


def _unstack(r, masks, n):
    out = r[0:n] * masks[0]
    for h in range(1, len(masks)):
        out = out + r[h * n:(h + 1) * n] * masks[h]
    return out


def _rmsnorm(x, w):
    return x * lax.rsqrt(jnp.mean(x * x, axis=-1, keepdims=True) + EPS) * w


def _split_lanes(x, lane_in_head, at, fill, fill_at):
    hi = x.astype(BF16).astype(F32)
    rest = x - hi
    mid = rest.astype(BF16).astype(F32)
    lo = rest - mid
    out = jnp.where(lane_in_head == at, hi, jnp.where(lane_in_head == at + 1, mid,
                                                      jnp.where(lane_in_head == at + 2, lo, 0.0)))
    is_fill = (lane_in_head >= fill_at) & (lane_in_head < fill_at + 3)
    return jnp.where(is_fill, fill, out)


def _mod_kernel(c_ref, w_ref, b_ref, o_ref):
    s = _silu(c_ref[...])
    o_ref[0] = _mm(s, w_ref[0]) + b_ref[0]


def _modulation(cvec, w_mod, b_mod):
    n = N_MOD * D_MODEL
    tn = n // 8
    return pl.pallas_call(
        _mod_kernel,
        grid=(DEPTH, n // tn),
        in_specs=[
            pl.BlockSpec((8, D_MODEL), lambda l, j: (0, 0)),
            pl.BlockSpec((1, D_MODEL, tn), lambda l, j: (l, 0, j)),
            pl.BlockSpec((1, 1, tn), lambda l, j: (l, 0, j)),
        ],
        out_specs=pl.BlockSpec((1, 8, tn), lambda l, j: (l, 0, j)),
        out_shape=jax.ShapeDtypeStruct((DEPTH, 8, n), F32),
        compiler_params=pltpu.CompilerParams(dimension_semantics=("parallel", "parallel"),
                                             vmem_limit_bytes=VMEM_LIMIT),
        name="modulation",
    )(cvec, w_mod, b_mod.reshape(DEPTH, 1, n))


def _half_ffn(x, mod, g, w1_ref, w3_ref, w2_ref):
    h = _rmsnorm(x, g[0:1]) * (1.0 + mod[1:2]) + mod[0:1]
    hb = h.astype(BF16)
    h1 = jnp.dot(hb, w1_ref[...], preferred_element_type=F32)
    h3 = jnp.dot(hb, w3_ref[...], preferred_element_type=F32)
    yield
    y = jnp.dot((_silu(h1) * h3).astype(BF16), w2_ref[...], preferred_element_type=F32)
    yield
    return x + 0.5 * mod[2:3] * _rmsnorm(y, g[1:2])


def _row_parts(tm):
    part = tm // ROW_PARTS
    return [slice(j * part, (j + 1) * part) for j in range(ROW_PARTS)]


def _mlstm_gate_scans(us, gate_bias):
    lane = _iota(us.shape, 1)
    field = 2 * N_HEADS
    fwd = (lane & (field - 1)) < N_HEADS
    pre = us + gate_bias
    lf = jnp.where((lane >= MF_OFF) & (lane < MF_OFF + field), _log_sigmoid(pre), 0.0)
    f = jnp.where(fwd, _chunk_scan(lf, False, jnp.add), _chunk_scan(lf, True, jnp.add))
    x = pre - pltpu.roll(f, SMALL_LANES - (MF_OFF - MI_OFF), 1)
    p = jnp.where(fwd, _chunk_scan(x, False, jnp.maximum), _chunk_scan(x, True, jnp.maximum))
    p = pltpu.roll(p, MP_OFF - MI_OFF, 1)
    return jnp.where(lane < MI_OFF, us,
                     jnp.where(lane < MF_OFF, x, jnp.where(lane < MP_OFF, f, jnp.where(lane < MP_OFF + field, p, 0.0))))


def _gla_gate(u_small, wa, ba):
    return _mm(u_small, wa) + ba


def _gla_log_decay(a_pre):
    la = _log_sigmoid(a_pre) * (LOG2E / GLA_TAU)
    return jnp.concatenate([_chunk_scan(la[:, :GROUP_WIDTH], False, jnp.add),
                            _chunk_scan(la[:, GROUP_WIDTH:], True, jnp.add)], axis=1)


def _ffn_inproj_kernel(x_ref, mod_ref, g_ref, gb_ref, wa_ref, ba_ref, w1_ref, w3_ref, w2_ref, win_ref,
                       x_out_ref, um_ref, us_ref, ub_ref):
    mod = mod_ref[0]
    g = g_ref[...]

    def part(rows):
        x = yield from _half_ffn(x_ref[rows, :], mod[0:3], g[0:2], w1_ref, w3_ref, w2_ref)
        x_out_ref[rows, :] = x
        hb = (_rmsnorm(x, g[2:3]) * (1.0 + mod[4:5]) + mod[3:4]).astype(BF16)
        u_small = jnp.dot(hb, win_ref[:, MAIN_COLS:], preferred_element_type=F32)
        u_main = jnp.dot(hb, win_ref[:, :MAIN_COLS], preferred_element_type=F32)
        us_ref[rows, :] = _mlstm_gate_scans(u_small, gb_ref[...])
        ub_ref[rows, :] = _gla_log_decay(_gla_gate(u_small, wa_ref[...], ba_ref[...]))
        yield
        um_ref[rows, :] = u_main

    _interleave([part(rows) for rows in _row_parts(x_ref.shape[0])])


def _outproj_ffn_kernel(x_ref, ma_ref, mb_ref, mc_ref, md_ref, mod_ref, g_ref, wo_ref, w1_ref, w3_ref, w2_ref,
                        o_ref):
    mod = mod_ref[0]
    g = g_ref[...]

    def part(rows):
        mix = jnp.concatenate([ma_ref[rows, :], mb_ref[rows, :], mc_ref[rows, :], md_ref[rows, :]], axis=-1)
        y = jnp.dot(mix.astype(BF16), wo_ref[...], preferred_element_type=F32)
        yield
        x = x_ref[rows, :] + mod[2:3] * _rmsnorm(y, g[0:1])
        o_ref[rows, :] = yield from _half_ffn(x, mod[3:6], g[1:3], w1_ref, w3_ref, w2_ref)

    _interleave([part(rows) for rows in _row_parts(x_ref.shape[0])])


def _row_tile(n_mod, rows_per_batch, tile):
    return tile if n_mod == 1 else min(tile, rows_per_batch)


def _mod_index(n_mod, rows_per_batch, tm):
    if n_mod == 1:
        return lambda i: (0, 0, 0)
    per = rows_per_batch // tm
    return lambda i: (i // per, 0, 0)


_ROW_PARAMS = pltpu.CompilerParams(dimension_semantics=("parallel",), vmem_limit_bytes=VMEM_LIMIT)


def _ffn_weight_specs(layer, half):
    pick = lambda i: (layer, half, 0, 0)
    return [pl.BlockSpec((None, None, D_MODEL, D_FF), pick, pipeline_mode=pl.Buffered(1)),
            pl.BlockSpec((None, None, D_MODEL, D_FF), pick, pipeline_mode=pl.Buffered(1)),
            pl.BlockSpec((None, None, D_FF, D_MODEL), pick, pipeline_mode=pl.Buffered(1))]


def _ffn_inproj(x, mod6, g3, gate_bias, gla_wa, gla_ba, w1, w3, w2, w_big, layer, rows_per_batch):
    rows = x.shape[0]
    tm = _row_tile(mod6.shape[0], rows_per_batch, ROW_TILE_IN)
    row_spec = lambda width: pl.BlockSpec((tm, width), lambda i: (i, 0))
    const = lambda i: (0, 0)
    widths = (D_MODEL, MAIN_COLS, SMALL_LANES, 2 * GROUP_WIDTH)
    return pl.pallas_call(
        _ffn_inproj_kernel,
        grid=(rows // tm,),
        in_specs=[
            row_spec(D_MODEL),
            pl.BlockSpec((1, 6, D_MODEL), _mod_index(mod6.shape[0], rows_per_batch, tm)),
            pl.BlockSpec((3, D_MODEL), const),
            pl.BlockSpec((1, SMALL_LANES), const),
            pl.BlockSpec((SMALL_LANES, 2 * GROUP_WIDTH), const),
            pl.BlockSpec((1, 2 * GROUP_WIDTH), const),
            *_ffn_weight_specs(layer, 0),
            pl.BlockSpec((None, D_MODEL, MAIN_COLS + SMALL_LANES), lambda i: (layer, 0, 0),
                         pipeline_mode=pl.Buffered(1)),
        ],
        out_specs=[row_spec(w) for w in widths],
        out_shape=[jax.ShapeDtypeStruct((rows, w), F32) for w in widths],
        compiler_params=_ROW_PARAMS,
        name="ffn_inproj",
    )(x, mod6, g3, gate_bias, gla_wa, gla_ba, w1, w3, w2, w_big)


def _outproj_ffn(x, mixes, mod6, g3, w_out, w1, w3, w2, layer, rows_per_batch):
    rows = x.shape[0]
    tm = _row_tile(mod6.shape[0], rows_per_batch, ROW_TILE_OUT)
    row_spec = lambda width: pl.BlockSpec((tm, width), lambda i: (i, 0))
    return pl.pallas_call(
        _outproj_ffn_kernel,
        grid=(rows // tm,),
        in_specs=[
            row_spec(D_MODEL), *[row_spec(GROUP_WIDTH)] * 4,
            pl.BlockSpec((1, 6, D_MODEL), _mod_index(mod6.shape[0], rows_per_batch, tm)),
            pl.BlockSpec((3, D_MODEL), lambda i: (0, 0)),
            pl.BlockSpec((None, D_MODEL, D_MODEL), lambda i: (layer, 0, 0), pipeline_mode=pl.Buffered(1)),
            *_ffn_weight_specs(layer, 1),
        ],
        out_specs=row_spec(D_MODEL),
        out_shape=jax.ShapeDtypeStruct((rows, D_MODEL), F32),
        compiler_params=_ROW_PARAMS,
        name="outproj_ffn",
    )(x, *mixes, mod6, g3, w_out, w1, w3, w2)


def _seqs_per_step(latent):
    return 1 if latent else CTX_SEQS_PER_STEP


def _mixer_specs(mixer, seq, nb):
    gl = GROUP_LANES
    n_blocks = len(_MIXER_COLS[mixer])
    first = sum(len(names) for names in _MIXER_COLS[:mixer]) * N_GROUPS
    assert first % n_blocks == 0
    u_spec = pl.BlockSpec((nb * seq, n_blocks * gl), lambda b, g: (b, first // n_blocks + g))
    small_spec = pl.BlockSpec((nb * seq, SMALL_LANES), lambda b, g: (b, 0))
    out_spec = pl.BlockSpec((nb * seq, gl), lambda b, g: (b, g))
    return u_spec, small_spec, out_spec


def _state_spec(rows, nb=1):
    return pl.BlockSpec((nb, 1, 2, rows, GROUP_LANES), lambda b, g: (b, g, 0, 0, 0))


def _head_state_shape(batch):
    return jax.ShapeDtypeStruct((batch, 2, N_HEADS, HEAD_DIM, HEAD_DIM), F32)


def _head_state_spec(nb):
    return pl.BlockSpec((nb, 2, HEADS_PER_GROUP, HEAD_DIM, HEAD_DIM), lambda b, g: (b, 0, g, 0, 0))


def _store_head_blocks(ref, s, d, mat):
    for h in range(HEADS_PER_GROUP):
        lo = h * HEAD_DIM
        ref[s, d, h] = mat[lo:lo + HEAD_DIM, lo:lo + HEAD_DIM]


_MIXER_PARAMS = pltpu.CompilerParams(dimension_semantics=("parallel", "parallel"), vmem_limit_bytes=VMEM_LIMIT)


def _chunk_rows(c):
    return pl.ds(pl.multiple_of(c * CHUNK, CHUNK), CHUNK)


def _scan_order(t, n_chunks, reverse):
    return (n_chunks - 1 - t) if reverse else t


def _tile_row(row):
    return jnp.broadcast_to(row, (8, row.shape[1]))


def _edge_row(ref, c, reverse):
    if reverse:
        return ref[pl.ds(pl.multiple_of(c * CHUNK, 8), 8), :][0:1]
    return ref[pl.ds(pl.multiple_of(c * CHUNK + CHUNK - 8, 8), 8), :][7:8]


def _chunk_scan(x, reverse, op):
    n = x.shape[0]
    row = _iota(x.shape, 0) & (CHUNK - 1)
    s = 1
    while s < CHUNK:
        if reverse:
            shifted = pltpu.roll(x, n - s, 0)
            ok = row < CHUNK - s
        else:
            shifted = pltpu.roll(x, s, 0)
            ok = row >= s
        x = jnp.where(ok, op(x, shifted), x)
        s *= 2
    return x


def _row_loop(seq, fn):
    blk = min(seq, PRE_ROWS)
    k = min(seq // blk, ROWS_IN_FLIGHT)

    def body(i, c):
        _interleave([fn(pl.ds(pl.multiple_of((i * k + j) * blk, blk), blk)) for j in range(k)])
        return c

    if seq == blk * k:
        _interleave([fn(pl.ds(j * blk, blk)) for j in range(k)])
    else:
        lax.fori_loop(0, seq // (blk * k), body, 0)


def _chunk_loop(n_chunks, make_stream):
    k = min(n_chunks, CHUNKS_IN_FLIGHT)

    def body(t, carry):
        _interleave([make_stream(t * k + j) for j in range(k)])
        return carry

    if n_chunks == k:
        body(0, 0)
    else:
        lax.fori_loop(0, n_chunks // k, body, 0)


def _interleave(streams):
    results = [None] * len(streams)
    live = []
    for i, s in enumerate(streams):
        if hasattr(s, "send"):
            live.append(i)
        else:
            results[i] = s
    while live:
        for i in list(live):
            try:
                next(streams[i])
            except StopIteration as done:
                results[i] = done.value
                live.remove(i)
    return results


def _state_scan(n_chunks, reverse, s_ref, decay_ref, init, base):
    def body(t, s):
        c = base + _scan_order(t, n_chunks, reverse)
        inc = s_ref[c]
        s_ref[c] = s
        return decay_ref[c][0:1] * s + inc

    return lax.fori_loop(0, n_chunks, body, init)


def _finalize(seq, acc_ref, o_ref, fn):
    def blk(rows):
        res = fn(rows, acc_ref[rows, :] + o_ref[rows, :])
        if hasattr(res, "send"):
            res = yield from res
        o_ref[rows, :] = res

    _row_loop(seq, blk)


def _rope(x, cos, sin_signed):
    lane = _iota(x.shape, 1)
    first = (lane & 31) < 16
    swapped = jnp.where(first, pltpu.roll(x, GROUP_LANES - 16, 1), pltpu.roll(x, 16, 1))
    return x * cos + swapped * sin_signed


def _ret_kernel(dec_ref, nw_ref, u_ref, *rest, seq, nb, rope, has_state, emit_state):
    rest = list(rest)
    cos_ref, sin_ref = (rest.pop(0), rest.pop(0)) if rope else (None, None)
    s0_ref = rest.pop(0) if has_state else None
    o_ref = rest.pop(0)
    st_ref = rest.pop(0) if emit_state else None
    acc_ref, q_ref, k_ref, s_ref, cdec_ref = rest

    gl, L, G = GROUP_LANES, CHUNK, HEADS_PER_GROUP
    n_chunks = seq // L
    g = pl.program_id(1)
    masks = _head_masks()
    bd = _bd_mask()
    bd_avg = bd * (1.0 / HEAD_DIM)
    scale = HEAD_DIM ** -0.5
    ri = _iota((L, gl), 0).astype(F32)
    si = _iota((G * L, L), 0)
    sj = _iota((G * L, L), 1)
    i_idx = si & (L - 1)
    hrow = _head_of(_iota((G * L, 1), 0))

    def prepare(rows):
        q = u_ref[rows, 0:gl]
        k = u_ref[rows, gl:2 * gl]
        if rope:
            cos = cos_ref[rows, :]
            sin = sin_ref[rows, :]
            q = _rope(q, cos, sin)
            k = _rope(k, cos, sin)
        q_ref[rows, :] = q * scale
        k_ref[rows, :] = k

    _row_loop(nb * seq, prepare)

    for d in (0, 1):
        reverse = d == 1
        raw_row = masks[0] * dec_ref[d, g * G]
        raw_col = jnp.where(hrow == 0, dec_ref[d, g * G], 0.0)
        for h in range(1, G):
            raw_row = raw_row + masks[h] * dec_ref[d, g * G + h]
            raw_col = jnp.where(hrow == h, dec_ref[d, g * G + h], raw_col)
        lg_row = _log_sigmoid(raw_row)
        lg_col = _log_sigmoid(raw_col)
        diff = ((sj - i_idx) if reverse else (i_idx - sj)).astype(F32)
        dmat = jnp.where(diff >= 0, jnp.exp(jnp.maximum(diff, 0.0) * lg_col), 0.0)
        qdec = jnp.exp(((L - ri) if reverse else (ri + 1.0)) * lg_row)
        kdec = jnp.exp((ri if reverse else (L - 1.0 - ri)) * lg_row)
        cdec_tile = _tile_row(jnp.exp(float(L) * lg_row))
        out_ref = o_ref if reverse else acc_ref

        def increments(c, kdec=kdec, cdec_tile=cdec_tile):
            rows = _chunk_rows(c)
            inc = _mm_tn(k_ref[rows, :] * kdec, u_ref[rows, 2 * gl:3 * gl])
            yield
            s_ref[c] = bd * inc
            cdec_ref[c] = cdec_tile

        _chunk_loop(nb * n_chunks, increments)
        for s in range(nb):
            s_fin = _state_scan(n_chunks, reverse, s_ref, cdec_ref,
                                s0_ref[s, 0, d] if has_state else jnp.zeros((gl, gl), F32), s * n_chunks)
            if emit_state:
                _store_head_blocks(st_ref, s, d, s_fin)

        def outputs(c, dmat=dmat, qdec=qdec, out_ref=out_ref):
            rows = _chunk_rows(c)
            q = q_ref[rows, :]
            v = u_ref[rows, 2 * gl:3 * gl]
            att = _mm_nt(_stack(q, masks), k_ref[rows, :])
            inter = _mm(q * qdec, s_ref[c])
            yield
            intra = _mm(att * dmat, v)
            yield
            out_ref[rows, :] = _unstack(intra, masks, L) + inter

        _chunk_loop(nb * n_chunks, outputs)

    def finalize(rows, o):
        mu = _mm_split(o, bd_avg)
        yield
        dev = o - mu
        var = _mm_split(dev * dev, bd_avg)
        yield
        return dev * lax.rsqrt(var + EPS) * nw_ref[...] * _silu(u_ref[rows, 3 * gl:4 * gl])

    _finalize(nb * seq, acc_ref, o_ref, finalize)


def _ret_tables(seq):
    t = np.arange(seq)
    quarter = HEAD_DIM // 4
    inv = (ROPE_BASE ** (-np.arange(quarter, dtype=np.float32) / quarter)).astype(np.float32)
    ang_r = (t // GRID_W).astype(np.float32)[:, None] * inv
    ang_c = (t % GRID_W).astype(np.float32)[:, None] * inv
    cos = np.concatenate([np.cos(ang_r), np.cos(ang_r), np.cos(ang_c), np.cos(ang_c)], axis=1)
    sin = np.concatenate([-np.sin(ang_r), np.sin(ang_r), -np.sin(ang_c), np.sin(ang_c)], axis=1)
    tile = lambda a: jnp.asarray(np.tile(a.astype(np.float32), (1, HEADS_PER_GROUP)))
    return tile(cos), tile(sin)


def _retention(u_main, ret_decay, norm_w, batch, seq, state):
    latent = state is not None
    nb = _seqs_per_step(latent)
    u_spec, _, out_spec = _mixer_specs(MIX_RET, seq, nb)
    gl = GROUP_LANES
    rows, chunks = nb * seq, nb * seq // CHUNK
    const2 = lambda b, g: (0, 0)
    in_specs = [pl.BlockSpec(memory_space=pltpu.SMEM), pl.BlockSpec((1, gl), const2), u_spec]
    args = [ret_decay, jnp.tile(norm_w.reshape(1, HEAD_DIM), (1, HEADS_PER_GROUP)), u_main]
    if latent:
        cos, sin = _ret_tables(seq)
        in_specs += [pl.BlockSpec((seq, gl), const2), pl.BlockSpec((seq, gl), const2), _state_spec(gl)]
        args += [cos, sin, state]
    out_shape = [jax.ShapeDtypeStruct((batch * seq, GROUP_WIDTH), F32)]
    out_specs = [out_spec]
    if not latent:
        out_shape.append(_head_state_shape(batch))
        out_specs.append(_head_state_spec(nb))
    res = pl.pallas_call(
        functools.partial(_ret_kernel, seq=seq, nb=nb, rope=latent, has_state=latent, emit_state=not latent),
        grid=(batch // nb, N_GROUPS),
        in_specs=in_specs,
        out_specs=out_specs,
        out_shape=out_shape,
        scratch_shapes=[pltpu.VMEM((rows, gl), F32), pltpu.VMEM((rows, gl), F32), pltpu.VMEM((rows, gl), F32),
                        pltpu.VMEM((chunks, gl, gl), F32), pltpu.VMEM((chunks, 8, gl), F32)],
        compiler_params=_MIXER_PARAMS,
        name="retention",
    )(*args)
    return res[0], (None if latent else res[1])


def _gla_kernel(nw_ref, u_ref, bf_ref, bb_ref, *rest, seq, nb, has_state, emit_state):
    rest = list(rest)
    s0_ref = rest.pop(0) if has_state else None
    o_ref = rest.pop(0)
    st_ref = rest.pop(0) if emit_state else None
    acc_ref, s_ref, dec_ref, steep_ref = rest

    gl, L, G, SB = GROUP_LANES, CHUNK, HEADS_PER_GROUP, GLA_SUB
    n_chunks = seq // L
    n_sub = L // SB
    masks = _head_masks()
    bd = _bd_mask()
    bd_avg = bd * (1.0 / HEAD_DIM)
    scale = HEAD_DIM ** -0.5
    half = SB // 2
    pair_row = _iota((SB * SB, gl), 0)
    pi = lax.shift_right_logical(pair_row, 4)
    pj = pair_row & (SB - 1)
    pair_sum = (lax.shift_right_logical(_iota((SB, SB * SB), 1), 4) == _iota((SB, SB * SB), 0)).astype(BF16)
    zeros_half = jnp.zeros((half, gl), F32)

    for d in (0, 1):
        reverse = d == 1
        pair_ok = (pj >= pi) if reverse else (pj <= pi)
        out_ref = o_ref if reverse else acc_ref

        b_ref = bb_ref if reverse else bf_ref
        steep_ref[...] = jnp.zeros((8, gl), F32)

        def increments(c, reverse=reverse, b_ref=b_ref):
            rows = _chunk_rows(c)
            b = b_ref[rows, :]
            btot = b[0:1] if reverse else b[L - 1:L]
            inc = _mm_tn(u_ref[rows, 2 * gl:3 * gl], u_ref[rows, gl:2 * gl] * jnp.exp2(btot - b))
            yield
            s_ref[c] = bd * inc
            dec_ref[c] = _tile_row(jnp.exp2(btot))
            steep_ref[...] = jnp.maximum(steep_ref[...], _tile_row(-btot))

        _chunk_loop(nb * n_chunks, increments)
        factorise_all = jnp.max(steep_ref[...]) <= GLA_SAFE_LOG2
        for s in range(nb):
            s_fin = _state_scan(n_chunks, reverse, s_ref, dec_ref,
                                s0_ref[s, 0, d] if has_state else jnp.zeros((gl, gl), F32), s * n_chunks)
            if emit_state:
                _store_head_blocks(st_ref, s, d, s_fin.T)

        def outputs_factorised(c, reverse=reverse, out_ref=out_ref, b_ref=b_ref):
            rows = _chunk_rows(c)
            b = b_ref[rows, :]
            q = u_ref[rows, 0:gl] * scale
            k = u_ref[rows, gl:2 * gl]
            v = u_ref[rows, 2 * gl:3 * gl]
            o_inter = _mm_nt(q * jnp.exp2(b), s_ref[c])
            atts = []
            for i_blk in range(n_sub):
                lo = i_blk * SB
                bi = b[lo:lo + SB]
                ref, keys = (bi[SB - 1:SB], slice(lo, L)) if reverse else (bi[0:1], slice(0, lo + SB))
                qd = q[lo:lo + SB] * jnp.exp2(bi - ref)
                kd = k[keys] * jnp.exp2(ref - b[keys])
                att = _mm_nt(_stack(qd, masks), kd)
                n_keys = keys.stop - keys.start
                qrow = _iota((G * SB, n_keys), 0) & (SB - 1)
                kcol = _iota((G * SB, n_keys), 1)
                ok = (kcol >= qrow) if reverse else (kcol <= qrow + lo)
                atts.append((jnp.where(ok, att, 0.0), keys))
            yield
            outs = [_mm(a, v[keys]) for a, keys in atts]
            yield
            out_ref[rows, :] = o_inter + jnp.concatenate([_unstack(o, masks, SB) for o in outs], axis=0)

        def outputs_direct(c, reverse=reverse, pair_ok=pair_ok, out_ref=out_ref, b_ref=b_ref):
            rows = _chunk_rows(c)
            b = b_ref[rows, :]
            q = u_ref[rows, 0:gl] * scale
            k = u_ref[rows, gl:2 * gl]
            v = u_ref[rows, 2 * gl:3 * gl]
            o_inter = _mm_nt(q * jnp.exp2(b), s_ref[c])
            diag, off = [], []
            for i_blk in range(n_sub):
                lo = i_blk * SB
                qi, ki, bi = q[lo:lo + SB], k[lo:lo + SB], b[lo:lo + SB]
                prods = []
                for i in range(SB):
                    if reverse and i >= half:
                        part = slice(half, SB)
                    elif (not reverse) and i < half:
                        part = slice(0, half)
                    else:
                        part = slice(0, SB)
                    p = (qi[i:i + 1] * ki[part]) * jnp.exp2(bi[i:i + 1] - bi[part])
                    if part.start == half:
                        prods += [zeros_half, p]
                    elif part.stop == half:
                        prods += [p, zeros_half]
                    else:
                        prods.append(p)
                diag.append(_mm(jnp.concatenate(prods, axis=0), bd))
                if reverse and i_blk < n_sub - 1:
                    ref, other = bi[SB - 1:SB], slice(lo + SB, L)
                elif (not reverse) and i_blk > 0:
                    ref, other = bi[0:1], slice(0, lo)
                else:
                    off.append(None)
                    continue
                qd = qi * jnp.exp2(bi - ref)
                kd = k[other] * jnp.exp2(ref - b[other])
                off.append((_mm_nt(_stack(qd, masks), kd), other))
            yield
            off = [None if a is None else _mm(a[0], v[a[1]]) for a in off]
            for i_blk in range(n_sub):
                vi = v[i_blk * SB:(i_blk + 1) * SB]
                weighted = jnp.where(pair_ok, diag[i_blk] * jnp.concatenate([vi] * SB, axis=0), 0.0)
                diag[i_blk] = jnp.dot(pair_sum, weighted.astype(BF16), preferred_element_type=F32)
            yield
            blocks = []
            for i_blk in range(n_sub):
                o_blk = diag[i_blk]
                if off[i_blk] is not None:
                    o_blk = o_blk + _unstack(off[i_blk], masks, SB)
                blocks.append(o_blk)
            out_ref[rows, :] = o_inter + jnp.concatenate(blocks, axis=0)

        @pl.when(factorise_all)
        def _():
            _chunk_loop(nb * n_chunks, outputs_factorised)

        @pl.when(jnp.logical_not(factorise_all))
        def _():
            _chunk_loop(nb * n_chunks, outputs_direct)

    def finalize(rows, o):
        ms = _mm_split(o * o, bd_avg)
        yield
        return o * lax.rsqrt(ms + EPS) * nw_ref[...] * _silu(u_ref[rows, 3 * gl:4 * gl])

    _finalize(nb * seq, acc_ref, o_ref, finalize)


def _gla(u_main, log_decay, norm_w, batch, seq, state):
    latent = state is not None
    nb = _seqs_per_step(latent)
    u_spec, _, out_spec = _mixer_specs(MIX_GLA, seq, nb)
    gl = GROUP_LANES
    rows, chunks = nb * seq, nb * seq // CHUNK
    in_specs = [
        pl.BlockSpec((1, gl), lambda b, g: (0, 0)),
        u_spec,
        pl.BlockSpec((rows, gl), lambda b, g: (b, g)),
        pl.BlockSpec((rows, gl), lambda b, g: (b, N_GROUPS + g)),
    ]
    args = [jnp.tile(norm_w.reshape(1, HEAD_DIM), (1, HEADS_PER_GROUP)), u_main, log_decay, log_decay]
    if latent:
        in_specs.append(_state_spec(gl))
        args.append(state)
    out_shape = [jax.ShapeDtypeStruct((batch * seq, GROUP_WIDTH), F32)]
    out_specs = [out_spec]
    if not latent:
        out_shape.append(_head_state_shape(batch))
        out_specs.append(_head_state_spec(nb))
    res = pl.pallas_call(
        functools.partial(_gla_kernel, seq=seq, nb=nb, has_state=latent, emit_state=not latent),
        grid=(batch // nb, N_GROUPS),
        in_specs=in_specs,
        out_specs=out_specs,
        out_shape=out_shape,
        scratch_shapes=[pltpu.VMEM((rows, gl), F32),
                        pltpu.VMEM((chunks, gl, gl), F32), pltpu.VMEM((chunks, 8, gl), F32),
                        pltpu.VMEM((8, gl), F32)],
        compiler_params=_MIXER_PARAMS,
        name="gla",
    )(*args)
    return res[0], (None if latent else res[1])


def _mlstm_kernel(e_ref, u_ref, us_ref, *rest, seq, nb, has_state, emit_state):
    rest = list(rest)
    if has_state:
        c0_ref, n0_ref, m0_ref = rest.pop(0), rest.pop(0), rest.pop(0)
    o_ref = rest.pop(0)
    if emit_state:
        cst_ref, nst_ref, mst_ref = rest.pop(0), rest.pop(0), rest.pop(0)
    acc_ref, f_ref, x_ref, p_ref, s_ref, nu_ref, wp_ref, min_ref, mout_ref = rest

    gl, L, G = GROUP_LANES, CHUNK, HEADS_PER_GROUP
    n_chunks = seq // L
    masks = _head_masks()
    bd = _bd_mask()
    scale = HEAD_DIM ** -0.5
    si = _iota((G * L, L), 0) & (L - 1)
    sj = _iota((G * L, L), 1)
    lane_in_head = _iota((L, gl), 1) & (HEAD_DIM - 1)

    for d in (0, 1):
        reverse = d == 1
        causal = (sj >= si) if reverse else (sj <= si)
        last = 0 if reverse else L - 1
        out_ref = o_ref if reverse else acc_ref

        def gates(rows, d=d):
            r = _mm_split(us_ref[rows, :], e_ref[d, 0])
            yield
            x_ref[rows, :] = r[:, 0:gl]
            f_ref[rows, :] = r[:, gl:2 * gl]
            p_ref[rows, :] = r[:, 2 * gl:3 * gl]

        _row_loop(nb * seq, gates)

        def stabiliser(t, m_row, base, reverse=reverse):
            c = base + _scan_order(t, n_chunks, reverse)
            m_new = _edge_row(f_ref, c, reverse) + jnp.maximum(m_row, _edge_row(p_ref, c, reverse))
            min_ref[c] = _tile_row(m_row)
            mout_ref[c] = _tile_row(m_new)
            return m_new

        m_fin = [lax.fori_loop(0, n_chunks, functools.partial(stabiliser, base=s * n_chunks),
                               m0_ref[s, 0, d] if has_state else jnp.zeros((1, gl), F32)) for s in range(nb)]

        def increments(c, last=last):
            rows = _chunk_rows(c)
            f_tot = f_ref[rows, :][last:last + 1]
            m_in = min_ref[c][0:1]
            m_out = mout_ref[c][0:1]
            kw = u_ref[rows, gl:2 * gl] * jnp.exp(f_tot + x_ref[rows, :] - m_out)
            inc = _mm_tn(kw, u_ref[rows, 2 * gl:3 * gl])
            yield
            s_ref[c] = bd * inc
            nu_ref[c] = _tile_row(jnp.sum(kw, axis=0, keepdims=True))
            wp_ref[c] = _tile_row(jnp.exp(f_tot + m_in - m_out))

        _chunk_loop(nb * n_chunks, increments)

        def memory(t, carry, base, reverse=reverse):
            c_mat, n_row = carry
            c = base + _scan_order(t, n_chunks, reverse)
            inc = s_ref[c]
            nu = nu_ref[c][0:1]
            wp = wp_ref[c][0:1]
            s_ref[c] = c_mat
            nu_ref[c] = _tile_row(n_row)
            return wp * c_mat + inc, wp * n_row + nu

        for s in range(nb):
            if has_state:
                init = (c0_ref[s, 0, d], n0_ref[s, 0, d])
            else:
                init = (jnp.zeros((gl, gl), F32), jnp.zeros((1, gl), F32))
            c_fin, n_fin = lax.fori_loop(0, n_chunks, functools.partial(memory, base=s * n_chunks), init)
            if emit_state:
                _store_head_blocks(cst_ref, s, d, c_fin)
                nst_ref[s, 0, d], mst_ref[s, 0, d] = n_fin, m_fin[s]

        def outputs(c, causal=causal, out_ref=out_ref):
            rows = _chunk_rows(c)
            x = x_ref[rows, :]
            m_in = min_ref[c][0:1]
            q = u_ref[rows, 0:gl] * scale
            k = u_ref[rows, gl:2 * gl]
            v = u_ref[rows, 2 * gl:3 * gl]
            m_full = jnp.maximum(m_in, p_ref[rows, :])
            lhs = _split_lanes(-m_full, lane_in_head, 0, 1.0, 3)
            rhs = _split_lanes(x, lane_in_head, 3, 1.0, 0)
            logw = _mm_nt(_stack(lhs, masks), rhs)
            qk = _mm_nt(_stack(q, masks), k)
            inter = _mm(q, s_ref[c])
            q_n = _mm_split(q * nu_ref[c][0:1], bd)
            yield
            s = qk * jnp.exp(jnp.where(causal, logw, NEG))
            s_hi = s.astype(BF16)
            ones = jnp.ones((L, gl), BF16)
            r = jnp.dot(s_hi, jnp.concatenate([v.astype(BF16), ones], axis=1), preferred_element_type=F32)
            sum_lo = jnp.dot((s - s_hi.astype(F32)).astype(BF16), ones, preferred_element_type=F32)
            yield
            w_inter = jnp.exp(m_in - m_full)
            num = _unstack(r[:, :gl], masks, L) + w_inter * inter
            den = _unstack(r[:, gl:] + sum_lo, masks, L) + w_inter * q_n
            out_ref[rows, :] = num / jnp.maximum(jnp.abs(den), jnp.exp(-(f_ref[rows, :] + m_full)))

        _chunk_loop(nb * n_chunks, outputs)

    def finalize(rows, o):
        return o * jax.nn.sigmoid(u_ref[rows, 3 * gl:4 * gl])

    _finalize(nb * seq, acc_ref, o_ref, finalize)


def _gate_expanders(offset):
    e = np.zeros((2, N_GROUPS, SMALL_LANES, GROUP_LANES), np.float32)
    for d in range(2):
        for g in range(N_GROUPS):
            for h in range(HEADS_PER_GROUP):
                src = offset + d * N_HEADS + g * HEADS_PER_GROUP + h
                e[d, g, src, h * HEAD_DIM:(h + 1) * HEAD_DIM] = 1.0
    return jnp.asarray(e)


def _gate_bias_row(b_i, b_f):
    row = jnp.zeros((1, SMALL_LANES), F32)
    row = row.at[0, MI_OFF:MI_OFF + 2 * N_HEADS].set(b_i.astype(F32).reshape(-1))
    return row.at[0, MF_OFF:MF_OFF + 2 * N_HEADS].set(b_f.astype(F32).reshape(-1))


def _mlstm(u_main, u_small, batch, seq, state):
    latent = state is not None
    nb = _seqs_per_step(latent)
    u_spec, small_spec, out_spec = _mixer_specs(MIX_MLSTM, seq, nb)
    gl = GROUP_LANES
    rows, chunks = nb * seq, nb * seq // CHUNK
    expand = jnp.concatenate([_gate_expanders(MI_OFF), _gate_expanders(MF_OFF), _gate_expanders(MP_OFF)], axis=-1)
    in_specs = [pl.BlockSpec((2, 1, SMALL_LANES, 3 * gl), lambda b, g: (0, g, 0, 0)), u_spec, small_spec]
    args = [expand, u_main, u_small]
    if latent:
        in_specs += [_state_spec(gl), _state_spec(1), _state_spec(1)]
        args += list(state)
    out_shape = [jax.ShapeDtypeStruct((batch * seq, GROUP_WIDTH), F32)]
    out_specs = [out_spec]
    if not latent:
        out_shape += [_head_state_shape(batch),
                      jax.ShapeDtypeStruct((batch, N_GROUPS, 2, 1, gl), F32),
                      jax.ShapeDtypeStruct((batch, N_GROUPS, 2, 1, gl), F32)]
        out_specs += [_head_state_spec(nb), _state_spec(1, nb), _state_spec(1, nb)]
    res = pl.pallas_call(
        functools.partial(_mlstm_kernel, seq=seq, nb=nb, has_state=latent, emit_state=not latent),
        grid=(batch // nb, N_GROUPS),
        in_specs=in_specs,
        out_specs=out_specs,
        out_shape=out_shape,
        scratch_shapes=[pltpu.VMEM((rows, gl), F32)] * 4 + [pltpu.VMEM((chunks, gl, gl), F32)]
        + [pltpu.VMEM((chunks, 8, gl), F32)] * 4,
        compiler_params=_MIXER_PARAMS,
        name="mlstm",
    )(*args)
    return res[0], (None if latent else tuple(res[1:]))


def _na_ctx_kernel(u_ref, o_ref, ko_ref, vo_ref, *, seq, nb):
    gl = GROUP_LANES
    masks = _head_masks()

    def sequence(s):
        rows = slice(s * seq, (s + 1) * seq)
        q = u_ref[rows, 0:gl] * (HEAD_DIM ** -0.5)
        k = u_ref[rows, gl:2 * gl]
        v = u_ref[rows, 2 * gl:3 * gl]
        sc = _mm_nt(_stack(q, masks), k)
        yield
        p = jnp.exp(sc - jnp.max(sc, axis=1, keepdims=True))
        o = _mm(p, v)
        yield
        o_ref[rows, :] = _unstack(o / jnp.sum(p, axis=1, keepdims=True), masks, seq)
        for h in range(HEADS_PER_GROUP):
            ko_ref[s, h] = k[:, h * HEAD_DIM:(h + 1) * HEAD_DIM]
            vo_ref[s, h] = v[:, h * HEAD_DIM:(h + 1) * HEAD_DIM]

    _interleave([sequence(s) for s in range(nb)])


def _na_context(u_main, batch, seq):
    nb = _seqs_per_step(False)
    u_spec, _, out_spec = _mixer_specs(MIX_NA, seq, nb)
    kv_shape = jax.ShapeDtypeStruct((batch, N_HEADS, seq, HEAD_DIM), F32)
    kv_spec = pl.BlockSpec((nb, HEADS_PER_GROUP, seq, HEAD_DIM), lambda b, g: (b, g, 0, 0))
    return pl.pallas_call(
        functools.partial(_na_ctx_kernel, seq=seq, nb=nb),
        grid=(batch // nb, N_GROUPS),
        in_specs=[u_spec],
        out_specs=[out_spec, kv_spec, kv_spec],
        out_shape=[jax.ShapeDtypeStruct((batch * seq, GROUP_WIDTH), F32), kv_shape, kv_shape],
        compiler_params=_MIXER_PARAMS,
        name="na_context",
    )(u_main)


def _na_key_col0(qb):
    return int(np.clip(qb * NA_QC - NA_KW // 2, 0, GRID_W - NA_KCB))


def _na_lat_kernel(tab_ref, kc_ref, vc_ref, u_ref, o_ref, *, seq):
    gl = GROUP_LANES
    grid_rows = seq // GRID_W
    masks = _head_masks()
    scale = HEAD_DIM ** -0.5
    kc = kc_ref[0, 0]
    vc = vc_ref[0, 0]

    def query_block(r, qb):
        ks = jnp.clip(r - NA_KH // 2, 0, grid_rows - NA_KH)
        q_rows = pl.ds(pl.multiple_of(r * GRID_W, GRID_W) + qb * NA_QC, NA_QC)
        k_rows = [pl.ds(pl.multiple_of((ks + j) * GRID_W, GRID_W) + _na_key_col0(qb), NA_KCB) for j in range(NA_KH)]
        qs = _stack(u_ref[q_rows, 0:gl] * scale, masks)
        s_loc = _mm_nt(qs, jnp.concatenate([u_ref[rows, gl:2 * gl] for rows in k_rows], axis=0))
        s_ctx = _mm_nt(qs, kc)
        yield
        s_loc = s_loc + tab_ref[0, qb, ks - r + NA_KH - 1]
        m = jnp.maximum(jnp.max(s_loc, axis=1, keepdims=True), jnp.max(s_ctx, axis=1, keepdims=True))
        p_loc = jnp.exp(s_loc - m)
        p_ctx = jnp.exp(s_ctx - m)
        den = jnp.sum(p_loc, axis=1, keepdims=True) + jnp.sum(p_ctx, axis=1, keepdims=True)
        o = _mm(p_loc, jnp.concatenate([u_ref[rows, 2 * gl:3 * gl] for rows in k_rows], axis=0)) + _mm(p_ctx, vc)
        yield
        o_ref[q_rows, :] = _unstack(o / den, masks, NA_QC)

    def body(t, carry):
        _interleave([query_block(t * NA_ROWS_IN_FLIGHT + j, qb)
                     for j in range(NA_ROWS_IN_FLIGHT) for qb in range(GRID_W // NA_QC)])
        return carry

    lax.fori_loop(0, grid_rows // NA_ROWS_IN_FLIGHT, body, 0)


def _na_bias_table(rpb):
    c = np.arange(GRID_W)
    rel = c[None, :] - c[:, None]
    cs = np.clip(c - NA_KW // 2, 0, GRID_W - NA_KW)
    valid = (c[None, :] >= cs[:, None]) & (c[None, :] < cs[:, None] + NA_KW)
    n_rel = 2 * NA_KW - 1
    pick = (np.arange(n_rel)[:, None, None] == (np.clip(rel, -(NA_KW - 1), NA_KW - 1) + NA_KW - 1)[None])
    bmat = jnp.einsum("hdk,kcz->hdcz", rpb.astype(F32), jnp.asarray(pick, F32), precision=HI)
    bmat = jnp.where(valid[None, None], bmat, NEG)
    blocks = []
    for qb in range(GRID_W // NA_QC):
        sub = bmat[:, :, qb * NA_QC:(qb + 1) * NA_QC, _na_key_col0(qb):_na_key_col0(qb) + NA_KCB]
        tabs = jnp.stack([sub[:, s:s + NA_KH] for s in range(NA_KH)], axis=1)
        tabs = tabs.transpose(0, 1, 3, 2, 4).reshape(N_GROUPS, HEADS_PER_GROUP, NA_KH, NA_QC, NA_KH * NA_KCB)
        blocks.append(tabs.transpose(0, 2, 1, 3, 4).reshape(N_GROUPS, NA_KH, HEADS_PER_GROUP * NA_QC,
                                                             NA_KH * NA_KCB))
    return jnp.stack(blocks, axis=1)


def _heads_to_lanes(t):
    b, _, s, _ = t.shape
    t = t.reshape(b, N_GROUPS, HEADS_PER_GROUP, s, HEAD_DIM).transpose(0, 1, 3, 2, 4)
    return t.reshape(b, N_GROUPS, s, GROUP_LANES)


def _na_latent(u_main, rpb, k_ctx, v_ctx, batch, seq):
    u_spec, _, out_spec = _mixer_specs(MIX_NA, seq, 1)
    gl = GROUP_LANES
    past = k_ctx.shape[2]
    tab = _na_bias_table(rpb)
    ctx_spec = pl.BlockSpec((1, 1, past, gl), lambda b, g: (b, g, 0, 0))
    return pl.pallas_call(
        functools.partial(_na_lat_kernel, seq=seq),
        grid=(batch, N_GROUPS),
        in_specs=[pl.BlockSpec((1,) + tab.shape[1:], lambda b, g: (g, 0, 0, 0, 0)), ctx_spec, ctx_spec, u_spec],
        out_specs=out_spec,
        out_shape=jax.ShapeDtypeStruct((batch * seq, GROUP_WIDTH), F32),
        compiler_params=_MIXER_PARAMS,
        name="na_latent",
    )(tab, _heads_to_lanes(k_ctx.astype(F32)), _heads_to_lanes(v_ctx.astype(F32)), u_main)


def _to_block_diag(s):
    b = s.shape[0]
    G = HEADS_PER_GROUP
    s = s.astype(F32).reshape(b, 2, N_GROUPS, G, HEAD_DIM, 1, HEAD_DIM)
    eye = jnp.eye(G, dtype=F32).reshape(1, 1, 1, G, 1, G, 1)
    return (s * eye).reshape(b, 2, N_GROUPS, GROUP_LANES, GROUP_LANES).transpose(0, 2, 1, 3, 4)


def _rows_to_lanes(v):
    b = v.shape[0]
    return v.astype(F32).reshape(b, 2, N_GROUPS, 1, GROUP_LANES).transpose(0, 2, 1, 3, 4)


def _lanes_to_rows(v):
    b = v.shape[0]
    return v.transpose(0, 2, 1, 3, 4).reshape(b, 2, N_HEADS, HEAD_DIM)


def _swap_last(s):
    return jnp.swapaxes(s, -1, -2)


def _in_weight(w_in):
    gl = GROUP_LANES
    w_in = w_in.astype(BF16)
    cols = []
    for names in _MIXER_COLS:
        for g in range(N_GROUPS):
            for name in names:
                cols.append(w_in[..., _OFF[name] + g * gl:_OFF[name] + (g + 1) * gl])
    cols.append(w_in[..., _OFF["ga"]:_OFF["ga"] + 2 * GLA_RANK])
    cols.append(w_in[..., _OFF["mi"]:_OFF["mi"] + 2 * N_HEADS])
    cols.append(w_in[..., _OFF["mf"]:_OFF["mf"] + 2 * N_HEADS])
    cols.append(jnp.zeros((DEPTH, D_MODEL, SMALL_LANES - MF_OFF - 2 * N_HEADS), BF16))
    return jnp.concatenate(cols, axis=-1)


def _gla_gate_weight(w_a2_l):
    out = jnp.zeros((SMALL_LANES, 2 * GROUP_WIDTH), F32)
    for d in range(2):
        out = out.at[d * GLA_RANK:(d + 1) * GLA_RANK, d * GROUP_WIDTH:(d + 1) * GROUP_WIDTH].set(
            w_a2_l[d].astype(F32))
    return out.astype(BF16)


def _layer(x, mod, p, batch, seq, ctx):
    latent = ctx is not None
    l = p["layer"]
    x, u_main, u_small, log_decay = _ffn_inproj(x, mod[:, 0:6], p["norm_w"][0:3], p["gate_bias"], p["gla_wa"],
                                                p["gla_ba"], p["w1"], p["w3"], p["w2"], p["w_in"], l, seq)
    mix_a, st_gla = _gla(u_main, log_decay, p["gla_norm_w"], batch, seq,
                         _swap_last(_to_block_diag(ctx["gla"])) if latent else None)
    if latent:
        mix_b = _na_latent(u_main, p["na_rpb"], ctx["na_k"], ctx["na_v"], batch, seq)
    else:
        mix_b, na_k, na_v = _na_context(u_main, batch, seq)
    mix_c, st_ml = _mlstm(u_main, u_small, batch, seq,
                          (_to_block_diag(ctx["mC"]), _rows_to_lanes(ctx["mn"]),
                           _rows_to_lanes(jnp.repeat(ctx["mm"][..., None], HEAD_DIM, axis=-1))) if latent else None)
    mix_d, st_ret = _retention(u_main, p["ret_decay"], p["ret_norm_w"], batch, seq,
                               _to_block_diag(ctx["ret"]) if latent else None)
    x = _outproj_ffn(x, (mix_a, mix_b, mix_c, mix_d), mod[:, 3:9], p["norm_w"][3:6], p["w_out"],
                     p["w1"], p["w3"], p["w2"], l, seq)
    if latent:
        return x, None
    new = {
        "na_k": na_k, "na_v": na_v, "gla": st_gla, "mC": st_ml[0],
        "mn": _lanes_to_rows(st_ml[1]), "mm": _lanes_to_rows(st_ml[2])[..., 0], "ret": st_ret,
    }
    return x, new


def kernel(x_prompt, x_sample, cache_na_k, cache_na_v, state_gla, state_mlstm_C, state_mlstm_n, state_mlstm_m,
           state_ret, c, c_ctx, w_mod, b_mod, norm_w, ffn_w1, ffn_w3, ffn_w2, w_in, w_out, gla_w_a2, gla_b_a,
           gla_norm_w, na_rpb, mlstm_b_i, mlstm_b_f, ret_decay, ret_norm_w):
    batch, seq, _ = x_prompt.shape
    dec_batch, dec_seq, _ = x_sample.shape

    cvec = jnp.zeros((8, D_MODEL), F32).at[0].set(c_ctx).at[1:1 + dec_batch].set(c)
    mods = _modulation(cvec, w_mod, b_mod).reshape(DEPTH, 8, N_MOD, D_MODEL)

    w1, w3, w2 = ffn_w1.astype(BF16), ffn_w3.astype(BF16), ffn_w2.astype(BF16)
    w_in_all, w_out_all = _in_weight(w_in), w_out.astype(BF16)
    params = []
    for l in range(DEPTH):
        params.append({
            "layer": l, "norm_w": norm_w[l],
            "w1": w1, "w3": w3, "w2": w2, "w_in": w_in_all, "w_out": w_out_all,
            "gla_wa": _gla_gate_weight(gla_w_a2[l]), "gla_ba": gla_b_a[l].astype(F32).reshape(1, 2 * GROUP_WIDTH),
            "gla_norm_w": gla_norm_w[l],
            "na_rpb": na_rpb[l], "gate_bias": _gate_bias_row(mlstm_b_i[l], mlstm_b_f[l]),
            "ret_decay": ret_decay[l], "ret_norm_w": ret_norm_w[l],
        })

    xp = x_prompt.reshape(batch * seq, D_MODEL)
    states = []
    for l in range(DEPTH):
        xp, st = _layer(xp, mods[l, 0:1], params[l], batch, seq, None)
        states.append(st)

    xs = x_sample.reshape(dec_batch * dec_seq, D_MODEL)
    for l in range(DEPTH):
        ctx = {"na_k": cache_na_k[:, l], "na_v": cache_na_v[:, l], "gla": state_gla[:, l],
               "mC": state_mlstm_C[:, l], "mn": state_mlstm_n[:, l], "mm": state_mlstm_m[:, l],
               "ret": state_ret[:, l]}
        xs, _ = _layer(xs, mods[l, 1:1 + dec_batch], params[l], dec_batch, dec_seq, ctx)

    stack = lambda name: jnp.stack([s[name] for s in states], axis=1)
    return (xp.reshape(batch, seq, D_MODEL), xs.reshape(dec_batch, dec_seq, D_MODEL),
            stack("na_k"), stack("na_v"), stack("gla"), stack("mC"), stack("mn"), stack("mm"), stack("ret"))
```

```python
import functools

import numpy as np
import jax
import jax.numpy as jnp
from jax import lax
from jax.experimental import pallas as pl
from jax.experimental.pallas import tpu as pltpu

D_MODEL = 1024
DEPTH = 2
HEAD_DIM = 64
N_HEADS = 4
GROUP_WIDTH = N_HEADS * HEAD_DIM
N_MOD = 9
GLA_RANK = 16
GLA_TAU = 16.0
CHUNK = 64
GRID_W = 64
NA_KH = 8
NA_KW = 16
NA_QC = 16
NA_KCB = NA_QC + NA_KW
ROPE_BASE = 10000.0
EPS = 1e-6
D_FF = 2816

HEADS_PER_GROUP = 2
GROUP_LANES = HEADS_PER_GROUP * HEAD_DIM
N_GROUPS = N_HEADS // HEADS_PER_GROUP
GLA_SUB = 16
GLA_SAFE_LOG2 = 64.0
ROW_TILE_IN = 256
ROW_TILE_OUT = 512
ROW_PARTS = 2
CTX_SEQS_PER_STEP = 4
PRE_ROWS = 128
ROWS_IN_FLIGHT = 8
NA_ROWS_IN_FLIGHT = 4
LOG2E = 1.4426950408889634
CHUNKS_IN_FLIGHT = 8
LIGHT_CHUNKS_IN_FLIGHT = 16
SMALL_LANES = 128
MI_OFF = 2 * GLA_RANK
MF_OFF = MI_OFF + 2 * N_HEADS
MP_OFF = MF_OFF + 2 * N_HEADS
NEG = -1e30
VMEM_LIMIT = 56 * 1024 * 1024

F32 = jnp.float32
BF16 = jnp.bfloat16
HI = lax.Precision.HIGHEST

_OFF = {}
_o = 0
for _name, _size in (("gq", 256), ("gk", 256), ("gv", 256), ("gg", 256), ("ga", 32), ("nq", 256), ("nk", 256),
                     ("nv", 256), ("mq", 256), ("mk", 256), ("mv", 256), ("mo", 256), ("mi", 8), ("mf", 8),
                     ("rq", 256), ("rk", 256), ("rv", 256), ("rg", 256)):
    _OFF[_name] = _o
    _o += _size
N_IN = _o
_MIXER_COLS = (("gq", "gk", "gv", "gg"), ("mq", "mk", "mv", "mo"), ("rq", "rk", "rv", "rg"), ("nq", "nk", "nv"))
MIX_GLA, MIX_MLSTM, MIX_RET, MIX_NA = range(4)
MAIN_COLS = sum(len(names) for names in _MIXER_COLS) * N_GROUPS * GROUP_LANES


def _mm(a, b):
    return jnp.dot(a.astype(BF16), b.astype(BF16), preferred_element_type=F32)


def _mm_nt(a, b):
    return lax.dot_general(a.astype(BF16), b.astype(BF16), (((1,), (1,)), ((), ())), preferred_element_type=F32)


def _mm_tn(a, b):
    return lax.dot_general(a.astype(BF16), b.astype(BF16), (((0,), (0,)), ((), ())), preferred_element_type=F32)


def _split3(x):
    hi = x.astype(BF16)
    rest = x - hi.astype(F32)
    mid = rest.astype(BF16)
    lo = (rest - mid.astype(F32)).astype(BF16)
    return jnp.concatenate([hi, mid, lo], axis=1)


def _dot_split(pieces, w):
    wb = w.astype(BF16)
    return jnp.dot(pieces, jnp.concatenate([wb, wb, wb], axis=0), preferred_element_type=F32)


def _mm_split(x, w):
    return _dot_split(_split3(x), w)


def _log_sigmoid(x):
    return jnp.minimum(x, 0.0) - jnp.log1p(jnp.exp(-jnp.abs(x)))


def _silu(x):
    return x * jax.nn.sigmoid(x)


def _iota(shape, dim):
    return lax.broadcasted_iota(jnp.int32, shape, dim)


def _head_of(idx):
    return lax.shift_right_logical(idx, 6)


def _head_masks():
    lane = _iota((1, GROUP_LANES), 1)
    return [(_head_of(lane) == h).astype(F32) for h in range(HEADS_PER_GROUP)]


def _bd_mask():
    n = GROUP_LANES
    return (_head_of(_iota((n, n), 0)) == _head_of(_iota((n, n), 1))).astype(F32)


def _stack(x, masks):
    return jnp.concatenate([x * m for m in masks], axis=0)


def _unstack(r, masks, n):
    out = r[0:n] * masks[0]
    for h in range(1, len(masks)):
        out = out + r[h * n:(h + 1) * n] * masks[h]
    return out


def _rmsnorm(x, w):
    return x * lax.rsqrt(jnp.mean(x * x, axis=-1, keepdims=True) + EPS) * w


def _split_lanes(x, lane_in_head, at, fill, fill_at):
    hi = x.astype(BF16).astype(F32)
    out = jnp.where(lane_in_head == at, hi, jnp.where(lane_in_head == at + 1, x - hi, 0.0))
    is_fill = (lane_in_head >= fill_at) & (lane_in_head < fill_at + 2)
    return jnp.where(is_fill, fill, out)


def _mod_kernel(c_ref, w_ref, b_ref, o_ref):
    s = _silu(c_ref[...])
    o_ref[0] = _mm(s, w_ref[0]) + b_ref[0]


def _modulation(cvec, w_mod, b_mod):
    n = N_MOD * D_MODEL
    tn = n // 8
    return pl.pallas_call(
        _mod_kernel,
        grid=(DEPTH, n // tn),
        in_specs=[
            pl.BlockSpec((8, D_MODEL), lambda l, j: (0, 0)),
            pl.BlockSpec((1, D_MODEL, tn), lambda l, j: (l, 0, j)),
            pl.BlockSpec((1, 1, tn), lambda l, j: (l, 0, j)),
        ],
        out_specs=pl.BlockSpec((1, 8, tn), lambda l, j: (l, 0, j)),
        out_shape=jax.ShapeDtypeStruct((DEPTH, 8, n), F32),
        compiler_params=pltpu.CompilerParams(dimension_semantics=("parallel", "parallel"),
                                             vmem_limit_bytes=VMEM_LIMIT),
        name="modulation",
    )(cvec, w_mod, b_mod.reshape(DEPTH, 1, n))


def _half_ffn(x, mod, g, w1_ref, w3_ref, w2_ref):
    h = _rmsnorm(x, g[0:1]) * (1.0 + mod[1:2]) + mod[0:1]
    hb = h.astype(BF16)
    h1 = jnp.dot(hb, w1_ref[...], preferred_element_type=F32)
    h3 = jnp.dot(hb, w3_ref[...], preferred_element_type=F32)
    yield
    y = jnp.dot((_silu(h1) * h3).astype(BF16), w2_ref[...], preferred_element_type=F32)
    yield
    return x + 0.5 * mod[2:3] * _rmsnorm(y, g[1:2])


def _row_parts(tm):
    part = tm // ROW_PARTS
    return [slice(j * part, (j + 1) * part) for j in range(ROW_PARTS)]


def _mlstm_gate_scans(us, gate_bias):
    lane = _iota(us.shape, 1)
    field = 2 * N_HEADS
    fwd = (lane & (field - 1)) < N_HEADS
    pre = us + gate_bias
    lf = jnp.where((lane >= MF_OFF) & (lane < MF_OFF + field), _log_sigmoid(pre), 0.0)
    f = jnp.where(fwd, _chunk_scan(lf, False, jnp.add), _chunk_scan(lf, True, jnp.add))
    x = pre - pltpu.roll(f, SMALL_LANES - (MF_OFF - MI_OFF), 1)
    p = jnp.where(fwd, _chunk_scan(x, False, jnp.maximum), _chunk_scan(x, True, jnp.maximum))
    p = pltpu.roll(p, MP_OFF - MI_OFF, 1)
    return jnp.where(lane < MI_OFF, us,
                     jnp.where(lane < MF_OFF, x, jnp.where(lane < MP_OFF, f, jnp.where(lane < MP_OFF + field, p, 0.0))))


def _gla_gate(u_small, wa, ba):
    return _mm(u_small, wa) + ba


def _gla_log_decay(a_pre):
    la = _log_sigmoid(a_pre) * (LOG2E / GLA_TAU)
    return jnp.concatenate([_chunk_scan(la[:, :GROUP_WIDTH], False, jnp.add),
                            _chunk_scan(la[:, GROUP_WIDTH:], True, jnp.add)], axis=1)


def _ffn_inproj_kernel(x_ref, mod_ref, g_ref, gb_ref, wa_ref, ba_ref, w1_ref, w3_ref, w2_ref, win_ref,
                       x_out_ref, um_ref, us_ref, ub_ref):
    mod = mod_ref[0]
    g = g_ref[...]

    def part(rows):
        x = yield from _half_ffn(x_ref[rows, :], mod[0:3], g[0:2], w1_ref, w3_ref, w2_ref)
        x_out_ref[rows, :] = x
        hb = (_rmsnorm(x, g[2:3]) * (1.0 + mod[4:5]) + mod[3:4]).astype(BF16)
        u_small = jnp.dot(hb, win_ref[:, MAIN_COLS:], preferred_element_type=F32)
        u_main = jnp.dot(hb, win_ref[:, :MAIN_COLS], preferred_element_type=F32)
        us_ref[rows, :] = _mlstm_gate_scans(u_small, gb_ref[...])
        ub_ref[rows, :] = _gla_log_decay(_gla_gate(u_small, wa_ref[...], ba_ref[...]))
        yield
        um_ref[rows, :] = u_main

    _interleave([part(rows) for rows in _row_parts(x_ref.shape[0])])


def _outproj_ffn_kernel(x_ref, ma_ref, mb_ref, mc_ref, md_ref, mod_ref, g_ref, wo_ref, w1_ref, w3_ref, w2_ref,
                        o_ref):
    mod = mod_ref[0]
    g = g_ref[...]

    def part(rows):
        mix = jnp.concatenate([ma_ref[rows, :], mb_ref[rows, :], mc_ref[rows, :], md_ref[rows, :]], axis=-1)
        y = jnp.dot(mix.astype(BF16), wo_ref[...], preferred_element_type=F32)
        yield
        x = x_ref[rows, :] + mod[2:3] * _rmsnorm(y, g[0:1])
        o_ref[rows, :] = yield from _half_ffn(x, mod[3:6], g[1:3], w1_ref, w3_ref, w2_ref)

    _interleave([part(rows) for rows in _row_parts(x_ref.shape[0])])


def _row_tile(n_mod, rows_per_batch, tile):
    return tile if n_mod == 1 else min(tile, rows_per_batch)


def _mod_index(n_mod, rows_per_batch, tm):
    if n_mod == 1:
        return lambda i: (0, 0, 0)
    per = rows_per_batch // tm
    return lambda i: (i // per, 0, 0)


_ROW_PARAMS = pltpu.CompilerParams(dimension_semantics=("parallel",), vmem_limit_bytes=VMEM_LIMIT)


def _ffn_weight_specs(layer, half):
    pick = lambda i: (layer, half, 0, 0)
    return [pl.BlockSpec((None, None, D_MODEL, D_FF), pick, pipeline_mode=pl.Buffered(1)),
            pl.BlockSpec((None, None, D_MODEL, D_FF), pick, pipeline_mode=pl.Buffered(1)),
            pl.BlockSpec((None, None, D_FF, D_MODEL), pick, pipeline_mode=pl.Buffered(1))]


def _ffn_inproj(x, mod6, g3, gate_bias, gla_wa, gla_ba, w1, w3, w2, w_big, layer, rows_per_batch):
    rows = x.shape[0]
    tm = _row_tile(mod6.shape[0], rows_per_batch, ROW_TILE_IN)
    row_spec = lambda width: pl.BlockSpec((tm, width), lambda i: (i, 0))
    const = lambda i: (0, 0)
    widths = (D_MODEL, MAIN_COLS, SMALL_LANES, 2 * GROUP_WIDTH)
    return pl.pallas_call(
        _ffn_inproj_kernel,
        grid=(rows // tm,),
        in_specs=[
            row_spec(D_MODEL),
            pl.BlockSpec((1, 6, D_MODEL), _mod_index(mod6.shape[0], rows_per_batch, tm)),
            pl.BlockSpec((3, D_MODEL), const),
            pl.BlockSpec((1, SMALL_LANES), const),
            pl.BlockSpec((SMALL_LANES, 2 * GROUP_WIDTH), const),
            pl.BlockSpec((1, 2 * GROUP_WIDTH), const),
            *_ffn_weight_specs(layer, 0),
            pl.BlockSpec((None, D_MODEL, MAIN_COLS + SMALL_LANES), lambda i: (layer, 0, 0),
                         pipeline_mode=pl.Buffered(1)),
        ],
        out_specs=[row_spec(w) for w in widths],
        out_shape=[jax.ShapeDtypeStruct((rows, w), F32) for w in widths],
        compiler_params=_ROW_PARAMS,
        name="ffn_inproj",
    )(x, mod6, g3, gate_bias, gla_wa, gla_ba, w1, w3, w2, w_big)


def _outproj_ffn(x, mixes, mod6, g3, w_out, w1, w3, w2, layer, rows_per_batch):
    rows = x.shape[0]
    tm = _row_tile(mod6.shape[0], rows_per_batch, ROW_TILE_OUT)
    row_spec = lambda width: pl.BlockSpec((tm, width), lambda i: (i, 0))
    return pl.pallas_call(
        _outproj_ffn_kernel,
        grid=(rows // tm,),
        in_specs=[
            row_spec(D_MODEL), *[row_spec(GROUP_WIDTH)] * 4,
            pl.BlockSpec((1, 6, D_MODEL), _mod_index(mod6.shape[0], rows_per_batch, tm)),
            pl.BlockSpec((3, D_MODEL), lambda i: (0, 0)),
            pl.BlockSpec((None, D_MODEL, D_MODEL), lambda i: (layer, 0, 0), pipeline_mode=pl.Buffered(1)),
            *_ffn_weight_specs(layer, 1),
        ],
        out_specs=row_spec(D_MODEL),
        out_shape=jax.ShapeDtypeStruct((rows, D_MODEL), F32),
        compiler_params=_ROW_PARAMS,
        name="outproj_ffn",
    )(x, *mixes, mod6, g3, w_out, w1, w3, w2)


def _seqs_per_step(latent):
    return 1 if latent else CTX_SEQS_PER_STEP


def _mixer_specs(mixer, seq, nb):
    gl = GROUP_LANES
    n_blocks = len(_MIXER_COLS[mixer])
    first = sum(len(names) for names in _MIXER_COLS[:mixer]) * N_GROUPS
    assert first % n_blocks == 0
    u_spec = pl.BlockSpec((nb * seq, n_blocks * gl), lambda b, g: (b, first // n_blocks + g))
    small_spec = pl.BlockSpec((nb * seq, SMALL_LANES), lambda b, g: (b, 0))
    out_spec = pl.BlockSpec((nb * seq, gl), lambda b, g: (b, g))
    return u_spec, small_spec, out_spec


def _state_spec(rows, nb=1):
    return pl.BlockSpec((nb, 1, 2, rows, GROUP_LANES), lambda b, g: (b, g, 0, 0, 0))


def _head_state_shape(batch):
    return jax.ShapeDtypeStruct((batch, 2, N_HEADS, HEAD_DIM, HEAD_DIM), F32)


def _head_state_spec(nb):
    return pl.BlockSpec((nb, 2, HEADS_PER_GROUP, HEAD_DIM, HEAD_DIM), lambda b, g: (b, 0, g, 0, 0))


def _store_head_blocks(ref, s, d, mat):
    for h in range(HEADS_PER_GROUP):
        lo = h * HEAD_DIM
        ref[s, d, h] = mat[lo:lo + HEAD_DIM, lo:lo + HEAD_DIM]


_MIXER_PARAMS = pltpu.CompilerParams(dimension_semantics=("parallel", "parallel"), vmem_limit_bytes=VMEM_LIMIT)


def _chunk_rows(c):
    return pl.ds(pl.multiple_of(c * CHUNK, CHUNK), CHUNK)


def _scan_order(t, n_chunks, reverse):
    return (n_chunks - 1 - t) if reverse else t


def _tile_row(row):
    return jnp.broadcast_to(row, (8, row.shape[1]))


def _edge_row(ref, c, reverse):
    if reverse:
        return ref[pl.ds(pl.multiple_of(c * CHUNK, 8), 8), :][0:1]
    return ref[pl.ds(pl.multiple_of(c * CHUNK + CHUNK - 8, 8), 8), :][7:8]


def _chunk_scan(x, reverse, op):
    n = x.shape[0]
    row = _iota(x.shape, 0) & (CHUNK - 1)
    s = 1
    while s < CHUNK:
        if reverse:
            shifted = pltpu.roll(x, n - s, 0)
            ok = row < CHUNK - s
        else:
            shifted = pltpu.roll(x, s, 0)
            ok = row >= s
        x = jnp.where(ok, op(x, shifted), x)
        s *= 2
    return x


def _row_loop(seq, fn):
    blk = min(seq, PRE_ROWS)
    k = min(seq // blk, ROWS_IN_FLIGHT)

    def body(i, c):
        _interleave([fn(pl.ds(pl.multiple_of((i * k + j) * blk, blk), blk)) for j in range(k)])
        return c

    if seq == blk * k:
        _interleave([fn(pl.ds(j * blk, blk)) for j in range(k)])
    else:
        lax.fori_loop(0, seq // (blk * k), body, 0)


def _chunk_loop(n_chunks, make_stream, in_flight=None):
    k = min(n_chunks, in_flight or CHUNKS_IN_FLIGHT)

    def body(t, carry):
        _interleave([make_stream(t * k + j) for j in range(k)])
        return carry

    if n_chunks == k:
        body(0, 0)
    else:
        lax.fori_loop(0, n_chunks // k, body, 0)


def _interleave(streams):
    results = [None] * len(streams)
    live = []
    for i, s in enumerate(streams):
        if hasattr(s, "send"):
            live.append(i)
        else:
            results[i] = s
    while live:
        for i in list(live):
            try:
                next(streams[i])
            except StopIteration as done:
                results[i] = done.value
                live.remove(i)
    return results


def _state_scan(n_chunks, reverse, s_ref, decay_ref, init, base):
    def body(t, s):
        c = base + _scan_order(t, n_chunks, reverse)
        inc = s_ref[c]
        s_ref[c] = s
        return decay_ref[c][0:1] * s + inc

    return lax.fori_loop(0, n_chunks, body, init)


def _finalize(seq, acc_ref, o_ref, fn):
    def blk(rows):
        res = fn(rows, acc_ref[rows, :] + o_ref[rows, :])
        if hasattr(res, "send"):
            res = yield from res
        o_ref[rows, :] = res

    _row_loop(seq, blk)


def _rope(x, cos, sin_signed):
    lane = _iota(x.shape, 1)
    first = (lane & 31) < 16
    swapped = jnp.where(first, pltpu.roll(x, GROUP_LANES - 16, 1), pltpu.roll(x, 16, 1))
    return x * cos + swapped * sin_signed


def _ret_kernel(dec_ref, nw_ref, u_ref, *rest, seq, nb, rope, has_state, emit_state):
    rest = list(rest)
    cos_ref, sin_ref = (rest.pop(0), rest.pop(0)) if rope else (None, None)
    s0_ref = rest.pop(0) if has_state else None
    o_ref = rest.pop(0)
    st_ref = rest.pop(0) if emit_state else None
    acc_ref, q_ref, k_ref, s_ref, cdec_ref = rest

    gl, L, G = GROUP_LANES, CHUNK, HEADS_PER_GROUP
    n_chunks = seq // L
    g = pl.program_id(1)
    masks = _head_masks()
    bd = _bd_mask()
    bd_avg = bd * (1.0 / HEAD_DIM)
    scale = HEAD_DIM ** -0.5
    ri = _iota((L, gl), 0).astype(F32)
    si = _iota((G * L, L), 0)
    sj = _iota((G * L, L), 1)
    i_idx = si & (L - 1)
    hrow = _head_of(_iota((G * L, 1), 0))

    def prepare(rows):
        q = u_ref[rows, 0:gl]
        k = u_ref[rows, gl:2 * gl]
        if rope:
            cos = cos_ref[rows, :]
            sin = sin_ref[rows, :]
            q = _rope(q, cos, sin)
            k = _rope(k, cos, sin)
        q_ref[rows, :] = q * scale
        k_ref[rows, :] = k

    _row_loop(nb * seq, prepare)

    for d in (0, 1):
        reverse = d == 1
        raw_row = masks[0] * dec_ref[d, g * G]
        raw_col = jnp.where(hrow == 0, dec_ref[d, g * G], 0.0)
        for h in range(1, G):
            raw_row = raw_row + masks[h] * dec_ref[d, g * G + h]
            raw_col = jnp.where(hrow == h, dec_ref[d, g * G + h], raw_col)
        lg_row = _log_sigmoid(raw_row)
        lg_col = _log_sigmoid(raw_col)
        diff = ((sj - i_idx) if reverse else (i_idx - sj)).astype(F32)
        dmat = jnp.where(diff >= 0, jnp.exp(jnp.maximum(diff, 0.0) * lg_col), 0.0)
        qdec = jnp.exp(((L - ri) if reverse else (ri + 1.0)) * lg_row)
        kdec = jnp.exp((ri if reverse else (L - 1.0 - ri)) * lg_row)
        cdec_tile = _tile_row(jnp.exp(float(L) * lg_row))
        out_ref = o_ref if reverse else acc_ref

        def increments(c, kdec=kdec, cdec_tile=cdec_tile):
            rows = _chunk_rows(c)
            inc = _mm_tn(k_ref[rows, :] * kdec, u_ref[rows, 2 * gl:3 * gl])
            yield
            s_ref[c] = bd * inc
            cdec_ref[c] = cdec_tile

        _chunk_loop(nb * n_chunks, increments, LIGHT_CHUNKS_IN_FLIGHT)
        for s in range(nb):
            s_fin = _state_scan(n_chunks, reverse, s_ref, cdec_ref,
                                s0_ref[s, 0, d] if has_state else jnp.zeros((gl, gl), F32), s * n_chunks)
            if emit_state:
                _store_head_blocks(st_ref, s, d, s_fin)

        def outputs(c, dmat=dmat, qdec=qdec, out_ref=out_ref):
            rows = _chunk_rows(c)
            q = q_ref[rows, :]
            v = u_ref[rows, 2 * gl:3 * gl]
            att = _mm_nt(_stack(q, masks), k_ref[rows, :])
            inter = _mm(q * qdec, s_ref[c])
            yield
            intra = _mm(att * dmat, v)
            yield
            out_ref[rows, :] = _unstack(intra, masks, L) + inter

        _chunk_loop(nb * n_chunks, outputs)

    def finalize(rows, o):
        mu = _mm_split(o, bd_avg)
        yield
        dev = o - mu
        var = _mm_split(dev * dev, bd_avg)
        yield
        return dev * lax.rsqrt(var + EPS) * nw_ref[...] * _silu(u_ref[rows, 3 * gl:4 * gl])

    _finalize(nb * seq, acc_ref, o_ref, finalize)


def _ret_tables(seq):
    t = np.arange(seq)
    quarter = HEAD_DIM // 4
    inv = (ROPE_BASE ** (-np.arange(quarter, dtype=np.float32) / quarter)).astype(np.float32)
    ang_r = (t // GRID_W).astype(np.float32)[:, None] * inv
    ang_c = (t % GRID_W).astype(np.float32)[:, None] * inv
    cos = np.concatenate([np.cos(ang_r), np.cos(ang_r), np.cos(ang_c), np.cos(ang_c)], axis=1)
    sin = np.concatenate([-np.sin(ang_r), np.sin(ang_r), -np.sin(ang_c), np.sin(ang_c)], axis=1)
    tile = lambda a: jnp.asarray(np.tile(a.astype(np.float32), (1, HEADS_PER_GROUP)))
    return tile(cos), tile(sin)


def _retention(u_main, ret_decay, norm_w, batch, seq, state):
    latent = state is not None
    nb = _seqs_per_step(latent)
    u_spec, _, out_spec = _mixer_specs(MIX_RET, seq, nb)
    gl = GROUP_LANES
    rows, chunks = nb * seq, nb * seq // CHUNK
    const2 = lambda b, g: (0, 0)
    in_specs = [pl.BlockSpec(memory_space=pltpu.SMEM), pl.BlockSpec((1, gl), const2), u_spec]
    args = [ret_decay, jnp.tile(norm_w.reshape(1, HEAD_DIM), (1, HEADS_PER_GROUP)), u_main]
    if latent:
        cos, sin = _ret_tables(seq)
        in_specs += [pl.BlockSpec((seq, gl), const2), pl.BlockSpec((seq, gl), const2), _state_spec(gl)]
        args += [cos, sin, state]
    out_shape = [jax.ShapeDtypeStruct((batch * seq, GROUP_WIDTH), F32)]
    out_specs = [out_spec]
    if not latent:
        out_shape.append(_head_state_shape(batch))
        out_specs.append(_head_state_spec(nb))
    res = pl.pallas_call(
        functools.partial(_ret_kernel, seq=seq, nb=nb, rope=latent, has_state=latent, emit_state=not latent),
        grid=(batch // nb, N_GROUPS),
        in_specs=in_specs,
        out_specs=out_specs,
        out_shape=out_shape,
        scratch_shapes=[pltpu.VMEM((rows, gl), F32), pltpu.VMEM((rows, gl), F32), pltpu.VMEM((rows, gl), F32),
                        pltpu.VMEM((chunks, gl, gl), F32), pltpu.VMEM((chunks, 8, gl), F32)],
        compiler_params=_MIXER_PARAMS,
        name="retention",
    )(*args)
    return res[0], (None if latent else res[1])


def _gla_kernel(nw_ref, u_ref, bf_ref, bb_ref, *rest, seq, nb, has_state, emit_state):
    rest = list(rest)
    s0_ref = rest.pop(0) if has_state else None
    o_ref = rest.pop(0)
    st_ref = rest.pop(0) if emit_state else None
    acc_ref, s_ref, dec_ref, steep_ref = rest

    gl, L, G, SB = GROUP_LANES, CHUNK, HEADS_PER_GROUP, GLA_SUB
    n_chunks = seq // L
    n_sub = L // SB
    masks = _head_masks()
    bd = _bd_mask()
    bd_avg = bd * (1.0 / HEAD_DIM)
    scale = HEAD_DIM ** -0.5
    half = SB // 2
    pair_row = _iota((SB * SB, gl), 0)
    pi = lax.shift_right_logical(pair_row, 4)
    pj = pair_row & (SB - 1)
    pair_sum = (lax.shift_right_logical(_iota((SB, SB * SB), 1), 4) == _iota((SB, SB * SB), 0)).astype(BF16)
    zeros_half = jnp.zeros((half, gl), F32)

    for d in (0, 1):
        reverse = d == 1
        pair_ok = (pj >= pi) if reverse else (pj <= pi)
        out_ref = o_ref if reverse else acc_ref

        b_ref = bb_ref if reverse else bf_ref
        steep_ref[...] = jnp.zeros((8, gl), F32)

        def increments(c, reverse=reverse, b_ref=b_ref):
            rows = _chunk_rows(c)
            b = b_ref[rows, :]
            btot = b[0:1] if reverse else b[L - 1:L]
            inc = _mm_tn(u_ref[rows, 2 * gl:3 * gl], u_ref[rows, gl:2 * gl] * jnp.exp2(btot - b))
            yield
            s_ref[c] = bd * inc
            dec_ref[c] = _tile_row(jnp.exp2(btot))
            steep_ref[...] = jnp.maximum(steep_ref[...], _tile_row(-btot))

        _chunk_loop(nb * n_chunks, increments, LIGHT_CHUNKS_IN_FLIGHT)
        factorise_all = jnp.max(steep_ref[...]) <= GLA_SAFE_LOG2
        for s in range(nb):
            s_fin = _state_scan(n_chunks, reverse, s_ref, dec_ref,
                                s0_ref[s, 0, d] if has_state else jnp.zeros((gl, gl), F32), s * n_chunks)
            if emit_state:
                _store_head_blocks(st_ref, s, d, s_fin.T)

        def outputs_factorised(c, reverse=reverse, out_ref=out_ref, b_ref=b_ref):
            rows = _chunk_rows(c)
            b = b_ref[rows, :]
            q = u_ref[rows, 0:gl] * scale
            k = u_ref[rows, gl:2 * gl]
            v = u_ref[rows, 2 * gl:3 * gl]
            o_inter = _mm_nt(q * jnp.exp2(b), s_ref[c])
            atts = []
            for i_blk in range(n_sub):
                lo = i_blk * SB
                bi = b[lo:lo + SB]
                ref, keys = (bi[SB - 1:SB], slice(lo, L)) if reverse else (bi[0:1], slice(0, lo + SB))
                qd = q[lo:lo + SB] * jnp.exp2(bi - ref)
                kd = k[keys] * jnp.exp2(ref - b[keys])
                att = _mm_nt(_stack(qd, masks), kd)
                n_keys = keys.stop - keys.start
                qrow = _iota((G * SB, n_keys), 0) & (SB - 1)
                kcol = _iota((G * SB, n_keys), 1)
                ok = (kcol >= qrow) if reverse else (kcol <= qrow + lo)
                atts.append((jnp.where(ok, att, 0.0), keys))
            yield
            outs = [_mm(a, v[keys]) for a, keys in atts]
            yield
            out_ref[rows, :] = o_inter + jnp.concatenate([_unstack(o, masks, SB) for o in outs], axis=0)

        def outputs_direct(c, reverse=reverse, pair_ok=pair_ok, out_ref=out_ref, b_ref=b_ref):
            rows = _chunk_rows(c)
            b = b_ref[rows, :]
            q = u_ref[rows, 0:gl] * scale
            k = u_ref[rows, gl:2 * gl]
            v = u_ref[rows, 2 * gl:3 * gl]
            o_inter = _mm_nt(q * jnp.exp2(b), s_ref[c])
            diag, off = [], []
            for i_blk in range(n_sub):
                lo = i_blk * SB
                qi, ki, bi = q[lo:lo + SB], k[lo:lo + SB], b[lo:lo + SB]
                prods = []
                for i in range(SB):
                    if reverse and i >= half:
                        part = slice(half, SB)
                    elif (not reverse) and i < half:
                        part = slice(0, half)
                    else:
                        part = slice(0, SB)
                    p = (qi[i:i + 1] * ki[part]) * jnp.exp2(bi[i:i + 1] - bi[part])
                    if part.start == half:
                        prods += [zeros_half, p]
                    elif part.stop == half:
                        prods += [p, zeros_half]
                    else:
                        prods.append(p)
                diag.append(_mm(jnp.concatenate(prods, axis=0), bd))
                if reverse and i_blk < n_sub - 1:
                    ref, other = bi[SB - 1:SB], slice(lo + SB, L)
                elif (not reverse) and i_blk > 0:
                    ref, other = bi[0:1], slice(0, lo)
                else:
                    off.append(None)
                    continue
                qd = qi * jnp.exp2(bi - ref)
                kd = k[other] * jnp.exp2(ref - b[other])
                off.append((_mm_nt(_stack(qd, masks), kd), other))
            yield
            off = [None if a is None else _mm(a[0], v[a[1]]) for a in off]
            for i_blk in range(n_sub):
                vi = v[i_blk * SB:(i_blk + 1) * SB]
                weighted = jnp.where(pair_ok, diag[i_blk] * jnp.concatenate([vi] * SB, axis=0), 0.0)
                diag[i_blk] = jnp.dot(pair_sum, weighted.astype(BF16), preferred_element_type=F32)
            yield
            blocks = []
            for i_blk in range(n_sub):
                o_blk = diag[i_blk]
                if off[i_blk] is not None:
                    o_blk = o_blk + _unstack(off[i_blk], masks, SB)
                blocks.append(o_blk)
            out_ref[rows, :] = o_inter + jnp.concatenate(blocks, axis=0)

        @pl.when(factorise_all)
        def _():
            _chunk_loop(nb * n_chunks, outputs_factorised, LIGHT_CHUNKS_IN_FLIGHT)

        @pl.when(jnp.logical_not(factorise_all))
        def _():
            _chunk_loop(nb * n_chunks, outputs_direct)

    def finalize(rows, o):
        ms = _mm_split(o * o, bd_avg)
        yield
        return o * lax.rsqrt(ms + EPS) * nw_ref[...] * _silu(u_ref[rows, 3 * gl:4 * gl])

    _finalize(nb * seq, acc_ref, o_ref, finalize)


def _gla(u_main, log_decay, norm_w, batch, seq, state):
    latent = state is not None
    nb = _seqs_per_step(latent)
    u_spec, _, out_spec = _mixer_specs(MIX_GLA, seq, nb)
    gl = GROUP_LANES
    rows, chunks = nb * seq, nb * seq // CHUNK
    in_specs = [
        pl.BlockSpec((1, gl), lambda b, g: (0, 0)),
        u_spec,
        pl.BlockSpec((rows, gl), lambda b, g: (b, g)),
        pl.BlockSpec((rows, gl), lambda b, g: (b, N_GROUPS + g)),
    ]
    args = [jnp.tile(norm_w.reshape(1, HEAD_DIM), (1, HEADS_PER_GROUP)), u_main, log_decay, log_decay]
    if latent:
        in_specs.append(_state_spec(gl))
        args.append(state)
    out_shape = [jax.ShapeDtypeStruct((batch * seq, GROUP_WIDTH), F32)]
    out_specs = [out_spec]
    if not latent:
        out_shape.append(_head_state_shape(batch))
        out_specs.append(_head_state_spec(nb))
    res = pl.pallas_call(
        functools.partial(_gla_kernel, seq=seq, nb=nb, has_state=latent, emit_state=not latent),
        grid=(batch // nb, N_GROUPS),
        in_specs=in_specs,
        out_specs=out_specs,
        out_shape=out_shape,
        scratch_shapes=[pltpu.VMEM((rows, gl), F32),
                        pltpu.VMEM((chunks, gl, gl), F32), pltpu.VMEM((chunks, 8, gl), F32),
                        pltpu.VMEM((8, gl), F32)],
        compiler_params=_MIXER_PARAMS,
        name="gla",
    )(*args)
    return res[0], (None if latent else res[1])


def _mlstm_kernel(e_ref, u_ref, us_ref, *rest, seq, nb, has_state, emit_state):
    rest = list(rest)
    if has_state:
        c0_ref, n0_ref, m0_ref = rest.pop(0), rest.pop(0), rest.pop(0)
    o_ref = rest.pop(0)
    if emit_state:
        cst_ref, nst_ref, mst_ref = rest.pop(0), rest.pop(0), rest.pop(0)
    acc_ref, f_ref, x_ref, p_ref, s_ref, nu_ref, wp_ref, min_ref, mout_ref = rest

    gl, L, G = GROUP_LANES, CHUNK, HEADS_PER_GROUP
    n_chunks = seq // L
    masks = _head_masks()
    bd = _bd_mask()
    scale = HEAD_DIM ** -0.5
    si = _iota((G * L, L), 0) & (L - 1)
    sj = _iota((G * L, L), 1)
    lane_in_head = _iota((L, gl), 1) & (HEAD_DIM - 1)

    for d in (0, 1):
        reverse = d == 1
        causal = (sj >= si) if reverse else (sj <= si)
        last = 0 if reverse else L - 1
        out_ref = o_ref if reverse else acc_ref

        def gates(rows, d=d):
            r = _mm_split(us_ref[rows, :], e_ref[d, 0])
            yield
            x_ref[rows, :] = r[:, 0:gl]
            f_ref[rows, :] = r[:, gl:2 * gl]
            p_ref[rows, :] = r[:, 2 * gl:3 * gl]

        _row_loop(nb * seq, gates)

        def stabiliser(t, m_row, base, reverse=reverse):
            c = base + _scan_order(t, n_chunks, reverse)
            m_new = _edge_row(f_ref, c, reverse) + jnp.maximum(m_row, _edge_row(p_ref, c, reverse))
            min_ref[c] = _tile_row(m_row)
            mout_ref[c] = _tile_row(m_new)
            return m_new

        m_fin = [lax.fori_loop(0, n_chunks, functools.partial(stabiliser, base=s * n_chunks),
                               m0_ref[s, 0, d] if has_state else jnp.zeros((1, gl), F32)) for s in range(nb)]

        def increments(c, last=last):
            rows = _chunk_rows(c)
            f_tot = f_ref[rows, :][last:last + 1]
            m_in = min_ref[c][0:1]
            m_out = mout_ref[c][0:1]
            kw = u_ref[rows, gl:2 * gl] * jnp.exp(f_tot + x_ref[rows, :] - m_out)
            inc = _mm_tn(kw, u_ref[rows, 2 * gl:3 * gl])
            yield
            s_ref[c] = bd * inc
            nu_ref[c] = _tile_row(jnp.sum(kw, axis=0, keepdims=True))
            wp_ref[c] = _tile_row(jnp.exp(f_tot + m_in - m_out))

        _chunk_loop(nb * n_chunks, increments, LIGHT_CHUNKS_IN_FLIGHT)

        def memory(t, carry, base, reverse=reverse):
            c_mat, n_row = carry
            c = base + _scan_order(t, n_chunks, reverse)
            inc = s_ref[c]
            nu = nu_ref[c][0:1]
            wp = wp_ref[c][0:1]
            s_ref[c] = c_mat
            nu_ref[c] = _tile_row(n_row)
            return wp * c_mat + inc, wp * n_row + nu

        for s in range(nb):
            if has_state:
                init = (c0_ref[s, 0, d], n0_ref[s, 0, d])
            else:
                init = (jnp.zeros((gl, gl), F32), jnp.zeros((1, gl), F32))
            c_fin, n_fin = lax.fori_loop(0, n_chunks, functools.partial(memory, base=s * n_chunks), init)
            if emit_state:
                _store_head_blocks(cst_ref, s, d, c_fin)
                nst_ref[s, 0, d], mst_ref[s, 0, d] = n_fin, m_fin[s]

        def outputs(c, causal=causal, out_ref=out_ref):
            rows = _chunk_rows(c)
            x = x_ref[rows, :]
            m_in = min_ref[c][0:1]
            q = u_ref[rows, 0:gl] * scale
            k = u_ref[rows, gl:2 * gl]
            v = u_ref[rows, 2 * gl:3 * gl]
            m_full = jnp.maximum(m_in, p_ref[rows, :])
            lhs = _split_lanes(-m_full, lane_in_head, 0, 1.0, 2)
            rhs = _split_lanes(x, lane_in_head, 2, 1.0, 0)
            logw = _mm_nt(_stack(lhs, masks), rhs)
            qk = _mm_nt(_stack(q, masks), k)
            inter = _mm(q, s_ref[c])
            q_n = _mm_split(q * nu_ref[c][0:1], bd)
            yield
            s = qk * jnp.exp(jnp.where(causal, logw, NEG))
            r = _mm(s, v)
            row_sum = jnp.sum(s, axis=1, keepdims=True)
            yield
            w_inter = jnp.exp(m_in - m_full)
            num = _unstack(r, masks, L) + w_inter * inter
            den = _unstack(jnp.broadcast_to(row_sum, (G * L, gl)), masks, L) + w_inter * q_n
            out_ref[rows, :] = num / jnp.maximum(jnp.abs(den), jnp.exp(-(f_ref[rows, :] + m_full)))

        _chunk_loop(nb * n_chunks, outputs)

    def finalize(rows, o):
        return o * jax.nn.sigmoid(u_ref[rows, 3 * gl:4 * gl])

    _finalize(nb * seq, acc_ref, o_ref, finalize)


def _gate_expanders(offset):
    e = np.zeros((2, N_GROUPS, SMALL_LANES, GROUP_LANES), np.float32)
    for d in range(2):
        for g in range(N_GROUPS):
            for h in range(HEADS_PER_GROUP):
                src = offset + d * N_HEADS + g * HEADS_PER_GROUP + h
                e[d, g, src, h * HEAD_DIM:(h + 1) * HEAD_DIM] = 1.0
    return jnp.asarray(e)


def _gate_bias_row(b_i, b_f):
    row = jnp.zeros((1, SMALL_LANES), F32)
    row = row.at[0, MI_OFF:MI_OFF + 2 * N_HEADS].set(b_i.astype(F32).reshape(-1))
    return row.at[0, MF_OFF:MF_OFF + 2 * N_HEADS].set(b_f.astype(F32).reshape(-1))


def _mlstm(u_main, u_small, batch, seq, state):
    latent = state is not None
    nb = _seqs_per_step(latent)
    u_spec, small_spec, out_spec = _mixer_specs(MIX_MLSTM, seq, nb)
    gl = GROUP_LANES
    rows, chunks = nb * seq, nb * seq // CHUNK
    expand = jnp.concatenate([_gate_expanders(MI_OFF), _gate_expanders(MF_OFF), _gate_expanders(MP_OFF)], axis=-1)
    in_specs = [pl.BlockSpec((2, 1, SMALL_LANES, 3 * gl), lambda b, g: (0, g, 0, 0)), u_spec, small_spec]
    args = [expand, u_main, u_small]
    if latent:
        in_specs += [_state_spec(gl), _state_spec(1), _state_spec(1)]
        args += list(state)
    out_shape = [jax.ShapeDtypeStruct((batch * seq, GROUP_WIDTH), F32)]
    out_specs = [out_spec]
    if not latent:
        out_shape += [_head_state_shape(batch),
                      jax.ShapeDtypeStruct((batch, N_GROUPS, 2, 1, gl), F32),
                      jax.ShapeDtypeStruct((batch, N_GROUPS, 2, 1, gl), F32)]
        out_specs += [_head_state_spec(nb), _state_spec(1, nb), _state_spec(1, nb)]
    res = pl.pallas_call(
        functools.partial(_mlstm_kernel, seq=seq, nb=nb, has_state=latent, emit_state=not latent),
        grid=(batch // nb, N_GROUPS),
        in_specs=in_specs,
        out_specs=out_specs,
        out_shape=out_shape,
        scratch_shapes=[pltpu.VMEM((rows, gl), F32)] * 4 + [pltpu.VMEM((chunks, gl, gl), F32)]
        + [pltpu.VMEM((chunks, 8, gl), F32)] * 4,
        compiler_params=_MIXER_PARAMS,
        name="mlstm",
    )(*args)
    return res[0], (None if latent else tuple(res[1:]))


def _na_ctx_kernel(u_ref, o_ref, ko_ref, vo_ref, *, seq, nb):
    gl = GROUP_LANES
    masks = _head_masks()

    def sequence(s):
        rows = slice(s * seq, (s + 1) * seq)
        q = u_ref[rows, 0:gl] * (HEAD_DIM ** -0.5)
        k = u_ref[rows, gl:2 * gl]
        v = u_ref[rows, 2 * gl:3 * gl]
        sc = _mm_nt(_stack(q, masks), k)
        yield
        p = jnp.exp(sc - jnp.max(sc, axis=1, keepdims=True))
        o = _mm(p, v)
        yield
        o_ref[rows, :] = _unstack(o / jnp.sum(p, axis=1, keepdims=True), masks, seq)
        for h in range(HEADS_PER_GROUP):
            ko_ref[s, h] = k[:, h * HEAD_DIM:(h + 1) * HEAD_DIM]
            vo_ref[s, h] = v[:, h * HEAD_DIM:(h + 1) * HEAD_DIM]

    _interleave([sequence(s) for s in range(nb)])


def _na_context(u_main, batch, seq):
    nb = _seqs_per_step(False)
    u_spec, _, out_spec = _mixer_specs(MIX_NA, seq, nb)
    kv_shape = jax.ShapeDtypeStruct((batch, N_HEADS, seq, HEAD_DIM), F32)
    kv_spec = pl.BlockSpec((nb, HEADS_PER_GROUP, seq, HEAD_DIM), lambda b, g: (b, g, 0, 0))
    return pl.pallas_call(
        functools.partial(_na_ctx_kernel, seq=seq, nb=nb),
        grid=(batch // nb, N_GROUPS),
        in_specs=[u_spec],
        out_specs=[out_spec, kv_spec, kv_spec],
        out_shape=[jax.ShapeDtypeStruct((batch * seq, GROUP_WIDTH), F32), kv_shape, kv_shape],
        compiler_params=_MIXER_PARAMS,
        name="na_context",
    )(u_main)


def _na_key_col0(qb):
    return int(np.clip(qb * NA_QC - NA_KW // 2, 0, GRID_W - NA_KCB))


def _na_lat_kernel(tab_ref, kc_ref, vc_ref, u_ref, o_ref, *, seq):
    gl = GROUP_LANES
    grid_rows = seq // GRID_W
    masks = _head_masks()
    scale = HEAD_DIM ** -0.5
    kc = kc_ref[0, 0]
    vc = vc_ref[0, 0]

    def query_block(r, qb):
        ks = jnp.clip(r - NA_KH // 2, 0, grid_rows - NA_KH)
        q_rows = pl.ds(pl.multiple_of(r * GRID_W, GRID_W) + qb * NA_QC, NA_QC)
        k_rows = [pl.ds(pl.multiple_of((ks + j) * GRID_W, GRID_W) + _na_key_col0(qb), NA_KCB) for j in range(NA_KH)]
        qs = _stack(u_ref[q_rows, 0:gl] * scale, masks)
        s_loc = _mm_nt(qs, jnp.concatenate([u_ref[rows, gl:2 * gl] for rows in k_rows], axis=0))
        s_ctx = _mm_nt(qs, kc)
        yield
        s_loc = s_loc + tab_ref[0, qb, ks - r + NA_KH - 1]
        m = jnp.maximum(jnp.max(s_loc, axis=1, keepdims=True), jnp.max(s_ctx, axis=1, keepdims=True))
        p_loc = jnp.exp(s_loc - m)
        p_ctx = jnp.exp(s_ctx - m)
        den = jnp.sum(p_loc, axis=1, keepdims=True) + jnp.sum(p_ctx, axis=1, keepdims=True)
        o = _mm(p_loc, jnp.concatenate([u_ref[rows, 2 * gl:3 * gl] for rows in k_rows], axis=0)) + _mm(p_ctx, vc)
        yield
        o_ref[q_rows, :] = _unstack(o / den, masks, NA_QC)

    def body(t, carry):
        _interleave([query_block(t * NA_ROWS_IN_FLIGHT + j, qb)
                     for j in range(NA_ROWS_IN_FLIGHT) for qb in range(GRID_W // NA_QC)])
        return carry

    lax.fori_loop(0, grid_rows // NA_ROWS_IN_FLIGHT, body, 0)


def _na_bias_table(rpb):
    c = np.arange(GRID_W)
    rel = c[None, :] - c[:, None]
    cs = np.clip(c - NA_KW // 2, 0, GRID_W - NA_KW)
    valid = (c[None, :] >= cs[:, None]) & (c[None, :] < cs[:, None] + NA_KW)
    n_rel = 2 * NA_KW - 1
    pick = (np.arange(n_rel)[:, None, None] == (np.clip(rel, -(NA_KW - 1), NA_KW - 1) + NA_KW - 1)[None])
    bmat = jnp.einsum("hdk,kcz->hdcz", rpb.astype(F32), jnp.asarray(pick, F32), precision=HI)
    bmat = jnp.where(valid[None, None], bmat, NEG)
    blocks = []
    for qb in range(GRID_W // NA_QC):
        sub = bmat[:, :, qb * NA_QC:(qb + 1) * NA_QC, _na_key_col0(qb):_na_key_col0(qb) + NA_KCB]
        tabs = jnp.stack([sub[:, s:s + NA_KH] for s in range(NA_KH)], axis=1)
        tabs = tabs.transpose(0, 1, 3, 2, 4).reshape(N_GROUPS, HEADS_PER_GROUP, NA_KH, NA_QC, NA_KH * NA_KCB)
        blocks.append(tabs.transpose(0, 2, 1, 3, 4).reshape(N_GROUPS, NA_KH, HEADS_PER_GROUP * NA_QC,
                                                             NA_KH * NA_KCB))
    return jnp.stack(blocks, axis=1)


def _heads_to_lanes(t):
    b, _, s, _ = t.shape
    t = t.reshape(b, N_GROUPS, HEADS_PER_GROUP, s, HEAD_DIM).transpose(0, 1, 3, 2, 4)
    return t.reshape(b, N_GROUPS, s, GROUP_LANES)


def _na_latent(u_main, rpb, k_ctx, v_ctx, batch, seq):
    u_spec, _, out_spec = _mixer_specs(MIX_NA, seq, 1)
    gl = GROUP_LANES
    past = k_ctx.shape[2]
    tab = _na_bias_table(rpb)
    ctx_spec = pl.BlockSpec((1, 1, past, gl), lambda b, g: (b, g, 0, 0))
    return pl.pallas_call(
        functools.partial(_na_lat_kernel, seq=seq),
        grid=(batch, N_GROUPS),
        in_specs=[pl.BlockSpec((1,) + tab.shape[1:], lambda b, g: (g, 0, 0, 0, 0)), ctx_spec, ctx_spec, u_spec],
        out_specs=out_spec,
        out_shape=jax.ShapeDtypeStruct((batch * seq, GROUP_WIDTH), F32),
        compiler_params=_MIXER_PARAMS,
        name="na_latent",
    )(tab, _heads_to_lanes(k_ctx.astype(F32)), _heads_to_lanes(v_ctx.astype(F32)), u_main)


def _to_block_diag(s):
    b = s.shape[0]
    G = HEADS_PER_GROUP
    s = s.astype(F32).reshape(b, 2, N_GROUPS, G, HEAD_DIM, 1, HEAD_DIM)
    eye = jnp.eye(G, dtype=F32).reshape(1, 1, 1, G, 1, G, 1)
    return (s * eye).reshape(b, 2, N_GROUPS, GROUP_LANES, GROUP_LANES).transpose(0, 2, 1, 3, 4)


def _rows_to_lanes(v):
    b = v.shape[0]
    return v.astype(F32).reshape(b, 2, N_GROUPS, 1, GROUP_LANES).transpose(0, 2, 1, 3, 4)


def _lanes_to_rows(v):
    b = v.shape[0]
    return v.transpose(0, 2, 1, 3, 4).reshape(b, 2, N_HEADS, HEAD_DIM)


def _swap_last(s):
    return jnp.swapaxes(s, -1, -2)


def _in_weight(w_in):
    gl = GROUP_LANES
    w_in = w_in.astype(BF16)
    cols = []
    for names in _MIXER_COLS:
        for g in range(N_GROUPS):
            for name in names:
                cols.append(w_in[..., _OFF[name] + g * gl:_OFF[name] + (g + 1) * gl])
    cols.append(w_in[..., _OFF["ga"]:_OFF["ga"] + 2 * GLA_RANK])
    cols.append(w_in[..., _OFF["mi"]:_OFF["mi"] + 2 * N_HEADS])
    cols.append(w_in[..., _OFF["mf"]:_OFF["mf"] + 2 * N_HEADS])
    cols.append(jnp.zeros((DEPTH, D_MODEL, SMALL_LANES - MF_OFF - 2 * N_HEADS), BF16))
    return jnp.concatenate(cols, axis=-1)


def _gla_gate_weight(w_a2_l):
    out = jnp.zeros((SMALL_LANES, 2 * GROUP_WIDTH), F32)
    for d in range(2):
        out = out.at[d * GLA_RANK:(d + 1) * GLA_RANK, d * GROUP_WIDTH:(d + 1) * GROUP_WIDTH].set(
            w_a2_l[d].astype(F32))
    return out.astype(BF16)


def _layer(x, mod, p, batch, seq, ctx):
    latent = ctx is not None
    l = p["layer"]
    x, u_main, u_small, log_decay = _ffn_inproj(x, mod[:, 0:6], p["norm_w"][0:3], p["gate_bias"], p["gla_wa"],
                                                p["gla_ba"], p["w1"], p["w3"], p["w2"], p["w_in"], l, seq)
    mix_a, st_gla = _gla(u_main, log_decay, p["gla_norm_w"], batch, seq,
                         _swap_last(_to_block_diag(ctx["gla"])) if latent else None)
    if latent:
        mix_b = _na_latent(u_main, p["na_rpb"], ctx["na_k"], ctx["na_v"], batch, seq)
    else:
        mix_b, na_k, na_v = _na_context(u_main, batch, seq)
    mix_c, st_ml = _mlstm(u_main, u_small, batch, seq,
                          (_to_block_diag(ctx["mC"]), _rows_to_lanes(ctx["mn"]),
                           _rows_to_lanes(jnp.repeat(ctx["mm"][..., None], HEAD_DIM, axis=-1))) if latent else None)
    mix_d, st_ret = _retention(u_main, p["ret_decay"], p["ret_norm_w"], batch, seq,
                               _to_block_diag(ctx["ret"]) if latent else None)
    x = _outproj_ffn(x, (mix_a, mix_b, mix_c, mix_d), mod[:, 3:9], p["norm_w"][3:6], p["w_out"],
                     p["w1"], p["w3"], p["w2"], l, seq)
    if latent:
        return x, None
    new = {
        "na_k": na_k, "na_v": na_v, "gla": st_gla, "mC": st_ml[0],
        "mn": _lanes_to_rows(st_ml[1]), "mm": _lanes_to_rows(st_ml[2])[..., 0], "ret": st_ret,
    }
    return x, new


def kernel(x_prompt, x_sample, cache_na_k, cache_na_v, state_gla, state_mlstm_C, state_mlstm_n, state_mlstm_m,
           state_ret, c, c_ctx, w_mod, b_mod, norm_w, ffn_w1, ffn_w3, ffn_w2, w_in, w_out, gla_w_a2, gla_b_a,
           gla_norm_w, na_rpb, mlstm_b_i, mlstm_b_f, ret_decay, ret_norm_w):
    batch, seq, _ = x_prompt.shape
    dec_batch, dec_seq, _ = x_sample.shape

    cvec = jnp.zeros((8, D_MODEL), F32).at[0].set(c_ctx).at[1:1 + dec_batch].set(c)
    mods = _modulation(cvec, w_mod, b_mod).reshape(DEPTH, 8, N_MOD, D_MODEL)

    w1, w3, w2 = ffn_w1.astype(BF16), ffn_w3.astype(BF16), ffn_w2.astype(BF16)
    w_in_all, w_out_all = _in_weight(w_in), w_out.astype(BF16)
    params = []
    for l in range(DEPTH):
        params.append({
            "layer": l, "norm_w": norm_w[l],
            "w1": w1, "w3": w3, "w2": w2, "w_in": w_in_all, "w_out": w_out_all,
            "gla_wa": _gla_gate_weight(gla_w_a2[l]), "gla_ba": gla_b_a[l].astype(F32).reshape(1, 2 * GROUP_WIDTH),
            "gla_norm_w": gla_norm_w[l],
            "na_rpb": na_rpb[l], "gate_bias": _gate_bias_row(mlstm_b_i[l], mlstm_b_f[l]),
            "ret_decay": ret_decay[l], "ret_norm_w": ret_norm_w[l],
        })

    xp = x_prompt.reshape(batch * seq, D_MODEL)
    states = []
    for l in range(DEPTH):
        xp, st = _layer(xp, mods[l, 0:1], params[l], batch, seq, None)
        states.append(st)

    xs = x_sample.reshape(dec_batch * dec_seq, D_MODEL)
    for l in range(DEPTH):
        ctx = {"na_k": cache_na_k[:, l], "na_v": cache_na_v[:, l], "gla": state_gla[:, l],
               "mC": state_mlstm_C[:, l], "mn": state_mlstm_n[:, l], "mm": state_mlstm_m[:, l],
               "ret": state_ret[:, l]}
        xs, _ = _layer(xs, mods[l, 1:1 + dec_batch], params[l], dec_batch, dec_seq, ctx)

    stack = lambda name: jnp.stack([s[name] for s in states], axis=1)
    return (xp.reshape(batch, seq, D_MODEL), xs.reshape(dec_batch, dec_seq, D_MODEL),
            stack("na_k"), stack("na_v"), stack("gla"), stack("mC"), stack("mn"), stack("mm"), stack("ret"))
```

```python
import functools

import numpy as np
import jax
import jax.numpy as jnp
from jax import lax
from jax.experimental import pallas as pl
from jax.experimental.pallas import tpu as pltpu

D_MODEL = 1024
DEPTH = 2
HEAD_DIM = 64
N_HEADS = 4
GROUP_WIDTH = N_HEADS * HEAD_DIM
N_MOD = 9
GLA_RANK = 16
GLA_TAU = 16.0
CHUNK = 64
GRID_W = 64
NA_KH = 8
NA_KW = 16
NA_QC = 16
NA_KCB = NA_QC + NA_KW
ROPE_BASE = 10000.0
EPS = 1e-6
D_FF = 2816

HEADS_PER_GROUP = 2
GROUP_LANES = HEADS_PER_GROUP * HEAD_DIM
N_GROUPS = N_HEADS // HEADS_PER_GROUP
GLA_SUB = 16
GLA_SAFE_LOG2 = 64.0
ROW_TILE_IN = 256
ROW_TILE_OUT = 512
ROW_PART = 128
CTX_SEQS_PER_STEP = 4
PRE_ROWS = 128
ROWS_IN_FLIGHT = 8
NA_ROWS_IN_FLIGHT = 8
LOG2E = 1.4426950408889634
CHUNKS_IN_FLIGHT = 8
LIGHT_CHUNKS_IN_FLIGHT = 16
SMALL_LANES = 128
MI_OFF = 2 * GLA_RANK
MF_OFF = MI_OFF + 2 * N_HEADS
MP_OFF = MF_OFF + 2 * N_HEADS
NEG = -1e30
VMEM_LIMIT = 56 * 1024 * 1024

F32 = jnp.float32
BF16 = jnp.bfloat16
HI = lax.Precision.HIGHEST

_OFF = {}
_o = 0
for _name, _size in (("gq", 256), ("gk", 256), ("gv", 256), ("gg", 256), ("ga", 32), ("nq", 256), ("nk", 256),
                     ("nv", 256), ("mq", 256), ("mk", 256), ("mv", 256), ("mo", 256), ("mi", 8), ("mf", 8),
                     ("rq", 256), ("rk", 256), ("rv", 256), ("rg", 256)):
    _OFF[_name] = _o
    _o += _size
N_IN = _o
_MIXER_COLS = (("gq", "gk", "gv", "gg"), ("mq", "mk", "mv", "mo"), ("rq", "rk", "rv", "rg"), ("nq", "nk", "nv"))
MIX_GLA, MIX_MLSTM, MIX_RET, MIX_NA = range(4)
MAIN_COLS = sum(len(names) for names in _MIXER_COLS) * N_GROUPS * GROUP_LANES


def _mm(a, b):
    return jnp.dot(a.astype(BF16), b.astype(BF16), preferred_element_type=F32)


def _mm_nt(a, b):
    return lax.dot_general(a.astype(BF16), b.astype(BF16), (((1,), (1,)), ((), ())), preferred_element_type=F32)


def _mm_tn(a, b):
    return lax.dot_general(a.astype(BF16), b.astype(BF16), (((0,), (0,)), ((), ())), preferred_element_type=F32)


def _split3(x):
    hi = x.astype(BF16)
    rest = x - hi.astype(F32)
    mid = rest.astype(BF16)
    lo = (rest - mid.astype(F32)).astype(BF16)
    return jnp.concatenate([hi, mid, lo], axis=1)


def _dot_split(pieces, w):
    wb = w.astype(BF16)
    return jnp.dot(pieces, jnp.concatenate([wb, wb, wb], axis=0), preferred_element_type=F32)


def _mm_split(x, w):
    return _dot_split(_split3(x), w)


def _log_sigmoid(x):
    return jnp.minimum(x, 0.0) - jnp.log1p(jnp.exp(-jnp.abs(x)))


def _silu(x):
    return x * jax.nn.sigmoid(x)


def _iota(shape, dim):
    return lax.broadcasted_iota(jnp.int32, shape, dim)


def _head_of(idx):
    return lax.shift_right_logical(idx, 6)


def _head_masks():
    lane = _iota((1, GROUP_LANES), 1)
    return [(_head_of(lane) == h).astype(F32) for h in range(HEADS_PER_GROUP)]


def _bd_mask():
    n = GROUP_LANES
    return (_head_of(_iota((n, n), 0)) == _head_of(_iota((n, n), 1))).astype(F32)


def _stack(x, masks):
    return jnp.concatenate([x * m for m in masks], axis=0)


def _unstack(r, masks, n):
    out = r[0:n] * masks[0]
    for h in range(1, len(masks)):
        out = out + r[h * n:(h + 1) * n] * masks[h]
    return out


def _rmsnorm(x, w):
    return x * lax.rsqrt(jnp.mean(x * x, axis=-1, keepdims=True) + EPS) * w


def _split_lanes(x, lane_in_head, at, fill, fill_at):
    hi = x.astype(BF16).astype(F32)
    out = jnp.where(lane_in_head == at, hi, jnp.where(lane_in_head == at + 1, x - hi, 0.0))
    is_fill = (lane_in_head >= fill_at) & (lane_in_head < fill_at + 2)
    return jnp.where(is_fill, fill, out)


def _mod_kernel(c_ref, w_ref, b_ref, o_ref):
    s = _silu(c_ref[...])
    o_ref[0] = _mm(s, w_ref[0]) + b_ref[0]


def _modulation(cvec, w_mod, b_mod):
    n = N_MOD * D_MODEL
    tn = n // 8
    return pl.pallas_call(
        _mod_kernel,
        grid=(DEPTH, n // tn),
        in_specs=[
            pl.BlockSpec((8, D_MODEL), lambda l, j: (0, 0)),
            pl.BlockSpec((1, D_MODEL, tn), lambda l, j: (l, 0, j)),
            pl.BlockSpec((1, 1, tn), lambda l, j: (l, 0, j)),
        ],
        out_specs=pl.BlockSpec((1, 8, tn), lambda l, j: (l, 0, j)),
        out_shape=jax.ShapeDtypeStruct((DEPTH, 8, n), F32),
        compiler_params=pltpu.CompilerParams(dimension_semantics=("parallel", "parallel"),
                                             vmem_limit_bytes=VMEM_LIMIT),
        name="modulation",
    )(cvec, w_mod, b_mod.reshape(DEPTH, 1, n))


def _half_ffn(x, mod, g, w1_ref, w3_ref, w2_ref):
    h = _rmsnorm(x, g[0:1]) * (1.0 + mod[1:2]) + mod[0:1]
    hb = h.astype(BF16)
    h1 = jnp.dot(hb, w1_ref[...], preferred_element_type=F32)
    h3 = jnp.dot(hb, w3_ref[...], preferred_element_type=F32)
    yield
    y = jnp.dot((_silu(h1) * h3).astype(BF16), w2_ref[...], preferred_element_type=F32)
    yield
    return x + 0.5 * mod[2:3] * _rmsnorm(y, g[1:2])


def _row_parts(tm):
    return [slice(j, j + ROW_PART) for j in range(0, tm, ROW_PART)]


def _mlstm_gate_scans(us, gate_bias):
    lane = _iota(us.shape, 1)
    field = 2 * N_HEADS
    fwd = (lane & (field - 1)) < N_HEADS
    pre = us + gate_bias
    lf = jnp.where((lane >= MF_OFF) & (lane < MF_OFF + field), _log_sigmoid(pre), 0.0)
    f = jnp.where(fwd, _chunk_scan(lf, False, jnp.add), _chunk_scan(lf, True, jnp.add))
    x = pre - pltpu.roll(f, SMALL_LANES - (MF_OFF - MI_OFF), 1)
    p = jnp.where(fwd, _chunk_scan(x, False, jnp.maximum), _chunk_scan(x, True, jnp.maximum))
    p = pltpu.roll(p, MP_OFF - MI_OFF, 1)
    return jnp.where(lane < MI_OFF, us,
                     jnp.where(lane < MF_OFF, x, jnp.where(lane < MP_OFF, f, jnp.where(lane < MP_OFF + field, p, 0.0))))


def _gla_gate(u_small, wa, ba):
    return _mm(u_small, wa) + ba


def _gla_log_decay(a_pre):
    la = _log_sigmoid(a_pre) * (LOG2E / GLA_TAU)
    return jnp.concatenate([_chunk_scan(la[:, :GROUP_WIDTH], False, jnp.add),
                            _chunk_scan(la[:, GROUP_WIDTH:], True, jnp.add)], axis=1)


def _ffn_inproj_kernel(x_ref, mod_ref, g_ref, gb_ref, wa_ref, ba_ref, w1_ref, w3_ref, w2_ref, win_ref,
                       x_out_ref, um_ref, us_ref, ub_ref):
    mod = mod_ref[0]
    g = g_ref[...]

    def part(rows):
        x = yield from _half_ffn(x_ref[rows, :], mod[0:3], g[0:2], w1_ref, w3_ref, w2_ref)
        x_out_ref[rows, :] = x
        hb = (_rmsnorm(x, g[2:3]) * (1.0 + mod[4:5]) + mod[3:4]).astype(BF16)
        u_small = jnp.dot(hb, win_ref[:, MAIN_COLS:], preferred_element_type=F32)
        u_main = jnp.dot(hb, win_ref[:, :MAIN_COLS], preferred_element_type=F32)
        us_ref[rows, :] = _mlstm_gate_scans(u_small, gb_ref[...])
        ub_ref[rows, :] = _gla_log_decay(_gla_gate(u_small, wa_ref[...], ba_ref[...]))
        yield
        um_ref[rows, :] = u_main

    _interleave([part(rows) for rows in _row_parts(x_ref.shape[0])])


def _outproj_ffn_kernel(x_ref, ma_ref, mb_ref, mc_ref, md_ref, mod_ref, g_ref, wo_ref, w1_ref, w3_ref, w2_ref,
                        o_ref):
    mod = mod_ref[0]
    g = g_ref[...]

    def part(rows):
        mix = jnp.concatenate([ma_ref[rows, :], mb_ref[rows, :], mc_ref[rows, :], md_ref[rows, :]], axis=-1)
        y = jnp.dot(mix.astype(BF16), wo_ref[...], preferred_element_type=F32)
        yield
        x = x_ref[rows, :] + mod[2:3] * _rmsnorm(y, g[0:1])
        o_ref[rows, :] = yield from _half_ffn(x, mod[3:6], g[1:3], w1_ref, w3_ref, w2_ref)

    _interleave([part(rows) for rows in _row_parts(x_ref.shape[0])])


def _row_tile(n_mod, rows_per_batch, tile):
    return tile if n_mod == 1 else min(tile, rows_per_batch)


def _mod_index(n_mod, rows_per_batch, tm):
    if n_mod == 1:
        return lambda i: (0, 0, 0)
    per = rows_per_batch // tm
    return lambda i: (i // per, 0, 0)


_ROW_PARAMS = pltpu.CompilerParams(dimension_semantics=("parallel",), vmem_limit_bytes=VMEM_LIMIT)


def _ffn_weight_specs(layer, half):
    pick = lambda i: (layer, half, 0, 0)
    return [pl.BlockSpec((None, None, D_MODEL, D_FF), pick, pipeline_mode=pl.Buffered(1)),
            pl.BlockSpec((None, None, D_MODEL, D_FF), pick, pipeline_mode=pl.Buffered(1)),
            pl.BlockSpec((None, None, D_FF, D_MODEL), pick, pipeline_mode=pl.Buffered(1))]


def _ffn_inproj(x, mod6, g3, gate_bias, gla_wa, gla_ba, w1, w3, w2, w_big, layer, rows_per_batch):
    rows = x.shape[0]
    tm = _row_tile(mod6.shape[0], rows_per_batch, ROW_TILE_IN)
    row_spec = lambda width: pl.BlockSpec((tm, width), lambda i: (i, 0))
    const = lambda i: (0, 0)
    widths = (D_MODEL, MAIN_COLS, SMALL_LANES, 2 * GROUP_WIDTH)
    return pl.pallas_call(
        _ffn_inproj_kernel,
        grid=(rows // tm,),
        in_specs=[
            row_spec(D_MODEL),
            pl.BlockSpec((1, 6, D_MODEL), _mod_index(mod6.shape[0], rows_per_batch, tm)),
            pl.BlockSpec((3, D_MODEL), const),
            pl.BlockSpec((1, SMALL_LANES), const),
            pl.BlockSpec((SMALL_LANES, 2 * GROUP_WIDTH), const),
            pl.BlockSpec((1, 2 * GROUP_WIDTH), const),
            *_ffn_weight_specs(layer, 0),
            pl.BlockSpec((None, D_MODEL, MAIN_COLS + SMALL_LANES), lambda i: (layer, 0, 0),
                         pipeline_mode=pl.Buffered(1)),
        ],
        out_specs=[row_spec(w) for w in widths],
        out_shape=[jax.ShapeDtypeStruct((rows, w), F32) for w in widths],
        compiler_params=_ROW_PARAMS,
        name="ffn_inproj",
    )(x, mod6, g3, gate_bias, gla_wa, gla_ba, w1, w3, w2, w_big)


def _outproj_ffn(x, mixes, mod6, g3, w_out, w1, w3, w2, layer, rows_per_batch):
    rows = x.shape[0]
    tm = _row_tile(mod6.shape[0], rows_per_batch, ROW_TILE_OUT)
    row_spec = lambda width: pl.BlockSpec((tm, width), lambda i: (i, 0))
    return pl.pallas_call(
        _outproj_ffn_kernel,
        grid=(rows // tm,),
        in_specs=[
            row_spec(D_MODEL), *[row_spec(GROUP_WIDTH)] * 4,
            pl.BlockSpec((1, 6, D_MODEL), _mod_index(mod6.shape[0], rows_per_batch, tm)),
            pl.BlockSpec((3, D_MODEL), lambda i: (0, 0)),
            pl.BlockSpec((None, D_MODEL, D_MODEL), lambda i: (layer, 0, 0), pipeline_mode=pl.Buffered(1)),
            *_ffn_weight_specs(layer, 1),
        ],
        out_specs=row_spec(D_MODEL),
        out_shape=jax.ShapeDtypeStruct((rows, D_MODEL), F32),
        compiler_params=_ROW_PARAMS,
        name="outproj_ffn",
    )(x, *mixes, mod6, g3, w_out, w1, w3, w2)


def _seqs_per_step(latent):
    return 1 if latent else CTX_SEQS_PER_STEP


def _mixer_specs(mixer, seq, nb):
    gl = GROUP_LANES
    n_blocks = len(_MIXER_COLS[mixer])
    first = sum(len(names) for names in _MIXER_COLS[:mixer]) * N_GROUPS
    assert first % n_blocks == 0
    u_spec = pl.BlockSpec((nb * seq, n_blocks * gl), lambda b, g: (b, first // n_blocks + g))
    small_spec = pl.BlockSpec((nb * seq, SMALL_LANES), lambda b, g: (b, 0))
    out_spec = pl.BlockSpec((nb * seq, gl), lambda b, g: (b, g))
    return u_spec, small_spec, out_spec


def _state_spec(rows, nb=1):
    return pl.BlockSpec((nb, 1, 2, rows, GROUP_LANES), lambda b, g: (b, g, 0, 0, 0))


def _head_state_shape(batch):
    return jax.ShapeDtypeStruct((batch, 2, N_HEADS, HEAD_DIM, HEAD_DIM), F32)


def _head_state_spec(nb):
    return pl.BlockSpec((nb, 2, HEADS_PER_GROUP, HEAD_DIM, HEAD_DIM), lambda b, g: (b, 0, g, 0, 0))


def _store_head_blocks(ref, s, d, mat):
    for h in range(HEADS_PER_GROUP):
        lo = h * HEAD_DIM
        ref[s, d, h] = mat[lo:lo + HEAD_DIM, lo:lo + HEAD_DIM]


_MIXER_PARAMS = pltpu.CompilerParams(dimension_semantics=("parallel", "parallel"), vmem_limit_bytes=VMEM_LIMIT)


def _chunk_rows(c):
    return pl.ds(pl.multiple_of(c * CHUNK, CHUNK), CHUNK)


def _scan_order(t, n_chunks, reverse):
    return (n_chunks - 1 - t) if reverse else t


def _tile_row(row):
    return jnp.broadcast_to(row, (8, row.shape[1]))


def _edge_row(ref, c, reverse):
    if reverse:
        return ref[pl.ds(pl.multiple_of(c * CHUNK, 8), 8), :][0:1]
    return ref[pl.ds(pl.multiple_of(c * CHUNK + CHUNK - 8, 8), 8), :][7:8]


def _chunk_scan(x, reverse, op):
    n = x.shape[0]
    row = _iota(x.shape, 0) & (CHUNK - 1)
    s = 1
    while s < CHUNK:
        if reverse:
            shifted = pltpu.roll(x, n - s, 0)
            ok = row < CHUNK - s
        else:
            shifted = pltpu.roll(x, s, 0)
            ok = row >= s
        x = jnp.where(ok, op(x, shifted), x)
        s *= 2
    return x


def _row_loop(seq, fn):
    blk = min(seq, PRE_ROWS)
    k = min(seq // blk, ROWS_IN_FLIGHT)

    def body(i, c):
        _interleave([fn(pl.ds(pl.multiple_of((i * k + j) * blk, blk), blk)) for j in range(k)])
        return c

    if seq == blk * k:
        _interleave([fn(pl.ds(j * blk, blk)) for j in range(k)])
    else:
        lax.fori_loop(0, seq // (blk * k), body, 0)


def _chunk_loop(n_chunks, make_stream, in_flight=None):
    k = min(n_chunks, in_flight or CHUNKS_IN_FLIGHT)

    def body(t, carry):
        _interleave([make_stream(t * k + j) for j in range(k)])
        return carry

    if n_chunks == k:
        body(0, 0)
    else:
        lax.fori_loop(0, n_chunks // k, body, 0)


def _interleave(streams):
    results = [None] * len(streams)
    live = []
    for i, s in enumerate(streams):
        if hasattr(s, "send"):
            live.append(i)
        else:
            results[i] = s
    while live:
        for i in list(live):
            try:
                next(streams[i])
            except StopIteration as done:
                results[i] = done.value
                live.remove(i)
    return results


def _state_scan(n_chunks, reverse, s_ref, decay_ref, init, base):
    def body(t, s):
        c = base + _scan_order(t, n_chunks, reverse)
        inc = s_ref[c]
        s_ref[c] = s
        return decay_ref[c][0:1] * s + inc

    return lax.fori_loop(0, n_chunks, body, init)


def _finalize(seq, acc_ref, o_ref, fn):
    def blk(rows):
        res = fn(rows, acc_ref[rows, :] + o_ref[rows, :])
        if hasattr(res, "send"):
            res = yield from res
        o_ref[rows, :] = res

    _row_loop(seq, blk)


def _rope(x, cos, sin_signed):
    lane = _iota(x.shape, 1)
    first = (lane & 31) < 16
    swapped = jnp.where(first, pltpu.roll(x, GROUP_LANES - 16, 1), pltpu.roll(x, 16, 1))
    return x * cos + swapped * sin_signed


def _ret_kernel(dec_ref, nw_ref, u_ref, *rest, seq, nb, rope, has_state, emit_state):
    rest = list(rest)
    cos_ref, sin_ref = (rest.pop(0), rest.pop(0)) if rope else (None, None)
    s0_ref = rest.pop(0) if has_state else None
    o_ref = rest.pop(0)
    st_ref = rest.pop(0) if emit_state else None
    acc_ref, q_ref, k_ref, s_ref, cdec_ref = rest

    gl, L, G = GROUP_LANES, CHUNK, HEADS_PER_GROUP
    n_chunks = seq // L
    g = pl.program_id(1)
    masks = _head_masks()
    bd = _bd_mask()
    bd_avg = bd * (1.0 / HEAD_DIM)
    scale = HEAD_DIM ** -0.5
    ri = _iota((L, gl), 0).astype(F32)
    si = _iota((G * L, L), 0)
    sj = _iota((G * L, L), 1)
    i_idx = si & (L - 1)
    hrow = _head_of(_iota((G * L, 1), 0))

    def prepare(rows):
        q = u_ref[rows, 0:gl]
        k = u_ref[rows, gl:2 * gl]
        if rope:
            cos = cos_ref[rows, :]
            sin = sin_ref[rows, :]
            q = _rope(q, cos, sin)
            k = _rope(k, cos, sin)
        q_ref[rows, :] = q * scale
        k_ref[rows, :] = k

    _row_loop(nb * seq, prepare)

    for d in (0, 1):
        reverse = d == 1
        raw_row = masks[0] * dec_ref[d, g * G]
        raw_col = jnp.where(hrow == 0, dec_ref[d, g * G], 0.0)
        for h in range(1, G):
            raw_row = raw_row + masks[h] * dec_ref[d, g * G + h]
            raw_col = jnp.where(hrow == h, dec_ref[d, g * G + h], raw_col)
        lg_row = _log_sigmoid(raw_row)
        lg_col = _log_sigmoid(raw_col)
        diff = ((sj - i_idx) if reverse else (i_idx - sj)).astype(F32)
        dmat = jnp.where(diff >= 0, jnp.exp(jnp.maximum(diff, 0.0) * lg_col), 0.0)
        qdec = jnp.exp(((L - ri) if reverse else (ri + 1.0)) * lg_row)
        kdec = jnp.exp((ri if reverse else (L - 1.0 - ri)) * lg_row)
        cdec_tile = _tile_row(jnp.exp(float(L) * lg_row))
        out_ref = o_ref if reverse else acc_ref

        def increments(c, kdec=kdec, cdec_tile=cdec_tile):
            rows = _chunk_rows(c)
            inc = _mm_tn(k_ref[rows, :] * kdec, u_ref[rows, 2 * gl:3 * gl])
            yield
            s_ref[c] = bd * inc
            cdec_ref[c] = cdec_tile

        _chunk_loop(nb * n_chunks, increments, LIGHT_CHUNKS_IN_FLIGHT)
        for s in range(nb):
            s_fin = _state_scan(n_chunks, reverse, s_ref, cdec_ref,
                                s0_ref[s, 0, d] if has_state else jnp.zeros((gl, gl), F32), s * n_chunks)
            if emit_state:
                _store_head_blocks(st_ref, s, d, s_fin)

        def outputs(c, dmat=dmat, qdec=qdec, out_ref=out_ref):
            rows = _chunk_rows(c)
            q = q_ref[rows, :]
            v = u_ref[rows, 2 * gl:3 * gl]
            att = _mm_nt(_stack(q, masks), k_ref[rows, :])
            inter = _mm(q * qdec, s_ref[c])
            yield
            intra = _mm(att * dmat, v)
            yield
            out_ref[rows, :] = _unstack(intra, masks, L) + inter

        _chunk_loop(nb * n_chunks, outputs, LIGHT_CHUNKS_IN_FLIGHT)

    def finalize(rows, o):
        mu = _mm_split(o, bd_avg)
        yield
        dev = o - mu
        var = _mm_split(dev * dev, bd_avg)
        yield
        return dev * lax.rsqrt(var + EPS) * nw_ref[...] * _silu(u_ref[rows, 3 * gl:4 * gl])

    _finalize(nb * seq, acc_ref, o_ref, finalize)


def _ret_tables(seq):
    t = np.arange(seq)
    quarter = HEAD_DIM // 4
    inv = (ROPE_BASE ** (-np.arange(quarter, dtype=np.float32) / quarter)).astype(np.float32)
    ang_r = (t // GRID_W).astype(np.float32)[:, None] * inv
    ang_c = (t % GRID_W).astype(np.float32)[:, None] * inv
    cos = np.concatenate([np.cos(ang_r), np.cos(ang_r), np.cos(ang_c), np.cos(ang_c)], axis=1)
    sin = np.concatenate([-np.sin(ang_r), np.sin(ang_r), -np.sin(ang_c), np.sin(ang_c)], axis=1)
    tile = lambda a: jnp.asarray(np.tile(a.astype(np.float32), (1, HEADS_PER_GROUP)))
    return tile(cos), tile(sin)


def _retention(u_main, ret_decay, norm_w, batch, seq, state):
    latent = state is not None
    nb = _seqs_per_step(latent)
    u_spec, _, out_spec = _mixer_specs(MIX_RET, seq, nb)
    gl = GROUP_LANES
    rows, chunks = nb * seq, nb * seq // CHUNK
    const2 = lambda b, g: (0, 0)
    in_specs = [pl.BlockSpec(memory_space=pltpu.SMEM), pl.BlockSpec((1, gl), const2), u_spec]
    args = [ret_decay, jnp.tile(norm_w.reshape(1, HEAD_DIM), (1, HEADS_PER_GROUP)), u_main]
    if latent:
        cos, sin = _ret_tables(seq)
        in_specs += [pl.BlockSpec((seq, gl), const2), pl.BlockSpec((seq, gl), const2), _state_spec(gl)]
        args += [cos, sin, state]
    out_shape = [jax.ShapeDtypeStruct((batch * seq, GROUP_WIDTH), F32)]
    out_specs = [out_spec]
    if not latent:
        out_shape.append(_head_state_shape(batch))
        out_specs.append(_head_state_spec(nb))
    res = pl.pallas_call(
        functools.partial(_ret_kernel, seq=seq, nb=nb, rope=latent, has_state=latent, emit_state=not latent),
        grid=(batch // nb, N_GROUPS),
        in_specs=in_specs,
        out_specs=out_specs,
        out_shape=out_shape,
        scratch_shapes=[pltpu.VMEM((rows, gl), F32), pltpu.VMEM((rows, gl), F32), pltpu.VMEM((rows, gl), F32),
                        pltpu.VMEM((chunks, gl, gl), F32), pltpu.VMEM((chunks, 8, gl), F32)],
        compiler_params=_MIXER_PARAMS,
        name="retention",
    )(*args)
    return res[0], (None if latent else res[1])


def _gla_kernel(nw_ref, u_ref, bf_ref, bb_ref, *rest, seq, nb, has_state, emit_state):
    rest = list(rest)
    s0_ref = rest.pop(0) if has_state else None
    o_ref = rest.pop(0)
    st_ref = rest.pop(0) if emit_state else None
    acc_ref, s_ref, dec_ref, steep_ref = rest

    gl, L, G, SB = GROUP_LANES, CHUNK, HEADS_PER_GROUP, GLA_SUB
    n_chunks = seq // L
    n_sub = L // SB
    masks = _head_masks()
    bd = _bd_mask()
    bd_avg = bd * (1.0 / HEAD_DIM)
    scale = HEAD_DIM ** -0.5
    half = SB // 2
    pair_row = _iota((SB * SB, gl), 0)
    pi = lax.shift_right_logical(pair_row, 4)
    pj = pair_row & (SB - 1)
    pair_sum = (lax.shift_right_logical(_iota((SB, SB * SB), 1), 4) == _iota((SB, SB * SB), 0)).astype(BF16)
    zeros_half = jnp.zeros((half, gl), F32)

    for d in (0, 1):
        reverse = d == 1
        pair_ok = (pj >= pi) if reverse else (pj <= pi)
        out_ref = o_ref if reverse else acc_ref

        b_ref = bb_ref if reverse else bf_ref
        steep_ref[...] = jnp.zeros((8, gl), F32)

        def increments(c, reverse=reverse, b_ref=b_ref):
            rows = _chunk_rows(c)
            b = b_ref[rows, :]
            btot = b[0:1] if reverse else b[L - 1:L]
            inc = _mm_tn(u_ref[rows, 2 * gl:3 * gl], u_ref[rows, gl:2 * gl] * jnp.exp2(btot - b))
            yield
            s_ref[c] = bd * inc
            dec_ref[c] = _tile_row(jnp.exp2(btot))
            steep_ref[...] = jnp.maximum(steep_ref[...], _tile_row(-btot))

        _chunk_loop(nb * n_chunks, increments, LIGHT_CHUNKS_IN_FLIGHT)
        factorise_all = jnp.max(steep_ref[...]) <= GLA_SAFE_LOG2
        for s in range(nb):
            s_fin = _state_scan(n_chunks, reverse, s_ref, dec_ref,
                                s0_ref[s, 0, d] if has_state else jnp.zeros((gl, gl), F32), s * n_chunks)
            if emit_state:
                _store_head_blocks(st_ref, s, d, s_fin.T)

        def outputs_factorised(c, reverse=reverse, out_ref=out_ref, b_ref=b_ref):
            rows = _chunk_rows(c)
            b = b_ref[rows, :]
            q = u_ref[rows, 0:gl] * scale
            k = u_ref[rows, gl:2 * gl]
            v = u_ref[rows, 2 * gl:3 * gl]
            o_inter = _mm_nt(q * jnp.exp2(b), s_ref[c])
            atts = []
            for i_blk in range(n_sub):
                lo = i_blk * SB
                bi = b[lo:lo + SB]
                ref, keys = (bi[SB - 1:SB], slice(lo, L)) if reverse else (bi[0:1], slice(0, lo + SB))
                qd = q[lo:lo + SB] * jnp.exp2(bi - ref)
                kd = k[keys] * jnp.exp2(ref - b[keys])
                att = _mm_nt(_stack(qd, masks), kd)
                n_keys = keys.stop - keys.start
                qrow = _iota((G * SB, n_keys), 0) & (SB - 1)
                kcol = _iota((G * SB, n_keys), 1)
                ok = (kcol >= qrow) if reverse else (kcol <= qrow + lo)
                atts.append((jnp.where(ok, att, 0.0), keys))
            yield
            outs = [_mm(a, v[keys]) for a, keys in atts]
            yield
            out_ref[rows, :] = o_inter + jnp.concatenate([_unstack(o, masks, SB) for o in outs], axis=0)

        def outputs_direct(c, reverse=reverse, pair_ok=pair_ok, out_ref=out_ref, b_ref=b_ref):
            rows = _chunk_rows(c)
            b = b_ref[rows, :]
            q = u_ref[rows, 0:gl] * scale
            k = u_ref[rows, gl:2 * gl]
            v = u_ref[rows, 2 * gl:3 * gl]
            o_inter = _mm_nt(q * jnp.exp2(b), s_ref[c])
            diag, off = [], []
            for i_blk in range(n_sub):
                lo = i_blk * SB
                qi, ki, bi = q[lo:lo + SB], k[lo:lo + SB], b[lo:lo + SB]
                prods = []
                for i in range(SB):
                    if reverse and i >= half:
                        part = slice(half, SB)
                    elif (not reverse) and i < half:
                        part = slice(0, half)
                    else:
                        part = slice(0, SB)
                    p = (qi[i:i + 1] * ki[part]) * jnp.exp2(bi[i:i + 1] - bi[part])
                    if part.start == half:
                        prods += [zeros_half, p]
                    elif part.stop == half:
                        prods += [p, zeros_half]
                    else:
                        prods.append(p)
                diag.append(_mm(jnp.concatenate(prods, axis=0), bd))
                if reverse and i_blk < n_sub - 1:
                    ref, other = bi[SB - 1:SB], slice(lo + SB, L)
                elif (not reverse) and i_blk > 0:
                    ref, other = bi[0:1], slice(0, lo)
                else:
                    off.append(None)
                    continue
                qd = qi * jnp.exp2(bi - ref)
                kd = k[other] * jnp.exp2(ref - b[other])
                off.append((_mm_nt(_stack(qd, masks), kd), other))
            yield
            off = [None if a is None else _mm(a[0], v[a[1]]) for a in off]
            for i_blk in range(n_sub):
                vi = v[i_blk * SB:(i_blk + 1) * SB]
                weighted = jnp.where(pair_ok, diag[i_blk] * jnp.concatenate([vi] * SB, axis=0), 0.0)
                diag[i_blk] = jnp.dot(pair_sum, weighted.astype(BF16), preferred_element_type=F32)
            yield
            blocks = []
            for i_blk in range(n_sub):
                o_blk = diag[i_blk]
                if off[i_blk] is not None:
                    o_blk = o_blk + _unstack(off[i_blk], masks, SB)
                blocks.append(o_blk)
            out_ref[rows, :] = o_inter + jnp.concatenate(blocks, axis=0)

        @pl.when(factorise_all)
        def _():
            _chunk_loop(nb * n_chunks, outputs_factorised, LIGHT_CHUNKS_IN_FLIGHT)

        @pl.when(jnp.logical_not(factorise_all))
        def _():
            _chunk_loop(nb * n_chunks, outputs_direct)

    def finalize(rows, o):
        ms = _mm_split(o * o, bd_avg)
        yield
        return o * lax.rsqrt(ms + EPS) * nw_ref[...] * _silu(u_ref[rows, 3 * gl:4 * gl])

    _finalize(nb * seq, acc_ref, o_ref, finalize)


def _gla(u_main, log_decay, norm_w, batch, seq, state):
    latent = state is not None
    nb = _seqs_per_step(latent)
    u_spec, _, out_spec = _mixer_specs(MIX_GLA, seq, nb)
    gl = GROUP_LANES
    rows, chunks = nb * seq, nb * seq // CHUNK
    in_specs = [
        pl.BlockSpec((1, gl), lambda b, g: (0, 0)),
        u_spec,
        pl.BlockSpec((rows, gl), lambda b, g: (b, g)),
        pl.BlockSpec((rows, gl), lambda b, g: (b, N_GROUPS + g)),
    ]
    args = [jnp.tile(norm_w.reshape(1, HEAD_DIM), (1, HEADS_PER_GROUP)), u_main, log_decay, log_decay]
    if latent:
        in_specs.append(_state_spec(gl))
        args.append(state)
    out_shape = [jax.ShapeDtypeStruct((batch * seq, GROUP_WIDTH), F32)]
    out_specs = [out_spec]
    if not latent:
        out_shape.append(_head_state_shape(batch))
        out_specs.append(_head_state_spec(nb))
    res = pl.pallas_call(
        functools.partial(_gla_kernel, seq=seq, nb=nb, has_state=latent, emit_state=not latent),
        grid=(batch // nb, N_GROUPS),
        in_specs=in_specs,
        out_specs=out_specs,
        out_shape=out_shape,
        scratch_shapes=[pltpu.VMEM((rows, gl), F32),
                        pltpu.VMEM((chunks, gl, gl), F32), pltpu.VMEM((chunks, 8, gl), F32),
                        pltpu.VMEM((8, gl), F32)],
        compiler_params=_MIXER_PARAMS,
        name="gla",
    )(*args)
    return res[0], (None if latent else res[1])


def _mlstm_kernel(e_ref, u_ref, us_ref, *rest, seq, nb, has_state, emit_state):
    rest = list(rest)
    if has_state:
        c0_ref, n0_ref, m0_ref = rest.pop(0), rest.pop(0), rest.pop(0)
    o_ref = rest.pop(0)
    if emit_state:
        cst_ref, nst_ref, mst_ref = rest.pop(0), rest.pop(0), rest.pop(0)
    acc_ref, f_ref, x_ref, p_ref, s_ref, nu_ref, wp_ref, min_ref, mout_ref = rest

    gl, L, G = GROUP_LANES, CHUNK, HEADS_PER_GROUP
    n_chunks = seq // L
    masks = _head_masks()
    bd = _bd_mask()
    scale = HEAD_DIM ** -0.5
    si = _iota((G * L, L), 0) & (L - 1)
    sj = _iota((G * L, L), 1)
    lane_in_head = _iota((L, gl), 1) & (HEAD_DIM - 1)

    for d in (0, 1):
        reverse = d == 1
        causal = (sj >= si) if reverse else (sj <= si)
        last = 0 if reverse else L - 1
        out_ref = o_ref if reverse else acc_ref

        def gates(rows, d=d):
            r = _mm_split(us_ref[rows, :], e_ref[d, 0])
            yield
            x_ref[rows, :] = r[:, 0:gl]
            f_ref[rows, :] = r[:, gl:2 * gl]
            p_ref[rows, :] = r[:, 2 * gl:3 * gl]

        _row_loop(nb * seq, gates)

        def stabiliser(t, m_row, base, reverse=reverse):
            c = base + _scan_order(t, n_chunks, reverse)
            m_new = _edge_row(f_ref, c, reverse) + jnp.maximum(m_row, _edge_row(p_ref, c, reverse))
            min_ref[c] = _tile_row(m_row)
            mout_ref[c] = _tile_row(m_new)
            return m_new

        m_fin = [lax.fori_loop(0, n_chunks, functools.partial(stabiliser, base=s * n_chunks),
                               m0_ref[s, 0, d] if has_state else jnp.zeros((1, gl), F32)) for s in range(nb)]

        def increments(c, last=last):
            rows = _chunk_rows(c)
            f_tot = f_ref[rows, :][last:last + 1]
            m_in = min_ref[c][0:1]
            m_out = mout_ref[c][0:1]
            kw = u_ref[rows, gl:2 * gl] * jnp.exp(f_tot + x_ref[rows, :] - m_out)
            inc = _mm_tn(kw, u_ref[rows, 2 * gl:3 * gl])
            yield
            s_ref[c] = bd * inc
            nu_ref[c] = _tile_row(jnp.sum(kw, axis=0, keepdims=True))
            wp_ref[c] = _tile_row(jnp.exp(f_tot + m_in - m_out))

        _chunk_loop(nb * n_chunks, increments, LIGHT_CHUNKS_IN_FLIGHT)

        def memory(t, carry, base, reverse=reverse):
            c_mat, n_row = carry
            c = base + _scan_order(t, n_chunks, reverse)
            inc = s_ref[c]
            nu = nu_ref[c][0:1]
            wp = wp_ref[c][0:1]
            s_ref[c] = c_mat
            nu_ref[c] = _tile_row(n_row)
            return wp * c_mat + inc, wp * n_row + nu

        for s in range(nb):
            if has_state:
                init = (c0_ref[s, 0, d], n0_ref[s, 0, d])
            else:
                init = (jnp.zeros((gl, gl), F32), jnp.zeros((1, gl), F32))
            c_fin, n_fin = lax.fori_loop(0, n_chunks, functools.partial(memory, base=s * n_chunks), init)
            if emit_state:
                _store_head_blocks(cst_ref, s, d, c_fin)
                nst_ref[s, 0, d], mst_ref[s, 0, d] = n_fin, m_fin[s]

        def outputs(c, causal=causal, out_ref=out_ref):
            rows = _chunk_rows(c)
            x = x_ref[rows, :]
            m_in = min_ref[c][0:1]
            q = u_ref[rows, 0:gl] * scale
            k = u_ref[rows, gl:2 * gl]
            v = u_ref[rows, 2 * gl:3 * gl]
            m_full = jnp.maximum(m_in, p_ref[rows, :])
            lhs = _split_lanes(-m_full, lane_in_head, 0, 1.0, 2)
            rhs = _split_lanes(x, lane_in_head, 2, 1.0, 0)
            logw = _mm_nt(_stack(lhs, masks), rhs)
            qk = _mm_nt(_stack(q, masks), k)
            inter = _mm(q, s_ref[c])
            q_n = _mm_split(q * nu_ref[c][0:1], bd)
            yield
            s = qk * jnp.exp(jnp.where(causal, logw, NEG))
            r = _mm(s, v)
            row_sum = jnp.sum(s, axis=1, keepdims=True)
            yield
            w_inter = jnp.exp(m_in - m_full)
            num = _unstack(r, masks, L) + w_inter * inter
            den = _unstack(jnp.broadcast_to(row_sum, (G * L, gl)), masks, L) + w_inter * q_n
            out_ref[rows, :] = num / jnp.maximum(jnp.abs(den), jnp.exp(-(f_ref[rows, :] + m_full)))

        _chunk_loop(nb * n_chunks, outputs)

    def finalize(rows, o):
        return o * jax.nn.sigmoid(u_ref[rows, 3 * gl:4 * gl])

    _finalize(nb * seq, acc_ref, o_ref, finalize)


def _gate_expanders(offset):
    e = np.zeros((2, N_GROUPS, SMALL_LANES, GROUP_LANES), np.float32)
    for d in range(2):
        for g in range(N_GROUPS):
            for h in range(HEADS_PER_GROUP):
                src = offset + d * N_HEADS + g * HEADS_PER_GROUP + h
                e[d, g, src, h * HEAD_DIM:(h + 1) * HEAD_DIM] = 1.0
    return jnp.asarray(e)


def _gate_bias_row(b_i, b_f):
    row = jnp.zeros((1, SMALL_LANES), F32)
    row = row.at[0, MI_OFF:MI_OFF + 2 * N_HEADS].set(b_i.astype(F32).reshape(-1))
    return row.at[0, MF_OFF:MF_OFF + 2 * N_HEADS].set(b_f.astype(F32).reshape(-1))


def _mlstm(u_main, u_small, batch, seq, state):
    latent = state is not None
    nb = _seqs_per_step(latent)
    u_spec, small_spec, out_spec = _mixer_specs(MIX_MLSTM, seq, nb)
    gl = GROUP_LANES
    rows, chunks = nb * seq, nb * seq // CHUNK
    expand = jnp.concatenate([_gate_expanders(MI_OFF), _gate_expanders(MF_OFF), _gate_expanders(MP_OFF)], axis=-1)
    in_specs = [pl.BlockSpec((2, 1, SMALL_LANES, 3 * gl), lambda b, g: (0, g, 0, 0)), u_spec, small_spec]
    args = [expand, u_main, u_small]
    if latent:
        in_specs += [_state_spec(gl), _state_spec(1), _state_spec(1)]
        args += list(state)
    out_shape = [jax.ShapeDtypeStruct((batch * seq, GROUP_WIDTH), F32)]
    out_specs = [out_spec]
    if not latent:
        out_shape += [_head_state_shape(batch),
                      jax.ShapeDtypeStruct((batch, N_GROUPS, 2, 1, gl), F32),
                      jax.ShapeDtypeStruct((batch, N_GROUPS, 2, 1, gl), F32)]
        out_specs += [_head_state_spec(nb), _state_spec(1, nb), _state_spec(1, nb)]
    res = pl.pallas_call(
        functools.partial(_mlstm_kernel, seq=seq, nb=nb, has_state=latent, emit_state=not latent),
        grid=(batch // nb, N_GROUPS),
        in_specs=in_specs,
        out_specs=out_specs,
        out_shape=out_shape,
        scratch_shapes=[pltpu.VMEM((rows, gl), F32)] * 4 + [pltpu.VMEM((chunks, gl, gl), F32)]
        + [pltpu.VMEM((chunks, 8, gl), F32)] * 4,
        compiler_params=_MIXER_PARAMS,
        name="mlstm",
    )(*args)
    return res[0], (None if latent else tuple(res[1:]))


def _na_ctx_kernel(u_ref, o_ref, ko_ref, vo_ref, *, seq, nb):
    gl = GROUP_LANES
    masks = _head_masks()

    def sequence(s):
        rows = slice(s * seq, (s + 1) * seq)
        q = u_ref[rows, 0:gl] * (HEAD_DIM ** -0.5)
        k = u_ref[rows, gl:2 * gl]
        v = u_ref[rows, 2 * gl:3 * gl]
        sc = _mm_nt(_stack(q, masks), k)
        yield
        p = jnp.exp(sc - jnp.max(sc, axis=1, keepdims=True))
        o = _mm(p, v)
        yield
        o_ref[rows, :] = _unstack(o / jnp.sum(p, axis=1, keepdims=True), masks, seq)
        for h in range(HEADS_PER_GROUP):
            ko_ref[s, h] = k[:, h * HEAD_DIM:(h + 1) * HEAD_DIM]
            vo_ref[s, h] = v[:, h * HEAD_DIM:(h + 1) * HEAD_DIM]

    _interleave([sequence(s) for s in range(nb)])


def _na_context(u_main, batch, seq):
    nb = _seqs_per_step(False)
    u_spec, _, out_spec = _mixer_specs(MIX_NA, seq, nb)
    kv_shape = jax.ShapeDtypeStruct((batch, N_HEADS, seq, HEAD_DIM), F32)
    kv_spec = pl.BlockSpec((nb, HEADS_PER_GROUP, seq, HEAD_DIM), lambda b, g: (b, g, 0, 0))
    return pl.pallas_call(
        functools.partial(_na_ctx_kernel, seq=seq, nb=nb),
        grid=(batch // nb, N_GROUPS),
        in_specs=[u_spec],
        out_specs=[out_spec, kv_spec, kv_spec],
        out_shape=[jax.ShapeDtypeStruct((batch * seq, GROUP_WIDTH), F32), kv_shape, kv_shape],
        compiler_params=_MIXER_PARAMS,
        name="na_context",
    )(u_main)


def _na_key_col0(qb):
    return int(np.clip(qb * NA_QC - NA_KW // 2, 0, GRID_W - NA_KCB))


def _na_lat_kernel(tab_ref, kc_ref, vc_ref, u_ref, o_ref, *, seq):
    gl = GROUP_LANES
    grid_rows = seq // GRID_W
    masks = _head_masks()
    scale = HEAD_DIM ** -0.5
    kc = kc_ref[0, 0]
    vc = vc_ref[0, 0]

    def query_block(r, qb):
        ks = jnp.clip(r - NA_KH // 2, 0, grid_rows - NA_KH)
        q_rows = pl.ds(pl.multiple_of(r * GRID_W, GRID_W) + qb * NA_QC, NA_QC)
        k_rows = [pl.ds(pl.multiple_of((ks + j) * GRID_W, GRID_W) + _na_key_col0(qb), NA_KCB) for j in range(NA_KH)]
        qs = _stack(u_ref[q_rows, 0:gl] * scale, masks)
        s_loc = _mm_nt(qs, jnp.concatenate([u_ref[rows, gl:2 * gl] for rows in k_rows], axis=0))
        s_ctx = _mm_nt(qs, kc)
        yield
        s_loc = s_loc + tab_ref[0, qb, ks - r + NA_KH - 1]
        m = jnp.maximum(jnp.max(s_loc, axis=1, keepdims=True), jnp.max(s_ctx, axis=1, keepdims=True))
        p_loc = jnp.exp(s_loc - m)
        p_ctx = jnp.exp(s_ctx - m)
        den = jnp.sum(p_loc, axis=1, keepdims=True) + jnp.sum(p_ctx, axis=1, keepdims=True)
        o = _mm(p_loc, jnp.concatenate([u_ref[rows, 2 * gl:3 * gl] for rows in k_rows], axis=0)) + _mm(p_ctx, vc)
        yield
        o_ref[q_rows, :] = _unstack(o / den, masks, NA_QC)

    def body(t, carry):
        _interleave([query_block(t * NA_ROWS_IN_FLIGHT + j, qb)
                     for j in range(NA_ROWS_IN_FLIGHT) for qb in range(GRID_W // NA_QC)])
        return carry

    lax.fori_loop(0, grid_rows // NA_ROWS_IN_FLIGHT, body, 0)


def _na_bias_table(rpb):
    c = np.arange(GRID_W)
    rel = c[None, :] - c[:, None]
    cs = np.clip(c - NA_KW // 2, 0, GRID_W - NA_KW)
    valid = (c[None, :] >= cs[:, None]) & (c[None, :] < cs[:, None] + NA_KW)
    n_rel = 2 * NA_KW - 1
    pick = (np.arange(n_rel)[:, None, None] == (np.clip(rel, -(NA_KW - 1), NA_KW - 1) + NA_KW - 1)[None])
    bmat = jnp.einsum("hdk,kcz->hdcz", rpb.astype(F32), jnp.asarray(pick, F32), precision=HI)
    bmat = jnp.where(valid[None, None], bmat, NEG)
    blocks = []
    for qb in range(GRID_W // NA_QC):
        sub = bmat[:, :, qb * NA_QC:(qb + 1) * NA_QC, _na_key_col0(qb):_na_key_col0(qb) + NA_KCB]
        tabs = jnp.stack([sub[:, s:s + NA_KH] for s in range(NA_KH)], axis=1)
        tabs = tabs.transpose(0, 1, 3, 2, 4).reshape(N_GROUPS, HEADS_PER_GROUP, NA_KH, NA_QC, NA_KH * NA_KCB)
        blocks.append(tabs.transpose(0, 2, 1, 3, 4).reshape(N_GROUPS, NA_KH, HEADS_PER_GROUP * NA_QC,
                                                             NA_KH * NA_KCB))
    return jnp.stack(blocks, axis=1)


def _heads_to_lanes(t):
    b, _, s, _ = t.shape
    t = t.reshape(b, N_GROUPS, HEADS_PER_GROUP, s, HEAD_DIM).transpose(0, 1, 3, 2, 4)
    return t.reshape(b, N_GROUPS, s, GROUP_LANES)


def _na_latent(u_main, rpb, k_ctx, v_ctx, batch, seq):
    u_spec, _, out_spec = _mixer_specs(MIX_NA, seq, 1)
    gl = GROUP_LANES
    past = k_ctx.shape[2]
    tab = _na_bias_table(rpb)
    ctx_spec = pl.BlockSpec((1, 1, past, gl), lambda b, g: (b, g, 0, 0))
    return pl.pallas_call(
        functools.partial(_na_lat_kernel, seq=seq),
        grid=(batch, N_GROUPS),
        in_specs=[pl.BlockSpec((1,) + tab.shape[1:], lambda b, g: (g, 0, 0, 0, 0)), ctx_spec, ctx_spec, u_spec],
        out_specs=out_spec,
        out_shape=jax.ShapeDtypeStruct((batch * seq, GROUP_WIDTH), F32),
        compiler_params=_MIXER_PARAMS,
        name="na_latent",
    )(tab, _heads_to_lanes(k_ctx.astype(F32)), _heads_to_lanes(v_ctx.astype(F32)), u_main)


def _to_block_diag(s):
    b = s.shape[0]
    G = HEADS_PER_GROUP
    s = s.astype(F32).reshape(b, 2, N_GROUPS, G, HEAD_DIM, 1, HEAD_DIM)
    eye = jnp.eye(G, dtype=F32).reshape(1, 1, 1, G, 1, G, 1)
    return (s * eye).reshape(b, 2, N_GROUPS, GROUP_LANES, GROUP_LANES).transpose(0, 2, 1, 3, 4)


def _rows_to_lanes(v):
    b = v.shape[0]
    return v.astype(F32).reshape(b, 2, N_GROUPS, 1, GROUP_LANES).transpose(0, 2, 1, 3, 4)


def _lanes_to_rows(v):
    b = v.shape[0]
    return v.transpose(0, 2, 1, 3, 4).reshape(b, 2, N_HEADS, HEAD_DIM)


def _swap_last(s):
    return jnp.swapaxes(s, -1, -2)


def _in_weight(w_in):
    gl = GROUP_LANES
    w_in = w_in.astype(BF16)
    cols = []
    for names in _MIXER_COLS:
        for g in range(N_GROUPS):
            for name in names:
                cols.append(w_in[..., _OFF[name] + g * gl:_OFF[name] + (g + 1) * gl])
    cols.append(w_in[..., _OFF["ga"]:_OFF["ga"] + 2 * GLA_RANK])
    cols.append(w_in[..., _OFF["mi"]:_OFF["mi"] + 2 * N_HEADS])
    cols.append(w_in[..., _OFF["mf"]:_OFF["mf"] + 2 * N_HEADS])
    cols.append(jnp.zeros((DEPTH, D_MODEL, SMALL_LANES - MF_OFF - 2 * N_HEADS), BF16))
    return jnp.concatenate(cols, axis=-1)


def _gla_gate_weight(w_a2_l):
    out = jnp.zeros((SMALL_LANES, 2 * GROUP_WIDTH), F32)
    for d in range(2):
        out = out.at[d * GLA_RANK:(d + 1) * GLA_RANK, d * GROUP_WIDTH:(d + 1) * GROUP_WIDTH].set(
            w_a2_l[d].astype(F32))
    return out.astype(BF16)


def _layer(x, mod, p, batch, seq, ctx):
    latent = ctx is not None
    l = p["layer"]
    x, u_main, u_small, log_decay = _ffn_inproj(x, mod[:, 0:6], p["norm_w"][0:3], p["gate_bias"], p["gla_wa"],
                                                p["gla_ba"], p["w1"], p["w3"], p["w2"], p["w_in"], l, seq)
    mix_a, st_gla = _gla(u_main, log_decay, p["gla_norm_w"], batch, seq,
                         _swap_last(_to_block_diag(ctx["gla"])) if latent else None)
    if latent:
        mix_b = _na_latent(u_main, p["na_rpb"], ctx["na_k"], ctx["na_v"], batch, seq)
    else:
        mix_b, na_k, na_v = _na_context(u_main, batch, seq)
    mix_c, st_ml = _mlstm(u_main, u_small, batch, seq,
                          (_to_block_diag(ctx["mC"]), _rows_to_lanes(ctx["mn"]),
                           _rows_to_lanes(jnp.repeat(ctx["mm"][..., None], HEAD_DIM, axis=-1))) if latent else None)
    mix_d, st_ret = _retention(u_main, p["ret_decay"], p["ret_norm_w"], batch, seq,
                               _to_block_diag(ctx["ret"]) if latent else None)
    x = _outproj_ffn(x, (mix_a, mix_b, mix_c, mix_d), mod[:, 3:9], p["norm_w"][3:6], p["w_out"],
                     p["w1"], p["w3"], p["w2"], l, seq)
    if latent:
        return x, None
    new = {
        "na_k": na_k, "na_v": na_v, "gla": st_gla, "mC": st_ml[0],
        "mn": _lanes_to_rows(st_ml[1]), "mm": _lanes_to_rows(st_ml[2])[..., 0], "ret": st_ret,
    }
    return x, new


def kernel(x_prompt, x_sample, cache_na_k, cache_na_v, state_gla, state_mlstm_C, state_mlstm_n, state_mlstm_m,
           state_ret, c, c_ctx, w_mod, b_mod, norm_w, ffn_w1, ffn_w3, ffn_w2, w_in, w_out, gla_w_a2, gla_b_a,
           gla_norm_w, na_rpb, mlstm_b_i, mlstm_b_f, ret_decay, ret_norm_w):
    batch, seq, _ = x_prompt.shape
    dec_batch, dec_seq, _ = x_sample.shape

    cvec = jnp.zeros((8, D_MODEL), F32).at[0].set(c_ctx).at[1:1 + dec_batch].set(c)
    mods = _modulation(cvec, w_mod, b_mod).reshape(DEPTH, 8, N_MOD, D_MODEL)

    w1, w3, w2 = ffn_w1.astype(BF16), ffn_w3.astype(BF16), ffn_w2.astype(BF16)
    w_in_all, w_out_all = _in_weight(w_in), w_out.astype(BF16)
    params = []
    for l in range(DEPTH):
        params.append({
            "layer": l, "norm_w": norm_w[l],
            "w1": w1, "w3": w3, "w2": w2, "w_in": w_in_all, "w_out": w_out_all,
            "gla_wa": _gla_gate_weight(gla_w_a2[l]), "gla_ba": gla_b_a[l].astype(F32).reshape(1, 2 * GROUP_WIDTH),
            "gla_norm_w": gla_norm_w[l],
            "na_rpb": na_rpb[l], "gate_bias": _gate_bias_row(mlstm_b_i[l], mlstm_b_f[l]),
            "ret_decay": ret_decay[l], "ret_norm_w": ret_norm_w[l],
        })

    xp = x_prompt.reshape(batch * seq, D_MODEL)
    states = []
    for l in range(DEPTH):
        xp, st = _layer(xp, mods[l, 0:1], params[l], batch, seq, None)
        states.append(st)

    xs = x_sample.reshape(dec_batch * dec_seq, D_MODEL)
    for l in range(DEPTH):
        ctx = {"na_k": cache_na_k[:, l], "na_v": cache_na_v[:, l], "gla": state_gla[:, l],
               "mC": state_mlstm_C[:, l], "mn": state_mlstm_n[:, l], "mm": state_mlstm_m[:, l],
               "ret": state_ret[:, l]}
        xs, _ = _layer(xs, mods[l, 1:1 + dec_batch], params[l], dec_batch, dec_seq, ctx)

    stack = lambda name: jnp.stack([s[name] for s in states], axis=1)
    return (xp.reshape(batch, seq, D_MODEL), xs.reshape(dec_batch, dec_seq, D_MODEL),
            stack("na_k"), stack("na_v"), stack("gla"), stack("mC"), stack("mn"), stack("mm"), stack("ret"))
```

```python
import functools

import numpy as np
import jax
import jax.numpy as jnp
from jax import lax
from jax.experimental import pallas as pl
from jax.experimental.pallas import tpu as pltpu

D_MODEL = 1024
DEPTH = 2
HEAD_DIM = 64
N_HEADS = 4
GROUP_WIDTH = N_HEADS * HEAD_DIM
N_MOD = 9
GLA_RANK = 16
GLA_TAU = 16.0
CHUNK = 64
GRID_W = 64
NA_KH = 8
NA_KW = 16
NA_QC = 16
NA_KCB = NA_QC + NA_KW
ROPE_BASE = 10000.0
EPS = 1e-6
D_FF = 2816

HEADS_PER_GROUP = 2
GROUP_LANES = HEADS_PER_GROUP * HEAD_DIM
N_GROUPS = N_HEADS // HEADS_PER_GROUP
GLA_SUB = 16
GLA_SAFE_LOG2 = 64.0
ROW_TILE_IN = 256
ROW_TILE_OUT = 512
ROW_PART = 128
CTX_SEQS_PER_STEP = 4
PRE_ROWS = 128
ROWS_IN_FLIGHT = 8
NA_ROWS_IN_FLIGHT = 16
LOG2E = 1.4426950408889634
CHUNKS_IN_FLIGHT = 8
LIGHT_CHUNKS_IN_FLIGHT = 16
SMALL_LANES = 128
MI_OFF = 2 * GLA_RANK
MF_OFF = MI_OFF + 2 * N_HEADS
MP_OFF = MF_OFF + 2 * N_HEADS
NEG = -1e30
VMEM_LIMIT = 56 * 1024 * 1024

F32 = jnp.float32
BF16 = jnp.bfloat16
HI = lax.Precision.HIGHEST

_OFF = {}
_o = 0
for _name, _size in (("gq", 256), ("gk", 256), ("gv", 256), ("gg", 256), ("ga", 32), ("nq", 256), ("nk", 256),
                     ("nv", 256), ("mq", 256), ("mk", 256), ("mv", 256), ("mo", 256), ("mi", 8), ("mf", 8),
                     ("rq", 256), ("rk", 256), ("rv", 256), ("rg", 256)):
    _OFF[_name] = _o
    _o += _size
N_IN = _o
_MIXER_COLS = (("gq", "gk", "gv", "gg"), ("mq", "mk", "mv", "mo"), ("rq", "rk", "rv", "rg"), ("nq", "nk", "nv"))
MIX_GLA, MIX_MLSTM, MIX_RET, MIX_NA = range(4)
MAIN_COLS = sum(len(names) for names in _MIXER_COLS) * N_GROUPS * GROUP_LANES


def _mm(a, b):
    return jnp.dot(a.astype(BF16), b.astype(BF16), preferred_element_type=F32)


def _mm_nt(a, b):
    return lax.dot_general(a.astype(BF16), b.astype(BF16), (((1,), (1,)), ((), ())), preferred_element_type=F32)


def _mm_tn(a, b):
    return lax.dot_general(a.astype(BF16), b.astype(BF16), (((0,), (0,)), ((), ())), preferred_element_type=F32)


def _split3(x):
    hi = x.astype(BF16)
    rest = x - hi.astype(F32)
    mid = rest.astype(BF16)
    lo = (rest - mid.astype(F32)).astype(BF16)
    return jnp.concatenate([hi, mid, lo], axis=1)


def _dot_split(pieces, w):
    wb = w.astype(BF16)
    return jnp.dot(pieces, jnp.concatenate([wb, wb, wb], axis=0), preferred_element_type=F32)


def _mm_split(x, w):
    return _dot_split(_split3(x), w)


def _log_sigmoid(x):
    return jnp.minimum(x, 0.0) - jnp.log1p(jnp.exp(-jnp.abs(x)))


def _silu(x):
    return x * jax.nn.sigmoid(x)


def _iota(shape, dim):
    return lax.broadcasted_iota(jnp.int32, shape, dim)


def _head_of(idx):
    return lax.shift_right_logical(idx, 6)


def _head_masks():
    lane = _iota((1, GROUP_LANES), 1)
    return [(_head_of(lane) == h).astype(F32) for h in range(HEADS_PER_GROUP)]


def _bd_mask():
    n = GROUP_LANES
    return (_head_of(_iota((n, n), 0)) == _head_of(_iota((n, n), 1))).astype(F32)


def _stack(x, masks):
    return jnp.concatenate([x * m for m in masks], axis=0)


def _unstack(r, masks, n):
    out = r[0:n] * masks[0]
    for h in range(1, len(masks)):
        out = out + r[h * n:(h + 1) * n] * masks[h]
    return out


def _rmsnorm(x, w):
    return x * lax.rsqrt(jnp.mean(x * x, axis=-1, keepdims=True) + EPS) * w


def _split_lanes(x, lane_in_head, at, fill, fill_at):
    hi = x.astype(BF16).astype(F32)
    out = jnp.where(lane_in_head == at, hi, jnp.where(lane_in_head == at + 1, x - hi, 0.0))
    is_fill = (lane_in_head >= fill_at) & (lane_in_head < fill_at + 2)
    return jnp.where(is_fill, fill, out)


def _mod_kernel(c_ref, w_ref, b_ref, o_ref):
    s = _silu(c_ref[...])
    o_ref[0] = _mm(s, w_ref[0]) + b_ref[0]


def _modulation(cvec, w_mod, b_mod):
    n = N_MOD * D_MODEL
    tn = n // 8
    return pl.pallas_call(
        _mod_kernel,
        grid=(DEPTH, n // tn),
        in_specs=[
            pl.BlockSpec((8, D_MODEL), lambda l, j: (0, 0)),
            pl.BlockSpec((1, D_MODEL, tn), lambda l, j: (l, 0, j)),
            pl.BlockSpec((1, 1, tn), lambda l, j: (l, 0, j)),
        ],
        out_specs=pl.BlockSpec((1, 8, tn), lambda l, j: (l, 0, j)),
        out_shape=jax.ShapeDtypeStruct((DEPTH, 8, n), F32),
        compiler_params=pltpu.CompilerParams(dimension_semantics=("parallel", "parallel"),
                                             vmem_limit_bytes=VMEM_LIMIT),
        name="modulation",
    )(cvec, w_mod, b_mod.reshape(DEPTH, 1, n))


def _half_ffn(x, mod, g, w1_ref, w3_ref, w2_ref):
    h = _rmsnorm(x, g[0:1]) * (1.0 + mod[1:2]) + mod[0:1]
    hb = h.astype(BF16)
    h1 = jnp.dot(hb, w1_ref[...], preferred_element_type=F32)
    h3 = jnp.dot(hb, w3_ref[...], preferred_element_type=F32)
    yield
    y = jnp.dot((_silu(h1) * h3).astype(BF16), w2_ref[...], preferred_element_type=F32)
    yield
    return x + 0.5 * mod[2:3] * _rmsnorm(y, g[1:2])


def _row_parts(tm):
    return [slice(j, j + ROW_PART) for j in range(0, tm, ROW_PART)]


def _mlstm_gate_scans(us, gate_bias):
    lane = _iota(us.shape, 1)
    field = 2 * N_HEADS
    fwd = (lane & (field - 1)) < N_HEADS
    pre = us + gate_bias
    lf = jnp.where((lane >= MF_OFF) & (lane < MF_OFF + field), _log_sigmoid(pre), 0.0)
    f = jnp.where(fwd, _chunk_scan(lf, False, jnp.add), _chunk_scan(lf, True, jnp.add))
    x = pre - pltpu.roll(f, SMALL_LANES - (MF_OFF - MI_OFF), 1)
    p = jnp.where(fwd, _chunk_scan(x, False, jnp.maximum), _chunk_scan(x, True, jnp.maximum))
    p = pltpu.roll(p, MP_OFF - MI_OFF, 1)
    return jnp.where(lane < MI_OFF, us,
                     jnp.where(lane < MF_OFF, x, jnp.where(lane < MP_OFF, f, jnp.where(lane < MP_OFF + field, p, 0.0))))


def _gla_gate(u_small, wa, ba):
    return _mm(u_small, wa) + ba


def _gla_log_decay(a_pre):
    la = _log_sigmoid(a_pre) * (LOG2E / GLA_TAU)
    return jnp.concatenate([_chunk_scan(la[:, :GROUP_WIDTH], False, jnp.add),
                            _chunk_scan(la[:, GROUP_WIDTH:], True, jnp.add)], axis=1)


def _ffn_inproj_kernel(x_ref, mod_ref, g_ref, gb_ref, wa_ref, ba_ref, w1_ref, w3_ref, w2_ref, win_ref,
                       x_out_ref, um_ref, us_ref, ub_ref):
    mod = mod_ref[0]
    g = g_ref[...]

    def part(rows):
        x = yield from _half_ffn(x_ref[rows, :], mod[0:3], g[0:2], w1_ref, w3_ref, w2_ref)
        x_out_ref[rows, :] = x
        hb = (_rmsnorm(x, g[2:3]) * (1.0 + mod[4:5]) + mod[3:4]).astype(BF16)
        u_small = jnp.dot(hb, win_ref[:, MAIN_COLS:], preferred_element_type=F32)
        u_main = jnp.dot(hb, win_ref[:, :MAIN_COLS], preferred_element_type=F32)
        us_ref[rows, :] = _mlstm_gate_scans(u_small, gb_ref[...])
        ub_ref[rows, :] = _gla_log_decay(_gla_gate(u_small, wa_ref[...], ba_ref[...]))
        yield
        um_ref[rows, :] = u_main

    _interleave([part(rows) for rows in _row_parts(x_ref.shape[0])])


def _outproj_ffn_kernel(x_ref, ma_ref, mb_ref, mc_ref, md_ref, mod_ref, g_ref, wo_ref, w1_ref, w3_ref, w2_ref,
                        o_ref):
    mod = mod_ref[0]
    g = g_ref[...]

    def part(rows):
        mix = jnp.concatenate([ma_ref[rows, :], mb_ref[rows, :], mc_ref[rows, :], md_ref[rows, :]], axis=-1)
        y = jnp.dot(mix.astype(BF16), wo_ref[...], preferred_element_type=F32)
        yield
        x = x_ref[rows, :] + mod[2:3] * _rmsnorm(y, g[0:1])
        o_ref[rows, :] = yield from _half_ffn(x, mod[3:6], g[1:3], w1_ref, w3_ref, w2_ref)

    _interleave([part(rows) for rows in _row_parts(x_ref.shape[0])])


def _row_tile(n_mod, rows_per_batch, tile):
    return tile if n_mod == 1 else min(tile, rows_per_batch)


def _mod_index(n_mod, rows_per_batch, tm):
    if n_mod == 1:
        return lambda i: (0, 0, 0)
    per = rows_per_batch // tm
    return lambda i: (i // per, 0, 0)


_ROW_PARAMS = pltpu.CompilerParams(dimension_semantics=("parallel",), vmem_limit_bytes=VMEM_LIMIT)


def _ffn_weight_specs(layer, half):
    pick = lambda i: (layer, half, 0, 0)
    return [pl.BlockSpec((None, None, D_MODEL, D_FF), pick, pipeline_mode=pl.Buffered(1)),
            pl.BlockSpec((None, None, D_MODEL, D_FF), pick, pipeline_mode=pl.Buffered(1)),
            pl.BlockSpec((None, None, D_FF, D_MODEL), pick, pipeline_mode=pl.Buffered(1))]


def _ffn_inproj(x, mod6, g3, gate_bias, gla_wa, gla_ba, w1, w3, w2, w_big, layer, rows_per_batch):
    rows = x.shape[0]
    tm = _row_tile(mod6.shape[0], rows_per_batch, ROW_TILE_IN)
    row_spec = lambda width: pl.BlockSpec((tm, width), lambda i: (i, 0))
    const = lambda i: (0, 0)
    widths = (D_MODEL, MAIN_COLS, SMALL_LANES, 2 * GROUP_WIDTH)
    return pl.pallas_call(
        _ffn_inproj_kernel,
        grid=(rows // tm,),
        in_specs=[
            row_spec(D_MODEL),
            pl.BlockSpec((1, 6, D_MODEL), _mod_index(mod6.shape[0], rows_per_batch, tm)),
            pl.BlockSpec((3, D_MODEL), const),
            pl.BlockSpec((1, SMALL_LANES), const),
            pl.BlockSpec((SMALL_LANES, 2 * GROUP_WIDTH), const),
            pl.BlockSpec((1, 2 * GROUP_WIDTH), const),
            *_ffn_weight_specs(layer, 0),
            pl.BlockSpec((None, D_MODEL, MAIN_COLS + SMALL_LANES), lambda i: (layer, 0, 0),
                         pipeline_mode=pl.Buffered(1)),
        ],
        out_specs=[row_spec(w) for w in widths],
        out_shape=[jax.ShapeDtypeStruct((rows, w), F32) for w in widths],
        compiler_params=_ROW_PARAMS,
        name="ffn_inproj",
    )(x, mod6, g3, gate_bias, gla_wa, gla_ba, w1, w3, w2, w_big)


def _outproj_ffn(x, mixes, mod6, g3, w_out, w1, w3, w2, layer, rows_per_batch):
    rows = x.shape[0]
    tm = _row_tile(mod6.shape[0], rows_per_batch, ROW_TILE_OUT)
    row_spec = lambda width: pl.BlockSpec((tm, width), lambda i: (i, 0))
    return pl.pallas_call(
        _outproj_ffn_kernel,
        grid=(rows // tm,),
        in_specs=[
            row_spec(D_MODEL), *[row_spec(GROUP_WIDTH)] * 4,
            pl.BlockSpec((1, 6, D_MODEL), _mod_index(mod6.shape[0], rows_per_batch, tm)),
            pl.BlockSpec((3, D_MODEL), lambda i: (0, 0)),
            pl.BlockSpec((None, D_MODEL, D_MODEL), lambda i: (layer, 0, 0), pipeline_mode=pl.Buffered(1)),
            *_ffn_weight_specs(layer, 1),
        ],
        out_specs=row_spec(D_MODEL),
        out_shape=jax.ShapeDtypeStruct((rows, D_MODEL), F32),
        compiler_params=_ROW_PARAMS,
        name="outproj_ffn",
    )(x, *mixes, mod6, g3, w_out, w1, w3, w2)


def _seqs_per_step(latent):
    return 1 if latent else CTX_SEQS_PER_STEP


def _mixer_specs(mixer, seq, nb):
    gl = GROUP_LANES
    n_blocks = len(_MIXER_COLS[mixer])
    first = sum(len(names) for names in _MIXER_COLS[:mixer]) * N_GROUPS
    assert first % n_blocks == 0
    u_spec = pl.BlockSpec((nb * seq, n_blocks * gl), lambda b, g: (b, first // n_blocks + g))
    small_spec = pl.BlockSpec((nb * seq, SMALL_LANES), lambda b, g: (b, 0))
    out_spec = pl.BlockSpec((nb * seq, gl), lambda b, g: (b, g))
    return u_spec, small_spec, out_spec


def _state_spec(rows, nb=1):
    return pl.BlockSpec((nb, 1, 2, rows, GROUP_LANES), lambda b, g: (b, g, 0, 0, 0))


def _head_state_shape(batch):
    return jax.ShapeDtypeStruct((batch, 2, N_HEADS, HEAD_DIM, HEAD_DIM), F32)


def _head_state_spec(nb):
    return pl.BlockSpec((nb, 2, HEADS_PER_GROUP, HEAD_DIM, HEAD_DIM), lambda b, g: (b, 0, g, 0, 0))


def _store_head_blocks(ref, s, d, mat):
    for h in range(HEADS_PER_GROUP):
        lo = h * HEAD_DIM
        ref[s, d, h] = mat[lo:lo + HEAD_DIM, lo:lo + HEAD_DIM]


_MIXER_PARAMS = pltpu.CompilerParams(dimension_semantics=("parallel", "parallel"), vmem_limit_bytes=VMEM_LIMIT)


def _chunk_rows(c):
    return pl.ds(pl.multiple_of(c * CHUNK, CHUNK), CHUNK)


def _scan_order(t, n_chunks, reverse):
    return (n_chunks - 1 - t) if reverse else t


def _tile_row(row):
    return jnp.broadcast_to(row, (8, row.shape[1]))


def _edge_row(ref, c, reverse):
    if reverse:
        return ref[pl.ds(pl.multiple_of(c * CHUNK, 8), 8), :][0:1]
    return ref[pl.ds(pl.multiple_of(c * CHUNK + CHUNK - 8, 8), 8), :][7:8]


def _chunk_scan(x, reverse, op):
    n = x.shape[0]
    row = _iota(x.shape, 0) & (CHUNK - 1)
    s = 1
    while s < CHUNK:
        if reverse:
            shifted = pltpu.roll(x, n - s, 0)
            ok = row < CHUNK - s
        else:
            shifted = pltpu.roll(x, s, 0)
            ok = row >= s
        x = jnp.where(ok, op(x, shifted), x)
        s *= 2
    return x


def _row_loop(seq, fn):
    blk = min(seq, PRE_ROWS)
    k = min(seq // blk, ROWS_IN_FLIGHT)
    assert seq % (blk * k) == 0

    def body(i, c):
        _interleave([fn(pl.ds(pl.multiple_of((i * k + j) * blk, blk), blk)) for j in range(k)])
        return c

    if seq == blk * k:
        _interleave([fn(pl.ds(j * blk, blk)) for j in range(k)])
    else:
        lax.fori_loop(0, seq // (blk * k), body, 0)


def _chunk_loop(n_chunks, make_stream, in_flight=None):
    k = min(n_chunks, in_flight or CHUNKS_IN_FLIGHT)
    assert n_chunks % k == 0

    def body(t, carry):
        _interleave([make_stream(t * k + j) for j in range(k)])
        return carry

    if n_chunks == k:
        body(0, 0)
    else:
        lax.fori_loop(0, n_chunks // k, body, 0)


def _interleave(streams):
    results = [None] * len(streams)
    live = []
    for i, s in enumerate(streams):
        if hasattr(s, "send"):
            live.append(i)
        else:
            results[i] = s
    while live:
        for i in list(live):
            try:
                next(streams[i])
            except StopIteration as done:
                results[i] = done.value
                live.remove(i)
    return results


def _state_scan(n_chunks, reverse, s_ref, decay_ref, init, base):
    def body(t, s):
        c = base + _scan_order(t, n_chunks, reverse)
        inc = s_ref[c]
        s_ref[c] = s
        return decay_ref[c][0:1] * s + inc

    return lax.fori_loop(0, n_chunks, body, init)


def _finalize(seq, acc_ref, o_ref, fn):
    def blk(rows):
        res = fn(rows, acc_ref[rows, :] + o_ref[rows, :])
        if hasattr(res, "send"):
            res = yield from res
        o_ref[rows, :] = res

    _row_loop(seq, blk)


def _rope(x, cos, sin_signed):
    lane = _iota(x.shape, 1)
    first = (lane & 31) < 16
    swapped = jnp.where(first, pltpu.roll(x, GROUP_LANES - 16, 1), pltpu.roll(x, 16, 1))
    return x * cos + swapped * sin_signed


def _ret_kernel(dec_ref, nw_ref, u_ref, *rest, seq, nb, rope, has_state, emit_state):
    rest = list(rest)
    cos_ref, sin_ref = (rest.pop(0), rest.pop(0)) if rope else (None, None)
    s0_ref = rest.pop(0) if has_state else None
    o_ref = rest.pop(0)
    st_ref = rest.pop(0) if emit_state else None
    acc_ref, q_ref, k_ref, s_ref, cdec_ref = rest

    gl, L, G = GROUP_LANES, CHUNK, HEADS_PER_GROUP
    n_chunks = seq // L
    g = pl.program_id(1)
    masks = _head_masks()
    bd = _bd_mask()
    bd_avg = bd * (1.0 / HEAD_DIM)
    scale = HEAD_DIM ** -0.5
    ri = _iota((L, gl), 0).astype(F32)
    si = _iota((G * L, L), 0)
    sj = _iota((G * L, L), 1)
    i_idx = si & (L - 1)
    hrow = _head_of(_iota((G * L, 1), 0))

    def prepare(rows):
        q = u_ref[rows, 0:gl]
        k = u_ref[rows, gl:2 * gl]
        if rope:
            cos = cos_ref[rows, :]
            sin = sin_ref[rows, :]
            q = _rope(q, cos, sin)
            k = _rope(k, cos, sin)
        q_ref[rows, :] = q * scale
        k_ref[rows, :] = k

    _row_loop(nb * seq, prepare)

    for d in (0, 1):
        reverse = d == 1
        raw_row = masks[0] * dec_ref[d, g * G]
        raw_col = jnp.where(hrow == 0, dec_ref[d, g * G], 0.0)
        for h in range(1, G):
            raw_row = raw_row + masks[h] * dec_ref[d, g * G + h]
            raw_col = jnp.where(hrow == h, dec_ref[d, g * G + h], raw_col)
        lg_row = _log_sigmoid(raw_row)
        lg_col = _log_sigmoid(raw_col)
        diff = ((sj - i_idx) if reverse else (i_idx - sj)).astype(F32)
        dmat = jnp.where(diff >= 0, jnp.exp(jnp.maximum(diff, 0.0) * lg_col), 0.0)
        qdec = jnp.exp(((L - ri) if reverse else (ri + 1.0)) * lg_row)
        kdec = jnp.exp((ri if reverse else (L - 1.0 - ri)) * lg_row)
        cdec_tile = _tile_row(jnp.exp(float(L) * lg_row))
        out_ref = o_ref if reverse else acc_ref

        def increments(c, kdec=kdec, cdec_tile=cdec_tile):
            rows = _chunk_rows(c)
            inc = _mm_tn(k_ref[rows, :] * kdec, u_ref[rows, 2 * gl:3 * gl])
            yield
            s_ref[c] = bd * inc
            cdec_ref[c] = cdec_tile

        _chunk_loop(nb * n_chunks, increments, LIGHT_CHUNKS_IN_FLIGHT)
        for s in range(nb):
            s_fin = _state_scan(n_chunks, reverse, s_ref, cdec_ref,
                                s0_ref[s, 0, d] if has_state else jnp.zeros((gl, gl), F32), s * n_chunks)
            if emit_state:
                _store_head_blocks(st_ref, s, d, s_fin)

        def outputs(c, dmat=dmat, qdec=qdec, out_ref=out_ref):
            rows = _chunk_rows(c)
            q = q_ref[rows, :]
            v = u_ref[rows, 2 * gl:3 * gl]
            att = _mm_nt(_stack(q, masks), k_ref[rows, :])
            inter = _mm(q * qdec, s_ref[c])
            yield
            intra = _mm(att * dmat, v)
            yield
            out_ref[rows, :] = _unstack(intra, masks, L) + inter

        _chunk_loop(nb * n_chunks, outputs, LIGHT_CHUNKS_IN_FLIGHT)

    def finalize(rows, o):
        mu = _mm_split(o, bd_avg)
        yield
        dev = o - mu
        var = _mm_split(dev * dev, bd_avg)
        yield
        return dev * lax.rsqrt(var + EPS) * nw_ref[...] * _silu(u_ref[rows, 3 * gl:4 * gl])

    _finalize(nb * seq, acc_ref, o_ref, finalize)


def _ret_tables(seq):
    t = np.arange(seq)
    quarter = HEAD_DIM // 4
    inv = (ROPE_BASE ** (-np.arange(quarter, dtype=np.float32) / quarter)).astype(np.float32)
    ang_r = (t // GRID_W).astype(np.float32)[:, None] * inv
    ang_c = (t % GRID_W).astype(np.float32)[:, None] * inv
    cos = np.concatenate([np.cos(ang_r), np.cos(ang_r), np.cos(ang_c), np.cos(ang_c)], axis=1)
    sin = np.concatenate([-np.sin(ang_r), np.sin(ang_r), -np.sin(ang_c), np.sin(ang_c)], axis=1)
    tile = lambda a: jnp.asarray(np.tile(a.astype(np.float32), (1, HEADS_PER_GROUP)))
    return tile(cos), tile(sin)


def _retention(u_main, ret_decay, norm_w, batch, seq, state):
    latent = state is not None
    nb = _seqs_per_step(latent)
    u_spec, _, out_spec = _mixer_specs(MIX_RET, seq, nb)
    gl = GROUP_LANES
    rows, chunks = nb * seq, nb * seq // CHUNK
    const2 = lambda b, g: (0, 0)
    in_specs = [pl.BlockSpec(memory_space=pltpu.SMEM), pl.BlockSpec((1, gl), const2), u_spec]
    args = [ret_decay, jnp.tile(norm_w.reshape(1, HEAD_DIM), (1, HEADS_PER_GROUP)), u_main]
    if latent:
        cos, sin = _ret_tables(seq)
        in_specs += [pl.BlockSpec((seq, gl), const2), pl.BlockSpec((seq, gl), const2), _state_spec(gl)]
        args += [cos, sin, state]
    out_shape = [jax.ShapeDtypeStruct((batch * seq, GROUP_WIDTH), F32)]
    out_specs = [out_spec]
    if not latent:
        out_shape.append(_head_state_shape(batch))
        out_specs.append(_head_state_spec(nb))
    res = pl.pallas_call(
        functools.partial(_ret_kernel, seq=seq, nb=nb, rope=latent, has_state=latent, emit_state=not latent),
        grid=(batch // nb, N_GROUPS),
        in_specs=in_specs,
        out_specs=out_specs,
        out_shape=out_shape,
        scratch_shapes=[pltpu.VMEM((rows, gl), F32), pltpu.VMEM((rows, gl), F32), pltpu.VMEM((rows, gl), F32),
                        pltpu.VMEM((chunks, gl, gl), F32), pltpu.VMEM((chunks, 8, gl), F32)],
        compiler_params=_MIXER_PARAMS,
        name="retention",
    )(*args)
    return res[0], (None if latent else res[1])


def _gla_kernel(nw_ref, u_ref, bf_ref, bb_ref, *rest, seq, nb, has_state, emit_state):
    rest = list(rest)
    s0_ref = rest.pop(0) if has_state else None
    o_ref = rest.pop(0)
    st_ref = rest.pop(0) if emit_state else None
    acc_ref, s_ref, dec_ref, steep_ref = rest

    gl, L, G, SB = GROUP_LANES, CHUNK, HEADS_PER_GROUP, GLA_SUB
    n_chunks = seq // L
    n_sub = L // SB
    masks = _head_masks()
    bd = _bd_mask()
    bd_avg = bd * (1.0 / HEAD_DIM)
    scale = HEAD_DIM ** -0.5
    half = SB // 2
    pair_row = _iota((SB * SB, gl), 0)
    pi = lax.shift_right_logical(pair_row, 4)
    pj = pair_row & (SB - 1)
    pair_sum = (lax.shift_right_logical(_iota((SB, SB * SB), 1), 4) == _iota((SB, SB * SB), 0)).astype(BF16)
    zeros_half = jnp.zeros((half, gl), F32)

    for d in (0, 1):
        reverse = d == 1
        pair_ok = (pj >= pi) if reverse else (pj <= pi)
        out_ref = o_ref if reverse else acc_ref

        b_ref = bb_ref if reverse else bf_ref
        steep_ref[...] = jnp.zeros((8, gl), F32)

        def increments(c, reverse=reverse, b_ref=b_ref):
            rows = _chunk_rows(c)
            b = b_ref[rows, :]
            btot = b[0:1] if reverse else b[L - 1:L]
            inc = _mm_tn(u_ref[rows, 2 * gl:3 * gl], u_ref[rows, gl:2 * gl] * jnp.exp2(btot - b))
            yield
            s_ref[c] = bd * inc
            dec_ref[c] = _tile_row(jnp.exp2(btot))
            steep_ref[...] = jnp.maximum(steep_ref[...], _tile_row(-btot))

        _chunk_loop(nb * n_chunks, increments, LIGHT_CHUNKS_IN_FLIGHT)
        factorise_all = jnp.max(steep_ref[...]) <= GLA_SAFE_LOG2
        for s in range(nb):
            s_fin = _state_scan(n_chunks, reverse, s_ref, dec_ref,
                                s0_ref[s, 0, d] if has_state else jnp.zeros((gl, gl), F32), s * n_chunks)
            if emit_state:
                _store_head_blocks(st_ref, s, d, s_fin.T)

        def outputs_factorised(c, reverse=reverse, out_ref=out_ref, b_ref=b_ref):
            rows = _chunk_rows(c)
            b = b_ref[rows, :]
            q = u_ref[rows, 0:gl] * scale
            k = u_ref[rows, gl:2 * gl]
            v = u_ref[rows, 2 * gl:3 * gl]
            o_inter = _mm_nt(q * jnp.exp2(b), s_ref[c])
            atts = []
            for i_blk in range(n_sub):
                lo = i_blk * SB
                bi = b[lo:lo + SB]
                ref, keys = (bi[SB - 1:SB], slice(lo, L)) if reverse else (bi[0:1], slice(0, lo + SB))
                qd = q[lo:lo + SB] * jnp.exp2(bi - ref)
                kd = k[keys] * jnp.exp2(ref - b[keys])
                att = _mm_nt(_stack(qd, masks), kd)
                n_keys = keys.stop - keys.start
                qrow = _iota((G * SB, n_keys), 0) & (SB - 1)
                kcol = _iota((G * SB, n_keys), 1)
                ok = (kcol >= qrow) if reverse else (kcol <= qrow + lo)
                atts.append((jnp.where(ok, att, 0.0), keys))
            yield
            outs = [_mm(a, v[keys]) for a, keys in atts]
            yield
            out_ref[rows, :] = o_inter + jnp.concatenate([_unstack(o, masks, SB) for o in outs], axis=0)

        def outputs_direct(c, reverse=reverse, pair_ok=pair_ok, out_ref=out_ref, b_ref=b_ref):
            rows = _chunk_rows(c)
            b = b_ref[rows, :]
            q = u_ref[rows, 0:gl] * scale
            k = u_ref[rows, gl:2 * gl]
            v = u_ref[rows, 2 * gl:3 * gl]
            o_inter = _mm_nt(q * jnp.exp2(b), s_ref[c])
            diag, off = [], []
            for i_blk in range(n_sub):
                lo = i_blk * SB
                qi, ki, bi = q[lo:lo + SB], k[lo:lo + SB], b[lo:lo + SB]
                prods = []
                for i in range(SB):
                    if reverse and i >= half:
                        part = slice(half, SB)
                    elif (not reverse) and i < half:
                        part = slice(0, half)
                    else:
                        part = slice(0, SB)
                    p = (qi[i:i + 1] * ki[part]) * jnp.exp2(bi[i:i + 1] - bi[part])
                    if part.start == half:
                        prods += [zeros_half, p]
                    elif part.stop == half:
                        prods += [p, zeros_half]
                    else:
                        prods.append(p)
                diag.append(_mm(jnp.concatenate(prods, axis=0), bd))
                if reverse and i_blk < n_sub - 1:
                    ref, other = bi[SB - 1:SB], slice(lo + SB, L)
                elif (not reverse) and i_blk > 0:
                    ref, other = bi[0:1], slice(0, lo)
                else:
                    off.append(None)
                    continue
                qd = qi * jnp.exp2(bi - ref)
                kd = k[other] * jnp.exp2(ref - b[other])
                off.append((_mm_nt(_stack(qd, masks), kd), other))
            yield
            off = [None if a is None else _mm(a[0], v[a[1]]) for a in off]
            for i_blk in range(n_sub):
                vi = v[i_blk * SB:(i_blk + 1) * SB]
                weighted = jnp.where(pair_ok, diag[i_blk] * jnp.concatenate([vi] * SB, axis=0), 0.0)
                diag[i_blk] = jnp.dot(pair_sum, weighted.astype(BF16), preferred_element_type=F32)
            yield
            blocks = []
            for i_blk in range(n_sub):
                o_blk = diag[i_blk]
                if off[i_blk] is not None:
                    o_blk = o_blk + _unstack(off[i_blk], masks, SB)
                blocks.append(o_blk)
            out_ref[rows, :] = o_inter + jnp.concatenate(blocks, axis=0)

        @pl.when(factorise_all)
        def _():
            _chunk_loop(nb * n_chunks, outputs_factorised, LIGHT_CHUNKS_IN_FLIGHT)

        @pl.when(jnp.logical_not(factorise_all))
        def _():
            _chunk_loop(nb * n_chunks, outputs_direct)

    def finalize(rows, o):
        ms = _mm_split(o * o, bd_avg)
        yield
        return o * lax.rsqrt(ms + EPS) * nw_ref[...] * _silu(u_ref[rows, 3 * gl:4 * gl])

    _finalize(nb * seq, acc_ref, o_ref, finalize)


def _gla(u_main, log_decay, norm_w, batch, seq, state):
    latent = state is not None
    nb = _seqs_per_step(latent)
    u_spec, _, out_spec = _mixer_specs(MIX_GLA, seq, nb)
    gl = GROUP_LANES
    rows, chunks = nb * seq, nb * seq // CHUNK
    in_specs = [
        pl.BlockSpec((1, gl), lambda b, g: (0, 0)),
        u_spec,
        pl.BlockSpec((rows, gl), lambda b, g: (b, g)),
        pl.BlockSpec((rows, gl), lambda b, g: (b, N_GROUPS + g)),
    ]
    args = [jnp.tile(norm_w.reshape(1, HEAD_DIM), (1, HEADS_PER_GROUP)), u_main, log_decay, log_decay]
    if latent:
        in_specs.append(_state_spec(gl))
        args.append(state)
    out_shape = [jax.ShapeDtypeStruct((batch * seq, GROUP_WIDTH), F32)]
    out_specs = [out_spec]
    if not latent:
        out_shape.append(_head_state_shape(batch))
        out_specs.append(_head_state_spec(nb))
    res = pl.pallas_call(
        functools.partial(_gla_kernel, seq=seq, nb=nb, has_state=latent, emit_state=not latent),
        grid=(batch // nb, N_GROUPS),
        in_specs=in_specs,
        out_specs=out_specs,
        out_shape=out_shape,
        scratch_shapes=[pltpu.VMEM((rows, gl), F32),
                        pltpu.VMEM((chunks, gl, gl), F32), pltpu.VMEM((chunks, 8, gl), F32),
                        pltpu.VMEM((8, gl), F32)],
        compiler_params=_MIXER_PARAMS,
        name="gla",
    )(*args)
    return res[0], (None if latent else res[1])


def _mlstm_kernel(e_ref, u_ref, us_ref, *rest, seq, nb, has_state, emit_state):
    rest = list(rest)
    if has_state:
        c0_ref, n0_ref, m0_ref = rest.pop(0), rest.pop(0), rest.pop(0)
    o_ref = rest.pop(0)
    if emit_state:
        cst_ref, nst_ref, mst_ref = rest.pop(0), rest.pop(0), rest.pop(0)
    acc_ref, f_ref, x_ref, p_ref, s_ref, nu_ref, wp_ref, min_ref, mout_ref = rest

    gl, L, G = GROUP_LANES, CHUNK, HEADS_PER_GROUP
    n_chunks = seq // L
    masks = _head_masks()
    bd = _bd_mask()
    scale = HEAD_DIM ** -0.5
    si = _iota((G * L, L), 0) & (L - 1)
    sj = _iota((G * L, L), 1)
    lane_in_head = _iota((L, gl), 1) & (HEAD_DIM - 1)

    for d in (0, 1):
        reverse = d == 1
        causal = (sj >= si) if reverse else (sj <= si)
        last = 0 if reverse else L - 1
        out_ref = o_ref if reverse else acc_ref

        def gates(rows, d=d):
            r = _mm_split(us_ref[rows, :], e_ref[d, 0])
            yield
            x_ref[rows, :] = r[:, 0:gl]
            f_ref[rows, :] = r[:, gl:2 * gl]
            p_ref[rows, :] = r[:, 2 * gl:3 * gl]

        _row_loop(nb * seq, gates)

        def stabiliser(t, m_row, base, reverse=reverse):
            c = base + _scan_order(t, n_chunks, reverse)
            m_new = _edge_row(f_ref, c, reverse) + jnp.maximum(m_row, _edge_row(p_ref, c, reverse))
            min_ref[c] = _tile_row(m_row)
            mout_ref[c] = _tile_row(m_new)
            return m_new

        m_fin = [lax.fori_loop(0, n_chunks, functools.partial(stabiliser, base=s * n_chunks),
                               m0_ref[s, 0, d] if has_state else jnp.zeros((1, gl), F32)) for s in range(nb)]

        def increments(c, last=last):
            rows = _chunk_rows(c)
            f_tot = f_ref[rows, :][last:last + 1]
            m_in = min_ref[c][0:1]
            m_out = mout_ref[c][0:1]
            kw = u_ref[rows, gl:2 * gl] * jnp.exp(f_tot + x_ref[rows, :] - m_out)
            inc = _mm_tn(kw, u_ref[rows, 2 * gl:3 * gl])
            yield
            s_ref[c] = bd * inc
            nu_ref[c] = _tile_row(jnp.sum(kw, axis=0, keepdims=True))
            wp_ref[c] = _tile_row(jnp.exp(f_tot + m_in - m_out))

        _chunk_loop(nb * n_chunks, increments, LIGHT_CHUNKS_IN_FLIGHT)

        def memory(t, carry, base, reverse=reverse):
            c_mat, n_row = carry
            c = base + _scan_order(t, n_chunks, reverse)
            inc = s_ref[c]
            nu = nu_ref[c][0:1]
            wp = wp_ref[c][0:1]
            s_ref[c] = c_mat
            nu_ref[c] = _tile_row(n_row)
            return wp * c_mat + inc, wp * n_row + nu

        for s in range(nb):
            if has_state:
                init = (c0_ref[s, 0, d], n0_ref[s, 0, d])
            else:
                init = (jnp.zeros((gl, gl), F32), jnp.zeros((1, gl), F32))
            c_fin, n_fin = lax.fori_loop(0, n_chunks, functools.partial(memory, base=s * n_chunks), init)
            if emit_state:
                _store_head_blocks(cst_ref, s, d, c_fin)
                nst_ref[s, 0, d], mst_ref[s, 0, d] = n_fin, m_fin[s]

        def outputs(c, causal=causal, out_ref=out_ref):
            rows = _chunk_rows(c)
            x = x_ref[rows, :]
            m_in = min_ref[c][0:1]
            q = u_ref[rows, 0:gl] * scale
            k = u_ref[rows, gl:2 * gl]
            v = u_ref[rows, 2 * gl:3 * gl]
            m_full = jnp.maximum(m_in, p_ref[rows, :])
            lhs = _split_lanes(-m_full, lane_in_head, 0, 1.0, 2)
            rhs = _split_lanes(x, lane_in_head, 2, 1.0, 0)
            logw = _mm_nt(_stack(lhs, masks), rhs)
            qk = _mm_nt(_stack(q, masks), k)
            inter = _mm(q, s_ref[c])
            q_n = _mm_split(q * nu_ref[c][0:1], bd)
            yield
            s = qk * jnp.exp(jnp.where(causal, logw, NEG))
            r = _mm(s, v)
            row_sum = jnp.sum(s, axis=1, keepdims=True)
            yield
            w_inter = jnp.exp(m_in - m_full)
            num = _unstack(r, masks, L) + w_inter * inter
            den = _unstack(jnp.broadcast_to(row_sum, (G * L, gl)), masks, L) + w_inter * q_n
            out_ref[rows, :] = num / jnp.maximum(jnp.abs(den), jnp.exp(-(f_ref[rows, :] + m_full)))

        _chunk_loop(nb * n_chunks, outputs)

    def finalize(rows, o):
        return o * jax.nn.sigmoid(u_ref[rows, 3 * gl:4 * gl])

    _finalize(nb * seq, acc_ref, o_ref, finalize)


def _gate_expanders(offset):
    e = np.zeros((2, N_GROUPS, SMALL_LANES, GROUP_LANES), np.float32)
    for d in range(2):
        for g in range(N_GROUPS):
            for h in range(HEADS_PER_GROUP):
                src = offset + d * N_HEADS + g * HEADS_PER_GROUP + h
                e[d, g, src, h * HEAD_DIM:(h + 1) * HEAD_DIM] = 1.0
    return jnp.asarray(e)


def _gate_bias_row(b_i, b_f):
    row = jnp.zeros((1, SMALL_LANES), F32)
    row = row.at[0, MI_OFF:MI_OFF + 2 * N_HEADS].set(b_i.astype(F32).reshape(-1))
    return row.at[0, MF_OFF:MF_OFF + 2 * N_HEADS].set(b_f.astype(F32).reshape(-1))


def _mlstm(u_main, u_small, batch, seq, state):
    latent = state is not None
    nb = _seqs_per_step(latent)
    u_spec, small_spec, out_spec = _mixer_specs(MIX_MLSTM, seq, nb)
    gl = GROUP_LANES
    rows, chunks = nb * seq, nb * seq // CHUNK
    expand = jnp.concatenate([_gate_expanders(MI_OFF), _gate_expanders(MF_OFF), _gate_expanders(MP_OFF)], axis=-1)
    in_specs = [pl.BlockSpec((2, 1, SMALL_LANES, 3 * gl), lambda b, g: (0, g, 0, 0)), u_spec, small_spec]
    args = [expand, u_main, u_small]
    if latent:
        in_specs += [_state_spec(gl), _state_spec(1), _state_spec(1)]
        args += list(state)
    out_shape = [jax.ShapeDtypeStruct((batch * seq, GROUP_WIDTH), F32)]
    out_specs = [out_spec]
    if not latent:
        out_shape += [_head_state_shape(batch),
                      jax.ShapeDtypeStruct((batch, N_GROUPS, 2, 1, gl), F32),
                      jax.ShapeDtypeStruct((batch, N_GROUPS, 2, 1, gl), F32)]
        out_specs += [_head_state_spec(nb), _state_spec(1, nb), _state_spec(1, nb)]
    res = pl.pallas_call(
        functools.partial(_mlstm_kernel, seq=seq, nb=nb, has_state=latent, emit_state=not latent),
        grid=(batch // nb, N_GROUPS),
        in_specs=in_specs,
        out_specs=out_specs,
        out_shape=out_shape,
        scratch_shapes=[pltpu.VMEM((rows, gl), F32)] * 4 + [pltpu.VMEM((chunks, gl, gl), F32)]
        + [pltpu.VMEM((chunks, 8, gl), F32)] * 4,
        compiler_params=_MIXER_PARAMS,
        name="mlstm",
    )(*args)
    return res[0], (None if latent else tuple(res[1:]))


def _na_ctx_kernel(u_ref, o_ref, ko_ref, vo_ref, *, seq, nb):
    gl = GROUP_LANES
    masks = _head_masks()

    def sequence(s):
        rows = slice(s * seq, (s + 1) * seq)
        q = u_ref[rows, 0:gl] * (HEAD_DIM ** -0.5)
        k = u_ref[rows, gl:2 * gl]
        v = u_ref[rows, 2 * gl:3 * gl]
        sc = _mm_nt(_stack(q, masks), k)
        yield
        p = jnp.exp(sc - jnp.max(sc, axis=1, keepdims=True))
        o = _mm(p, v)
        yield
        o_ref[rows, :] = _unstack(o / jnp.sum(p, axis=1, keepdims=True), masks, seq)
        for h in range(HEADS_PER_GROUP):
            ko_ref[s, h] = k[:, h * HEAD_DIM:(h + 1) * HEAD_DIM]
            vo_ref[s, h] = v[:, h * HEAD_DIM:(h + 1) * HEAD_DIM]

    _interleave([sequence(s) for s in range(nb)])


def _na_context(u_main, batch, seq):
    nb = _seqs_per_step(False)
    u_spec, _, out_spec = _mixer_specs(MIX_NA, seq, nb)
    kv_shape = jax.ShapeDtypeStruct((batch, N_HEADS, seq, HEAD_DIM), F32)
    kv_spec = pl.BlockSpec((nb, HEADS_PER_GROUP, seq, HEAD_DIM), lambda b, g: (b, g, 0, 0))
    return pl.pallas_call(
        functools.partial(_na_ctx_kernel, seq=seq, nb=nb),
        grid=(batch // nb, N_GROUPS),
        in_specs=[u_spec],
        out_specs=[out_spec, kv_spec, kv_spec],
        out_shape=[jax.ShapeDtypeStruct((batch * seq, GROUP_WIDTH), F32), kv_shape, kv_shape],
        compiler_params=_MIXER_PARAMS,
        name="na_context",
    )(u_main)


def _na_key_col0(qb):
    return int(np.clip(qb * NA_QC - NA_KW // 2, 0, GRID_W - NA_KCB))


def _na_lat_kernel(tab_ref, kc_ref, vc_ref, u_ref, o_ref, *, seq):
    gl = GROUP_LANES
    grid_rows = seq // GRID_W
    masks = _head_masks()
    scale = HEAD_DIM ** -0.5
    kc = kc_ref[0, 0]
    vc = vc_ref[0, 0]

    def query_block(r, qb):
        ks = jnp.clip(r - NA_KH // 2, 0, grid_rows - NA_KH)
        q_rows = pl.ds(pl.multiple_of(r * GRID_W, GRID_W) + qb * NA_QC, NA_QC)
        k_rows = [pl.ds(pl.multiple_of((ks + j) * GRID_W, GRID_W) + _na_key_col0(qb), NA_KCB) for j in range(NA_KH)]
        qs = _stack(u_ref[q_rows, 0:gl] * scale, masks)
        s_loc = _mm_nt(qs, jnp.concatenate([u_ref[rows, gl:2 * gl] for rows in k_rows], axis=0))
        s_ctx = _mm_nt(qs, kc)
        yield
        s_loc = s_loc + tab_ref[0, qb, ks - r + NA_KH - 1]
        m = jnp.maximum(jnp.max(s_loc, axis=1, keepdims=True), jnp.max(s_ctx, axis=1, keepdims=True))
        p_loc = jnp.exp(s_loc - m)
        p_ctx = jnp.exp(s_ctx - m)
        den = jnp.sum(p_loc, axis=1, keepdims=True) + jnp.sum(p_ctx, axis=1, keepdims=True)
        o = _mm(p_loc, jnp.concatenate([u_ref[rows, 2 * gl:3 * gl] for rows in k_rows], axis=0)) + _mm(p_ctx, vc)
        yield
        o_ref[q_rows, :] = _unstack(o / den, masks, NA_QC)

    in_flight = min(grid_rows, NA_ROWS_IN_FLIGHT)
    assert grid_rows % in_flight == 0

    def body(t, carry):
        _interleave([query_block(t * in_flight + j, qb) for j in range(in_flight) for qb in range(GRID_W // NA_QC)])
        return carry

    lax.fori_loop(0, grid_rows // in_flight, body, 0)


def _na_bias_table(rpb):
    c = np.arange(GRID_W)
    rel = c[None, :] - c[:, None]
    cs = np.clip(c - NA_KW // 2, 0, GRID_W - NA_KW)
    valid = (c[None, :] >= cs[:, None]) & (c[None, :] < cs[:, None] + NA_KW)
    n_rel = 2 * NA_KW - 1
    pick = (np.arange(n_rel)[:, None, None] == (np.clip(rel, -(NA_KW - 1), NA_KW - 1) + NA_KW - 1)[None])
    bmat = jnp.einsum("hdk,kcz->hdcz", rpb.astype(F32), jnp.asarray(pick, F32), precision=HI)
    bmat = jnp.where(valid[None, None], bmat, NEG)
    blocks = []
    for qb in range(GRID_W // NA_QC):
        sub = bmat[:, :, qb * NA_QC:(qb + 1) * NA_QC, _na_key_col0(qb):_na_key_col0(qb) + NA_KCB]
        tabs = jnp.stack([sub[:, s:s + NA_KH] for s in range(NA_KH)], axis=1)
        tabs = tabs.transpose(0, 1, 3, 2, 4).reshape(N_GROUPS, HEADS_PER_GROUP, NA_KH, NA_QC, NA_KH * NA_KCB)
        blocks.append(tabs.transpose(0, 2, 1, 3, 4).reshape(N_GROUPS, NA_KH, HEADS_PER_GROUP * NA_QC,
                                                             NA_KH * NA_KCB))
    return jnp.stack(blocks, axis=1)


def _heads_to_lanes(t):
    b, _, s, _ = t.shape
    t = t.reshape(b, N_GROUPS, HEADS_PER_GROUP, s, HEAD_DIM).transpose(0, 1, 3, 2, 4)
    return t.reshape(b, N_GROUPS, s, GROUP_LANES)


def _na_latent(u_main, rpb, k_ctx, v_ctx, batch, seq):
    u_spec, _, out_spec = _mixer_specs(MIX_NA, seq, 1)
    gl = GROUP_LANES
    past = k_ctx.shape[2]
    tab = _na_bias_table(rpb)
    ctx_spec = pl.BlockSpec((1, 1, past, gl), lambda b, g: (b, g, 0, 0))
    return pl.pallas_call(
        functools.partial(_na_lat_kernel, seq=seq),
        grid=(batch, N_GROUPS),
        in_specs=[pl.BlockSpec((1,) + tab.shape[1:], lambda b, g: (g, 0, 0, 0, 0)), ctx_spec, ctx_spec, u_spec],
        out_specs=out_spec,
        out_shape=jax.ShapeDtypeStruct((batch * seq, GROUP_WIDTH), F32),
        compiler_params=_MIXER_PARAMS,
        name="na_latent",
    )(tab, _heads_to_lanes(k_ctx.astype(F32)), _heads_to_lanes(v_ctx.astype(F32)), u_main)


def _to_block_diag(s):
    b = s.shape[0]
    G = HEADS_PER_GROUP
    s = s.astype(F32).reshape(b, 2, N_GROUPS, G, HEAD_DIM, 1, HEAD_DIM)
    eye = jnp.eye(G, dtype=F32).reshape(1, 1, 1, G, 1, G, 1)
    return (s * eye).reshape(b, 2, N_GROUPS, GROUP_LANES, GROUP_LANES).transpose(0, 2, 1, 3, 4)


def _rows_to_lanes(v):
    b = v.shape[0]
    return v.astype(F32).reshape(b, 2, N_GROUPS, 1, GROUP_LANES).transpose(0, 2, 1, 3, 4)


def _lanes_to_rows(v):
    b = v.shape[0]
    return v.transpose(0, 2, 1, 3, 4).reshape(b, 2, N_HEADS, HEAD_DIM)


def _swap_last(s):
    return jnp.swapaxes(s, -1, -2)


def _in_weight(w_in):
    gl = GROUP_LANES
    w_in = w_in.astype(BF16)
    cols = []
    for names in _MIXER_COLS:
        for g in range(N_GROUPS):
            for name in names:
                cols.append(w_in[..., _OFF[name] + g * gl:_OFF[name] + (g + 1) * gl])
    cols.append(w_in[..., _OFF["ga"]:_OFF["ga"] + 2 * GLA_RANK])
    cols.append(w_in[..., _OFF["mi"]:_OFF["mi"] + 2 * N_HEADS])
    cols.append(w_in[..., _OFF["mf"]:_OFF["mf"] + 2 * N_HEADS])
    cols.append(jnp.zeros((DEPTH, D_MODEL, SMALL_LANES - MF_OFF - 2 * N_HEADS), BF16))
    return jnp.concatenate(cols, axis=-1)


def _gla_gate_weight(w_a2_l):
    out = jnp.zeros((SMALL_LANES, 2 * GROUP_WIDTH), F32)
    for d in range(2):
        out = out.at[d * GLA_RANK:(d + 1) * GLA_RANK, d * GROUP_WIDTH:(d + 1) * GROUP_WIDTH].set(
            w_a2_l[d].astype(F32))
    return out.astype(BF16)


def _layer(x, mod, p, batch, seq, ctx):
    latent = ctx is not None
    l = p["layer"]
    x, u_main, u_small, log_decay = _ffn_inproj(x, mod[:, 0:6], p["norm_w"][0:3], p["gate_bias"], p["gla_wa"],
                                                p["gla_ba"], p["w1"], p["w3"], p["w2"], p["w_in"], l, seq)
    mix_a, st_gla = _gla(u_main, log_decay, p["gla_norm_w"], batch, seq,
                         _swap_last(_to_block_diag(ctx["gla"])) if latent else None)
    if latent:
        mix_b = _na_latent(u_main, p["na_rpb"], ctx["na_k"], ctx["na_v"], batch, seq)
    else:
        mix_b, na_k, na_v = _na_context(u_main, batch, seq)
    mix_c, st_ml = _mlstm(u_main, u_small, batch, seq,
                          (_to_block_diag(ctx["mC"]), _rows_to_lanes(ctx["mn"]),
                           _rows_to_lanes(jnp.repeat(ctx["mm"][..., None], HEAD_DIM, axis=-1))) if latent else None)
    mix_d, st_ret = _retention(u_main, p["ret_decay"], p["ret_norm_w"], batch, seq,
                               _to_block_diag(ctx["ret"]) if latent else None)
    x = _outproj_ffn(x, (mix_a, mix_b, mix_c, mix_d), mod[:, 3:9], p["norm_w"][3:6], p["w_out"],
                     p["w1"], p["w3"], p["w2"], l, seq)
    if latent:
        return x, None
    new = {
        "na_k": na_k, "na_v": na_v, "gla": st_gla, "mC": st_ml[0],
        "mn": _lanes_to_rows(st_ml[1]), "mm": _lanes_to_rows(st_ml[2])[..., 0], "ret": st_ret,
    }
    return x, new


def kernel(x_prompt, x_sample, cache_na_k, cache_na_v, state_gla, state_mlstm_C, state_mlstm_n, state_mlstm_m,
           state_ret, c, c_ctx, w_mod, b_mod, norm_w, ffn_w1, ffn_w3, ffn_w2, w_in, w_out, gla_w_a2, gla_b_a,
           gla_norm_w, na_rpb, mlstm_b_i, mlstm_b_f, ret_decay, ret_norm_w):
    batch, seq, _ = x_prompt.shape
    dec_batch, dec_seq, _ = x_sample.shape

    cvec = jnp.zeros((8, D_MODEL), F32).at[0].set(c_ctx).at[1:1 + dec_batch].set(c)
    mods = _modulation(cvec, w_mod, b_mod).reshape(DEPTH, 8, N_MOD, D_MODEL)

    w1, w3, w2 = ffn_w1.astype(BF16), ffn_w3.astype(BF16), ffn_w2.astype(BF16)
    w_in_all, w_out_all = _in_weight(w_in), w_out.astype(BF16)
    params = []
    for l in range(DEPTH):
        params.append({
            "layer": l, "norm_w": norm_w[l],
            "w1": w1, "w3": w3, "w2": w2, "w_in": w_in_all, "w_out": w_out_all,
            "gla_wa": _gla_gate_weight(gla_w_a2[l]), "gla_ba": gla_b_a[l].astype(F32).reshape(1, 2 * GROUP_WIDTH),
            "gla_norm_w": gla_norm_w[l],
            "na_rpb": na_rpb[l], "gate_bias": _gate_bias_row(mlstm_b_i[l], mlstm_b_f[l]),
            "ret_decay": ret_decay[l], "ret_norm_w": ret_norm_w[l],
        })

    xp = x_prompt.reshape(batch * seq, D_MODEL)
    states = []
    for l in range(DEPTH):
        xp, st = _layer(xp, mods[l, 0:1], params[l], batch, seq, None)
        states.append(st)

    xs = x_sample.reshape(dec_batch * dec_seq, D_MODEL)
    for l in range(DEPTH):
        ctx = {"na_k": cache_na_k[:, l], "na_v": cache_na_v[:, l], "gla": state_gla[:, l],
               "mC": state_mlstm_C[:, l], "mn": state_mlstm_n[:, l], "mm": state_mlstm_m[:, l],
               "ret": state_ret[:, l]}
        xs, _ = _layer(xs, mods[l, 1:1 + dec_batch], params[l], dec_batch, dec_seq, ctx)

    stack = lambda name: jnp.stack([s[name] for s in states], axis=1)
    return (xp.reshape(batch, seq, D_MODEL), xs.reshape(dec_batch, dec_seq, D_MODEL),
            stack("na_k"), stack("na_v"), stack("gla"), stack("mC"), stack("mn"), stack("mm"), stack("ret"))
```

```python
import functools

import numpy as np
import jax
import jax.numpy as jnp
from jax import lax
from jax.experimental import pallas as pl
from jax.experimental.pallas import tpu as pltpu

D_MODEL = 1024
DEPTH = 2
HEAD_DIM = 64
N_HEADS = 4
GROUP_WIDTH = N_HEADS * HEAD_DIM
N_MOD = 9
GLA_RANK = 16
GLA_TAU = 16.0
CHUNK = 64
GRID_W = 64
NA_KH = 8
NA_KW = 16
NA_QC = 16
NA_KCB = NA_QC + NA_KW
ROPE_BASE = 10000.0
EPS = 1e-6
D_FF = 2816

HEADS_PER_GROUP = 2
GROUP_LANES = HEADS_PER_GROUP * HEAD_DIM
N_GROUPS = N_HEADS // HEADS_PER_GROUP
GLA_SUB = 16
GLA_SAFE_LOG2 = 64.0
ROW_TILE_IN = 256
ROW_TILE_OUT = 512
ROW_PART = 128
CTX_SEQS_PER_STEP = 4
PRE_ROWS = 128
ROWS_IN_FLIGHT = 8
NA_ROWS_IN_FLIGHT = 16
LOG2E = 1.4426950408889634
CHUNKS_IN_FLIGHT = 8
LIGHT_CHUNKS_IN_FLIGHT = 16
SMALL_LANES = 128
MI_OFF = 2 * GLA_RANK
MF_OFF = MI_OFF + 2 * N_HEADS
MP_OFF = MF_OFF + 2 * N_HEADS
NEG = -1e30
VMEM_LIMIT = 56 * 1024 * 1024

F32 = jnp.float32
BF16 = jnp.bfloat16
HI = lax.Precision.HIGHEST

_OFF = {}
_o = 0
for _name, _size in (("gq", 256), ("gk", 256), ("gv", 256), ("gg", 256), ("ga", 32), ("nq", 256), ("nk", 256),
                     ("nv", 256), ("mq", 256), ("mk", 256), ("mv", 256), ("mo", 256), ("mi", 8), ("mf", 8),
                     ("rq", 256), ("rk", 256), ("rv", 256), ("rg", 256)):
    _OFF[_name] = _o
    _o += _size
N_IN = _o
_MIXER_COLS = (("gq", "gk", "gv", "gg"), ("mq", "mk", "mv", "mo"), ("rq", "rk", "rv", "rg"), ("nq", "nk", "nv"))
MIX_GLA, MIX_MLSTM, MIX_RET, MIX_NA = range(4)
MAIN_COLS = sum(len(names) for names in _MIXER_COLS) * N_GROUPS * GROUP_LANES


def _mm(a, b):
    return jnp.dot(a.astype(BF16), b.astype(BF16), preferred_element_type=F32)


def _mm_nt(a, b):
    return lax.dot_general(a.astype(BF16), b.astype(BF16), (((1,), (1,)), ((), ())), preferred_element_type=F32)


def _mm_tn(a, b):
    return lax.dot_general(a.astype(BF16), b.astype(BF16), (((0,), (0,)), ((), ())), preferred_element_type=F32)


def _split3(x):
    hi = x.astype(BF16)
    rest = x - hi.astype(F32)
    mid = rest.astype(BF16)
    lo = (rest - mid.astype(F32)).astype(BF16)
    return jnp.concatenate([hi, mid, lo], axis=1)


def _dot_split(pieces, w):
    wb = w.astype(BF16)
    return jnp.dot(pieces, jnp.concatenate([wb, wb, wb], axis=0), preferred_element_type=F32)


def _mm_split(x, w):
    return _dot_split(_split3(x), w)


def _log_sigmoid(x):
    return jnp.minimum(x, 0.0) - jnp.log1p(jnp.exp(-jnp.abs(x)))


def _silu(x):
    return x * jax.nn.sigmoid(x)


def _iota(shape, dim):
    return lax.broadcasted_iota(jnp.int32, shape, dim)


def _head_of(idx):
    return lax.shift_right_logical(idx, 6)


def _head_masks():
    lane = _iota((1, GROUP_LANES), 1)
    return [(_head_of(lane) == h).astype(F32) for h in range(HEADS_PER_GROUP)]


def _bd_mask():
    n = GROUP_LANES
    return (_head_of(_iota((n, n), 0)) == _head_of(_iota((n, n), 1))).astype(F32)


def _stack(x, masks):
    return jnp.concatenate([x * m for m in masks], axis=0)


def _unstack(r, masks, n):
    out = r[0:n] * masks[0]
    for h in range(1, len(masks)):
        out = out + r[h * n:(h + 1) * n] * masks[h]
    return out


def _rmsnorm(x, w):
    return x * lax.rsqrt(jnp.mean(x * x, axis=-1, keepdims=True) + EPS) * w


def _split_lanes(x, lane_in_head, at, fill, fill_at):
    hi = x.astype(BF16).astype(F32)
    out = jnp.where(lane_in_head == at, hi, jnp.where(lane_in_head == at + 1, x - hi, 0.0))
    is_fill = (lane_in_head >= fill_at) & (lane_in_head < fill_at + 2)
    return jnp.where(is_fill, fill, out)


def _mod_kernel(c_ref, w_ref, b_ref, o_ref):
    s = _silu(c_ref[...])
    o_ref[0] = _mm(s, w_ref[0]) + b_ref[0]


def _modulation(cvec, w_mod, b_mod):
    n = N_MOD * D_MODEL
    tn = n // 8
    return pl.pallas_call(
        _mod_kernel,
        grid=(DEPTH, n // tn),
        in_specs=[
            pl.BlockSpec((8, D_MODEL), lambda l, j: (0, 0)),
            pl.BlockSpec((1, D_MODEL, tn), lambda l, j: (l, 0, j)),
            pl.BlockSpec((1, 1, tn), lambda l, j: (l, 0, j)),
        ],
        out_specs=pl.BlockSpec((1, 8, tn), lambda l, j: (l, 0, j)),
        out_shape=jax.ShapeDtypeStruct((DEPTH, 8, n), F32),
        compiler_params=pltpu.CompilerParams(dimension_semantics=("parallel", "parallel"),
                                             vmem_limit_bytes=VMEM_LIMIT),
        name="modulation",
    )(cvec, w_mod, b_mod.reshape(DEPTH, 1, n))


def _half_ffn(x, mod, g, w1_ref, w3_ref, w2_ref):
    h = _rmsnorm(x, g[0:1]) * (1.0 + mod[1:2]) + mod[0:1]
    hb = h.astype(BF16)
    h1 = jnp.dot(hb, w1_ref[...], preferred_element_type=F32)
    h3 = jnp.dot(hb, w3_ref[...], preferred_element_type=F32)
    yield
    y = jnp.dot((_silu(h1) * h3).astype(BF16), w2_ref[...], preferred_element_type=F32)
    yield
    return x + 0.5 * mod[2:3] * _rmsnorm(y, g[1:2])


def _row_parts(tm):
    return [slice(j, j + ROW_PART) for j in range(0, tm, ROW_PART)]


def _mlstm_gate_scans(us, gate_bias):
    lane = _iota(us.shape, 1)
    field = 2 * N_HEADS
    fwd = (lane & (field - 1)) < N_HEADS
    pre = us + gate_bias
    lf = jnp.where((lane >= MF_OFF) & (lane < MF_OFF + field), _log_sigmoid(pre), 0.0)
    f = jnp.where(fwd, _chunk_scan(lf, False, jnp.add), _chunk_scan(lf, True, jnp.add))
    x = pre - pltpu.roll(f, SMALL_LANES - (MF_OFF - MI_OFF), 1)
    p = jnp.where(fwd, _chunk_scan(x, False, jnp.maximum), _chunk_scan(x, True, jnp.maximum))
    p = pltpu.roll(p, MP_OFF - MI_OFF, 1)
    return jnp.where(lane < MI_OFF, us,
                     jnp.where(lane < MF_OFF, x, jnp.where(lane < MP_OFF, f, jnp.where(lane < MP_OFF + field, p, 0.0))))


def _gla_gate(u_small, wa, ba):
    return _mm(u_small, wa) + ba


def _gla_log_decay(a_pre):
    la = _log_sigmoid(a_pre) * (LOG2E / GLA_TAU)
    return jnp.concatenate([_chunk_scan(la[:, :GROUP_WIDTH], False, jnp.add),
                            _chunk_scan(la[:, GROUP_WIDTH:], True, jnp.add)], axis=1)


def _ffn_inproj_kernel(x_ref, mod_ref, g_ref, gb_ref, wa_ref, ba_ref, w1_ref, w3_ref, w2_ref, win_ref,
                       x_out_ref, um_ref, us_ref, ub_ref):
    mod = mod_ref[0]
    g = g_ref[...]

    def part(rows):
        x = yield from _half_ffn(x_ref[rows, :], mod[0:3], g[0:2], w1_ref, w3_ref, w2_ref)
        x_out_ref[rows, :] = x
        hb = (_rmsnorm(x, g[2:3]) * (1.0 + mod[4:5]) + mod[3:4]).astype(BF16)
        u_small = jnp.dot(hb, win_ref[:, MAIN_COLS:], preferred_element_type=F32)
        u_main = jnp.dot(hb, win_ref[:, :MAIN_COLS], preferred_element_type=F32)
        us_ref[rows, :] = _mlstm_gate_scans(u_small, gb_ref[...])
        ub_ref[rows, :] = _gla_log_decay(_gla_gate(u_small, wa_ref[...], ba_ref[...]))
        yield
        um_ref[rows, :] = u_main

    _interleave([part(rows) for rows in _row_parts(x_ref.shape[0])])


def _outproj_ffn_kernel(x_ref, ma_ref, mb_ref, mc_ref, md_ref, mod_ref, g_ref, wo_ref, w1_ref, w3_ref, w2_ref,
                        o_ref):
    mod = mod_ref[0]
    g = g_ref[...]

    def part(rows):
        mix = jnp.concatenate([ma_ref[rows, :], mb_ref[rows, :], mc_ref[rows, :], md_ref[rows, :]], axis=-1)
        y = jnp.dot(mix.astype(BF16), wo_ref[...], preferred_element_type=F32)
        yield
        x = x_ref[rows, :] + mod[2:3] * _rmsnorm(y, g[0:1])
        o_ref[rows, :] = yield from _half_ffn(x, mod[3:6], g[1:3], w1_ref, w3_ref, w2_ref)

    _interleave([part(rows) for rows in _row_parts(x_ref.shape[0])])


def _row_tile(n_mod, rows_per_batch, tile):
    return tile if n_mod == 1 else min(tile, rows_per_batch)


def _mod_index(n_mod, rows_per_batch, tm):
    if n_mod == 1:
        return lambda i: (0, 0, 0)
    per = rows_per_batch // tm
    return lambda i: (i // per, 0, 0)


_ROW_PARAMS = pltpu.CompilerParams(dimension_semantics=("parallel",), vmem_limit_bytes=VMEM_LIMIT)


def _ffn_weight_specs(layer, half):
    pick = lambda i: (layer, half, 0, 0)
    return [pl.BlockSpec((None, None, D_MODEL, D_FF), pick, pipeline_mode=pl.Buffered(1)),
            pl.BlockSpec((None, None, D_MODEL, D_FF), pick, pipeline_mode=pl.Buffered(1)),
            pl.BlockSpec((None, None, D_FF, D_MODEL), pick, pipeline_mode=pl.Buffered(1))]


def _ffn_inproj(x, mod6, g3, gate_bias, gla_wa, gla_ba, w1, w3, w2, w_big, layer, rows_per_batch):
    rows = x.shape[0]
    tm = _row_tile(mod6.shape[0], rows_per_batch, ROW_TILE_IN)
    row_spec = lambda width: pl.BlockSpec((tm, width), lambda i: (i, 0))
    const = lambda i: (0, 0)
    widths = (D_MODEL, MAIN_COLS, SMALL_LANES, 2 * GROUP_WIDTH)
    return pl.pallas_call(
        _ffn_inproj_kernel,
        grid=(rows // tm,),
        in_specs=[
            row_spec(D_MODEL),
            pl.BlockSpec((1, 6, D_MODEL), _mod_index(mod6.shape[0], rows_per_batch, tm)),
            pl.BlockSpec((3, D_MODEL), const),
            pl.BlockSpec((1, SMALL_LANES), const),
            pl.BlockSpec((SMALL_LANES, 2 * GROUP_WIDTH), const),
            pl.BlockSpec((1, 2 * GROUP_WIDTH), const),
            *_ffn_weight_specs(layer, 0),
            pl.BlockSpec((None, D_MODEL, MAIN_COLS + SMALL_LANES), lambda i: (layer, 0, 0),
                         pipeline_mode=pl.Buffered(1)),
        ],
        out_specs=[row_spec(w) for w in widths],
        out_shape=[jax.ShapeDtypeStruct((rows, w), F32) for w in widths],
        compiler_params=_ROW_PARAMS,
        name="ffn_inproj",
    )(x, mod6, g3, gate_bias, gla_wa, gla_ba, w1, w3, w2, w_big)


def _outproj_ffn(x, mixes, mod6, g3, w_out, w1, w3, w2, layer, rows_per_batch):
    rows = x.shape[0]
    tm = _row_tile(mod6.shape[0], rows_per_batch, ROW_TILE_OUT)
    row_spec = lambda width: pl.BlockSpec((tm, width), lambda i: (i, 0))
    return pl.pallas_call(
        _outproj_ffn_kernel,
        grid=(rows // tm,),
        in_specs=[
            row_spec(D_MODEL), *[row_spec(GROUP_WIDTH)] * 4,
            pl.BlockSpec((1, 6, D_MODEL), _mod_index(mod6.shape[0], rows_per_batch, tm)),
            pl.BlockSpec((3, D_MODEL), lambda i: (0, 0)),
            pl.BlockSpec((None, D_MODEL, D_MODEL), lambda i: (layer, 0, 0), pipeline_mode=pl.Buffered(1)),
            *_ffn_weight_specs(layer, 1),
        ],
        out_specs=row_spec(D_MODEL),
        out_shape=jax.ShapeDtypeStruct((rows, D_MODEL), F32),
        compiler_params=_ROW_PARAMS,
        name="outproj_ffn",
    )(x, *mixes, mod6, g3, w_out, w1, w3, w2)


def _seqs_per_step(latent):
    return 1 if latent else CTX_SEQS_PER_STEP


def _mixer_specs(mixer, seq, nb):
    gl = GROUP_LANES
    n_blocks = len(_MIXER_COLS[mixer])
    first = sum(len(names) for names in _MIXER_COLS[:mixer]) * N_GROUPS
    assert first % n_blocks == 0
    u_spec = pl.BlockSpec((nb * seq, n_blocks * gl), lambda b, g: (b, first // n_blocks + g))
    small_spec = pl.BlockSpec((nb * seq, SMALL_LANES), lambda b, g: (b, 0))
    out_spec = pl.BlockSpec((nb * seq, gl), lambda b, g: (b, g))
    return u_spec, small_spec, out_spec


def _state_spec(rows, nb=1):
    return pl.BlockSpec((nb, 1, 2, rows, GROUP_LANES), lambda b, g: (b, g, 0, 0, 0))


def _head_state_shape(batch):
    return jax.ShapeDtypeStruct((batch, 2, N_HEADS, HEAD_DIM, HEAD_DIM), F32)


def _head_state_spec(nb):
    return pl.BlockSpec((nb, 2, HEADS_PER_GROUP, HEAD_DIM, HEAD_DIM), lambda b, g: (b, 0, g, 0, 0))


def _store_head_blocks(ref, s, d, mat):
    for h in range(HEADS_PER_GROUP):
        lo = h * HEAD_DIM
        ref[s, d, h] = mat[lo:lo + HEAD_DIM, lo:lo + HEAD_DIM]


_MIXER_PARAMS = pltpu.CompilerParams(dimension_semantics=("parallel", "parallel"), vmem_limit_bytes=VMEM_LIMIT)


def _chunk_rows(c):
    return pl.ds(pl.multiple_of(c * CHUNK, CHUNK), CHUNK)


def _scan_order(t, n_chunks, reverse):
    return (n_chunks - 1 - t) if reverse else t


def _tile_row(row):
    return jnp.broadcast_to(row, (8, row.shape[1]))


def _edge_row(ref, c, reverse):
    if reverse:
        return ref[pl.ds(pl.multiple_of(c * CHUNK, 8), 8), :][0:1]
    return ref[pl.ds(pl.multiple_of(c * CHUNK + CHUNK - 8, 8), 8), :][7:8]


def _chunk_scan(x, reverse, op):
    n = x.shape[0]
    row = _iota(x.shape, 0) & (CHUNK - 1)
    s = 1
    while s < CHUNK:
        if reverse:
            shifted = pltpu.roll(x, n - s, 0)
            ok = row < CHUNK - s
        else:
            shifted = pltpu.roll(x, s, 0)
            ok = row >= s
        x = jnp.where(ok, op(x, shifted), x)
        s *= 2
    return x


def _row_loop(seq, fn):
    blk = min(seq, PRE_ROWS)
    k = min(seq // blk, ROWS_IN_FLIGHT)
    assert seq % (blk * k) == 0

    def body(i, c):
        _interleave([fn(pl.ds(pl.multiple_of((i * k + j) * blk, blk), blk)) for j in range(k)])
        return c

    if seq == blk * k:
        _interleave([fn(pl.ds(j * blk, blk)) for j in range(k)])
    else:
        lax.fori_loop(0, seq // (blk * k), body, 0)


def _chunk_loop(n_chunks, make_stream, in_flight=None):
    k = min(n_chunks, in_flight or CHUNKS_IN_FLIGHT)
    assert n_chunks % k == 0

    def body(t, carry):
        _interleave([make_stream(t * k + j) for j in range(k)])
        return carry

    if n_chunks == k:
        body(0, 0)
    else:
        lax.fori_loop(0, n_chunks // k, body, 0)


def _interleave(streams):
    results = [None] * len(streams)
    live = []
    for i, s in enumerate(streams):
        if hasattr(s, "send"):
            live.append(i)
        else:
            results[i] = s
    while live:
        for i in list(live):
            try:
                next(streams[i])
            except StopIteration as done:
                results[i] = done.value
                live.remove(i)
    return results


def _state_scan(n_chunks, reverse, s_ref, decay_ref, init, base):
    def body(t, s):
        c = base + _scan_order(t, n_chunks, reverse)
        inc = s_ref[c]
        s_ref[c] = s
        return decay_ref[c][0:1] * s + inc

    return lax.fori_loop(0, n_chunks, body, init)


def _finalize(seq, acc_ref, o_ref, fn):
    def blk(rows):
        res = fn(rows, acc_ref[rows, :] + o_ref[rows, :])
        if hasattr(res, "send"):
            res = yield from res
        o_ref[rows, :] = res

    _row_loop(seq, blk)


def _rope(x, cos, sin_signed):
    n = x.shape[1]
    src = _iota((n, n), 0)
    dst = _iota((n, n), 1)
    partner = jnp.where((dst & 31) < 16, dst + 16, dst - 16)
    swapped = _mm_split(x, (src == partner).astype(F32))
    return x * cos + swapped * sin_signed


def _ret_kernel(dec_ref, nw_ref, u_ref, *rest, seq, nb, rope, has_state, emit_state):
    rest = list(rest)
    cos_ref, sin_ref = (rest.pop(0), rest.pop(0)) if rope else (None, None)
    s0_ref = rest.pop(0) if has_state else None
    o_ref = rest.pop(0)
    st_ref = rest.pop(0) if emit_state else None
    acc_ref, q_ref, k_ref, s_ref, cdec_ref = rest

    gl, L, G = GROUP_LANES, CHUNK, HEADS_PER_GROUP
    n_chunks = seq // L
    g = pl.program_id(1)
    masks = _head_masks()
    bd = _bd_mask()
    bd_avg = bd * (1.0 / HEAD_DIM)
    scale = HEAD_DIM ** -0.5
    ri = _iota((L, gl), 0).astype(F32)
    si = _iota((G * L, L), 0)
    sj = _iota((G * L, L), 1)
    i_idx = si & (L - 1)
    hrow = _head_of(_iota((G * L, 1), 0))

    def prepare(rows):
        q = u_ref[rows, 0:gl]
        k = u_ref[rows, gl:2 * gl]
        if rope:
            cos = cos_ref[rows, :]
            sin = sin_ref[rows, :]
            q = _rope(q, cos, sin)
            k = _rope(k, cos, sin)
        q_ref[rows, :] = q * scale
        k_ref[rows, :] = k

    _row_loop(nb * seq, prepare)

    for d in (0, 1):
        reverse = d == 1
        raw_row = masks[0] * dec_ref[d, g * G]
        raw_col = jnp.where(hrow == 0, dec_ref[d, g * G], 0.0)
        for h in range(1, G):
            raw_row = raw_row + masks[h] * dec_ref[d, g * G + h]
            raw_col = jnp.where(hrow == h, dec_ref[d, g * G + h], raw_col)
        lg_row = _log_sigmoid(raw_row)
        lg_col = _log_sigmoid(raw_col)
        diff = ((sj - i_idx) if reverse else (i_idx - sj)).astype(F32)
        dmat = jnp.where(diff >= 0, jnp.exp(jnp.maximum(diff, 0.0) * lg_col), 0.0)
        qdec = jnp.exp(((L - ri) if reverse else (ri + 1.0)) * lg_row)
        kdec = jnp.exp((ri if reverse else (L - 1.0 - ri)) * lg_row)
        cdec_tile = _tile_row(jnp.exp(float(L) * lg_row))
        out_ref = o_ref if reverse else acc_ref

        def increments(c, kdec=kdec, cdec_tile=cdec_tile):
            rows = _chunk_rows(c)
            inc = _mm_tn(k_ref[rows, :] * kdec, u_ref[rows, 2 * gl:3 * gl])
            yield
            s_ref[c] = bd * inc
            cdec_ref[c] = cdec_tile

        _chunk_loop(nb * n_chunks, increments, LIGHT_CHUNKS_IN_FLIGHT)
        for s in range(nb):
            s_fin = _state_scan(n_chunks, reverse, s_ref, cdec_ref,
                                s0_ref[s, 0, d] if has_state else jnp.zeros((gl, gl), F32), s * n_chunks)
            if emit_state:
                _store_head_blocks(st_ref, s, d, s_fin)

        def outputs(c, dmat=dmat, qdec=qdec, out_ref=out_ref):
            rows = _chunk_rows(c)
            q = q_ref[rows, :]
            v = u_ref[rows, 2 * gl:3 * gl]
            att = _mm_nt(_stack(q, masks), k_ref[rows, :])
            inter = _mm(q * qdec, s_ref[c])
            yield
            intra = _mm(att * dmat, v)
            yield
            out_ref[rows, :] = _unstack(intra, masks, L) + inter

        _chunk_loop(nb * n_chunks, outputs, LIGHT_CHUNKS_IN_FLIGHT)

    def finalize(rows, o):
        mu = _mm_split(o, bd_avg)
        yield
        dev = o - mu
        var = _mm_split(dev * dev, bd_avg)
        yield
        return dev * lax.rsqrt(var + EPS) * nw_ref[...] * _silu(u_ref[rows, 3 * gl:4 * gl])

    _finalize(nb * seq, acc_ref, o_ref, finalize)


def _ret_tables(seq):
    t = np.arange(seq)
    quarter = HEAD_DIM // 4
    inv = (ROPE_BASE ** (-np.arange(quarter, dtype=np.float32) / quarter)).astype(np.float32)
    ang_r = (t // GRID_W).astype(np.float32)[:, None] * inv
    ang_c = (t % GRID_W).astype(np.float32)[:, None] * inv
    cos = np.concatenate([np.cos(ang_r), np.cos(ang_r), np.cos(ang_c), np.cos(ang_c)], axis=1)
    sin = np.concatenate([-np.sin(ang_r), np.sin(ang_r), -np.sin(ang_c), np.sin(ang_c)], axis=1)
    tile = lambda a: jnp.asarray(np.tile(a.astype(np.float32), (1, HEADS_PER_GROUP)))
    return tile(cos), tile(sin)


def _retention(u_main, ret_decay, norm_w, batch, seq, state):
    latent = state is not None
    nb = _seqs_per_step(latent)
    u_spec, _, out_spec = _mixer_specs(MIX_RET, seq, nb)
    gl = GROUP_LANES
    rows, chunks = nb * seq, nb * seq // CHUNK
    const2 = lambda b, g: (0, 0)
    in_specs = [pl.BlockSpec(memory_space=pltpu.SMEM), pl.BlockSpec((1, gl), const2), u_spec]
    args = [ret_decay, jnp.tile(norm_w.reshape(1, HEAD_DIM), (1, HEADS_PER_GROUP)), u_main]
    if latent:
        cos, sin = _ret_tables(seq)
        in_specs += [pl.BlockSpec((seq, gl), const2), pl.BlockSpec((seq, gl), const2), _state_spec(gl)]
        args += [cos, sin, state]
    out_shape = [jax.ShapeDtypeStruct((batch * seq, GROUP_WIDTH), F32)]
    out_specs = [out_spec]
    if not latent:
        out_shape.append(_head_state_shape(batch))
        out_specs.append(_head_state_spec(nb))
    res = pl.pallas_call(
        functools.partial(_ret_kernel, seq=seq, nb=nb, rope=latent, has_state=latent, emit_state=not latent),
        grid=(batch // nb, N_GROUPS),
        in_specs=in_specs,
        out_specs=out_specs,
        out_shape=out_shape,
        scratch_shapes=[pltpu.VMEM((rows, gl), F32), pltpu.VMEM((rows, gl), F32), pltpu.VMEM((rows, gl), F32),
                        pltpu.VMEM((chunks, gl, gl), F32), pltpu.VMEM((chunks, 8, gl), F32)],
        compiler_params=_MIXER_PARAMS,
        name="retention",
    )(*args)
    return res[0], (None if latent else res[1])


def _gla_kernel(nw_ref, u_ref, bf_ref, bb_ref, *rest, seq, nb, has_state, emit_state):
    rest = list(rest)
    s0_ref = rest.pop(0) if has_state else None
    o_ref = rest.pop(0)
    st_ref = rest.pop(0) if emit_state else None
    acc_ref, s_ref, dec_ref, steep_ref = rest

    gl, L, G, SB = GROUP_LANES, CHUNK, HEADS_PER_GROUP, GLA_SUB
    n_chunks = seq // L
    n_sub = L // SB
    masks = _head_masks()
    bd = _bd_mask()
    bd_avg = bd * (1.0 / HEAD_DIM)
    scale = HEAD_DIM ** -0.5
    half = SB // 2
    pair_row = _iota((SB * SB, gl), 0)
    pi = lax.shift_right_logical(pair_row, 4)
    pj = pair_row & (SB - 1)
    pair_sum = (lax.shift_right_logical(_iota((SB, SB * SB), 1), 4) == _iota((SB, SB * SB), 0)).astype(BF16)
    zeros_half = jnp.zeros((half, gl), F32)

    for d in (0, 1):
        reverse = d == 1
        pair_ok = (pj >= pi) if reverse else (pj <= pi)
        out_ref = o_ref if reverse else acc_ref

        b_ref = bb_ref if reverse else bf_ref
        steep_ref[...] = jnp.zeros((8, gl), F32)

        def increments(c, reverse=reverse, b_ref=b_ref):
            rows = _chunk_rows(c)
            b = b_ref[rows, :]
            btot = b[0:1] if reverse else b[L - 1:L]
            inc = _mm_tn(u_ref[rows, 2 * gl:3 * gl], u_ref[rows, gl:2 * gl] * jnp.exp2(btot - b))
            yield
            s_ref[c] = bd * inc
            dec_ref[c] = _tile_row(jnp.exp2(btot))
            steep_ref[...] = jnp.maximum(steep_ref[...], _tile_row(-btot))

        _chunk_loop(nb * n_chunks, increments, LIGHT_CHUNKS_IN_FLIGHT)
        factorise_all = jnp.max(steep_ref[...]) <= GLA_SAFE_LOG2
        for s in range(nb):
            s_fin = _state_scan(n_chunks, reverse, s_ref, dec_ref,
                                s0_ref[s, 0, d] if has_state else jnp.zeros((gl, gl), F32), s * n_chunks)
            if emit_state:
                _store_head_blocks(st_ref, s, d, s_fin.T)

        def outputs_factorised(c, reverse=reverse, out_ref=out_ref, b_ref=b_ref):
            rows = _chunk_rows(c)
            b = b_ref[rows, :]
            q = u_ref[rows, 0:gl] * scale
            k = u_ref[rows, gl:2 * gl]
            v = u_ref[rows, 2 * gl:3 * gl]
            o_inter = _mm_nt(q * jnp.exp2(b), s_ref[c])
            atts = []
            for i_blk in range(n_sub):
                lo = i_blk * SB
                bi = b[lo:lo + SB]
                ref, keys = (bi[SB - 1:SB], slice(lo, L)) if reverse else (bi[0:1], slice(0, lo + SB))
                qd = q[lo:lo + SB] * jnp.exp2(bi - ref)
                kd = k[keys] * jnp.exp2(ref - b[keys])
                att = _mm_nt(_stack(qd, masks), kd)
                n_keys = keys.stop - keys.start
                qrow = _iota((G * SB, n_keys), 0) & (SB - 1)
                kcol = _iota((G * SB, n_keys), 1)
                ok = (kcol >= qrow) if reverse else (kcol <= qrow + lo)
                atts.append((jnp.where(ok, att, 0.0), keys))
            yield
            outs = [_mm(a, v[keys]) for a, keys in atts]
            yield
            out_ref[rows, :] = o_inter + jnp.concatenate([_unstack(o, masks, SB) for o in outs], axis=0)

        def outputs_direct(c, reverse=reverse, pair_ok=pair_ok, out_ref=out_ref, b_ref=b_ref):
            rows = _chunk_rows(c)
            b = b_ref[rows, :]
            q = u_ref[rows, 0:gl] * scale
            k = u_ref[rows, gl:2 * gl]
            v = u_ref[rows, 2 * gl:3 * gl]
            o_inter = _mm_nt(q * jnp.exp2(b), s_ref[c])
            diag, off = [], []
            for i_blk in range(n_sub):
                lo = i_blk * SB
                qi, ki, bi = q[lo:lo + SB], k[lo:lo + SB], b[lo:lo + SB]
                prods = []
                for i in range(SB):
                    if reverse and i >= half:
                        part = slice(half, SB)
                    elif (not reverse) and i < half:
                        part = slice(0, half)
                    else:
                        part = slice(0, SB)
                    p = (qi[i:i + 1] * ki[part]) * jnp.exp2(bi[i:i + 1] - bi[part])
                    if part.start == half:
                        prods += [zeros_half, p]
                    elif part.stop == half:
                        prods += [p, zeros_half]
                    else:
                        prods.append(p)
                diag.append(_mm(jnp.concatenate(prods, axis=0), bd))
                if reverse and i_blk < n_sub - 1:
                    ref, other = bi[SB - 1:SB], slice(lo + SB, L)
                elif (not reverse) and i_blk > 0:
                    ref, other = bi[0:1], slice(0, lo)
                else:
                    off.append(None)
                    continue
                qd = qi * jnp.exp2(bi - ref)
                kd = k[other] * jnp.exp2(ref - b[other])
                off.append((_mm_nt(_stack(qd, masks), kd), other))
            yield
            off = [None if a is None else _mm(a[0], v[a[1]]) for a in off]
            for i_blk in range(n_sub):
                vi = v[i_blk * SB:(i_blk + 1) * SB]
                weighted = jnp.where(pair_ok, diag[i_blk] * jnp.concatenate([vi] * SB, axis=0), 0.0)
                diag[i_blk] = jnp.dot(pair_sum, weighted.astype(BF16), preferred_element_type=F32)
            yield
            blocks = []
            for i_blk in range(n_sub):
                o_blk = diag[i_blk]
                if off[i_blk] is not None:
                    o_blk = o_blk + _unstack(off[i_blk], masks, SB)
                blocks.append(o_blk)
            out_ref[rows, :] = o_inter + jnp.concatenate(blocks, axis=0)

        @pl.when(factorise_all)
        def _():
            _chunk_loop(nb * n_chunks, outputs_factorised, LIGHT_CHUNKS_IN_FLIGHT)

        @pl.when(jnp.logical_not(factorise_all))
        def _():
            _chunk_loop(nb * n_chunks, outputs_direct)

    def finalize(rows, o):
        ms = _mm_split(o * o, bd_avg)
        yield
        return o * lax.rsqrt(ms + EPS) * nw_ref[...] * _silu(u_ref[rows, 3 * gl:4 * gl])

    _finalize(nb * seq, acc_ref, o_ref, finalize)


def _gla(u_main, log_decay, norm_w, batch, seq, state):
    latent = state is not None
    nb = _seqs_per_step(latent)
    u_spec, _, out_spec = _mixer_specs(MIX_GLA, seq, nb)
    gl = GROUP_LANES
    rows, chunks = nb * seq, nb * seq // CHUNK
    in_specs = [
        pl.BlockSpec((1, gl), lambda b, g: (0, 0)),
        u_spec,
        pl.BlockSpec((rows, gl), lambda b, g: (b, g)),
        pl.BlockSpec((rows, gl), lambda b, g: (b, N_GROUPS + g)),
    ]
    args = [jnp.tile(norm_w.reshape(1, HEAD_DIM), (1, HEADS_PER_GROUP)), u_main, log_decay, log_decay]
    if latent:
        in_specs.append(_state_spec(gl))
        args.append(state)
    out_shape = [jax.ShapeDtypeStruct((batch * seq, GROUP_WIDTH), F32)]
    out_specs = [out_spec]
    if not latent:
        out_shape.append(_head_state_shape(batch))
        out_specs.append(_head_state_spec(nb))
    res = pl.pallas_call(
        functools.partial(_gla_kernel, seq=seq, nb=nb, has_state=latent, emit_state=not latent),
        grid=(batch // nb, N_GROUPS),
        in_specs=in_specs,
        out_specs=out_specs,
        out_shape=out_shape,
        scratch_shapes=[pltpu.VMEM((rows, gl), F32),
                        pltpu.VMEM((chunks, gl, gl), F32), pltpu.VMEM((chunks, 8, gl), F32),
                        pltpu.VMEM((8, gl), F32)],
        compiler_params=_MIXER_PARAMS,
        name="gla",
    )(*args)
    return res[0], (None if latent else res[1])


def _mlstm_kernel(e_ref, u_ref, us_ref, *rest, seq, nb, has_state, emit_state):
    rest = list(rest)
    if has_state:
        c0_ref, n0_ref, m0_ref = rest.pop(0), rest.pop(0), rest.pop(0)
    o_ref = rest.pop(0)
    if emit_state:
        cst_ref, nst_ref, mst_ref = rest.pop(0), rest.pop(0), rest.pop(0)
    acc_ref, f_ref, x_ref, p_ref, s_ref, nu_ref, wp_ref, min_ref, mout_ref = rest

    gl, L, G = GROUP_LANES, CHUNK, HEADS_PER_GROUP
    n_chunks = seq // L
    masks = _head_masks()
    bd = _bd_mask()
    scale = HEAD_DIM ** -0.5
    si = _iota((G * L, L), 0) & (L - 1)
    sj = _iota((G * L, L), 1)
    lane_in_head = _iota((L, gl), 1) & (HEAD_DIM - 1)

    for d in (0, 1):
        reverse = d == 1
        causal = (sj >= si) if reverse else (sj <= si)
        last = 0 if reverse else L - 1
        out_ref = o_ref if reverse else acc_ref

        def gates(rows, d=d):
            r = _mm_split(us_ref[rows, :], e_ref[d, 0])
            yield
            x_ref[rows, :] = r[:, 0:gl]
            f_ref[rows, :] = r[:, gl:2 * gl]
            p_ref[rows, :] = r[:, 2 * gl:3 * gl]

        _row_loop(nb * seq, gates)

        def stabiliser(t, m_row, base, reverse=reverse):
            c = base + _scan_order(t, n_chunks, reverse)
            m_new = _edge_row(f_ref, c, reverse) + jnp.maximum(m_row, _edge_row(p_ref, c, reverse))
            min_ref[c] = _tile_row(m_row)
            mout_ref[c] = _tile_row(m_new)
            return m_new

        m_fin = [lax.fori_loop(0, n_chunks, functools.partial(stabiliser, base=s * n_chunks),
                               m0_ref[s, 0, d] if has_state else jnp.zeros((1, gl), F32)) for s in range(nb)]

        def increments(c, last=last):
            rows = _chunk_rows(c)
            f_tot = f_ref[rows, :][last:last + 1]
            m_in = min_ref[c][0:1]
            m_out = mout_ref[c][0:1]
            kw = u_ref[rows, gl:2 * gl] * jnp.exp(f_tot + x_ref[rows, :] - m_out)
            inc = _mm_tn(kw, u_ref[rows, 2 * gl:3 * gl])
            yield
            s_ref[c] = bd * inc
            nu_ref[c] = _tile_row(jnp.sum(kw, axis=0, keepdims=True))
            wp_ref[c] = _tile_row(jnp.exp(f_tot + m_in - m_out))

        _chunk_loop(nb * n_chunks, increments, LIGHT_CHUNKS_IN_FLIGHT)

        def memory(t, carry, base, reverse=reverse):
            c_mat, n_row = carry
            c = base + _scan_order(t, n_chunks, reverse)
            inc = s_ref[c]
            nu = nu_ref[c][0:1]
            wp = wp_ref[c][0:1]
            s_ref[c] = c_mat
            nu_ref[c] = _tile_row(n_row)
            return wp * c_mat + inc, wp * n_row + nu

        for s in range(nb):
            if has_state:
                init = (c0_ref[s, 0, d], n0_ref[s, 0, d])
            else:
                init = (jnp.zeros((gl, gl), F32), jnp.zeros((1, gl), F32))
            c_fin, n_fin = lax.fori_loop(0, n_chunks, functools.partial(memory, base=s * n_chunks), init)
            if emit_state:
                _store_head_blocks(cst_ref, s, d, c_fin)
                nst_ref[s, 0, d], mst_ref[s, 0, d] = n_fin, m_fin[s]

        def outputs(c, causal=causal, out_ref=out_ref):
            rows = _chunk_rows(c)
            x = x_ref[rows, :]
            m_in = min_ref[c][0:1]
            q = u_ref[rows, 0:gl] * scale
            k = u_ref[rows, gl:2 * gl]
            v = u_ref[rows, 2 * gl:3 * gl]
            m_full = jnp.maximum(m_in, p_ref[rows, :])
            lhs = _split_lanes(-m_full, lane_in_head, 0, 1.0, 2)
            rhs = _split_lanes(x, lane_in_head, 2, 1.0, 0)
            logw = _mm_nt(_stack(lhs, masks), rhs)
            qk = _mm_nt(_stack(q, masks), k)
            inter = _mm(q, s_ref[c])
            q_n = _mm_split(q * nu_ref[c][0:1], bd)
            yield
            s = qk * jnp.exp(jnp.where(causal, logw, NEG))
            r = _mm(s, v)
            row_sum = jnp.sum(s, axis=1, keepdims=True)
            yield
            w_inter = jnp.exp(m_in - m_full)
            num = _unstack(r, masks, L) + w_inter * inter
            den = _unstack(jnp.broadcast_to(row_sum, (G * L, gl)), masks, L) + w_inter * q_n
            out_ref[rows, :] = num / jnp.maximum(jnp.abs(den), jnp.exp(-(f_ref[rows, :] + m_full)))

        _chunk_loop(nb * n_chunks, outputs)

    def finalize(rows, o):
        return o * jax.nn.sigmoid(u_ref[rows, 3 * gl:4 * gl])

    _finalize(nb * seq, acc_ref, o_ref, finalize)


def _gate_expanders(offset):
    e = np.zeros((2, N_GROUPS, SMALL_LANES, GROUP_LANES), np.float32)
    for d in range(2):
        for g in range(N_GROUPS):
            for h in range(HEADS_PER_GROUP):
                src = offset + d * N_HEADS + g * HEADS_PER_GROUP + h
                e[d, g, src, h * HEAD_DIM:(h + 1) * HEAD_DIM] = 1.0
    return jnp.asarray(e)


def _gate_bias_row(b_i, b_f):
    row = jnp.zeros((1, SMALL_LANES), F32)
    row = row.at[0, MI_OFF:MI_OFF + 2 * N_HEADS].set(b_i.astype(F32).reshape(-1))
    return row.at[0, MF_OFF:MF_OFF + 2 * N_HEADS].set(b_f.astype(F32).reshape(-1))


def _mlstm(u_main, u_small, batch, seq, state):
    latent = state is not None
    nb = _seqs_per_step(latent)
    u_spec, small_spec, out_spec = _mixer_specs(MIX_MLSTM, seq, nb)
    gl = GROUP_LANES
    rows, chunks = nb * seq, nb * seq // CHUNK
    expand = jnp.concatenate([_gate_expanders(MI_OFF), _gate_expanders(MF_OFF), _gate_expanders(MP_OFF)], axis=-1)
    in_specs = [pl.BlockSpec((2, 1, SMALL_LANES, 3 * gl), lambda b, g: (0, g, 0, 0)), u_spec, small_spec]
    args = [expand, u_main, u_small]
    if latent:
        in_specs += [_state_spec(gl), _state_spec(1), _state_spec(1)]
        args += list(state)
    out_shape = [jax.ShapeDtypeStruct((batch * seq, GROUP_WIDTH), F32)]
    out_specs = [out_spec]
    if not latent:
        out_shape += [_head_state_shape(batch),
                      jax.ShapeDtypeStruct((batch, N_GROUPS, 2, 1, gl), F32),
                      jax.ShapeDtypeStruct((batch, N_GROUPS, 2, 1, gl), F32)]
        out_specs += [_head_state_spec(nb), _state_spec(1, nb), _state_spec(1, nb)]
    res = pl.pallas_call(
        functools.partial(_mlstm_kernel, seq=seq, nb=nb, has_state=latent, emit_state=not latent),
        grid=(batch // nb, N_GROUPS),
        in_specs=in_specs,
        out_specs=out_specs,
        out_shape=out_shape,
        scratch_shapes=[pltpu.VMEM((rows, gl), F32)] * 4 + [pltpu.VMEM((chunks, gl, gl), F32)]
        + [pltpu.VMEM((chunks, 8, gl), F32)] * 4,
        compiler_params=_MIXER_PARAMS,
        name="mlstm",
    )(*args)
    return res[0], (None if latent else tuple(res[1:]))


def _na_ctx_kernel(u_ref, o_ref, ko_ref, vo_ref, *, seq, nb):
    gl = GROUP_LANES
    masks = _head_masks()

    def sequence(s):
        rows = slice(s * seq, (s + 1) * seq)
        q = u_ref[rows, 0:gl] * (HEAD_DIM ** -0.5)
        k = u_ref[rows, gl:2 * gl]
        v = u_ref[rows, 2 * gl:3 * gl]
        sc = _mm_nt(_stack(q, masks), k)
        yield
        p = jnp.exp(sc - jnp.max(sc, axis=1, keepdims=True))
        o = _mm(p, v)
        yield
        o_ref[rows, :] = _unstack(o / jnp.sum(p, axis=1, keepdims=True), masks, seq)
        for h in range(HEADS_PER_GROUP):
            ko_ref[s, h] = k[:, h * HEAD_DIM:(h + 1) * HEAD_DIM]
            vo_ref[s, h] = v[:, h * HEAD_DIM:(h + 1) * HEAD_DIM]

    _interleave([sequence(s) for s in range(nb)])


def _na_context(u_main, batch, seq):
    nb = _seqs_per_step(False)
    u_spec, _, out_spec = _mixer_specs(MIX_NA, seq, nb)
    kv_shape = jax.ShapeDtypeStruct((batch, N_HEADS, seq, HEAD_DIM), F32)
    kv_spec = pl.BlockSpec((nb, HEADS_PER_GROUP, seq, HEAD_DIM), lambda b, g: (b, g, 0, 0))
    return pl.pallas_call(
        functools.partial(_na_ctx_kernel, seq=seq, nb=nb),
        grid=(batch // nb, N_GROUPS),
        in_specs=[u_spec],
        out_specs=[out_spec, kv_spec, kv_spec],
        out_shape=[jax.ShapeDtypeStruct((batch * seq, GROUP_WIDTH), F32), kv_shape, kv_shape],
        compiler_params=_MIXER_PARAMS,
        name="na_context",
    )(u_main)


def _na_key_col0(qb):
    return int(np.clip(qb * NA_QC - NA_KW // 2, 0, GRID_W - NA_KCB))


def _na_lat_kernel(tab_ref, kc_ref, vc_ref, u_ref, o_ref, *, seq):
    gl = GROUP_LANES
    grid_rows = seq // GRID_W
    masks = _head_masks()
    scale = HEAD_DIM ** -0.5
    kc = kc_ref[0, 0]
    vc = vc_ref[0, 0]

    def query_block(r, qb):
        ks = jnp.clip(r - NA_KH // 2, 0, grid_rows - NA_KH)
        q_rows = pl.ds(pl.multiple_of(r * GRID_W, GRID_W) + qb * NA_QC, NA_QC)
        k_rows = [pl.ds(pl.multiple_of((ks + j) * GRID_W, GRID_W) + _na_key_col0(qb), NA_KCB) for j in range(NA_KH)]
        qs = _stack(u_ref[q_rows, 0:gl] * scale, masks)
        s_loc = _mm_nt(qs, jnp.concatenate([u_ref[rows, gl:2 * gl] for rows in k_rows], axis=0))
        s_ctx = _mm_nt(qs, kc)
        yield
        s_loc = s_loc + tab_ref[0, qb, ks - r + NA_KH - 1]
        m = jnp.maximum(jnp.max(s_loc, axis=1, keepdims=True), jnp.max(s_ctx, axis=1, keepdims=True))
        p_loc = jnp.exp(s_loc - m)
        p_ctx = jnp.exp(s_ctx - m)
        den = jnp.sum(p_loc, axis=1, keepdims=True) + jnp.sum(p_ctx, axis=1, keepdims=True)
        o = _mm(p_loc, jnp.concatenate([u_ref[rows, 2 * gl:3 * gl] for rows in k_rows], axis=0)) + _mm(p_ctx, vc)
        yield
        o_ref[q_rows, :] = _unstack(o / den, masks, NA_QC)

    in_flight = min(grid_rows, NA_ROWS_IN_FLIGHT)
    assert grid_rows % in_flight == 0

    def body(t, carry):
        _interleave([query_block(t * in_flight + j, qb) for j in range(in_flight) for qb in range(GRID_W // NA_QC)])
        return carry

    lax.fori_loop(0, grid_rows // in_flight, body, 0)


def _na_bias_table(rpb):
    c = np.arange(GRID_W)
    rel = c[None, :] - c[:, None]
    cs = np.clip(c - NA_KW // 2, 0, GRID_W - NA_KW)
    valid = (c[None, :] >= cs[:, None]) & (c[None, :] < cs[:, None] + NA_KW)
    n_rel = 2 * NA_KW - 1
    pick = (np.arange(n_rel)[:, None, None] == (np.clip(rel, -(NA_KW - 1), NA_KW - 1) + NA_KW - 1)[None])
    bmat = jnp.einsum("hdk,kcz->hdcz", rpb.astype(F32), jnp.asarray(pick, F32), precision=HI)
    bmat = jnp.where(valid[None, None], bmat, NEG)
    blocks = []
    for qb in range(GRID_W // NA_QC):
        sub = bmat[:, :, qb * NA_QC:(qb + 1) * NA_QC, _na_key_col0(qb):_na_key_col0(qb) + NA_KCB]
        tabs = jnp.stack([sub[:, s:s + NA_KH] for s in range(NA_KH)], axis=1)
        tabs = tabs.transpose(0, 1, 3, 2, 4).reshape(N_GROUPS, HEADS_PER_GROUP, NA_KH, NA_QC, NA_KH * NA_KCB)
        blocks.append(tabs.transpose(0, 2, 1, 3, 4).reshape(N_GROUPS, NA_KH, HEADS_PER_GROUP * NA_QC,
                                                             NA_KH * NA_KCB))
    return jnp.stack(blocks, axis=1)


def _heads_to_lanes(t):
    b, _, s, _ = t.shape
    t = t.reshape(b, N_GROUPS, HEADS_PER_GROUP, s, HEAD_DIM).transpose(0, 1, 3, 2, 4)
    return t.reshape(b, N_GROUPS, s, GROUP_LANES)


def _na_latent(u_main, rpb, k_ctx, v_ctx, batch, seq):
    u_spec, _, out_spec = _mixer_specs(MIX_NA, seq, 1)
    gl = GROUP_LANES
    past = k_ctx.shape[2]
    tab = _na_bias_table(rpb)
    ctx_spec = pl.BlockSpec((1, 1, past, gl), lambda b, g: (b, g, 0, 0))
    return pl.pallas_call(
        functools.partial(_na_lat_kernel, seq=seq),
        grid=(batch, N_GROUPS),
        in_specs=[pl.BlockSpec((1,) + tab.shape[1:], lambda b, g: (g, 0, 0, 0, 0)), ctx_spec, ctx_spec, u_spec],
        out_specs=out_spec,
        out_shape=jax.ShapeDtypeStruct((batch * seq, GROUP_WIDTH), F32),
        compiler_params=_MIXER_PARAMS,
        name="na_latent",
    )(tab, _heads_to_lanes(k_ctx.astype(F32)), _heads_to_lanes(v_ctx.astype(F32)), u_main)


def _to_block_diag(s):
    b = s.shape[0]
    G = HEADS_PER_GROUP
    s = s.astype(F32).reshape(b, 2, N_GROUPS, G, HEAD_DIM, 1, HEAD_DIM)
    eye = jnp.eye(G, dtype=F32).reshape(1, 1, 1, G, 1, G, 1)
    return (s * eye).reshape(b, 2, N_GROUPS, GROUP_LANES, GROUP_LANES).transpose(0, 2, 1, 3, 4)


def _rows_to_lanes(v):
    b = v.shape[0]
    return v.astype(F32).reshape(b, 2, N_GROUPS, 1, GROUP_LANES).transpose(0, 2, 1, 3, 4)


def _lanes_to_rows(v):
    b = v.shape[0]
    return v.transpose(0, 2, 1, 3, 4).reshape(b, 2, N_HEADS, HEAD_DIM)


def _swap_last(s):
    return jnp.swapaxes(s, -1, -2)


def _in_weight(w_in):
    gl = GROUP_LANES
    w_in = w_in.astype(BF16)
    cols = []
    for names in _MIXER_COLS:
        for g in range(N_GROUPS):
            for name in names:
                cols.append(w_in[..., _OFF[name] + g * gl:_OFF[name] + (g + 1) * gl])
    cols.append(w_in[..., _OFF["ga"]:_OFF["ga"] + 2 * GLA_RANK])
    cols.append(w_in[..., _OFF["mi"]:_OFF["mi"] + 2 * N_HEADS])
    cols.append(w_in[..., _OFF["mf"]:_OFF["mf"] + 2 * N_HEADS])
    cols.append(jnp.zeros((DEPTH, D_MODEL, SMALL_LANES - MF_OFF - 2 * N_HEADS), BF16))
    return jnp.concatenate(cols, axis=-1)


def _gla_gate_weight(w_a2_l):
    out = jnp.zeros((SMALL_LANES, 2 * GROUP_WIDTH), F32)
    for d in range(2):
        out = out.at[d * GLA_RANK:(d + 1) * GLA_RANK, d * GROUP_WIDTH:(d + 1) * GROUP_WIDTH].set(
            w_a2_l[d].astype(F32))
    return out.astype(BF16)


def _layer(x, mod, p, batch, seq, ctx):
    latent = ctx is not None
    l = p["layer"]
    x, u_main, u_small, log_decay = _ffn_inproj(x, mod[:, 0:6], p["norm_w"][0:3], p["gate_bias"], p["gla_wa"],
                                                p["gla_ba"], p["w1"], p["w3"], p["w2"], p["w_in"], l, seq)
    mix_a, st_gla = _gla(u_main, log_decay, p["gla_norm_w"], batch, seq,
                         _swap_last(_to_block_diag(ctx["gla"])) if latent else None)
    if latent:
        mix_b = _na_latent(u_main, p["na_rpb"], ctx["na_k"], ctx["na_v"], batch, seq)
    else:
        mix_b, na_k, na_v = _na_context(u_main, batch, seq)
    mix_c, st_ml = _mlstm(u_main, u_small, batch, seq,
                          (_to_block_diag(ctx["mC"]), _rows_to_lanes(ctx["mn"]),
                           _rows_to_lanes(jnp.repeat(ctx["mm"][..., None], HEAD_DIM, axis=-1))) if latent else None)
    mix_d, st_ret = _retention(u_main, p["ret_decay"], p["ret_norm_w"], batch, seq,
                               _to_block_diag(ctx["ret"]) if latent else None)
    x = _outproj_ffn(x, (mix_a, mix_b, mix_c, mix_d), mod[:, 3:9], p["norm_w"][3:6], p["w_out"],
                     p["w1"], p["w3"], p["w2"], l, seq)
    if latent:
        return x, None
    new = {
        "na_k": na_k, "na_v": na_v, "gla": st_gla, "mC": st_ml[0],
        "mn": _lanes_to_rows(st_ml[1]), "mm": _lanes_to_rows(st_ml[2])[..., 0], "ret": st_ret,
    }
    return x, new


def kernel(x_prompt, x_sample, cache_na_k, cache_na_v, state_gla, state_mlstm_C, state_mlstm_n, state_mlstm_m,
           state_ret, c, c_ctx, w_mod, b_mod, norm_w, ffn_w1, ffn_w3, ffn_w2, w_in, w_out, gla_w_a2, gla_b_a,
           gla_norm_w, na_rpb, mlstm_b_i, mlstm_b_f, ret_decay, ret_norm_w):
    batch, seq, _ = x_prompt.shape
    dec_batch, dec_seq, _ = x_sample.shape

    cvec = jnp.zeros((8, D_MODEL), F32).at[0].set(c_ctx).at[1:1 + dec_batch].set(c)
    mods = _modulation(cvec, w_mod, b_mod).reshape(DEPTH, 8, N_MOD, D_MODEL)

    w1, w3, w2 = ffn_w1.astype(BF16), ffn_w3.astype(BF16), ffn_w2.astype(BF16)
    w_in_all, w_out_all = _in_weight(w_in), w_out.astype(BF16)
    params = []
    for l in range(DEPTH):
        params.append({
            "layer": l, "norm_w": norm_w[l],
            "w1": w1, "w3": w3, "w2": w2, "w_in": w_in_all, "w_out": w_out_all,
            "gla_wa": _gla_gate_weight(gla_w_a2[l]), "gla_ba": gla_b_a[l].astype(F32).reshape(1, 2 * GROUP_WIDTH),
            "gla_norm_w": gla_norm_w[l],
            "na_rpb": na_rpb[l], "gate_bias": _gate_bias_row(mlstm_b_i[l], mlstm_b_f[l]),
            "ret_decay": ret_decay[l], "ret_norm_w": ret_norm_w[l],
        })

    xp = x_prompt.reshape(batch * seq, D_MODEL)
    states = []
    for l in range(DEPTH):
        xp, st = _layer(xp, mods[l, 0:1], params[l], batch, seq, None)
        states.append(st)

    xs = x_sample.reshape(dec_batch * dec_seq, D_MODEL)
    for l in range(DEPTH):
        ctx = {"na_k": cache_na_k[:, l], "na_v": cache_na_v[:, l], "gla": state_gla[:, l],
               "mC": state_mlstm_C[:, l], "mn": state_mlstm_n[:, l], "mm": state_mlstm_m[:, l],
               "ret": state_ret[:, l]}
        xs, _ = _layer(xs, mods[l, 1:1 + dec_batch], params[l], dec_batch, dec_seq, ctx)

    stack = lambda name: jnp.stack([s[name] for s in states], axis=1)
    return (xp.reshape(batch, seq, D_MODEL), xs.reshape(dec_batch, dec_seq, D_MODEL),
            stack("na_k"), stack("na_v"), stack("gla"), stack("mC"), stack("mn"), stack("mm"), stack("ret"))
```
